```python
import math
import jax
import jax.numpy as jnp
from jax import lax
import numpy as np

D_MODEL = 1024
BATCH = 8
SEQ = 4096
DEPTH = 2

CTX_LEN = 256
GRID_W = 64
ROPE_BASE = 10000.0
F32 = jnp.float32
NEG_INF = -1e30
LN_EPS = 1e-6

HY_W = 256
HY_ORDER = 2
HY_DIRS = 2
HY_EMB = 33
HY_FFN = 64
HY_MIN_DECAY = math.log(1e-2) / 1.5
HY_MAX_DECAY = math.log(1e-2) / 0.3

SWA_HEADS = 4
SWA_KV = 2
SWA_HD = 64
SWA_WIN = 128
SWA_BLOCK = 128

RW_HEADS = 4
RW_HD = 64
RW_W = RW_HEADS * RW_HD
RW_DECAY_R = 64
RW_AAA_R = 64
RW_GATE_R = 128
RW_GN_EPS = 64e-5

DF_HEADS = 4
DF_HD = 32
DF_VD = 2 * DF_HD
DF_W = DF_HEADS * DF_VD
DF_BLOCK = 128

N_BRANCH = 4
BR_W = 256
D_FF = 2816
DN_ALPHA = (2 * DEPTH) ** 0.25
DN_BETA = (8 * DEPTH) ** -0.25

SWA_SIZES = (SWA_HEADS * SWA_HD, SWA_KV * SWA_HD, SWA_KV * SWA_HD)
RW_SIZES = (RW_W, RW_W, RW_W, RW_DECAY_R, RW_DECAY_R, RW_AAA_R, RW_GATE_R, RW_GATE_R)
DF_SIZES = (DF_W, DF_W, DF_W)
IN_SIZES = (3 * HY_W, sum(SWA_SIZES), sum(RW_SIZES), sum(DF_SIZES), N_BRANCH * D_MODEL)
IN_COLS = sum(IN_SIZES)

kernel_name = 'hybrid_hyena_swa_rwkv7_diffattn_block'


def split_at(t, sizes):
    return jnp.split(t, np.cumsum(sizes)[:-1].tolist(), axis=-1)


def layer_norm(x, g=None, b=None):
    xf = x.astype(F32)
    mu = jnp.mean(xf, -1, keepdims=True)
    var = jnp.mean(jnp.square(xf - mu), -1, keepdims=True)
    y = (xf - mu) * lax.rsqrt(var + LN_EPS)
    if g is not None:
        y = y * g + b
    return y.astype(x.dtype)


def modulate(x, shift, scale):
    return layer_norm(x) * (1 + scale) + shift


def dwconv3(x, w, b):
    xp = jnp.pad(x, ((0, 0), (1, 1), (0, 0)))
    return xp[:, :-2] * w[0] + x * w[1] + xp[:, 2:] * w[2] + b


def axial_rope(x):
    L, d = x.shape[1], x.shape[-1]
    rows = L // GRID_W
    row = jnp.repeat(jnp.arange(rows), GRID_W)
    col = jnp.tile(jnp.arange(GRID_W), rows)
    nf = d // 4
    inv = ROPE_BASE ** (-jnp.arange(nf, dtype=F32) / nf)
    bshape = (L,) + (1,) * (x.ndim - 3) + (nf,)
    xf = x.astype(F32)
    out = []
    for half, pos in enumerate((row, col)):
        ang = (pos.astype(F32)[:, None] * inv[None, :]).reshape(bshape)
        cos, sin = jnp.cos(ang), jnp.sin(ang)
        xh = xf[..., half * 2 * nf:(half + 1) * 2 * nf]
        x1, x2 = xh[..., :nf], xh[..., nf:]
        out += [x1 * cos - x2 * sin, x1 * sin + x2 * cos]
    return jnp.concatenate(out, -1).astype(x.dtype)


def sink_softmax(logit_list, sink_b):
    lead = logit_list[0].shape[:-1]
    z = jnp.concatenate([t.astype(F32) for t in logit_list]
                        + [jnp.broadcast_to(sink_b.astype(F32), lead + (1,))], -1)
    return jax.nn.softmax(z, -1)[..., :-1]


def hyena_spectrum(L, w1, b1, w2, b2, w3, freq):
    t = jnp.linspace(0.0, 1.0, L, dtype=F32)[:, None]
    bands = (HY_EMB - 1) // 2
    w = 2.0 * math.pi * jnp.arange(L, dtype=F32) / L
    fb = jnp.linspace(1e-4, bands - 1, bands, dtype=F32)
    ang = w[:, None] * fb[None, :]
    z = jnp.concatenate([t, jnp.cos(ang), -jnp.sin(ang)], -1)
    fr = freq.astype(F32)
    hdn = jnp.sin(fr * (z @ w1.astype(F32) + b1.astype(F32)))
    hdn = jnp.sin(fr * (hdn @ w2.astype(F32) + b2.astype(F32)))
    filt = (hdn @ w3.astype(F32)).reshape(L, HY_ORDER, HY_DIRS, HY_W)
    deltas = jnp.abs(jnp.linspace(HY_MIN_DECAY, HY_MAX_DECAY, HY_W, dtype=F32))
    filt = filt * jnp.exp(-t * deltas[None, :])[:, None, None, :]
    filt = filt / jnp.sum(jnp.abs(filt), axis=(0, 2), keepdims=True)
    fwd, bwd = filt[:, :, 0], filt[:, :, 1]
    kern = jnp.concatenate([fwd, jnp.zeros_like(fwd[:1]), bwd[:0:-1]], axis=0)
    return jnp.fft.rfft(kern, axis=0)


def fft_longconv(z, kf, bias):
    L = z.shape[1]
    zf = jnp.fft.rfft(z.astype(F32), n=2 * L, axis=1)
    y = jnp.fft.irfft(zf * kf[None], n=2 * L, axis=1)[:, :L]
    return (y + z * bias).astype(z.dtype)


def hyena_mix(p, conv_w, conv_b, kf, bias):
    v, x1, x2 = jnp.split(dwconv3(p, conv_w, conv_b), 3, -1)
    zz = x1 * fft_longconv(v, kf[:, 0], bias[0])
    return x2 * fft_longconv(zz, kf[:, 1], bias[1])


def hyena_branch(p, pc, need_ctx, conv_w, conv_b, fw1, fb1, fw2, fb2, fw3, ffreq, bias):
    kf = hyena_spectrum(p.shape[1], fw1, fb1, fw2, fb2, fw3, ffreq)
    y = hyena_mix(p, conv_w, conv_b, kf, bias)
    yc = None
    if need_ctx:
        kfc = hyena_spectrum(pc.shape[1], fw1, fb1, fw2, fb2, fw3, ffreq)
        yc = hyena_mix(pc, conv_w, conv_b, kfc, bias)
    return y, yc


def swa_latent(q, k, v, kc, vc, sink):
    B, L = q.shape[:2]
    nb = L // SWA_BLOCK
    G = SWA_HEADS // SWA_KV
    scale = SWA_HD ** -0.5
    qb = q.reshape(B, nb, SWA_BLOCK, SWA_KV, G, SWA_HD)

    def band(t):
        tp = jnp.pad(t, ((0, 0), (SWA_BLOCK, SWA_BLOCK), (0, 0), (0, 0)))
        tp = tp.reshape(B, nb + 2, SWA_BLOCK, SWA_KV, SWA_HD)
        return jnp.concatenate([tp[:, :-2], tp[:, 1:-1], tp[:, 2:]], axis=2)

    kb, vb = band(k), band(v)
    s_loc = jnp.einsum('bnqkgd,bnskd->bnkgqs', qb, kb).astype(F32) * scale
    s_ctx = jnp.einsum('bnqkgd,bckd->bnkgqc', qb, kc).astype(F32) * scale
    blk = jnp.arange(nb)[:, None, None]
    qpos = blk * SWA_BLOCK + jnp.arange(SWA_BLOCK)[None, :, None]
    kpos = (blk - 1) * SWA_BLOCK + jnp.arange(3 * SWA_BLOCK)[None, None, :]
    valid = (jnp.abs(kpos - qpos) <= SWA_WIN) & (kpos >= 0) & (kpos < L)
    s_loc = jnp.where(valid[None, :, None, None], s_loc, NEG_INF)
    p = sink_softmax([s_loc, s_ctx], sink.reshape(SWA_KV, G)[:, :, None, None])
    nl = 3 * SWA_BLOCK
    o = (jnp.einsum('bnkgqs,bnskd->bnqkgd', p[..., :nl].astype(v.dtype), vb)
         + jnp.einsum('bnkgqc,bckd->bnqkgd', p[..., nl:].astype(v.dtype), vc))
    return o.reshape(B, L, SWA_HEADS * SWA_HD)


def swa_context(qc, kc, vc, sink):
    B, C = qc.shape[:2]
    G = SWA_HEADS // SWA_KV
    qg = qc.reshape(B, C, SWA_KV, G, SWA_HD)
    s = jnp.einsum('bqkgd,bckd->bkgqc', qg, kc).astype(F32) * SWA_HD ** -0.5
    p = sink_softmax([s], sink.reshape(SWA_KV, G)[:, :, None, None])
    o = jnp.einsum('bkgqc,bckd->bqkgd', p.astype(vc.dtype), vc)
    return o.reshape(B, C, SWA_HEADS * SWA_HD)


def swa_branch(p, pc, need_ctx, sink):
    B, L = p.shape[:2]
    C = pc.shape[1]
    q, k, v = split_at(p, SWA_SIZES)
    qc, kc, vc = split_at(pc, SWA_SIZES)
    q = axial_rope(q.reshape(B, L, SWA_HEADS, SWA_HD))
    k = axial_rope(k.reshape(B, L, SWA_KV, SWA_HD))
    v = v.reshape(B, L, SWA_KV, SWA_HD)
    kc = kc.reshape(B, C, SWA_KV, SWA_HD)
    vc = vc.reshape(B, C, SWA_KV, SWA_HD)
    y = swa_latent(q, k, v, kc, vc, sink)
    yc = swa_context(qc.reshape(B, C, SWA_HEADS, SWA_HD), kc, vc, sink) if need_ctx else None
    return y, yc


def rwkv_streams(p, mu, w0, w2, a0, a2, g2, k_k, k_a):
    B, L = p.shape[:2]
    pp = jnp.pad(p, ((0, 0), (1, 1), (0, 0)))
    p = p + (0.5 * (pp[:, :-2] + pp[:, 2:]) - p) * mu
    r, k, v, wd_f, wd_b, ad, gd_f, gd_b = split_at(p, RW_SIZES)

    def heads(t):
        return t.reshape(B, L, RW_HEADS, RW_HD)

    def decay(wd, w0d, w2d):
        wlog = -jax.nn.softplus(-(w0d + jnp.tanh(wd) @ w2d)) - 0.5
        return jnp.exp(-jnp.exp(wlog.astype(F32)))

    a = jax.nn.sigmoid(a0 + ad @ a2)
    kk = heads(k * k_k).astype(F32)
    kk = kk / jnp.maximum(jnp.sqrt(jnp.sum(kk * kk, -1, keepdims=True)), 1e-12)
    k = k * (1 + (a - 1) * k_a)
    g_f = jax.nn.sigmoid(gd_f) @ g2[0]
    g_b = jax.nn.sigmoid(gd_b) @ g2[1]
    return (heads(r), heads(k), heads(v), kk, heads(a),
            heads(decay(wd_f, w0[0], w2[0])), heads(decay(wd_b, w0[1], w2[1])),
            heads(g_f), heads(g_b))


def rwkv_scan(state0, r, dec, k, v, kk, a, reverse, emit):
    xs = tuple(jnp.moveaxis(t.astype(F32), 1, 0) for t in (r, dec, k, v, kk, a))

    def step(S, inp):
        r_t, w_t, k_t, v_t, kk_t, a_t = inp
        sa = jnp.einsum('bhvk,bhk->bhv', S, -kk_t)
        S = (S * w_t[:, :, None, :] + sa[..., None] * (kk_t * a_t)[:, :, None, :]
             + v_t[..., None] * k_t[:, :, None, :])
        return S, (jnp.einsum('bhvk,bhk->bhv', S, r_t) if emit else None)

    S, out = lax.scan(step, state0, xs, reverse=reverse)
    return S, (jnp.moveaxis(out, 0, 1) if emit else None)


def rwkv_out(o_f, o_b, r, k, v, g_f, g_b, r_k, lnx_g, lnx_b):
    B, L = r.shape[:2]
    gam = lnx_g.reshape(RW_HEADS, RW_HD)
    bet = lnx_b.reshape(RW_HEADS, RW_HD)

    def gn(o):
        mu = jnp.mean(o, -1, keepdims=True)
        var = jnp.mean(jnp.square(o - mu), -1, keepdims=True)
        return (o - mu) * lax.rsqrt(var + RW_GN_EPS) * gam + bet

    bonus = jnp.sum(r * k * r_k, -1, keepdims=True) * v
    y = (gn(o_f) + bonus) * g_f + (gn(o_b) + bonus) * g_b
    return y.reshape(B, L, RW_W).astype(v.dtype)


def rwkv_branch(p, pc, need_ctx, mu, w0, w2, a0, a2, g2, k_k, k_a, r_k, lnx_g, lnx_b):
    r, k, v, kk, a, d_f, d_b, g_f, g_b = rwkv_streams(p, mu, w0, w2, a0, a2, g2, k_k, k_a)
    rc, kc, vc, kkc, ac, dc_f, dc_b, gc_f, gc_b = rwkv_streams(pc, mu, w0, w2, a0, a2, g2, k_k, k_a)
    S0 = jnp.zeros((p.shape[0], RW_HEADS, RW_HD, RW_HD), F32)
    Sf, oc_f = rwkv_scan(S0, rc, dc_f, kc, vc, kkc, ac, False, need_ctx)
    Sb, oc_b = rwkv_scan(S0, rc, dc_b, kc, vc, kkc, ac, True, need_ctx)
    _, o_f = rwkv_scan(Sf, r, d_f, k, v, kk, a, False, True)
    _, o_b = rwkv_scan(Sb, r, d_b, k, v, kk, a, True, True)
    y = rwkv_out(o_f, o_b, r, k, v, g_f, g_b, r_k, lnx_g, lnx_b)
    yc = rwkv_out(oc_f, oc_b, rc, kc, vc, gc_f, gc_b, r_k, lnx_g, lnx_b) if need_ctx else None
    return y, yc


def diff_attend(q, k, v, lam, lam_init, subln_g):
    B, Lq = q.shape[:2]
    nb = Lq // DF_BLOCK
    qb = jnp.moveaxis(q.reshape(B, nb, DF_BLOCK, DF_HEADS, 2, DF_HD), 1, 0)
    scale = DF_HD ** -0.5

    def block(qblk):
        s = jnp.einsum('bqhcd,bshcd->bhcqs', qblk, k).astype(F32) * scale
        pr = jax.nn.softmax(s, -1)
        wgt = pr[:, :, 0] - lam * pr[:, :, 1]
        return jnp.einsum('bhqs,bshe->bqhe', wgt.astype(v.dtype), v)

    o = lax.map(block, qb)
    o = jnp.moveaxis(o, 0, 1).reshape(B, Lq, DF_HEADS, DF_VD).astype(F32)
    o = o * lax.rsqrt(jnp.mean(o * o, -1, keepdims=True) + 1e-5) * subln_g * (1.0 - lam_init)
    return o.reshape(B, Lq, DF_W).astype(v.dtype)


def diff_branch(p, pc, need_ctx, lq1, lk1, lq2, lk2, subln_g, lam_init):
    q, k, v = split_at(p, DF_SIZES)
    qc, kc, vc = split_at(pc, DF_SIZES)

    def qk(t):
        return t.reshape(t.shape[0], t.shape[1], DF_HEADS, 2, DF_HD)

    def vs(t):
        return t.reshape(t.shape[0], t.shape[1], DF_HEADS, DF_VD)

    lam = (jnp.exp(jnp.sum(lq1.astype(F32) * lk1.astype(F32)))
           - jnp.exp(jnp.sum(lq2.astype(F32) * lk2.astype(F32))) + lam_init)
    k_all = jnp.concatenate([axial_rope(qk(k)), qk(kc)], axis=1)
    v_all = jnp.concatenate([vs(v), vs(vc)], axis=1)
    y = diff_attend(axial_rope(qk(q)), k_all, v_all, lam, lam_init, subln_g)
    yc = diff_attend(qk(qc), qk(kc), vs(vc), lam, lam_init, subln_g) if need_ctx else None
    return y, yc


def merge_branches(ys, p_gate, w_br, w_o):
    gates = jnp.split(p_gate, N_BRANCH, -1)
    acc = None
    for j in range(N_BRANCH):
        term = jax.nn.sigmoid(gates[j]) * (ys[j] @ w_br[j])
        acc = term if acc is None else acc + term
    return acc @ w_o


def conv_ffn(u, w_up, conv_w, conv_b, w_down):
    hdn = dwconv3(u @ w_up, conv_w, conv_b)
    a, b = jnp.split(hdn, 2, -1)
    return (jax.nn.silu(a) * b) @ w_down


def setup_inputs(seed: int = 0) -> dict:
    key = jax.random.key(seed)
    ks = iter(jax.random.split(key, 64))
    D = D_MODEL

    def nrm(shape, scale):
        return jax.random.normal(next(ks), shape, F32) * scale

    def near_one(shape):
        return 1.0 + nrm(shape, 0.02)

    return {
        'x': nrm((BATCH, SEQ, D), 1.0),
        'c': nrm((BATCH, D), 1.0),
        'ctx': nrm((BATCH, CTX_LEN, D), 1.0),
        'c_ctx': nrm((D,), 1.0),
        'ada_w': nrm((DEPTH, D, 6 * D), D ** -0.5),
        'ada_b': nrm((DEPTH, 6 * D), 0.02),
        'w_in': nrm((DEPTH, D, IN_COLS), D ** -0.5),
        'hy_conv_w': nrm((DEPTH, 3, 3 * HY_W), 3 ** -0.5),
        'hy_conv_b': nrm((DEPTH, 3 * HY_W), 0.02),
        'hy_f_w1': nrm((DEPTH, HY_EMB, HY_FFN), HY_EMB ** -0.5),
        'hy_f_b1': nrm((DEPTH, HY_FFN), 0.02),
        'hy_f_w2': nrm((DEPTH, HY_FFN, HY_FFN), HY_FFN ** -0.5),
        'hy_f_b2': nrm((DEPTH, HY_FFN), 0.02),
        'hy_f_w3': nrm((DEPTH, HY_FFN, HY_ORDER * HY_DIRS * HY_W), HY_FFN ** -0.5),
        'hy_f_freq': near_one((DEPTH, HY_FFN)),
        'hy_bias': nrm((DEPTH, HY_ORDER, HY_W), 1.0),
        'swa_sink': nrm((DEPTH, SWA_HEADS), 0.5),
        'rwkv_mu': jax.random.uniform(next(ks), (DEPTH, sum(RW_SIZES)), F32),
        'rwkv_w0': jnp.linspace(-6.5, -1.5, RW_W, dtype=F32)[None, None, :] + nrm((DEPTH, 2, RW_W), 0.1),
        'rwkv_w2': nrm((DEPTH, 2, RW_DECAY_R, RW_W), 0.5 * RW_DECAY_R ** -0.5),
        'rwkv_a0': nrm((DEPTH, RW_W), 0.1),
        'rwkv_a2': nrm((DEPTH, RW_AAA_R, RW_W), 0.5 * RW_AAA_R ** -0.5),
        'rwkv_g2': nrm((DEPTH, 2, RW_GATE_R, RW_W), RW_GATE_R ** -0.5),
        'rwkv_kk': 0.85 + nrm((DEPTH, RW_W), 0.02),
        'rwkv_ka': near_one((DEPTH, RW_W)),
        'rwkv_rk': nrm((DEPTH, RW_HEADS, RW_HD), 0.1),
        'rwkv_lnx_g': near_one((DEPTH, RW_W)),
        'rwkv_lnx_b': nrm((DEPTH, RW_W), 0.02),
        'diff_lq1': nrm((DEPTH, DF_HD), 0.1),
        'diff_lk1': nrm((DEPTH, DF_HD), 0.1),
        'diff_lq2': nrm((DEPTH, DF_HD), 0.1),
        'diff_lk2': nrm((DEPTH, DF_HD), 0.1),
        'diff_subln_g': near_one((DEPTH, DF_VD)),
        'w_branch': nrm((DEPTH, N_BRANCH, BR_W, D), DN_BETA * BR_W ** -0.5),
        'w_out': nrm((DEPTH, D, D), DN_BETA * D ** -0.5),
        'ln1_g': near_one((DEPTH, D)),
        'ln1_b': nrm((DEPTH, D), 0.02),
        'ffn_w_up': nrm((DEPTH, D, 2 * D_FF), D ** -0.5),
        'ffn_conv_w': nrm((DEPTH, 3, 2 * D_FF), 3 ** -0.5),
        'ffn_conv_b': nrm((DEPTH, 2 * D_FF), 0.02),
        'ffn_w_down': nrm((DEPTH, D_FF, D), DN_BETA * D_FF ** -0.5),
        'ln2_g': near_one((DEPTH, D)),
        'ln2_b': nrm((DEPTH, D), 0.02),
    }


def reference(x, c, ctx, c_ctx, ada_w, ada_b, w_in, hy_conv_w, hy_conv_b, hy_f_w1, hy_f_b1,
              hy_f_w2, hy_f_b2, hy_f_w3, hy_f_freq, hy_bias, swa_sink, rwkv_mu, rwkv_w0, rwkv_w2,
              rwkv_a0, rwkv_a2, rwkv_g2, rwkv_kk, rwkv_ka, rwkv_rk, rwkv_lnx_g, rwkv_lnx_b,
              diff_lq1, diff_lk1, diff_lq2, diff_lk2, diff_subln_g, w_branch, w_out, ln1_g, ln1_b,
              ffn_w_up, ffn_conv_w, ffn_conv_b, ffn_w_down, ln2_g, ln2_b):
    h, hc = x, ctx
    s_lat = jax.nn.silu(c)
    s_ctx = jax.nn.silu(c_ctx)
    for i in range(DEPTH):
        need_ctx = i < DEPTH - 1
        mod = (s_lat @ ada_w[i] + ada_b[i])[:, None, :]
        mod_c = s_ctx @ ada_w[i] + ada_b[i]
        sh1, sc1, g1, sh2, sc2, g2 = jnp.split(mod, 6, -1)
        csh1, csc1, cg1, csh2, csc2, cg2 = jnp.split(mod_c, 6, -1)

        u = modulate(h, sh1, sc1)
        uc = modulate(hc, csh1, csc1)
        p_hy, p_sw, p_rw, p_df, p_gt = split_at(u @ w_in[i], IN_SIZES)
        pc_hy, pc_sw, pc_rw, pc_df, pc_gt = split_at(uc @ w_in[i], IN_SIZES)

        y_hy, yc_hy = hyena_branch(p_hy, pc_hy, need_ctx, hy_conv_w[i], hy_conv_b[i], hy_f_w1[i],
                                   hy_f_b1[i], hy_f_w2[i], hy_f_b2[i], hy_f_w3[i], hy_f_freq[i],
                                   hy_bias[i])
        y_sw, yc_sw = swa_branch(p_sw, pc_sw, need_ctx, swa_sink[i])
        y_rw, yc_rw = rwkv_branch(p_rw, pc_rw, need_ctx, rwkv_mu[i], rwkv_w0[i], rwkv_w2[i],
                                  rwkv_a0[i], rwkv_a2[i], rwkv_g2[i], rwkv_kk[i], rwkv_ka[i],
                                  rwkv_rk[i], rwkv_lnx_g[i], rwkv_lnx_b[i])
        lam_init = 0.8 - 0.6 * math.exp(-0.3 * i)
        y_df, yc_df = diff_branch(p_df, pc_df, need_ctx, diff_lq1[i], diff_lk1[i], diff_lq2[i],
                                  diff_lk2[i], diff_subln_g[i], lam_init)

        mix = merge_branches((y_hy, y_sw, y_rw, y_df), p_gt, w_branch[i], w_out[i])
        h = layer_norm(DN_ALPHA * h + g1 * mix, ln1_g[i], ln1_b[i])
        f = conv_ffn(modulate(h, sh2, sc2), ffn_w_up[i], ffn_conv_w[i], ffn_conv_b[i], ffn_w_down[i])
        h = layer_norm(DN_ALPHA * h + g2 * f, ln2_g[i], ln2_b[i])

        if need_ctx:
            mix_c = merge_branches((yc_hy, yc_sw, yc_rw, yc_df), pc_gt, w_branch[i], w_out[i])
            hc = layer_norm(DN_ALPHA * hc + cg1 * mix_c, ln1_g[i], ln1_b[i])
            fc = conv_ffn(modulate(hc, csh2, csc2), ffn_w_up[i], ffn_conv_w[i], ffn_conv_b[i],
                          ffn_w_down[i])
            hc = layer_norm(DN_ALPHA * hc + cg2 * fc, ln2_g[i], ln2_b[i])
    return h
```

```python
import functools
import math

import jax
import jax.numpy as jnp
import numpy as np
from jax import lax
from jax.experimental import pallas as pl
from jax.experimental.pallas import tpu as pltpu

D_MODEL = 1024
BATCH = 8
SEQ = 4096
DEPTH = 2
CTX_LEN = 256
GRID_W = 64
ROPE_BASE = 10000.0
F32 = jnp.float32
BF16 = jnp.bfloat16
NEG_INF = -1e30
LN_EPS = 1e-6

HY_W = 256
HY_ORDER = 2
HY_DIRS = 2
HY_EMB = 33
HY_FFN = 64
HY_MIN_DECAY = math.log(1e-2) / 1.5
HY_MAX_DECAY = math.log(1e-2) / 0.3

SWA_HEADS = 4
SWA_KV = 2
SWA_HD = 64
SWA_WIN = 128
SWA_BLOCK = 128

RW_HEADS = 4
RW_HD = 64
RW_W = RW_HEADS * RW_HD
RW_DECAY_R = 64
RW_AAA_R = 64
RW_GATE_R = 128
RW_GN_EPS = 64e-5

DF_HEADS = 4
DF_HD = 32
DF_VD = 2 * DF_HD
DF_W = DF_HEADS * DF_VD
DF_BLOCK = 128

N_BRANCH = 4
BR_W = 256
D_FF = 2816
DN_ALPHA = (2 * DEPTH) ** 0.25
DN_BETA = (8 * DEPTH) ** -0.25

SWA_SIZES = (SWA_HEADS * SWA_HD, SWA_KV * SWA_HD, SWA_KV * SWA_HD)
RW_SIZES = (RW_W, RW_W, RW_W, RW_DECAY_R, RW_DECAY_R, RW_AAA_R, RW_GATE_R, RW_GATE_R)
DF_SIZES = (DF_W, DF_W, DF_W)
IN_SIZES = (3 * HY_W, sum(SWA_SIZES), sum(RW_SIZES), sum(DF_SIZES), N_BRANCH * D_MODEL)

LANES = 128
MM_TM = 512
MM_TN = 512
VMEM_LIMIT = 48 * 1024 * 1024


def _mm_kernel(x_ref, w_ref, o_ref):
    o_ref[...] = jnp.dot(x_ref[...].astype(BF16), w_ref[...], preferred_element_type=F32)


def _mm(x, w):
    m, k = x.shape
    n = w.shape[1]
    n_pad = -(-n // LANES) * LANES
    wb = w.astype(BF16)
    if n_pad != n:
        wb = jnp.pad(wb, ((0, 0), (0, n_pad - n)))
    tn = MM_TN if n_pad % MM_TN == 0 else (256 if n_pad % 256 == 0 else LANES)
    tm = MM_TM if m % MM_TM == 0 else m
    out = pl.pallas_call(
        _mm_kernel,
        out_shape=jax.ShapeDtypeStruct((m, n_pad), F32),
        grid=(m // tm, n_pad // tn),
        in_specs=[pl.BlockSpec((tm, k), lambda i, j: (i, 0)),
                  pl.BlockSpec((k, tn), lambda i, j: (0, j))],
        out_specs=pl.BlockSpec((tm, tn), lambda i, j: (i, j)),
        compiler_params=pltpu.CompilerParams(
            dimension_semantics=("parallel", "parallel"), vmem_limit_bytes=VMEM_LIMIT),
        name="matmul",
    )(x, wb)
    return out[:, :n] if n_pad != n else out


def _mm3(x, w):
    b, l, k = x.shape
    return _mm(x.reshape(b * l, k), w).reshape(b, l, w.shape[1])


def split_at(t, sizes):
    return jnp.split(t, np.cumsum(sizes)[:-1].tolist(), axis=-1)


def layer_norm(x, g=None, b=None):
    xf = x.astype(F32)
    mu = jnp.mean(xf, -1, keepdims=True)
    var = jnp.mean(jnp.square(xf - mu), -1, keepdims=True)
    y = (xf - mu) * lax.rsqrt(var + LN_EPS)
    if g is not None:
        y = y * g + b
    return y.astype(x.dtype)


def modulate(x, shift, scale):
    return layer_norm(x) * (1 + scale) + shift


def dwconv3(x, w, b):
    xp = jnp.pad(x, ((0, 0), (1, 1), (0, 0)))
    return xp[:, :-2] * w[0] + x * w[1] + xp[:, 2:] * w[2] + b


def axial_rope(x):
    L, d = x.shape[1], x.shape[-1]
    rows = L // GRID_W
    row = jnp.repeat(jnp.arange(rows), GRID_W)
    col = jnp.tile(jnp.arange(GRID_W), rows)
    nf = d // 4
    inv = ROPE_BASE ** (-jnp.arange(nf, dtype=F32) / nf)
    bshape = (L,) + (1,) * (x.ndim - 3) + (nf,)
    xf = x.astype(F32)
    out = []
    for half, pos in enumerate((row, col)):
        ang = (pos.astype(F32)[:, None] * inv[None, :]).reshape(bshape)
        cos, sin = jnp.cos(ang), jnp.sin(ang)
        xh = xf[..., half * 2 * nf:(half + 1) * 2 * nf]
        x1, x2 = xh[..., :nf], xh[..., nf:]
        out += [x1 * cos - x2 * sin, x1 * sin + x2 * cos]
    return jnp.concatenate(out, -1).astype(x.dtype)


def sink_softmax(logit_list, sink_b):
    lead = logit_list[0].shape[:-1]
    z = jnp.concatenate([t.astype(F32) for t in logit_list]
                        + [jnp.broadcast_to(sink_b.astype(F32), lead + (1,))], -1)
    return jax.nn.softmax(z, -1)[..., :-1]


def hyena_spectrum(L, w1, b1, w2, b2, w3, freq):
    t = jnp.linspace(0.0, 1.0, L, dtype=F32)[:, None]
    bands = (HY_EMB - 1) // 2
    w = 2.0 * math.pi * jnp.arange(L, dtype=F32) / L
    fb = jnp.linspace(1e-4, bands - 1, bands, dtype=F32)
    ang = w[:, None] * fb[None, :]
    z = jnp.concatenate([t, jnp.cos(ang), -jnp.sin(ang)], -1)
    fr = freq.astype(F32)
    hdn = jnp.sin(fr * (z @ w1.astype(F32) + b1.astype(F32)))
    hdn = jnp.sin(fr * (hdn @ w2.astype(F32) + b2.astype(F32)))
    filt = (hdn @ w3.astype(F32)).reshape(L, HY_ORDER, HY_DIRS, HY_W)
    deltas = jnp.abs(jnp.linspace(HY_MIN_DECAY, HY_MAX_DECAY, HY_W, dtype=F32))
    filt = filt * jnp.exp(-t * deltas[None, :])[:, None, None, :]
    filt = filt / jnp.sum(jnp.abs(filt), axis=(0, 2), keepdims=True)
    fwd, bwd = filt[:, :, 0], filt[:, :, 1]
    kern = jnp.concatenate([fwd, jnp.zeros_like(fwd[:1]), bwd[:0:-1]], axis=0)
    return jnp.fft.rfft(kern, axis=0)


def fft_longconv(z, kf, bias):
    L = z.shape[1]
    zf = jnp.fft.rfft(z.astype(F32), n=2 * L, axis=1)
    y = jnp.fft.irfft(zf * kf[None], n=2 * L, axis=1)[:, :L]
    return (y + z * bias).astype(z.dtype)


def hyena_mix(p, conv_w, conv_b, kf, bias):
    v, x1, x2 = jnp.split(dwconv3(p, conv_w, conv_b), 3, -1)
    zz = x1 * fft_longconv(v, kf[:, 0], bias[0])
    return x2 * fft_longconv(zz, kf[:, 1], bias[1])


def hyena_branch(p, pc, need_ctx, conv_w, conv_b, fw1, fb1, fw2, fb2, fw3, ffreq, bias):
    kf = hyena_spectrum(p.shape[1], fw1, fb1, fw2, fb2, fw3, ffreq)
    y = hyena_mix(p, conv_w, conv_b, kf, bias)
    yc = None
    if need_ctx:
        kfc = hyena_spectrum(pc.shape[1], fw1, fb1, fw2, fb2, fw3, ffreq)
        yc = hyena_mix(pc, conv_w, conv_b, kfc, bias)
    return y, yc


def swa_latent(q, k, v, kc, vc, sink):
    B, L = q.shape[:2]
    nb = L // SWA_BLOCK
    G = SWA_HEADS // SWA_KV
    scale = SWA_HD ** -0.5
    qb = q.reshape(B, nb, SWA_BLOCK, SWA_KV, G, SWA_HD)

    def band(t):
        tp = jnp.pad(t, ((0, 0), (SWA_BLOCK, SWA_BLOCK), (0, 0), (0, 0)))
        tp = tp.reshape(B, nb + 2, SWA_BLOCK, SWA_KV, SWA_HD)
        return jnp.concatenate([tp[:, :-2], tp[:, 1:-1], tp[:, 2:]], axis=2)

    kb, vb = band(k), band(v)
    s_loc = jnp.einsum('bnqkgd,bnskd->bnkgqs', qb, kb).astype(F32) * scale
    s_ctx = jnp.einsum('bnqkgd,bckd->bnkgqc', qb, kc).astype(F32) * scale
    blk = jnp.arange(nb)[:, None, None]
    qpos = blk * SWA_BLOCK + jnp.arange(SWA_BLOCK)[None, :, None]
    kpos = (blk - 1) * SWA_BLOCK + jnp.arange(3 * SWA_BLOCK)[None, None, :]
    valid = (jnp.abs(kpos - qpos) <= SWA_WIN) & (kpos >= 0) & (kpos < L)
    s_loc = jnp.where(valid[None, :, None, None], s_loc, NEG_INF)
    p = sink_softmax([s_loc, s_ctx], sink.reshape(SWA_KV, G)[:, :, None, None])
    nl = 3 * SWA_BLOCK
    o = (jnp.einsum('bnkgqs,bnskd->bnqkgd', p[..., :nl].astype(v.dtype), vb)
         + jnp.einsum('bnkgqc,bckd->bnqkgd', p[..., nl:].astype(v.dtype), vc))
    return o.reshape(B, L, SWA_HEADS * SWA_HD)


def swa_context(qc, kc, vc, sink):
    B, C = qc.shape[:2]
    G = SWA_HEADS // SWA_KV
    qg = qc.reshape(B, C, SWA_KV, G, SWA_HD)
    s = jnp.einsum('bqkgd,bckd->bkgqc', qg, kc).astype(F32) * SWA_HD ** -0.5
    p = sink_softmax([s], sink.reshape(SWA_KV, G)[:, :, None, None])
    o = jnp.einsum('bkgqc,bckd->bqkgd', p.astype(vc.dtype), vc)
    return o.reshape(B, C, SWA_HEADS * SWA_HD)


def swa_branch(p, pc, need_ctx, sink):
    B, L = p.shape[:2]
    C = pc.shape[1]
    q, k, v = split_at(p, SWA_SIZES)
    qc, kc, vc = split_at(pc, SWA_SIZES)
    q = axial_rope(q.reshape(B, L, SWA_HEADS, SWA_HD))
    k = axial_rope(k.reshape(B, L, SWA_KV, SWA_HD))
    v = v.reshape(B, L, SWA_KV, SWA_HD)
    kc = kc.reshape(B, C, SWA_KV, SWA_HD)
    vc = vc.reshape(B, C, SWA_KV, SWA_HD)
    y = swa_latent(q, k, v, kc, vc, sink)
    yc = swa_context(qc.reshape(B, C, SWA_HEADS, SWA_HD), kc, vc, sink) if need_ctx else None
    return y, yc


def rwkv_streams(p, mu, w0, w2, a0, a2, g2, k_k, k_a):
    B, L = p.shape[:2]
    pp = jnp.pad(p, ((0, 0), (1, 1), (0, 0)))
    p = p + (0.5 * (pp[:, :-2] + pp[:, 2:]) - p) * mu
    r, k, v, wd_f, wd_b, ad, gd_f, gd_b = split_at(p, RW_SIZES)

    def heads(t):
        return t.reshape(B, L, RW_HEADS, RW_HD)

    def decay(wd, w0d, w2d):
        wlog = -jax.nn.softplus(-(w0d + jnp.tanh(wd) @ w2d)) - 0.5
        return jnp.exp(-jnp.exp(wlog.astype(F32)))

    a = jax.nn.sigmoid(a0 + ad @ a2)
    kk = heads(k * k_k).astype(F32)
    kk = kk / jnp.maximum(jnp.sqrt(jnp.sum(kk * kk, -1, keepdims=True)), 1e-12)
    k = k * (1 + (a - 1) * k_a)
    g_f = jax.nn.sigmoid(gd_f) @ g2[0]
    g_b = jax.nn.sigmoid(gd_b) @ g2[1]
    return (heads(r), heads(k), heads(v), kk, heads(a),
            heads(decay(wd_f, w0[0], w2[0])), heads(decay(wd_b, w0[1], w2[1])),
            heads(g_f), heads(g_b))


def rwkv_scan(state0, r, dec, k, v, kk, a, reverse, emit):
    xs = tuple(jnp.moveaxis(t.astype(F32), 1, 0) for t in (r, dec, k, v, kk, a))

    def step(S, inp):
        r_t, w_t, k_t, v_t, kk_t, a_t = inp
        sa = jnp.einsum('bhvk,bhk->bhv', S, -kk_t)
        S = (S * w_t[:, :, None, :] + sa[..., None] * (kk_t * a_t)[:, :, None, :]
             + v_t[..., None] * k_t[:, :, None, :])
        return S, (jnp.einsum('bhvk,bhk->bhv', S, r_t) if emit else None)

    S, out = lax.scan(step, state0, xs, reverse=reverse)
    return S, (jnp.moveaxis(out, 0, 1) if emit else None)


def rwkv_out(o_f, o_b, r, k, v, g_f, g_b, r_k, lnx_g, lnx_b):
    B, L = r.shape[:2]
    gam = lnx_g.reshape(RW_HEADS, RW_HD)
    bet = lnx_b.reshape(RW_HEADS, RW_HD)

    def gn(o):
        mu = jnp.mean(o, -1, keepdims=True)
        var = jnp.mean(jnp.square(o - mu), -1, keepdims=True)
        return (o - mu) * lax.rsqrt(var + RW_GN_EPS) * gam + bet

    bonus = jnp.sum(r * k * r_k, -1, keepdims=True) * v
    y = (gn(o_f) + bonus) * g_f + (gn(o_b) + bonus) * g_b
    return y.reshape(B, L, RW_W).astype(v.dtype)


def rwkv_branch(p, pc, need_ctx, mu, w0, w2, a0, a2, g2, k_k, k_a, r_k, lnx_g, lnx_b):
    r, k, v, kk, a, d_f, d_b, g_f, g_b = rwkv_streams(p, mu, w0, w2, a0, a2, g2, k_k, k_a)
    rc, kc, vc, kkc, ac, dc_f, dc_b, gc_f, gc_b = rwkv_streams(pc, mu, w0, w2, a0, a2, g2, k_k, k_a)
    S0 = jnp.zeros((p.shape[0], RW_HEADS, RW_HD, RW_HD), F32)
    Sf, oc_f = rwkv_scan(S0, rc, dc_f, kc, vc, kkc, ac, False, need_ctx)
    Sb, oc_b = rwkv_scan(S0, rc, dc_b, kc, vc, kkc, ac, True, need_ctx)
    _, o_f = rwkv_scan(Sf, r, d_f, k, v, kk, a, False, True)
    _, o_b = rwkv_scan(Sb, r, d_b, k, v, kk, a, True, True)
    y = rwkv_out(o_f, o_b, r, k, v, g_f, g_b, r_k, lnx_g, lnx_b)
    yc = rwkv_out(oc_f, oc_b, rc, kc, vc, gc_f, gc_b, r_k, lnx_g, lnx_b) if need_ctx else None
    return y, yc


def diff_attend(q, k, v, lam, lam_init, subln_g):
    B, Lq = q.shape[:2]
    nb = Lq // DF_BLOCK
    qb = jnp.moveaxis(q.reshape(B, nb, DF_BLOCK, DF_HEADS, 2, DF_HD), 1, 0)
    scale = DF_HD ** -0.5

    def block(qblk):
        s = jnp.einsum('bqhcd,bshcd->bhcqs', qblk, k).astype(F32) * scale
        pr = jax.nn.softmax(s, -1)
        wgt = pr[:, :, 0] - lam * pr[:, :, 1]
        return jnp.einsum('bhqs,bshe->bqhe', wgt.astype(v.dtype), v)

    o = lax.map(block, qb)
    o = jnp.moveaxis(o, 0, 1).reshape(B, Lq, DF_HEADS, DF_VD).astype(F32)
    o = o * lax.rsqrt(jnp.mean(o * o, -1, keepdims=True) + 1e-5) * subln_g * (1.0 - lam_init)
    return o.reshape(B, Lq, DF_W).astype(v.dtype)


def diff_branch(p, pc, need_ctx, lq1, lk1, lq2, lk2, subln_g, lam_init):
    q, k, v = split_at(p, DF_SIZES)
    qc, kc, vc = split_at(pc, DF_SIZES)

    def qk(t):
        return t.reshape(t.shape[0], t.shape[1], DF_HEADS, 2, DF_HD)

    def vs(t):
        return t.reshape(t.shape[0], t.shape[1], DF_HEADS, DF_VD)

    lam = (jnp.exp(jnp.sum(lq1.astype(F32) * lk1.astype(F32)))
           - jnp.exp(jnp.sum(lq2.astype(F32) * lk2.astype(F32))) + lam_init)
    k_all = jnp.concatenate([axial_rope(qk(k)), qk(kc)], axis=1)
    v_all = jnp.concatenate([vs(v), vs(vc)], axis=1)
    y = diff_attend(axial_rope(qk(q)), k_all, v_all, lam, lam_init, subln_g)
    yc = diff_attend(qk(qc), qk(kc), vs(vc), lam, lam_init, subln_g) if need_ctx else None
    return y, yc


def merge_branches(ys, p_gate, w_br, w_o):
    gates = jnp.split(p_gate, N_BRANCH, -1)
    acc = None
    for j in range(N_BRANCH):
        term = jax.nn.sigmoid(gates[j]) * _mm3(ys[j], w_br[j])
        acc = term if acc is None else acc + term
    return _mm3(acc, w_o)


def conv_ffn(u, w_up, conv_w, conv_b, w_down):
    hdn = dwconv3(_mm3(u, w_up), conv_w, conv_b)
    a, b = jnp.split(hdn, 2, -1)
    return _mm3(jax.nn.silu(a) * b, w_down)


def _in_proj(u, w):
    offs = np.cumsum((0,) + IN_SIZES)
    return [_mm3(u, w[:, offs[j]:offs[j + 1]]) for j in range(len(IN_SIZES))]


def kernel(x, c, ctx, c_ctx, ada_w, ada_b, w_in, hy_conv_w, hy_conv_b, hy_f_w1, hy_f_b1,
           hy_f_w2, hy_f_b2, hy_f_w3, hy_f_freq, hy_bias, swa_sink, rwkv_mu, rwkv_w0, rwkv_w2,
           rwkv_a0, rwkv_a2, rwkv_g2, rwkv_kk, rwkv_ka, rwkv_rk, rwkv_lnx_g, rwkv_lnx_b,
           diff_lq1, diff_lk1, diff_lq2, diff_lk2, diff_subln_g, w_branch, w_out, ln1_g, ln1_b,
           ffn_w_up, ffn_conv_w, ffn_conv_b, ffn_w_down, ln2_g, ln2_b):
    h, hc = x, ctx
    s_lat = jax.nn.silu(c)
    s_ctx = jax.nn.silu(c_ctx)
    for i in range(DEPTH):
        need_ctx = i < DEPTH - 1
        mod = (s_lat @ ada_w[i] + ada_b[i])[:, None, :]
        mod_c = s_ctx @ ada_w[i] + ada_b[i]
        sh1, sc1, g1, sh2, sc2, g2 = jnp.split(mod, 6, -1)
        csh1, csc1, cg1, csh2, csc2, cg2 = jnp.split(mod_c, 6, -1)

        u = modulate(h, sh1, sc1)
        uc = modulate(hc, csh1, csc1)
        p_hy, p_sw, p_rw, p_df, p_gt = _in_proj(u, w_in[i])
        pc_hy, pc_sw, pc_rw, pc_df, pc_gt = _in_proj(uc, w_in[i])

        y_hy, yc_hy = hyena_branch(p_hy, pc_hy, need_ctx, hy_conv_w[i], hy_conv_b[i], hy_f_w1[i],
                                   hy_f_b1[i], hy_f_w2[i], hy_f_b2[i], hy_f_w3[i], hy_f_freq[i],
                                   hy_bias[i])
        y_sw, yc_sw = swa_branch(p_sw, pc_sw, need_ctx, swa_sink[i])
        y_rw, yc_rw = rwkv_branch(p_rw, pc_rw, need_ctx, rwkv_mu[i], rwkv_w0[i], rwkv_w2[i],
                                  rwkv_a0[i], rwkv_a2[i], rwkv_g2[i], rwkv_kk[i], rwkv_ka[i],
                                  rwkv_rk[i], rwkv_lnx_g[i], rwkv_lnx_b[i])
        lam_init = 0.8 - 0.6 * math.exp(-0.3 * i)
        y_df, yc_df = diff_branch(p_df, pc_df, need_ctx, diff_lq1[i], diff_lk1[i], diff_lq2[i],
                                  diff_lk2[i], diff_subln_g[i], lam_init)

        mix = merge_branches((y_hy, y_sw, y_rw, y_df), p_gt, w_branch[i], w_out[i])
        h = layer_norm(DN_ALPHA * h + g1 * mix, ln1_g[i], ln1_b[i])
        f = conv_ffn(modulate(h, sh2, sc2), ffn_w_up[i], ffn_conv_w[i], ffn_conv_b[i], ffn_w_down[i])
        h = layer_norm(DN_ALPHA * h + g2 * f, ln2_g[i], ln2_b[i])

        if need_ctx:
            mix_c = merge_branches((yc_hy, yc_sw, yc_rw, yc_df), pc_gt, w_branch[i], w_out[i])
            hc = layer_norm(DN_ALPHA * hc + cg1 * mix_c, ln1_g[i], ln1_b[i])
            fc = conv_ffn(modulate(hc, csh2, csc2), ffn_w_up[i], ffn_conv_w[i], ffn_conv_b[i],
                          ffn_w_down[i])
            hc = layer_norm(DN_ALPHA * hc + cg2 * fc, ln2_g[i], ln2_b[i])
    return h
```

```python
import functools
import math

import jax
import jax.numpy as jnp
import numpy as np
from jax import lax
from jax.experimental import pallas as pl
from jax.experimental.pallas import tpu as pltpu

D_MODEL = 1024
BATCH = 8
SEQ = 4096
DEPTH = 2
CTX_LEN = 256
GRID_W = 64
ROPE_BASE = 10000.0
F32 = jnp.float32
BF16 = jnp.bfloat16
NEG_INF = -1e30
LN_EPS = 1e-6

HY_W = 256
HY_ORDER = 2
HY_DIRS = 2
HY_EMB = 33
HY_FFN = 64
HY_MIN_DECAY = math.log(1e-2) / 1.5
HY_MAX_DECAY = math.log(1e-2) / 0.3

SWA_HEADS = 4
SWA_KV = 2
SWA_HD = 64
SWA_WIN = 128
SWA_BLOCK = 128

RW_HEADS = 4
RW_HD = 64
RW_W = RW_HEADS * RW_HD
RW_DECAY_R = 64
RW_AAA_R = 64
RW_GATE_R = 128
RW_GN_EPS = 64e-5

DF_HEADS = 4
DF_HD = 32
DF_VD = 2 * DF_HD
DF_W = DF_HEADS * DF_VD
DF_BLOCK = 128

N_BRANCH = 4
BR_W = 256
D_FF = 2816
DN_ALPHA = (2 * DEPTH) ** 0.25
DN_BETA = (8 * DEPTH) ** -0.25

SWA_SIZES = (SWA_HEADS * SWA_HD, SWA_KV * SWA_HD, SWA_KV * SWA_HD)
RW_SIZES = (RW_W, RW_W, RW_W, RW_DECAY_R, RW_DECAY_R, RW_AAA_R, RW_GATE_R, RW_GATE_R)
DF_SIZES = (DF_W, DF_W, DF_W)
IN_SIZES = (3 * HY_W, sum(SWA_SIZES), sum(RW_SIZES), sum(DF_SIZES), N_BRANCH * D_MODEL)

LANES = 128
MM_TM = 512
MM_TN = 512
RW_TBLK = 64
DF_TQ = 256
VMEM_LIMIT = 48 * 1024 * 1024


def _mm_kernel(x_ref, w_ref, o_ref):
    o_ref[...] = jnp.dot(x_ref[...].astype(BF16), w_ref[...], preferred_element_type=F32)


def _mm(x, w):
    m, k = x.shape
    n = w.shape[1]
    n_pad = -(-n // LANES) * LANES
    wb = w.astype(BF16)
    if n_pad != n:
        wb = jnp.pad(wb, ((0, 0), (0, n_pad - n)))
    tn = MM_TN if n_pad % MM_TN == 0 else (256 if n_pad % 256 == 0 else LANES)
    tm = MM_TM if m % MM_TM == 0 else m
    out = pl.pallas_call(
        _mm_kernel,
        out_shape=jax.ShapeDtypeStruct((m, n_pad), F32),
        grid=(m // tm, n_pad // tn),
        in_specs=[pl.BlockSpec((tm, k), lambda i, j: (i, 0)),
                  pl.BlockSpec((k, tn), lambda i, j: (0, j))],
        out_specs=pl.BlockSpec((tm, tn), lambda i, j: (i, j)),
        compiler_params=pltpu.CompilerParams(
            dimension_semantics=("parallel", "parallel"), vmem_limit_bytes=VMEM_LIMIT),
        name="matmul",
    )(x, wb)
    return out[:, :n] if n_pad != n else out


def _mm3(x, w):
    b, l, k = x.shape
    return _mm(x.reshape(b * l, k), w).reshape(b, l, w.shape[1])


def split_at(t, sizes):
    return jnp.split(t, np.cumsum(sizes)[:-1].tolist(), axis=-1)


def layer_norm(x, g=None, b=None):
    xf = x.astype(F32)
    mu = jnp.mean(xf, -1, keepdims=True)
    var = jnp.mean(jnp.square(xf - mu), -1, keepdims=True)
    y = (xf - mu) * lax.rsqrt(var + LN_EPS)
    if g is not None:
        y = y * g + b
    return y.astype(x.dtype)


def modulate(x, shift, scale):
    return layer_norm(x) * (1 + scale) + shift


def dwconv3(x, w, b):
    xp = jnp.pad(x, ((0, 0), (1, 1), (0, 0)))
    return xp[:, :-2] * w[0] + x * w[1] + xp[:, 2:] * w[2] + b


def axial_rope(x):
    L, d = x.shape[1], x.shape[-1]
    rows = L // GRID_W
    row = jnp.repeat(jnp.arange(rows), GRID_W)
    col = jnp.tile(jnp.arange(GRID_W), rows)
    nf = d // 4
    inv = ROPE_BASE ** (-jnp.arange(nf, dtype=F32) / nf)
    bshape = (L,) + (1,) * (x.ndim - 3) + (nf,)
    xf = x.astype(F32)
    out = []
    for half, pos in enumerate((row, col)):
        ang = (pos.astype(F32)[:, None] * inv[None, :]).reshape(bshape)
        cos, sin = jnp.cos(ang), jnp.sin(ang)
        xh = xf[..., half * 2 * nf:(half + 1) * 2 * nf]
        x1, x2 = xh[..., :nf], xh[..., nf:]
        out += [x1 * cos - x2 * sin, x1 * sin + x2 * cos]
    return jnp.concatenate(out, -1).astype(x.dtype)


def sink_softmax(logit_list, sink_b):
    lead = logit_list[0].shape[:-1]
    z = jnp.concatenate([t.astype(F32) for t in logit_list]
                        + [jnp.broadcast_to(sink_b.astype(F32), lead + (1,))], -1)
    return jax.nn.softmax(z, -1)[..., :-1]


def hyena_spectrum(L, w1, b1, w2, b2, w3, freq):
    t = jnp.linspace(0.0, 1.0, L, dtype=F32)[:, None]
    bands = (HY_EMB - 1) // 2
    w = 2.0 * math.pi * jnp.arange(L, dtype=F32) / L
    fb = jnp.linspace(1e-4, bands - 1, bands, dtype=F32)
    ang = w[:, None] * fb[None, :]
    z = jnp.concatenate([t, jnp.cos(ang), -jnp.sin(ang)], -1)
    fr = freq.astype(F32)
    hdn = jnp.sin(fr * (z @ w1.astype(F32) + b1.astype(F32)))
    hdn = jnp.sin(fr * (hdn @ w2.astype(F32) + b2.astype(F32)))
    filt = (hdn @ w3.astype(F32)).reshape(L, HY_ORDER, HY_DIRS, HY_W)
    deltas = jnp.abs(jnp.linspace(HY_MIN_DECAY, HY_MAX_DECAY, HY_W, dtype=F32))
    filt = filt * jnp.exp(-t * deltas[None, :])[:, None, None, :]
    filt = filt / jnp.sum(jnp.abs(filt), axis=(0, 2), keepdims=True)
    fwd, bwd = filt[:, :, 0], filt[:, :, 1]
    kern = jnp.concatenate([fwd, jnp.zeros_like(fwd[:1]), bwd[:0:-1]], axis=0)
    return jnp.fft.rfft(kern, axis=0)


def fft_longconv(z, kf, bias):
    L = z.shape[1]
    zf = jnp.fft.rfft(z.astype(F32), n=2 * L, axis=1)
    y = jnp.fft.irfft(zf * kf[None], n=2 * L, axis=1)[:, :L]
    return (y + z * bias).astype(z.dtype)


def hyena_mix(p, conv_w, conv_b, kf, bias):
    v, x1, x2 = jnp.split(dwconv3(p, conv_w, conv_b), 3, -1)
    zz = x1 * fft_longconv(v, kf[:, 0], bias[0])
    return x2 * fft_longconv(zz, kf[:, 1], bias[1])


def hyena_branch(p, pc, need_ctx, conv_w, conv_b, fw1, fb1, fw2, fb2, fw3, ffreq, bias):
    kf = hyena_spectrum(p.shape[1], fw1, fb1, fw2, fb2, fw3, ffreq)
    y = hyena_mix(p, conv_w, conv_b, kf, bias)
    yc = None
    if need_ctx:
        kfc = hyena_spectrum(pc.shape[1], fw1, fb1, fw2, fb2, fw3, ffreq)
        yc = hyena_mix(pc, conv_w, conv_b, kfc, bias)
    return y, yc


def swa_latent(q, k, v, kc, vc, sink):
    B, L = q.shape[:2]
    nb = L // SWA_BLOCK
    G = SWA_HEADS // SWA_KV
    scale = SWA_HD ** -0.5
    qb = q.reshape(B, nb, SWA_BLOCK, SWA_KV, G, SWA_HD)

    def band(t):
        tp = jnp.pad(t, ((0, 0), (SWA_BLOCK, SWA_BLOCK), (0, 0), (0, 0)))
        tp = tp.reshape(B, nb + 2, SWA_BLOCK, SWA_KV, SWA_HD)
        return jnp.concatenate([tp[:, :-2], tp[:, 1:-1], tp[:, 2:]], axis=2)

    kb, vb = band(k), band(v)
    s_loc = jnp.einsum('bnqkgd,bnskd->bnkgqs', qb, kb).astype(F32) * scale
    s_ctx = jnp.einsum('bnqkgd,bckd->bnkgqc', qb, kc).astype(F32) * scale
    blk = jnp.arange(nb)[:, None, None]
    qpos = blk * SWA_BLOCK + jnp.arange(SWA_BLOCK)[None, :, None]
    kpos = (blk - 1) * SWA_BLOCK + jnp.arange(3 * SWA_BLOCK)[None, None, :]
    valid = (jnp.abs(kpos - qpos) <= SWA_WIN) & (kpos >= 0) & (kpos < L)
    s_loc = jnp.where(valid[None, :, None, None], s_loc, NEG_INF)
    p = sink_softmax([s_loc, s_ctx], sink.reshape(SWA_KV, G)[:, :, None, None])
    nl = 3 * SWA_BLOCK
    o = (jnp.einsum('bnkgqs,bnskd->bnqkgd', p[..., :nl].astype(v.dtype), vb)
         + jnp.einsum('bnkgqc,bckd->bnqkgd', p[..., nl:].astype(v.dtype), vc))
    return o.reshape(B, L, SWA_HEADS * SWA_HD)


def swa_context(qc, kc, vc, sink):
    B, C = qc.shape[:2]
    G = SWA_HEADS // SWA_KV
    qg = qc.reshape(B, C, SWA_KV, G, SWA_HD)
    s = jnp.einsum('bqkgd,bckd->bkgqc', qg, kc).astype(F32) * SWA_HD ** -0.5
    p = sink_softmax([s], sink.reshape(SWA_KV, G)[:, :, None, None])
    o = jnp.einsum('bkgqc,bckd->bqkgd', p.astype(vc.dtype), vc)
    return o.reshape(B, C, SWA_HEADS * SWA_HD)


def swa_branch(p, pc, need_ctx, sink):
    B, L = p.shape[:2]
    C = pc.shape[1]
    q, k, v = split_at(p, SWA_SIZES)
    qc, kc, vc = split_at(pc, SWA_SIZES)
    q = axial_rope(q.reshape(B, L, SWA_HEADS, SWA_HD))
    k = axial_rope(k.reshape(B, L, SWA_KV, SWA_HD))
    v = v.reshape(B, L, SWA_KV, SWA_HD)
    kc = kc.reshape(B, C, SWA_KV, SWA_HD)
    vc = vc.reshape(B, C, SWA_KV, SWA_HD)
    y = swa_latent(q, k, v, kc, vc, sink)
    yc = swa_context(qc.reshape(B, C, SWA_HEADS, SWA_HD), kc, vc, sink) if need_ctx else None
    return y, yc


def rwkv_streams(p, mu, w0, w2, a0, a2, g2, k_k, k_a):
    B, L = p.shape[:2]
    pp = jnp.pad(p, ((0, 0), (1, 1), (0, 0)))
    p = p + (0.5 * (pp[:, :-2] + pp[:, 2:]) - p) * mu
    r, k, v, wd_f, wd_b, ad, gd_f, gd_b = split_at(p, RW_SIZES)

    def heads(t):
        return t.reshape(B, L, RW_HEADS, RW_HD)

    def decay(wd, w0d, w2d):
        wlog = -jax.nn.softplus(-(w0d + jnp.tanh(wd) @ w2d)) - 0.5
        return jnp.exp(-jnp.exp(wlog.astype(F32)))

    a = jax.nn.sigmoid(a0 + ad @ a2)
    kk = heads(k * k_k).astype(F32)
    kk = kk / jnp.maximum(jnp.sqrt(jnp.sum(kk * kk, -1, keepdims=True)), 1e-12)
    k = k * (1 + (a - 1) * k_a)
    g_f = jax.nn.sigmoid(gd_f) @ g2[0]
    g_b = jax.nn.sigmoid(gd_b) @ g2[1]
    return (heads(r), heads(k), heads(v), kk, heads(a),
            heads(decay(wd_f, w0[0], w2[0])), heads(decay(wd_b, w0[1], w2[1])),
            heads(g_f), heads(g_b))


def _rwkv_scan_kernel(tblk, nb,
                      wf_ref, kkf_ref, bf_ref, kf_ref, vf_ref, rf_ref,
                      wb_ref, kkb_ref, bb_ref, kb_ref, vb_ref, rb_ref,
                      of_ref, ob_ref, sf_ref, sb_ref):
    c = pl.program_id(0)

    @pl.when(c == 0)
    def _():
        sf_ref[...] = jnp.zeros_like(sf_ref)
        sb_ref[...] = jnp.zeros_like(sb_ref)

    row = lax.broadcasted_iota(jnp.int32, (RW_W, RW_W), 0)
    col = lax.broadcasted_iota(jnp.int32, (RW_W, RW_W), 1)
    ones_blk = jnp.where(row // RW_HD == col // RW_HD, 1.0, 0.0).astype(BF16)
    r64 = lax.broadcasted_iota(jnp.int32, (RW_HD, RW_W), 0)
    c64 = lax.broadcasted_iota(jnp.int32, (RW_HD, RW_W), 1)
    eye_t = jnp.where(r64 == c64 % RW_HD, 1.0, 0.0).astype(F32)
    r8 = lax.broadcasted_iota(jnp.int32, (8, RW_W), 0)
    c8 = lax.broadcasted_iota(jnp.int32, (8, RW_W), 1)
    head_sel = jnp.where(r8 == c8 // RW_HD, 1.0, 0.0).astype(BF16)

    fwd = (wf_ref, kkf_ref, bf_ref, kf_ref, vf_ref, rf_ref, sf_ref)
    bwd = (wb_ref, kkb_ref, bb_ref, kb_ref, vb_ref, rb_ref, sb_ref)
    nrow = nb * RW_HD

    def reduce_operand(idx, refs):
        w_ref, kk_ref, b_ref, k_ref, v_ref, r_ref, s_ref = refs
        p1 = [(s_ref[b] * kk_ref[idx, b:b + 1, :]).astype(BF16) for b in range(nb)]
        vm = [(eye_t * v_ref[idx, b:b + 1, :]).astype(BF16) for b in range(nb)]
        return jnp.concatenate(p1 + vm, axis=0)

    def update(idx, refs, red):
        w_ref, kk_ref, b_ref, k_ref, v_ref, r_ref, s_ref = refs
        zs = []
        for b in range(nb):
            sa = red[b * RW_HD:(b + 1) * RW_HD]
            vc = red[nrow + b * RW_HD:nrow + (b + 1) * RW_HD]
            s_new = (s_ref[b] * w_ref[idx, b:b + 1, :] - sa * b_ref[idx, b:b + 1, :]
                     + vc * k_ref[idx, b:b + 1, :])
            s_ref[b] = s_new
            zs.append((s_new * r_ref[idx, b:b + 1, :]).astype(BF16))
        return jnp.concatenate(zs, axis=0)

    def emit(idx, z, o_ref):
        o_ref[idx] = lax.dot_general(head_sel, z, (((1,), (1,)), ((), ())),
                                     preferred_element_type=F32)

    def step(i, carry):
        lhs_f, z_f, z_b = carry
        ib = tblk - 1 - i
        red_f = jnp.dot(lhs_f, ones_blk, preferred_element_type=F32)
        red_b = jnp.dot(reduce_operand(ib, bwd), ones_blk, preferred_element_type=F32)
        emit(jnp.maximum(i - 1, 0), z_f, of_ref)
        emit(jnp.minimum(ib + 1, tblk - 1), z_b, ob_ref)
        z_f = update(i, fwd, red_f)
        lhs_f = reduce_operand(jnp.minimum(i + 1, tblk - 1), fwd)
        z_b = update(ib, bwd, red_b)
        return lhs_f, z_f, z_b

    z0 = jnp.zeros((nrow, RW_W), BF16)
    _, z_f, z_b = lax.fori_loop(0, tblk, step, (reduce_operand(0, fwd), z0, z0))
    emit(tblk - 1, z_f, of_ref)
    emit(0, z_b, ob_ref)


def rwkv_scan_pallas(w_f, w_b, kk, bvec, k, v, r, n_ctx):
    n, nb, _ = kk.shape
    tblk = RW_TBLK
    nblk = n // tblk
    nblk_ctx = n_ctx // tblk

    def fmap(c):
        return (c, 0, 0)

    def bmap(c):
        return (jnp.where(c < nblk_ctx, nblk_ctx - 1 - c, nblk + nblk_ctx - 1 - c), 0, 0)

    blk = (tblk, nb, RW_W)
    oblk = (tblk, 8, nb * RW_HD)
    return pl.pallas_call(
        functools.partial(_rwkv_scan_kernel, tblk, nb),
        out_shape=[jax.ShapeDtypeStruct((n, 8, nb * RW_HD), F32)] * 2,
        grid=(nblk,),
        in_specs=[pl.BlockSpec(blk, fmap)] * 6 + [pl.BlockSpec(blk, bmap)] * 6,
        out_specs=[pl.BlockSpec(oblk, fmap), pl.BlockSpec(oblk, bmap)],
        scratch_shapes=[pltpu.VMEM((nb, RW_HD, RW_W), F32), pltpu.VMEM((nb, RW_HD, RW_W), F32)],
        compiler_params=pltpu.CompilerParams(dimension_semantics=("arbitrary",),
                                             vmem_limit_bytes=VMEM_LIMIT),
        name="rwkv_scan",
    )(w_f, kk, bvec, k, v, r, w_b, kk, bvec, k, v, r)


def rwkv_out(o_f, o_b, r, k, v, g_f, g_b, r_k, lnx_g, lnx_b):
    B, L = r.shape[:2]
    gam = lnx_g.reshape(RW_HEADS, RW_HD)
    bet = lnx_b.reshape(RW_HEADS, RW_HD)

    def gn(o):
        mu = jnp.mean(o, -1, keepdims=True)
        var = jnp.mean(jnp.square(o - mu), -1, keepdims=True)
        return (o - mu) * lax.rsqrt(var + RW_GN_EPS) * gam + bet

    bonus = jnp.sum(r * k * r_k, -1, keepdims=True) * v
    y = (gn(o_f) + bonus) * g_f + (gn(o_b) + bonus) * g_b
    return y.reshape(B, L, RW_W).astype(v.dtype)


def rwkv_branch(p, pc, need_ctx, mu, w0, w2, a0, a2, g2, k_k, k_a, r_k, lnx_g, lnx_b):
    r, k, v, kk, a, d_f, d_b, g_f, g_b = rwkv_streams(p, mu, w0, w2, a0, a2, g2, k_k, k_a)
    rc, kc, vc, kkc, ac, dc_f, dc_b, gc_f, gc_b = rwkv_streams(pc, mu, w0, w2, a0, a2, g2, k_k, k_a)
    B, L = p.shape[:2]
    C = pc.shape[1]

    def tmajor(xc, xl):
        t = jnp.concatenate([xc, xl], axis=1).astype(F32).reshape(B, C + L, RW_W)
        return jnp.moveaxis(t, 1, 0)

    pf, pb = rwkv_scan_pallas(tmajor(dc_f, d_f), tmajor(dc_b, d_b), tmajor(kkc, kk),
                              tmajor(kkc * ac, kk * a), tmajor(kc, k), tmajor(vc, v),
                              tmajor(rc, r), C)

    def bmajor(t):
        t = t[:, :RW_HEADS].reshape(C + L, RW_HEADS, B, RW_HD)
        return jnp.transpose(t, (2, 0, 1, 3))

    pf, pb = bmajor(pf), bmajor(pb)
    oc_f, o_f = pf[:, :C], pf[:, C:]
    oc_b, o_b = pb[:, :C], pb[:, C:]
    y = rwkv_out(o_f, o_b, r, k, v, g_f, g_b, r_k, lnx_g, lnx_b)
    yc = rwkv_out(oc_f, oc_b, rc, kc, vc, gc_f, gc_b, r_k, lnx_g, lnx_b) if need_ctx else None
    return y, yc


def _diff_attn_kernel(q_ref, kt_ref, v_ref, lam_ref, gain_ref, o_ref):
    q = q_ref[0]
    v = v_ref[0]
    lam = lam_ref[...]
    tq = q.shape[0]
    head_of_lane = lax.broadcasted_iota(jnp.int32, (1, DF_W), 1) // DF_VD
    acc = jnp.zeros((tq, DF_W), F32)
    for h in range(DF_HEADS):
        parts = []
        for c in range(2):
            j = 2 * h + c
            s = jnp.dot(q[:, DF_HD * j:DF_HD * (j + 1)], kt_ref[0, j],
                        preferred_element_type=F32)
            p = jnp.exp2(s - jnp.max(s, -1, keepdims=True))
            l = jnp.sum(p, -1, keepdims=True)
            pv = jnp.dot(p.astype(BF16), v, preferred_element_type=F32)
            parts.append(pv / l)
        acc = jnp.where(head_of_lane == h, parts[0] - lam * parts[1], acc)
    row = lax.broadcasted_iota(jnp.int32, (DF_W, DF_W), 0)
    col = lax.broadcasted_iota(jnp.int32, (DF_W, DF_W), 1)
    seg_mean = jnp.where(row // DF_VD == col // DF_VD, 1.0 / DF_VD, 0.0).astype(F32)
    ms = jnp.dot(acc * acc, seg_mean, preferred_element_type=F32, precision=lax.Precision.HIGHEST)
    o_ref[0] = acc * lax.rsqrt(ms + 1e-5) * gain_ref[...]


def diff_attend(q, k, v, lam, lam_init, subln_g):
    B, Lq = q.shape[:2]
    S = k.shape[1]
    tq = min(DF_TQ, Lq)
    qs = (q.astype(F32) * (DF_HD ** -0.5 * math.log2(math.e))).reshape(B, Lq, DF_W).astype(BF16)
    kt = jnp.transpose(k.reshape(B, S, 2 * DF_HEADS, DF_HD), (0, 2, 3, 1)).astype(BF16)
    vb = v.reshape(B, S, DF_W).astype(BF16)
    gain = jnp.tile(subln_g.astype(F32) * (1.0 - lam_init), DF_HEADS).reshape(1, DF_W)
    return pl.pallas_call(
        _diff_attn_kernel,
        out_shape=jax.ShapeDtypeStruct((B, Lq, DF_W), F32),
        grid=(B, Lq // tq),
        in_specs=[pl.BlockSpec((1, tq, DF_W), lambda b, i: (b, i, 0)),
                  pl.BlockSpec((1, 2 * DF_HEADS, DF_HD, S), lambda b, i: (b, 0, 0, 0)),
                  pl.BlockSpec((1, S, DF_W), lambda b, i: (b, 0, 0)),
                  pl.BlockSpec((1, 1), lambda b, i: (0, 0)),
                  pl.BlockSpec((1, DF_W), lambda b, i: (0, 0))],
        out_specs=pl.BlockSpec((1, tq, DF_W), lambda b, i: (b, i, 0)),
        compiler_params=pltpu.CompilerParams(
            dimension_semantics=("parallel", "parallel"), vmem_limit_bytes=VMEM_LIMIT),
        name="diff_attn",
    )(qs, kt, vb, lam.reshape(1, 1).astype(F32), gain)


def diff_branch(p, pc, need_ctx, lq1, lk1, lq2, lk2, subln_g, lam_init):
    q, k, v = split_at(p, DF_SIZES)
    qc, kc, vc = split_at(pc, DF_SIZES)

    def qk(t):
        return t.reshape(t.shape[0], t.shape[1], DF_HEADS, 2, DF_HD)

    def vs(t):
        return t.reshape(t.shape[0], t.shape[1], DF_HEADS, DF_VD)

    lam = (jnp.exp(jnp.sum(lq1.astype(F32) * lk1.astype(F32)))
           - jnp.exp(jnp.sum(lq2.astype(F32) * lk2.astype(F32))) + lam_init)
    k_all = jnp.concatenate([axial_rope(qk(k)), qk(kc)], axis=1)
    v_all = jnp.concatenate([vs(v), vs(vc)], axis=1)
    y = diff_attend(axial_rope(qk(q)), k_all, v_all, lam, lam_init, subln_g)
    yc = diff_attend(qk(qc), qk(kc), vs(vc), lam, lam_init, subln_g) if need_ctx else None
    return y, yc


def merge_branches(ys, p_gate, w_br, w_o):
    gates = jnp.split(p_gate, N_BRANCH, -1)
    acc = None
    for j in range(N_BRANCH):
        term = jax.nn.sigmoid(gates[j]) * _mm3(ys[j], w_br[j])
        acc = term if acc is None else acc + term
    return _mm3(acc, w_o)


def conv_ffn(u, w_up, conv_w, conv_b, w_down):
    hdn = dwconv3(_mm3(u, w_up), conv_w, conv_b)
    a, b = jnp.split(hdn, 2, -1)
    return _mm3(jax.nn.silu(a) * b, w_down)


def _in_proj(u, w):
    offs = np.cumsum((0,) + IN_SIZES)
    return [_mm3(u, w[:, offs[j]:offs[j + 1]]) for j in range(len(IN_SIZES))]


def kernel(x, c, ctx, c_ctx, ada_w, ada_b, w_in, hy_conv_w, hy_conv_b, hy_f_w1, hy_f_b1,
           hy_f_w2, hy_f_b2, hy_f_w3, hy_f_freq, hy_bias, swa_sink, rwkv_mu, rwkv_w0, rwkv_w2,
           rwkv_a0, rwkv_a2, rwkv_g2, rwkv_kk, rwkv_ka, rwkv_rk, rwkv_lnx_g, rwkv_lnx_b,
           diff_lq1, diff_lk1, diff_lq2, diff_lk2, diff_subln_g, w_branch, w_out, ln1_g, ln1_b,
           ffn_w_up, ffn_conv_w, ffn_conv_b, ffn_w_down, ln2_g, ln2_b):
    h, hc = x, ctx
    s_lat = jax.nn.silu(c)
    s_ctx = jax.nn.silu(c_ctx)
    for i in range(DEPTH):
        need_ctx = i < DEPTH - 1
        mod = (s_lat @ ada_w[i] + ada_b[i])[:, None, :]
        mod_c = s_ctx @ ada_w[i] + ada_b[i]
        sh1, sc1, g1, sh2, sc2, g2 = jnp.split(mod, 6, -1)
        csh1, csc1, cg1, csh2, csc2, cg2 = jnp.split(mod_c, 6, -1)

        u = modulate(h, sh1, sc1)
        uc = modulate(hc, csh1, csc1)
        p_hy, p_sw, p_rw, p_df, p_gt = _in_proj(u, w_in[i])
        pc_hy, pc_sw, pc_rw, pc_df, pc_gt = _in_proj(uc, w_in[i])

        y_hy, yc_hy = hyena_branch(p_hy, pc_hy, need_ctx, hy_conv_w[i], hy_conv_b[i], hy_f_w1[i],
                                   hy_f_b1[i], hy_f_w2[i], hy_f_b2[i], hy_f_w3[i], hy_f_freq[i],
                                   hy_bias[i])
        y_sw, yc_sw = swa_branch(p_sw, pc_sw, need_ctx, swa_sink[i])
        y_rw, yc_rw = rwkv_branch(p_rw, pc_rw, need_ctx, rwkv_mu[i], rwkv_w0[i], rwkv_w2[i],
                                  rwkv_a0[i], rwkv_a2[i], rwkv_g2[i], rwkv_kk[i], rwkv_ka[i],
                                  rwkv_rk[i], rwkv_lnx_g[i], rwkv_lnx_b[i])
        lam_init = 0.8 - 0.6 * math.exp(-0.3 * i)
        y_df, yc_df = diff_branch(p_df, pc_df, need_ctx, diff_lq1[i], diff_lk1[i], diff_lq2[i],
                                  diff_lk2[i], diff_subln_g[i], lam_init)

        mix = merge_branches((y_hy, y_sw, y_rw, y_df), p_gt, w_branch[i], w_out[i])
        h = layer_norm(DN_ALPHA * h + g1 * mix, ln1_g[i], ln1_b[i])
        f = conv_ffn(modulate(h, sh2, sc2), ffn_w_up[i], ffn_conv_w[i], ffn_conv_b[i], ffn_w_down[i])
        h = layer_norm(DN_ALPHA * h + g2 * f, ln2_g[i], ln2_b[i])

        if need_ctx:
            mix_c = merge_branches((yc_hy, yc_sw, yc_rw, yc_df), pc_gt, w_branch[i], w_out[i])
            hc = layer_norm(DN_ALPHA * hc + cg1 * mix_c, ln1_g[i], ln1_b[i])
            fc = conv_ffn(modulate(hc, csh2, csc2), ffn_w_up[i], ffn_conv_w[i], ffn_conv_b[i],
                          ffn_w_down[i])
            hc = layer_norm(DN_ALPHA * hc + cg2 * fc, ln2_g[i], ln2_b[i])
    return h
```

```python
import functools
import math

import jax
import jax.numpy as jnp
import numpy as np
from jax import lax
from jax.experimental import pallas as pl
from jax.experimental.pallas import tpu as pltpu

D_MODEL = 1024
BATCH = 8
SEQ = 4096
DEPTH = 2
CTX_LEN = 256
GRID_W = 64
ROPE_BASE = 10000.0
F32 = jnp.float32
BF16 = jnp.bfloat16
NEG_INF = -1e30
LN_EPS = 1e-6

HY_W = 256
HY_ORDER = 2
HY_DIRS = 2
HY_EMB = 33
HY_FFN = 64
HY_MIN_DECAY = math.log(1e-2) / 1.5
HY_MAX_DECAY = math.log(1e-2) / 0.3

SWA_HEADS = 4
SWA_KV = 2
SWA_HD = 64
SWA_WIN = 128
SWA_BLOCK = 128

RW_HEADS = 4
RW_HD = 64
RW_W = RW_HEADS * RW_HD
RW_DECAY_R = 64
RW_AAA_R = 64
RW_GATE_R = 128
RW_GN_EPS = 64e-5

DF_HEADS = 4
DF_HD = 32
DF_VD = 2 * DF_HD
DF_W = DF_HEADS * DF_VD
DF_BLOCK = 128

N_BRANCH = 4
BR_W = 256
D_FF = 2816
DN_ALPHA = (2 * DEPTH) ** 0.25
DN_BETA = (8 * DEPTH) ** -0.25

SWA_SIZES = (SWA_HEADS * SWA_HD, SWA_KV * SWA_HD, SWA_KV * SWA_HD)
RW_SIZES = (RW_W, RW_W, RW_W, RW_DECAY_R, RW_DECAY_R, RW_AAA_R, RW_GATE_R, RW_GATE_R)
DF_SIZES = (DF_W, DF_W, DF_W)
IN_SIZES = (3 * HY_W, sum(SWA_SIZES), sum(RW_SIZES), sum(DF_SIZES), N_BRANCH * D_MODEL)

LANES = 128
MM_TM = 512
MM_TN = 512
RW_TBLK = 64
DF_TQ = 256
HALO = 8
FFN_TM = 1024
FFN_CHUNK = 256
MERGE_TM = 512
VMEM_LIMIT = 48 * 1024 * 1024


def _mm_kernel(x_ref, w_ref, o_ref):
    o_ref[...] = jnp.dot(x_ref[...].astype(BF16), w_ref[...], preferred_element_type=F32)


def _mm(x, w):
    m, k = x.shape
    n = w.shape[1]
    n_pad = -(-n // LANES) * LANES
    wb = w.astype(BF16)
    if n_pad != n:
        wb = jnp.pad(wb, ((0, 0), (0, n_pad - n)))
    tn = MM_TN if n_pad % MM_TN == 0 else (256 if n_pad % 256 == 0 else LANES)
    tm = MM_TM if m % MM_TM == 0 else m
    out = pl.pallas_call(
        _mm_kernel,
        out_shape=jax.ShapeDtypeStruct((m, n_pad), F32),
        grid=(m // tm, n_pad // tn),
        in_specs=[pl.BlockSpec((tm, k), lambda i, j: (i, 0)),
                  pl.BlockSpec((k, tn), lambda i, j: (0, j))],
        out_specs=pl.BlockSpec((tm, tn), lambda i, j: (i, j)),
        compiler_params=pltpu.CompilerParams(
            dimension_semantics=("parallel", "parallel"), vmem_limit_bytes=VMEM_LIMIT),
        name="matmul",
    )(x, wb)
    return out[:, :n] if n_pad != n else out


def _mm3(x, w):
    b, l, k = x.shape
    return _mm(x.reshape(b * l, k), w).reshape(b, l, w.shape[1])


def split_at(t, sizes):
    return jnp.split(t, np.cumsum(sizes)[:-1].tolist(), axis=-1)


def layer_norm(x, g=None, b=None):
    xf = x.astype(F32)
    mu = jnp.mean(xf, -1, keepdims=True)
    var = jnp.mean(jnp.square(xf - mu), -1, keepdims=True)
    y = (xf - mu) * lax.rsqrt(var + LN_EPS)
    if g is not None:
        y = y * g + b
    return y.astype(x.dtype)


def modulate(x, shift, scale):
    return layer_norm(x) * (1 + scale) + shift


def dwconv3(x, w, b):
    xp = jnp.pad(x, ((0, 0), (1, 1), (0, 0)))
    return xp[:, :-2] * w[0] + x * w[1] + xp[:, 2:] * w[2] + b


def axial_rope(x):
    L, d = x.shape[1], x.shape[-1]
    rows = L // GRID_W
    row = jnp.repeat(jnp.arange(rows), GRID_W)
    col = jnp.tile(jnp.arange(GRID_W), rows)
    nf = d // 4
    inv = ROPE_BASE ** (-jnp.arange(nf, dtype=F32) / nf)
    bshape = (L,) + (1,) * (x.ndim - 3) + (nf,)
    xf = x.astype(F32)
    out = []
    for half, pos in enumerate((row, col)):
        ang = (pos.astype(F32)[:, None] * inv[None, :]).reshape(bshape)
        cos, sin = jnp.cos(ang), jnp.sin(ang)
        xh = xf[..., half * 2 * nf:(half + 1) * 2 * nf]
        x1, x2 = xh[..., :nf], xh[..., nf:]
        out += [x1 * cos - x2 * sin, x1 * sin + x2 * cos]
    return jnp.concatenate(out, -1).astype(x.dtype)


def sink_softmax(logit_list, sink_b):
    lead = logit_list[0].shape[:-1]
    z = jnp.concatenate([t.astype(F32) for t in logit_list]
                        + [jnp.broadcast_to(sink_b.astype(F32), lead + (1,))], -1)
    return jax.nn.softmax(z, -1)[..., :-1]


def hyena_spectrum(L, w1, b1, w2, b2, w3, freq):
    t = jnp.linspace(0.0, 1.0, L, dtype=F32)[:, None]
    bands = (HY_EMB - 1) // 2
    w = 2.0 * math.pi * jnp.arange(L, dtype=F32) / L
    fb = jnp.linspace(1e-4, bands - 1, bands, dtype=F32)
    ang = w[:, None] * fb[None, :]
    z = jnp.concatenate([t, jnp.cos(ang), -jnp.sin(ang)], -1)
    fr = freq.astype(F32)
    hdn = jnp.sin(fr * (z @ w1.astype(F32) + b1.astype(F32)))
    hdn = jnp.sin(fr * (hdn @ w2.astype(F32) + b2.astype(F32)))
    filt = (hdn @ w3.astype(F32)).reshape(L, HY_ORDER, HY_DIRS, HY_W)
    deltas = jnp.abs(jnp.linspace(HY_MIN_DECAY, HY_MAX_DECAY, HY_W, dtype=F32))
    filt = filt * jnp.exp(-t * deltas[None, :])[:, None, None, :]
    filt = filt / jnp.sum(jnp.abs(filt), axis=(0, 2), keepdims=True)
    fwd, bwd = filt[:, :, 0], filt[:, :, 1]
    kern = jnp.concatenate([fwd, jnp.zeros_like(fwd[:1]), bwd[:0:-1]], axis=0)
    return jnp.fft.rfft(kern, axis=0)


def fft_longconv(z, kf, bias):
    L = z.shape[1]
    zf = jnp.fft.rfft(z.astype(F32), n=2 * L, axis=1)
    y = jnp.fft.irfft(zf * kf[None], n=2 * L, axis=1)[:, :L]
    return (y + z * bias).astype(z.dtype)


def hyena_mix(p, conv_w, conv_b, kf, bias):
    v, x1, x2 = jnp.split(dwconv3(p, conv_w, conv_b), 3, -1)
    zz = x1 * fft_longconv(v, kf[:, 0], bias[0])
    return x2 * fft_longconv(zz, kf[:, 1], bias[1])


def hyena_branch(p, pc, need_ctx, conv_w, conv_b, fw1, fb1, fw2, fb2, fw3, ffreq, bias):
    kf = hyena_spectrum(p.shape[1], fw1, fb1, fw2, fb2, fw3, ffreq)
    y = hyena_mix(p, conv_w, conv_b, kf, bias)
    yc = None
    if need_ctx:
        kfc = hyena_spectrum(pc.shape[1], fw1, fb1, fw2, fb2, fw3, ffreq)
        yc = hyena_mix(pc, conv_w, conv_b, kfc, bias)
    return y, yc


def swa_latent(q, k, v, kc, vc, sink):
    B, L = q.shape[:2]
    nb = L // SWA_BLOCK
    G = SWA_HEADS // SWA_KV
    scale = SWA_HD ** -0.5
    qb = q.reshape(B, nb, SWA_BLOCK, SWA_KV, G, SWA_HD)

    def band(t):
        tp = jnp.pad(t, ((0, 0), (SWA_BLOCK, SWA_BLOCK), (0, 0), (0, 0)))
        tp = tp.reshape(B, nb + 2, SWA_BLOCK, SWA_KV, SWA_HD)
        return jnp.concatenate([tp[:, :-2], tp[:, 1:-1], tp[:, 2:]], axis=2)

    kb, vb = band(k), band(v)
    s_loc = jnp.einsum('bnqkgd,bnskd->bnkgqs', qb, kb).astype(F32) * scale
    s_ctx = jnp.einsum('bnqkgd,bckd->bnkgqc', qb, kc).astype(F32) * scale
    blk = jnp.arange(nb)[:, None, None]
    qpos = blk * SWA_BLOCK + jnp.arange(SWA_BLOCK)[None, :, None]
    kpos = (blk - 1) * SWA_BLOCK + jnp.arange(3 * SWA_BLOCK)[None, None, :]
    valid = (jnp.abs(kpos - qpos) <= SWA_WIN) & (kpos >= 0) & (kpos < L)
    s_loc = jnp.where(valid[None, :, None, None], s_loc, NEG_INF)
    p = sink_softmax([s_loc, s_ctx], sink.reshape(SWA_KV, G)[:, :, None, None])
    nl = 3 * SWA_BLOCK
    o = (jnp.einsum('bnkgqs,bnskd->bnqkgd', p[..., :nl].astype(v.dtype), vb)
         + jnp.einsum('bnkgqc,bckd->bnqkgd', p[..., nl:].astype(v.dtype), vc))
    return o.reshape(B, L, SWA_HEADS * SWA_HD)


def swa_context(qc, kc, vc, sink):
    B, C = qc.shape[:2]
    G = SWA_HEADS // SWA_KV
    qg = qc.reshape(B, C, SWA_KV, G, SWA_HD)
    s = jnp.einsum('bqkgd,bckd->bkgqc', qg, kc).astype(F32) * SWA_HD ** -0.5
    p = sink_softmax([s], sink.reshape(SWA_KV, G)[:, :, None, None])
    o = jnp.einsum('bkgqc,bckd->bqkgd', p.astype(vc.dtype), vc)
    return o.reshape(B, C, SWA_HEADS * SWA_HD)


def swa_branch(p, pc, need_ctx, sink):
    B, L = p.shape[:2]
    C = pc.shape[1]
    q, k, v = split_at(p, SWA_SIZES)
    qc, kc, vc = split_at(pc, SWA_SIZES)
    q = axial_rope(q.reshape(B, L, SWA_HEADS, SWA_HD))
    k = axial_rope(k.reshape(B, L, SWA_KV, SWA_HD))
    v = v.reshape(B, L, SWA_KV, SWA_HD)
    kc = kc.reshape(B, C, SWA_KV, SWA_HD)
    vc = vc.reshape(B, C, SWA_KV, SWA_HD)
    y = swa_latent(q, k, v, kc, vc, sink)
    yc = swa_context(qc.reshape(B, C, SWA_HEADS, SWA_HD), kc, vc, sink) if need_ctx else None
    return y, yc


def rwkv_streams(p, mu, w0, w2, a0, a2, g2, k_k, k_a):
    B, L = p.shape[:2]
    pp = jnp.pad(p, ((0, 0), (1, 1), (0, 0)))
    p = p + (0.5 * (pp[:, :-2] + pp[:, 2:]) - p) * mu
    r, k, v, wd_f, wd_b, ad, gd_f, gd_b = split_at(p, RW_SIZES)

    def heads(t):
        return t.reshape(B, L, RW_HEADS, RW_HD)

    def decay(wd, w0d, w2d):
        wlog = -jax.nn.softplus(-(w0d + jnp.tanh(wd) @ w2d)) - 0.5
        return jnp.exp(-jnp.exp(wlog.astype(F32)))

    a = jax.nn.sigmoid(a0 + ad @ a2)
    kk = heads(k * k_k).astype(F32)
    kk = kk / jnp.maximum(jnp.sqrt(jnp.sum(kk * kk, -1, keepdims=True)), 1e-12)
    k = k * (1 + (a - 1) * k_a)
    g_f = jax.nn.sigmoid(gd_f) @ g2[0]
    g_b = jax.nn.sigmoid(gd_b) @ g2[1]
    return (heads(r), heads(k), heads(v), kk, heads(a),
            heads(decay(wd_f, w0[0], w2[0])), heads(decay(wd_b, w0[1], w2[1])),
            heads(g_f), heads(g_b))


def _rwkv_scan_kernel(tblk, nb,
                      wf_ref, kkf_ref, bf_ref, kf_ref, vf_ref, rf_ref,
                      wb_ref, kkb_ref, bb_ref, kb_ref, vb_ref, rb_ref,
                      of_ref, ob_ref, sf_ref, sb_ref):
    c = pl.program_id(0)

    @pl.when(c == 0)
    def _():
        sf_ref[...] = jnp.zeros_like(sf_ref)
        sb_ref[...] = jnp.zeros_like(sb_ref)

    row = lax.broadcasted_iota(jnp.int32, (RW_W, RW_W), 0)
    col = lax.broadcasted_iota(jnp.int32, (RW_W, RW_W), 1)
    ones_blk = jnp.where(row // RW_HD == col // RW_HD, 1.0, 0.0).astype(BF16)
    r64 = lax.broadcasted_iota(jnp.int32, (RW_HD, RW_W), 0)
    c64 = lax.broadcasted_iota(jnp.int32, (RW_HD, RW_W), 1)
    eye_t = jnp.where(r64 == c64 % RW_HD, 1.0, 0.0).astype(F32)
    r8 = lax.broadcasted_iota(jnp.int32, (8, RW_W), 0)
    c8 = lax.broadcasted_iota(jnp.int32, (8, RW_W), 1)
    head_sel = jnp.where(r8 == c8 // RW_HD, 1.0, 0.0).astype(BF16)

    fwd = (wf_ref, kkf_ref, bf_ref, kf_ref, vf_ref, rf_ref, sf_ref)
    bwd = (wb_ref, kkb_ref, bb_ref, kb_ref, vb_ref, rb_ref, sb_ref)
    nrow = nb * RW_HD

    def reduce_operand(idx, refs):
        w_ref, kk_ref, b_ref, k_ref, v_ref, r_ref, s_ref = refs
        p1 = [(s_ref[b] * kk_ref[idx, b:b + 1, :]).astype(BF16) for b in range(nb)]
        vm = [(eye_t * v_ref[idx, b:b + 1, :]).astype(BF16) for b in range(nb)]
        return jnp.concatenate(p1 + vm, axis=0)

    def update(idx, refs, red):
        w_ref, kk_ref, b_ref, k_ref, v_ref, r_ref, s_ref = refs
        zs = []
        for b in range(nb):
            sa = red[b * RW_HD:(b + 1) * RW_HD]
            vc = red[nrow + b * RW_HD:nrow + (b + 1) * RW_HD]
            s_new = (s_ref[b] * w_ref[idx, b:b + 1, :] - sa * b_ref[idx, b:b + 1, :]
                     + vc * k_ref[idx, b:b + 1, :])
            s_ref[b] = s_new
            zs.append((s_new * r_ref[idx, b:b + 1, :]).astype(BF16))
        return jnp.concatenate(zs, axis=0)

    def emit(idx, z, o_ref):
        o_ref[idx] = lax.dot_general(head_sel, z, (((1,), (1,)), ((), ())),
                                     preferred_element_type=F32)

    def step(i, carry):
        lhs_f, z_f, z_b = carry
        ib = tblk - 1 - i
        red_f = jnp.dot(lhs_f, ones_blk, preferred_element_type=F32)
        red_b = jnp.dot(reduce_operand(ib, bwd), ones_blk, preferred_element_type=F32)
        emit(jnp.maximum(i - 1, 0), z_f, of_ref)
        emit(jnp.minimum(ib + 1, tblk - 1), z_b, ob_ref)
        z_f = update(i, fwd, red_f)
        lhs_f = reduce_operand(jnp.minimum(i + 1, tblk - 1), fwd)
        z_b = update(ib, bwd, red_b)
        return lhs_f, z_f, z_b

    z0 = jnp.zeros((nrow, RW_W), BF16)
    _, z_f, z_b = lax.fori_loop(0, tblk, step, (reduce_operand(0, fwd), z0, z0))
    emit(tblk - 1, z_f, of_ref)
    emit(0, z_b, ob_ref)


def rwkv_scan_pallas(w_f, w_b, kk, bvec, k, v, r, n_ctx):
    n, nb, _ = kk.shape
    tblk = RW_TBLK
    nblk = n // tblk
    nblk_ctx = n_ctx // tblk

    def fmap(c):
        return (c, 0, 0)

    def bmap(c):
        return (jnp.where(c < nblk_ctx, nblk_ctx - 1 - c, nblk + nblk_ctx - 1 - c), 0, 0)

    blk = (tblk, nb, RW_W)
    oblk = (tblk, 8, nb * RW_HD)
    return pl.pallas_call(
        functools.partial(_rwkv_scan_kernel, tblk, nb),
        out_shape=[jax.ShapeDtypeStruct((n, 8, nb * RW_HD), F32)] * 2,
        grid=(nblk,),
        in_specs=[pl.BlockSpec(blk, fmap)] * 6 + [pl.BlockSpec(blk, bmap)] * 6,
        out_specs=[pl.BlockSpec(oblk, fmap), pl.BlockSpec(oblk, bmap)],
        scratch_shapes=[pltpu.VMEM((nb, RW_HD, RW_W), F32), pltpu.VMEM((nb, RW_HD, RW_W), F32)],
        compiler_params=pltpu.CompilerParams(dimension_semantics=("arbitrary",),
                                             vmem_limit_bytes=VMEM_LIMIT),
        name="rwkv_scan",
    )(w_f, kk, bvec, k, v, r, w_b, kk, bvec, k, v, r)


def rwkv_out(o_f, o_b, r, k, v, g_f, g_b, r_k, lnx_g, lnx_b):
    B, L = r.shape[:2]
    gam = lnx_g.reshape(RW_HEADS, RW_HD)
    bet = lnx_b.reshape(RW_HEADS, RW_HD)

    def gn(o):
        mu = jnp.mean(o, -1, keepdims=True)
        var = jnp.mean(jnp.square(o - mu), -1, keepdims=True)
        return (o - mu) * lax.rsqrt(var + RW_GN_EPS) * gam + bet

    bonus = jnp.sum(r * k * r_k, -1, keepdims=True) * v
    y = (gn(o_f) + bonus) * g_f + (gn(o_b) + bonus) * g_b
    return y.reshape(B, L, RW_W).astype(v.dtype)


def rwkv_branch(p, pc, need_ctx, mu, w0, w2, a0, a2, g2, k_k, k_a, r_k, lnx_g, lnx_b):
    r, k, v, kk, a, d_f, d_b, g_f, g_b = rwkv_streams(p, mu, w0, w2, a0, a2, g2, k_k, k_a)
    rc, kc, vc, kkc, ac, dc_f, dc_b, gc_f, gc_b = rwkv_streams(pc, mu, w0, w2, a0, a2, g2, k_k, k_a)
    B, L = p.shape[:2]
    C = pc.shape[1]

    def tmajor(xc, xl):
        t = jnp.concatenate([xc, xl], axis=1).astype(F32).reshape(B, C + L, RW_W)
        return jnp.moveaxis(t, 1, 0)

    pf, pb = rwkv_scan_pallas(tmajor(dc_f, d_f), tmajor(dc_b, d_b), tmajor(kkc, kk),
                              tmajor(kkc * ac, kk * a), tmajor(kc, k), tmajor(vc, v),
                              tmajor(rc, r), C)

    def bmajor(t):
        t = t[:, :RW_HEADS].reshape(C + L, RW_HEADS, B, RW_HD)
        return jnp.transpose(t, (2, 0, 1, 3))

    pf, pb = bmajor(pf), bmajor(pb)
    oc_f, o_f = pf[:, :C], pf[:, C:]
    oc_b, o_b = pb[:, :C], pb[:, C:]
    y = rwkv_out(o_f, o_b, r, k, v, g_f, g_b, r_k, lnx_g, lnx_b)
    yc = rwkv_out(oc_f, oc_b, rc, kc, vc, gc_f, gc_b, r_k, lnx_g, lnx_b) if need_ctx else None
    return y, yc


def _diff_attn_kernel(q_ref, kt_ref, v_ref, lam_ref, gain_ref, o_ref):
    q = q_ref[0]
    v = v_ref[0]
    lam = lam_ref[...]
    tq = q.shape[0]
    head_of_lane = lax.broadcasted_iota(jnp.int32, (1, DF_W), 1) // DF_VD
    acc = jnp.zeros((tq, DF_W), F32)
    for h in range(DF_HEADS):
        parts = []
        for c in range(2):
            j = 2 * h + c
            s = jnp.dot(q[:, DF_HD * j:DF_HD * (j + 1)], kt_ref[0, j],
                        preferred_element_type=F32)
            p = jnp.exp2(s - jnp.max(s, -1, keepdims=True))
            l = jnp.sum(p, -1, keepdims=True)
            pv = jnp.dot(p.astype(BF16), v, preferred_element_type=F32)
            parts.append(pv / l)
        acc = jnp.where(head_of_lane == h, parts[0] - lam * parts[1], acc)
    row = lax.broadcasted_iota(jnp.int32, (DF_W, DF_W), 0)
    col = lax.broadcasted_iota(jnp.int32, (DF_W, DF_W), 1)
    seg_mean = jnp.where(row // DF_VD == col // DF_VD, 1.0 / DF_VD, 0.0).astype(F32)
    ms = jnp.dot(acc * acc, seg_mean, preferred_element_type=F32, precision=lax.Precision.HIGHEST)
    o_ref[0] = acc * lax.rsqrt(ms + 1e-5) * gain_ref[...]


def diff_attend(q, k, v, lam, lam_init, subln_g):
    B, Lq = q.shape[:2]
    S = k.shape[1]
    tq = min(DF_TQ, Lq)
    qs = (q.astype(F32) * (DF_HD ** -0.5 * math.log2(math.e))).reshape(B, Lq, DF_W).astype(BF16)
    kt = jnp.transpose(k.reshape(B, S, 2 * DF_HEADS, DF_HD), (0, 2, 3, 1)).astype(BF16)
    vb = v.reshape(B, S, DF_W).astype(BF16)
    gain = jnp.tile(subln_g.astype(F32) * (1.0 - lam_init), DF_HEADS).reshape(1, DF_W)
    return pl.pallas_call(
        _diff_attn_kernel,
        out_shape=jax.ShapeDtypeStruct((B, Lq, DF_W), F32),
        grid=(B, Lq // tq),
        in_specs=[pl.BlockSpec((1, tq, DF_W), lambda b, i: (b, i, 0)),
                  pl.BlockSpec((1, 2 * DF_HEADS, DF_HD, S), lambda b, i: (b, 0, 0, 0)),
                  pl.BlockSpec((1, S, DF_W), lambda b, i: (b, 0, 0)),
                  pl.BlockSpec((1, 1), lambda b, i: (0, 0)),
                  pl.BlockSpec((1, DF_W), lambda b, i: (0, 0))],
        out_specs=pl.BlockSpec((1, tq, DF_W), lambda b, i: (b, i, 0)),
        compiler_params=pltpu.CompilerParams(
            dimension_semantics=("parallel", "parallel"), vmem_limit_bytes=VMEM_LIMIT),
        name="diff_attn",
    )(qs, kt, vb, lam.reshape(1, 1).astype(F32), gain)


def diff_branch(p, pc, need_ctx, lq1, lk1, lq2, lk2, subln_g, lam_init):
    q, k, v = split_at(p, DF_SIZES)
    qc, kc, vc = split_at(pc, DF_SIZES)

    def qk(t):
        return t.reshape(t.shape[0], t.shape[1], DF_HEADS, 2, DF_HD)

    def vs(t):
        return t.reshape(t.shape[0], t.shape[1], DF_HEADS, DF_VD)

    lam = (jnp.exp(jnp.sum(lq1.astype(F32) * lk1.astype(F32)))
           - jnp.exp(jnp.sum(lq2.astype(F32) * lk2.astype(F32))) + lam_init)
    k_all = jnp.concatenate([axial_rope(qk(k)), qk(kc)], axis=1)
    v_all = jnp.concatenate([vs(v), vs(vc)], axis=1)
    y = diff_attend(axial_rope(qk(q)), k_all, v_all, lam, lam_init, subln_g)
    yc = diff_attend(qk(qc), qk(kc), vs(vc), lam, lam_init, subln_g) if need_ctx else None
    return y, yc


def _ln_rows(x):
    mu = jnp.mean(x, -1, keepdims=True)
    xc = x - mu
    return xc * lax.rsqrt(jnp.mean(xc * xc, -1, keepdims=True) + LN_EPS)


def _ffn_kernel(tm, tiles_per_seq, n_chunks,
                h_ref, hprev_ref, hnext_ref, sh_ref, sc_ref, gate_ref,
                wa_ref, wb_ref, cwa_ref, cwb_ref, cba_ref, cbb_ref, wd_ref, lng_ref, lnb_ref,
                o_ref, u_ref, acc_ref):
    i = pl.program_id(0)
    j = pl.program_id(1)
    n = tm + 2 * HALO

    @pl.when(j == 0)
    def _():
        scale = 1.0 + sc_ref[0]
        shift = sh_ref[0]
        first = (i % tiles_per_seq) == 0
        last = (i % tiles_per_seq) == tiles_per_seq - 1
        u_ref[0:tm, :] = (_ln_rows(h_ref[...]) * scale + shift).astype(BF16)
        un = _ln_rows(hnext_ref[...]) * scale + shift
        up = _ln_rows(hprev_ref[...]) * scale + shift
        u_ref[tm:tm + HALO, :] = jnp.where(last, 0.0, un).astype(BF16)
        u_ref[tm + HALO:n, :] = jnp.where(first, 0.0, up).astype(BF16)
        acc_ref[...] = jnp.zeros_like(acc_ref)

    u = u_ref[...]

    def conv(w_ref, cw_ref, cb_ref):
        x = jnp.dot(u, w_ref[...], preferred_element_type=F32)
        cw = cw_ref[...]
        y = (pltpu.roll(x, 1, 0)[0:tm] * cw[0:1] + x[0:tm] * cw[1:2]
             + pltpu.roll(x, n - 1, 0)[0:tm] * cw[2:3] + cb_ref[...])
        return y

    a = conv(wa_ref, cwa_ref, cba_ref)
    b = conv(wb_ref, cwb_ref, cbb_ref)
    g = (a * jax.nn.sigmoid(a) * b).astype(BF16)
    acc_ref[...] += jnp.dot(g, wd_ref[...], preferred_element_type=F32)

    @pl.when(j == n_chunks - 1)
    def _():
        y = DN_ALPHA * h_ref[...] + gate_ref[0] * acc_ref[...]
        o_ref[...] = _ln_rows(y) * lng_ref[...] + lnb_ref[...]


def ffn_block(h, shift, scale, gate, w_up, conv_w, conv_b, w_down, ln_g, ln_b):
    B, L, D = h.shape
    tm = min(FFN_TM, L)
    tiles_per_seq = L // tm
    n_tok = B * L
    n_chunks = D_FF // FFN_CHUNK
    hb = tm // HALO

    def bcast(t):
        return jnp.broadcast_to(t.astype(F32), (B, 1, D))

    wu = w_up.astype(BF16)
    wd = w_down.astype(BF16)
    cb = conv_b.reshape(1, 2 * D_FF)
    h2 = h.reshape(n_tok, D)
    mod_spec = pl.BlockSpec((1, 1, D), lambda i, j: (i // tiles_per_seq, 0, 0))
    vec_spec = pl.BlockSpec((1, D), lambda i, j: (0, 0))
    out = pl.pallas_call(
        functools.partial(_ffn_kernel, tm, tiles_per_seq, n_chunks),
        out_shape=jax.ShapeDtypeStruct((n_tok, D), F32),
        grid=(n_tok // tm, n_chunks),
        in_specs=[pl.BlockSpec((tm, D), lambda i, j: (i, 0)),
                  pl.BlockSpec((HALO, D), lambda i, j: (jnp.maximum(i * hb - 1, 0), 0)),
                  pl.BlockSpec((HALO, D), lambda i, j: (jnp.minimum((i + 1) * hb, n_tok // HALO - 1), 0)),
                  mod_spec, mod_spec, mod_spec,
                  pl.BlockSpec((D, FFN_CHUNK), lambda i, j: (0, j)),
                  pl.BlockSpec((D, FFN_CHUNK), lambda i, j: (0, n_chunks + j)),
                  pl.BlockSpec((3, FFN_CHUNK), lambda i, j: (0, j)),
                  pl.BlockSpec((3, FFN_CHUNK), lambda i, j: (0, n_chunks + j)),
                  pl.BlockSpec((1, FFN_CHUNK), lambda i, j: (0, j)),
                  pl.BlockSpec((1, FFN_CHUNK), lambda i, j: (0, n_chunks + j)),
                  pl.BlockSpec((FFN_CHUNK, D), lambda i, j: (j, 0)),
                  vec_spec, vec_spec],
        out_specs=pl.BlockSpec((tm, D), lambda i, j: (i, 0)),
        scratch_shapes=[pltpu.VMEM((tm + 2 * HALO, D), BF16), pltpu.VMEM((tm, D), F32)],
        compiler_params=pltpu.CompilerParams(
            dimension_semantics=("parallel", "arbitrary"), vmem_limit_bytes=VMEM_LIMIT),
        name="conv_ffn",
    )(h2, h2, h2, bcast(shift), bcast(scale), bcast(gate), wu, wu, conv_w, conv_w, cb, cb, wd,
      ln_g.reshape(1, D), ln_b.reshape(1, D))
    return out.reshape(B, L, D)


P_GATE, P_HY, P_SW, P_RW, P_DF = 0, 4096, 4864, 5376, 6656
P_COLS = 7680
IN_CHUNK = 512


def _in_proj_kernel(x_ref, sh_ref, sc_ref, w_ref, o_ref, u_ref):
    @pl.when(pl.program_id(1) == 0)
    def _():
        u_ref[...] = (_ln_rows(x_ref[...]) * (1.0 + sc_ref[0]) + sh_ref[0]).astype(BF16)

    o_ref[...] = jnp.dot(u_ref[...], w_ref[...], preferred_element_type=F32)


def in_proj(h, shift, scale, w_in_p):
    B, L, D = h.shape
    tm = min(FFN_TM, L)
    tiles_per_seq = L // tm
    n_tok = B * L

    def bcast(t):
        return jnp.broadcast_to(t.astype(F32), (B, 1, D))

    mod_spec = pl.BlockSpec((1, 1, D), lambda i, j: (i // tiles_per_seq, 0, 0))
    return pl.pallas_call(
        _in_proj_kernel,
        out_shape=jax.ShapeDtypeStruct((n_tok, P_COLS), F32),
        grid=(n_tok // tm, P_COLS // IN_CHUNK),
        in_specs=[pl.BlockSpec((tm, D), lambda i, j: (i, 0)), mod_spec, mod_spec,
                  pl.BlockSpec((D, IN_CHUNK), lambda i, j: (0, j))],
        out_specs=pl.BlockSpec((tm, IN_CHUNK), lambda i, j: (i, j)),
        scratch_shapes=[pltpu.VMEM((tm, D), BF16)],
        compiler_params=pltpu.CompilerParams(
            dimension_semantics=("parallel", "arbitrary"), vmem_limit_bytes=VMEM_LIMIT),
        name="in_proj",
    )(h.reshape(n_tok, D), bcast(shift), bcast(scale), w_in_p)


def pack_w_in(w):
    offs = np.cumsum((0,) + IN_SIZES)
    hy, sw, rw, df, gt = [w[:, offs[j]:offs[j + 1]] for j in range(len(IN_SIZES))]

    def padto(t, n):
        return jnp.pad(t, ((0, 0), (0, n - t.shape[1])))

    return jnp.concatenate([gt, hy, sw, padto(rw, P_DF - P_RW), padto(df, P_COLS - P_DF)],
                           axis=1).astype(BF16)


def _merge_kernel(yh_ref, ys_ref, yr_ref, yd_ref, g0_ref, g1_ref, g2_ref, g3_ref, h_ref, gate_ref,
                  wbr_ref, wo_ref, lng_ref, lnb_ref, o_ref):
    acc = None
    for j, (y_ref, g_ref) in enumerate(((yh_ref, g0_ref), (ys_ref, g1_ref), (yr_ref, g2_ref),
                                        (yd_ref, g3_ref))):
        term = jax.nn.sigmoid(g_ref[...]) * jnp.dot(y_ref[...].astype(BF16), wbr_ref[j],
                                                    preferred_element_type=F32)
        acc = term if acc is None else acc + term
    mix = jnp.dot(acc.astype(BF16), wo_ref[...], preferred_element_type=F32)
    y = DN_ALPHA * h_ref[...] + gate_ref[0] * mix
    o_ref[...] = _ln_rows(y) * lng_ref[...] + lnb_ref[...]


def merge_block(ys, p, h, gate, w_br, w_o, ln_g, ln_b):
    B, L, D = h.shape
    n_tok = B * L
    tm = min(MERGE_TM, L)
    tiles_per_seq = L // tm
    y_spec = pl.BlockSpec((tm, BR_W), lambda i: (i, 0))
    vec_spec = pl.BlockSpec((1, D), lambda i: (0, 0))
    g_specs = [pl.BlockSpec((tm, D), functools.partial(lambda j, i: (i, j), j)) for j in range(N_BRANCH)]
    out = pl.pallas_call(
        _merge_kernel,
        out_shape=jax.ShapeDtypeStruct((n_tok, D), F32),
        grid=(n_tok // tm,),
        in_specs=[y_spec] * N_BRANCH + g_specs + [
            pl.BlockSpec((tm, D), lambda i: (i, 0)),
            pl.BlockSpec((1, 1, D), lambda i: (i // tiles_per_seq, 0, 0)),
            pl.BlockSpec((N_BRANCH, BR_W, D), lambda i: (0, 0, 0)),
            pl.BlockSpec((D, D), lambda i: (0, 0)), vec_spec, vec_spec],
        out_specs=pl.BlockSpec((tm, D), lambda i: (i, 0)),
        compiler_params=pltpu.CompilerParams(
            dimension_semantics=("parallel",), vmem_limit_bytes=VMEM_LIMIT),
        name="merge",
    )(*[y.reshape(n_tok, BR_W) for y in ys], p, p, p, p, h.reshape(n_tok, D),
      jnp.broadcast_to(gate.astype(F32), (B, 1, D)), w_br.astype(BF16), w_o.astype(BF16),
      ln_g.reshape(1, D), ln_b.reshape(1, D))
    return out.reshape(B, L, D)


def kernel(x, c, ctx, c_ctx, ada_w, ada_b, w_in, hy_conv_w, hy_conv_b, hy_f_w1, hy_f_b1,
           hy_f_w2, hy_f_b2, hy_f_w3, hy_f_freq, hy_bias, swa_sink, rwkv_mu, rwkv_w0, rwkv_w2,
           rwkv_a0, rwkv_a2, rwkv_g2, rwkv_kk, rwkv_ka, rwkv_rk, rwkv_lnx_g, rwkv_lnx_b,
           diff_lq1, diff_lk1, diff_lq2, diff_lk2, diff_subln_g, w_branch, w_out, ln1_g, ln1_b,
           ffn_w_up, ffn_conv_w, ffn_conv_b, ffn_w_down, ln2_g, ln2_b):
    h, hc = x, ctx
    s_lat = jax.nn.silu(c)
    s_ctx = jax.nn.silu(c_ctx)
    for i in range(DEPTH):
        need_ctx = i < DEPTH - 1
        mod = (s_lat @ ada_w[i] + ada_b[i])[:, None, :]
        mod_c = s_ctx @ ada_w[i] + ada_b[i]
        sh1, sc1, g1, sh2, sc2, g2 = jnp.split(mod, 6, -1)
        csh1, csc1, cg1, csh2, csc2, cg2 = jnp.split(mod_c, 6, -1)

        B, L, D = h.shape
        C = hc.shape[1]
        w_in_p = pack_w_in(w_in[i])
        p = in_proj(h, sh1, sc1, w_in_p)
        pc = in_proj(hc, csh1.reshape(1, 1, D), csc1.reshape(1, 1, D), w_in_p)

        def seg(t, n, off, width):
            return t[:, off:off + width].reshape(B, n, width)

        p_hy, p_sw, p_rw, p_df = [seg(p, L, o, w) for o, w in zip((P_HY, P_SW, P_RW, P_DF), IN_SIZES[:4])]
        pc_hy, pc_sw, pc_rw, pc_df = [seg(pc, C, o, w) for o, w in zip((P_HY, P_SW, P_RW, P_DF), IN_SIZES[:4])]

        y_hy, yc_hy = hyena_branch(p_hy, pc_hy, need_ctx, hy_conv_w[i], hy_conv_b[i], hy_f_w1[i],
                                   hy_f_b1[i], hy_f_w2[i], hy_f_b2[i], hy_f_w3[i], hy_f_freq[i],
                                   hy_bias[i])
        y_sw, yc_sw = swa_branch(p_sw, pc_sw, need_ctx, swa_sink[i])
        y_rw, yc_rw = rwkv_branch(p_rw, pc_rw, need_ctx, rwkv_mu[i], rwkv_w0[i], rwkv_w2[i],
                                  rwkv_a0[i], rwkv_a2[i], rwkv_g2[i], rwkv_kk[i], rwkv_ka[i],
                                  rwkv_rk[i], rwkv_lnx_g[i], rwkv_lnx_b[i])
        lam_init = 0.8 - 0.6 * math.exp(-0.3 * i)
        y_df, yc_df = diff_branch(p_df, pc_df, need_ctx, diff_lq1[i], diff_lk1[i], diff_lq2[i],
                                  diff_lk2[i], diff_subln_g[i], lam_init)

        h = merge_block((y_hy, y_sw, y_rw, y_df), p, h, g1, w_branch[i], w_out[i], ln1_g[i], ln1_b[i])
        h = ffn_block(h, sh2, sc2, g2, ffn_w_up[i], ffn_conv_w[i], ffn_conv_b[i], ffn_w_down[i],
                      ln2_g[i], ln2_b[i])

        if need_ctx:
            hc = merge_block((yc_hy, yc_sw, yc_rw, yc_df), pc, hc, cg1.reshape(1, 1, D), w_branch[i],
                             w_out[i], ln1_g[i], ln1_b[i])
            hc = ffn_block(hc, csh2.reshape(1, 1, -1), csc2.reshape(1, 1, -1), cg2.reshape(1, 1, -1),
                           ffn_w_up[i], ffn_conv_w[i], ffn_conv_b[i], ffn_w_down[i], ln2_g[i], ln2_b[i])
    return h
```

```python
import functools
import math

import jax
import jax.numpy as jnp
import numpy as np
from jax import lax
from jax.experimental import pallas as pl
from jax.experimental.pallas import tpu as pltpu

D_MODEL = 1024
BATCH = 8
SEQ = 4096
DEPTH = 2
CTX_LEN = 256
GRID_W = 64
ROPE_BASE = 10000.0
F32 = jnp.float32
BF16 = jnp.bfloat16
NEG_INF = -1e30
LN_EPS = 1e-6

HY_W = 256
HY_ORDER = 2
HY_DIRS = 2
HY_EMB = 33
HY_FFN = 64
HY_MIN_DECAY = math.log(1e-2) / 1.5
HY_MAX_DECAY = math.log(1e-2) / 0.3

SWA_HEADS = 4
SWA_KV = 2
SWA_HD = 64
SWA_WIN = 128
SWA_BLOCK = 128

RW_HEADS = 4
RW_HD = 64
RW_W = RW_HEADS * RW_HD
RW_DECAY_R = 64
RW_AAA_R = 64
RW_GATE_R = 128
RW_GN_EPS = 64e-5

DF_HEADS = 4
DF_HD = 32
DF_VD = 2 * DF_HD
DF_W = DF_HEADS * DF_VD
DF_BLOCK = 128

N_BRANCH = 4
BR_W = 256
D_FF = 2816
DN_ALPHA = (2 * DEPTH) ** 0.25
DN_BETA = (8 * DEPTH) ** -0.25

SWA_SIZES = (SWA_HEADS * SWA_HD, SWA_KV * SWA_HD, SWA_KV * SWA_HD)
RW_SIZES = (RW_W, RW_W, RW_W, RW_DECAY_R, RW_DECAY_R, RW_AAA_R, RW_GATE_R, RW_GATE_R)
DF_SIZES = (DF_W, DF_W, DF_W)
IN_SIZES = (3 * HY_W, sum(SWA_SIZES), sum(RW_SIZES), sum(DF_SIZES), N_BRANCH * D_MODEL)

LANES = 128
MM_TM = 512
MM_TN = 512
RW_TBLK = 64
DF_TQ = 256
HALO = 8
FFN_TM = 1024
FFN_CHUNK = 256
MERGE_TM = 512
HY_TK = 512
VMEM_LIMIT = 48 * 1024 * 1024


def _mm_kernel(x_ref, w_ref, o_ref):
    o_ref[...] = jnp.dot(x_ref[...].astype(BF16), w_ref[...], preferred_element_type=F32)


def _mm(x, w):
    m, k = x.shape
    n = w.shape[1]
    n_pad = -(-n // LANES) * LANES
    wb = w.astype(BF16)
    if n_pad != n:
        wb = jnp.pad(wb, ((0, 0), (0, n_pad - n)))
    tn = MM_TN if n_pad % MM_TN == 0 else (256 if n_pad % 256 == 0 else LANES)
    tm = MM_TM if m % MM_TM == 0 else m
    out = pl.pallas_call(
        _mm_kernel,
        out_shape=jax.ShapeDtypeStruct((m, n_pad), F32),
        grid=(m // tm, n_pad // tn),
        in_specs=[pl.BlockSpec((tm, k), lambda i, j: (i, 0)),
                  pl.BlockSpec((k, tn), lambda i, j: (0, j))],
        out_specs=pl.BlockSpec((tm, tn), lambda i, j: (i, j)),
        compiler_params=pltpu.CompilerParams(
            dimension_semantics=("parallel", "parallel"), vmem_limit_bytes=VMEM_LIMIT),
        name="matmul",
    )(x, wb)
    return out[:, :n] if n_pad != n else out


def _mm3(x, w):
    b, l, k = x.shape
    return _mm(x.reshape(b * l, k), w).reshape(b, l, w.shape[1])


def split_at(t, sizes):
    return jnp.split(t, np.cumsum(sizes)[:-1].tolist(), axis=-1)


def layer_norm(x, g=None, b=None):
    xf = x.astype(F32)
    mu = jnp.mean(xf, -1, keepdims=True)
    var = jnp.mean(jnp.square(xf - mu), -1, keepdims=True)
    y = (xf - mu) * lax.rsqrt(var + LN_EPS)
    if g is not None:
        y = y * g + b
    return y.astype(x.dtype)


def modulate(x, shift, scale):
    return layer_norm(x) * (1 + scale) + shift


def dwconv3(x, w, b):
    xp = jnp.pad(x, ((0, 0), (1, 1), (0, 0)))
    return xp[:, :-2] * w[0] + x * w[1] + xp[:, 2:] * w[2] + b


def axial_rope(x):
    L, d = x.shape[1], x.shape[-1]
    rows = L // GRID_W
    row = jnp.repeat(jnp.arange(rows), GRID_W)
    col = jnp.tile(jnp.arange(GRID_W), rows)
    nf = d // 4
    inv = ROPE_BASE ** (-jnp.arange(nf, dtype=F32) / nf)
    bshape = (L,) + (1,) * (x.ndim - 3) + (nf,)
    xf = x.astype(F32)
    out = []
    for half, pos in enumerate((row, col)):
        ang = (pos.astype(F32)[:, None] * inv[None, :]).reshape(bshape)
        cos, sin = jnp.cos(ang), jnp.sin(ang)
        xh = xf[..., half * 2 * nf:(half + 1) * 2 * nf]
        x1, x2 = xh[..., :nf], xh[..., nf:]
        out += [x1 * cos - x2 * sin, x1 * sin + x2 * cos]
    return jnp.concatenate(out, -1).astype(x.dtype)


def sink_softmax(logit_list, sink_b):
    lead = logit_list[0].shape[:-1]
    z = jnp.concatenate([t.astype(F32) for t in logit_list]
                        + [jnp.broadcast_to(sink_b.astype(F32), lead + (1,))], -1)
    return jax.nn.softmax(z, -1)[..., :-1]


def dft_mats(L):
    k = jnp.arange(L, dtype=jnp.int32)[:, None]
    t = jnp.arange(L, dtype=jnp.int32)[None, :]
    ang = (((2 * k + 1) * t) % (4 * L)).astype(F32) * (math.pi / (2 * L))
    c, s = jnp.cos(ang).astype(BF16), jnp.sin(ang).astype(BF16)
    return c, s, c.T, s.T


def hyena_kspec(L, mats, w1, b1, w2, b2, w3, freq):
    t = jnp.linspace(0.0, 1.0, L, dtype=F32)[:, None]
    bands = (HY_EMB - 1) // 2
    w = 2.0 * math.pi * jnp.arange(L, dtype=F32) / L
    fb = jnp.linspace(1e-4, bands - 1, bands, dtype=F32)
    ang = w[:, None] * fb[None, :]
    z = jnp.concatenate([t, jnp.cos(ang), -jnp.sin(ang)], -1)
    fr = freq.astype(F32)
    hdn = jnp.sin(fr * (z @ w1.astype(F32) + b1.astype(F32)))
    hdn = jnp.sin(fr * (hdn @ w2.astype(F32) + b2.astype(F32)))
    filt = (hdn @ w3.astype(F32)).reshape(L, HY_ORDER, HY_DIRS, HY_W)
    deltas = jnp.abs(jnp.linspace(HY_MIN_DECAY, HY_MAX_DECAY, HY_W, dtype=F32))
    filt = filt * jnp.exp(-t * deltas[None, :])[:, None, None, :]
    filt = filt / jnp.sum(jnp.abs(filt), axis=(0, 2), keepdims=True)
    fwd, bwd = filt[:, :, 0], filt[:, :, 1]
    hi = jnp.concatenate([jnp.zeros_like(bwd[:1]), -bwd[:0:-1]], axis=0)
    nw = HY_ORDER * HY_W
    x = jnp.concatenate([fwd.reshape(L, nw), hi.reshape(L, nw)], axis=1)
    cx, sx = _mm(mats[0], x), _mm(mats[1], x)
    sgn = (1 - 2 * (jnp.arange(L) % 2)).astype(F32)[:, None]
    kr = cx[:, :nw] - sgn * sx[:, nw:]
    ki = -sx[:, :nw] - sgn * cx[:, nw:]

    def per_order(a):
        return jnp.transpose(a.reshape(L, HY_ORDER, HY_W), (1, 0, 2))

    return per_order(kr), per_order(ki)


def _hy_fwd_kernel(ac_ref, as_ref, z_ref, kr_ref, ki_ref, yr_ref, yi_ref):
    z = z_ref[0].astype(BF16)
    zr = jnp.dot(ac_ref[...], z, preferred_element_type=F32)
    zi = -jnp.dot(as_ref[...], z, preferred_element_type=F32)
    kr, ki = kr_ref[...], ki_ref[...]
    yr_ref[0] = (zr * kr - zi * ki).astype(BF16)
    yi_ref[0] = (zr * ki + zi * kr).astype(BF16)


def _hy_inv_kernel(inv_l, act_ref, ast_ref, yr_ref, yi_ref, z_ref, x_ref, bias_ref, o_ref):
    y = (jnp.dot(act_ref[...], yr_ref[0], preferred_element_type=F32)
         - jnp.dot(ast_ref[...], yi_ref[0], preferred_element_type=F32))
    o_ref[0] = x_ref[0] * (y * inv_l + bias_ref[...] * z_ref[0])


def hyena_longconv(z, xmul, mats, kr, ki, bias):
    B, L, W = z.shape
    ac, as_, act, ast = mats
    tk = min(HY_TK, L)
    a_spec = pl.BlockSpec((tk, L), lambda i, b: (i, 0))
    full_spec = pl.BlockSpec((1, L, W), lambda i, b: (b, 0, 0))
    tile_spec = pl.BlockSpec((1, tk, W), lambda i, b: (b, i, 0))
    k_spec = pl.BlockSpec((tk, W), lambda i, b: (i, 0))
    params = pltpu.CompilerParams(dimension_semantics=("parallel", "arbitrary"),
                                  vmem_limit_bytes=VMEM_LIMIT)
    yr, yi = pl.pallas_call(
        _hy_fwd_kernel,
        out_shape=[jax.ShapeDtypeStruct((B, L, W), BF16)] * 2,
        grid=(L // tk, B),
        in_specs=[a_spec, a_spec, full_spec, k_spec, k_spec],
        out_specs=[tile_spec, tile_spec],
        compiler_params=params,
        name="hyena_fwd",
    )(ac, as_, z, kr, ki)
    return pl.pallas_call(
        functools.partial(_hy_inv_kernel, 1.0 / L),
        out_shape=jax.ShapeDtypeStruct((B, L, W), F32),
        grid=(L // tk, B),
        in_specs=[a_spec, a_spec, full_spec, full_spec, tile_spec, tile_spec,
                  pl.BlockSpec((1, W), lambda i, b: (0, 0))],
        out_specs=tile_spec,
        compiler_params=params,
        name="hyena_inv",
    )(act, ast, yr, yi, z, xmul, bias.reshape(1, W))


def hyena_mix(p, conv_w, conv_b, mats, kspec, bias):
    v, x1, x2 = jnp.split(dwconv3(p, conv_w, conv_b), 3, -1)
    kr, ki = kspec
    zz = hyena_longconv(v, x1, mats, kr[0], ki[0], bias[0])
    return hyena_longconv(zz, x2, mats, kr[1], ki[1], bias[1])


def hyena_branch(p, pc, need_ctx, mats, mats_c, conv_w, conv_b, fw1, fb1, fw2, fb2, fw3, ffreq, bias):
    kspec = hyena_kspec(p.shape[1], mats, fw1, fb1, fw2, fb2, fw3, ffreq)
    y = hyena_mix(p, conv_w, conv_b, mats, kspec, bias)
    yc = None
    if need_ctx:
        kspec_c = hyena_kspec(pc.shape[1], mats_c, fw1, fb1, fw2, fb2, fw3, ffreq)
        yc = hyena_mix(pc, conv_w, conv_b, mats_c, kspec_c, bias)
    return y, yc


def swa_latent(q, k, v, kc, vc, sink):
    B, L = q.shape[:2]
    nb = L // SWA_BLOCK
    G = SWA_HEADS // SWA_KV
    scale = SWA_HD ** -0.5
    qb = q.reshape(B, nb, SWA_BLOCK, SWA_KV, G, SWA_HD)

    def band(t):
        tp = jnp.pad(t, ((0, 0), (SWA_BLOCK, SWA_BLOCK), (0, 0), (0, 0)))
        tp = tp.reshape(B, nb + 2, SWA_BLOCK, SWA_KV, SWA_HD)
        return jnp.concatenate([tp[:, :-2], tp[:, 1:-1], tp[:, 2:]], axis=2)

    kb, vb = band(k), band(v)
    s_loc = jnp.einsum('bnqkgd,bnskd->bnkgqs', qb, kb).astype(F32) * scale
    s_ctx = jnp.einsum('bnqkgd,bckd->bnkgqc', qb, kc).astype(F32) * scale
    blk = jnp.arange(nb)[:, None, None]
    qpos = blk * SWA_BLOCK + jnp.arange(SWA_BLOCK)[None, :, None]
    kpos = (blk - 1) * SWA_BLOCK + jnp.arange(3 * SWA_BLOCK)[None, None, :]
    valid = (jnp.abs(kpos - qpos) <= SWA_WIN) & (kpos >= 0) & (kpos < L)
    s_loc = jnp.where(valid[None, :, None, None], s_loc, NEG_INF)
    p = sink_softmax([s_loc, s_ctx], sink.reshape(SWA_KV, G)[:, :, None, None])
    nl = 3 * SWA_BLOCK
    o = (jnp.einsum('bnkgqs,bnskd->bnqkgd', p[..., :nl].astype(v.dtype), vb)
         + jnp.einsum('bnkgqc,bckd->bnqkgd', p[..., nl:].astype(v.dtype), vc))
    return o.reshape(B, L, SWA_HEADS * SWA_HD)


def swa_context(qc, kc, vc, sink):
    B, C = qc.shape[:2]
    G = SWA_HEADS // SWA_KV
    qg = qc.reshape(B, C, SWA_KV, G, SWA_HD)
    s = jnp.einsum('bqkgd,bckd->bkgqc', qg, kc).astype(F32) * SWA_HD ** -0.5
    p = sink_softmax([s], sink.reshape(SWA_KV, G)[:, :, None, None])
    o = jnp.einsum('bkgqc,bckd->bqkgd', p.astype(vc.dtype), vc)
    return o.reshape(B, C, SWA_HEADS * SWA_HD)


def swa_branch(p, pc, need_ctx, sink):
    B, L = p.shape[:2]
    C = pc.shape[1]
    q, k, v = split_at(p, SWA_SIZES)
    qc, kc, vc = split_at(pc, SWA_SIZES)
    q = axial_rope(q.reshape(B, L, SWA_HEADS, SWA_HD))
    k = axial_rope(k.reshape(B, L, SWA_KV, SWA_HD))
    v = v.reshape(B, L, SWA_KV, SWA_HD)
    kc = kc.reshape(B, C, SWA_KV, SWA_HD)
    vc = vc.reshape(B, C, SWA_KV, SWA_HD)
    y = swa_latent(q, k, v, kc, vc, sink)
    yc = swa_context(qc.reshape(B, C, SWA_HEADS, SWA_HD), kc, vc, sink) if need_ctx else None
    return y, yc


def rwkv_streams(p, mu, w0, w2, a0, a2, g2, k_k, k_a):
    B, L = p.shape[:2]
    pp = jnp.pad(p, ((0, 0), (1, 1), (0, 0)))
    p = p + (0.5 * (pp[:, :-2] + pp[:, 2:]) - p) * mu
    r, k, v, wd_f, wd_b, ad, gd_f, gd_b = split_at(p, RW_SIZES)

    def heads(t):
        return t.reshape(B, L, RW_HEADS, RW_HD)

    def decay(wd, w0d, w2d):
        wlog = -jax.nn.softplus(-(w0d + jnp.tanh(wd) @ w2d)) - 0.5
        return jnp.exp(-jnp.exp(wlog.astype(F32)))

    a = jax.nn.sigmoid(a0 + ad @ a2)
    kk = heads(k * k_k).astype(F32)
    kk = kk / jnp.maximum(jnp.sqrt(jnp.sum(kk * kk, -1, keepdims=True)), 1e-12)
    k = k * (1 + (a - 1) * k_a)
    g_f = jax.nn.sigmoid(gd_f) @ g2[0]
    g_b = jax.nn.sigmoid(gd_b) @ g2[1]
    return (heads(r), heads(k), heads(v), kk, heads(a),
            heads(decay(wd_f, w0[0], w2[0])), heads(decay(wd_b, w0[1], w2[1])),
            heads(g_f), heads(g_b))


def _rwkv_scan_kernel(tblk, nb,
                      wf_ref, kkf_ref, bf_ref, kf_ref, vf_ref, rf_ref,
                      wb_ref, kkb_ref, bb_ref, kb_ref, vb_ref, rb_ref,
                      of_ref, ob_ref, sf_ref, sb_ref):
    c = pl.program_id(0)

    @pl.when(c == 0)
    def _():
        sf_ref[...] = jnp.zeros_like(sf_ref)
        sb_ref[...] = jnp.zeros_like(sb_ref)

    row = lax.broadcasted_iota(jnp.int32, (RW_W, RW_W), 0)
    col = lax.broadcasted_iota(jnp.int32, (RW_W, RW_W), 1)
    ones_blk = jnp.where(row // RW_HD == col // RW_HD, 1.0, 0.0).astype(BF16)
    r64 = lax.broadcasted_iota(jnp.int32, (RW_HD, RW_W), 0)
    c64 = lax.broadcasted_iota(jnp.int32, (RW_HD, RW_W), 1)
    eye_t = jnp.where(r64 == c64 % RW_HD, 1.0, 0.0).astype(F32)
    r8 = lax.broadcasted_iota(jnp.int32, (8, RW_W), 0)
    c8 = lax.broadcasted_iota(jnp.int32, (8, RW_W), 1)
    head_sel = jnp.where(r8 == c8 // RW_HD, 1.0, 0.0).astype(BF16)

    fwd = (wf_ref, kkf_ref, bf_ref, kf_ref, vf_ref, rf_ref, sf_ref)
    bwd = (wb_ref, kkb_ref, bb_ref, kb_ref, vb_ref, rb_ref, sb_ref)
    nrow = nb * RW_HD

    def reduce_operand(idx, refs):
        w_ref, kk_ref, b_ref, k_ref, v_ref, r_ref, s_ref = refs
        p1 = [(s_ref[b] * kk_ref[idx, b:b + 1, :]).astype(BF16) for b in range(nb)]
        vm = [(eye_t * v_ref[idx, b:b + 1, :]).astype(BF16) for b in range(nb)]
        return jnp.concatenate(p1 + vm, axis=0)

    def update(idx, refs, red):
        w_ref, kk_ref, b_ref, k_ref, v_ref, r_ref, s_ref = refs
        zs = []
        for b in range(nb):
            sa = red[b * RW_HD:(b + 1) * RW_HD]
            vc = red[nrow + b * RW_HD:nrow + (b + 1) * RW_HD]
            s_new = (s_ref[b] * w_ref[idx, b:b + 1, :] - sa * b_ref[idx, b:b + 1, :]
                     + vc * k_ref[idx, b:b + 1, :])
            s_ref[b] = s_new
            zs.append((s_new * r_ref[idx, b:b + 1, :]).astype(BF16))
        return jnp.concatenate(zs, axis=0)

    def emit(idx, z, o_ref):
        o_ref[idx] = lax.dot_general(head_sel, z, (((1,), (1,)), ((), ())),
                                     preferred_element_type=F32)

    def step(i, carry):
        lhs_f, z_f, z_b = carry
        ib = tblk - 1 - i
        red_f = jnp.dot(lhs_f, ones_blk, preferred_element_type=F32)
        red_b = jnp.dot(reduce_operand(ib, bwd), ones_blk, preferred_element_type=F32)
        emit(jnp.maximum(i - 1, 0), z_f, of_ref)
        emit(jnp.minimum(ib + 1, tblk - 1), z_b, ob_ref)
        z_f = update(i, fwd, red_f)
        lhs_f = reduce_operand(jnp.minimum(i + 1, tblk - 1), fwd)
        z_b = update(ib, bwd, red_b)
        return lhs_f, z_f, z_b

    z0 = jnp.zeros((nrow, RW_W), BF16)
    _, z_f, z_b = lax.fori_loop(0, tblk, step, (reduce_operand(0, fwd), z0, z0))
    emit(tblk - 1, z_f, of_ref)
    emit(0, z_b, ob_ref)


def rwkv_scan_pallas(w_f, w_b, kk, bvec, k, v, r, n_ctx):
    n, nb, _ = kk.shape
    tblk = RW_TBLK
    nblk = n // tblk
    nblk_ctx = n_ctx // tblk

    def fmap(c):
        return (c, 0, 0)

    def bmap(c):
        return (jnp.where(c < nblk_ctx, nblk_ctx - 1 - c, nblk + nblk_ctx - 1 - c), 0, 0)

    blk = (tblk, nb, RW_W)
    oblk = (tblk, 8, nb * RW_HD)
    return pl.pallas_call(
        functools.partial(_rwkv_scan_kernel, tblk, nb),
        out_shape=[jax.ShapeDtypeStruct((n, 8, nb * RW_HD), F32)] * 2,
        grid=(nblk,),
        in_specs=[pl.BlockSpec(blk, fmap)] * 6 + [pl.BlockSpec(blk, bmap)] * 6,
        out_specs=[pl.BlockSpec(oblk, fmap), pl.BlockSpec(oblk, bmap)],
        scratch_shapes=[pltpu.VMEM((nb, RW_HD, RW_W), F32), pltpu.VMEM((nb, RW_HD, RW_W), F32)],
        compiler_params=pltpu.CompilerParams(dimension_semantics=("arbitrary",),
                                             vmem_limit_bytes=VMEM_LIMIT),
        name="rwkv_scan",
    )(w_f, kk, bvec, k, v, r, w_b, kk, bvec, k, v, r)


def rwkv_out(o_f, o_b, r, k, v, g_f, g_b, r_k, lnx_g, lnx_b):
    B, L = r.shape[:2]
    gam = lnx_g.reshape(RW_HEADS, RW_HD)
    bet = lnx_b.reshape(RW_HEADS, RW_HD)

    def gn(o):
        mu = jnp.mean(o, -1, keepdims=True)
        var = jnp.mean(jnp.square(o - mu), -1, keepdims=True)
        return (o - mu) * lax.rsqrt(var + RW_GN_EPS) * gam + bet

    bonus = jnp.sum(r * k * r_k, -1, keepdims=True) * v
    y = (gn(o_f) + bonus) * g_f + (gn(o_b) + bonus) * g_b
    return y.reshape(B, L, RW_W).astype(v.dtype)


def rwkv_branch(p, pc, need_ctx, mu, w0, w2, a0, a2, g2, k_k, k_a, r_k, lnx_g, lnx_b):
    r, k, v, kk, a, d_f, d_b, g_f, g_b = rwkv_streams(p, mu, w0, w2, a0, a2, g2, k_k, k_a)
    rc, kc, vc, kkc, ac, dc_f, dc_b, gc_f, gc_b = rwkv_streams(pc, mu, w0, w2, a0, a2, g2, k_k, k_a)
    B, L = p.shape[:2]
    C = pc.shape[1]

    def tmajor(xc, xl):
        t = jnp.concatenate([xc, xl], axis=1).astype(F32).reshape(B, C + L, RW_W)
        return jnp.moveaxis(t, 1, 0)

    pf, pb = rwkv_scan_pallas(tmajor(dc_f, d_f), tmajor(dc_b, d_b), tmajor(kkc, kk),
                              tmajor(kkc * ac, kk * a), tmajor(kc, k), tmajor(vc, v),
                              tmajor(rc, r), C)

    def bmajor(t):
        t = t[:, :RW_HEADS].reshape(C + L, RW_HEADS, B, RW_HD)
        return jnp.transpose(t, (2, 0, 1, 3))

    pf, pb = bmajor(pf), bmajor(pb)
    oc_f, o_f = pf[:, :C], pf[:, C:]
    oc_b, o_b = pb[:, :C], pb[:, C:]
    y = rwkv_out(o_f, o_b, r, k, v, g_f, g_b, r_k, lnx_g, lnx_b)
    yc = rwkv_out(oc_f, oc_b, rc, kc, vc, gc_f, gc_b, r_k, lnx_g, lnx_b) if need_ctx else None
    return y, yc


def _diff_attn_kernel(q_ref, kt_ref, v_ref, lam_ref, gain_ref, o_ref):
    q = q_ref[0]
    v = v_ref[0]
    lam = lam_ref[...]
    tq = q.shape[0]
    head_of_lane = lax.broadcasted_iota(jnp.int32, (1, DF_W), 1) // DF_VD
    acc = jnp.zeros((tq, DF_W), F32)
    for h in range(DF_HEADS):
        parts = []
        for c in range(2):
            j = 2 * h + c
            s = jnp.dot(q[:, DF_HD * j:DF_HD * (j + 1)], kt_ref[0, j],
                        preferred_element_type=F32)
            p = jnp.exp2(s - jnp.max(s, -1, keepdims=True))
            l = jnp.sum(p, -1, keepdims=True)
            pv = jnp.dot(p.astype(BF16), v, preferred_element_type=F32)
            parts.append(pv / l)
        acc = jnp.where(head_of_lane == h, parts[0] - lam * parts[1], acc)
    row = lax.broadcasted_iota(jnp.int32, (DF_W, DF_W), 0)
    col = lax.broadcasted_iota(jnp.int32, (DF_W, DF_W), 1)
    seg_mean = jnp.where(row // DF_VD == col // DF_VD, 1.0 / DF_VD, 0.0).astype(F32)
    ms = jnp.dot(acc * acc, seg_mean, preferred_element_type=F32, precision=lax.Precision.HIGHEST)
    o_ref[0] = acc * lax.rsqrt(ms + 1e-5) * gain_ref[...]


def diff_attend(q, k, v, lam, lam_init, subln_g):
    B, Lq = q.shape[:2]
    S = k.shape[1]
    tq = min(DF_TQ, Lq)
    qs = (q.astype(F32) * (DF_HD ** -0.5 * math.log2(math.e))).reshape(B, Lq, DF_W).astype(BF16)
    kt = jnp.transpose(k.reshape(B, S, 2 * DF_HEADS, DF_HD), (0, 2, 3, 1)).astype(BF16)
    vb = v.reshape(B, S, DF_W).astype(BF16)
    gain = jnp.tile(subln_g.astype(F32) * (1.0 - lam_init), DF_HEADS).reshape(1, DF_W)
    return pl.pallas_call(
        _diff_attn_kernel,
        out_shape=jax.ShapeDtypeStruct((B, Lq, DF_W), F32),
        grid=(B, Lq // tq),
        in_specs=[pl.BlockSpec((1, tq, DF_W), lambda b, i: (b, i, 0)),
                  pl.BlockSpec((1, 2 * DF_HEADS, DF_HD, S), lambda b, i: (b, 0, 0, 0)),
                  pl.BlockSpec((1, S, DF_W), lambda b, i: (b, 0, 0)),
                  pl.BlockSpec((1, 1), lambda b, i: (0, 0)),
                  pl.BlockSpec((1, DF_W), lambda b, i: (0, 0))],
        out_specs=pl.BlockSpec((1, tq, DF_W), lambda b, i: (b, i, 0)),
        compiler_params=pltpu.CompilerParams(
            dimension_semantics=("parallel", "parallel"), vmem_limit_bytes=VMEM_LIMIT),
        name="diff_attn",
    )(qs, kt, vb, lam.reshape(1, 1).astype(F32), gain)


def diff_branch(p, pc, need_ctx, lq1, lk1, lq2, lk2, subln_g, lam_init):
    q, k, v = split_at(p, DF_SIZES)
    qc, kc, vc = split_at(pc, DF_SIZES)

    def qk(t):
        return t.reshape(t.shape[0], t.shape[1], DF_HEADS, 2, DF_HD)

    def vs(t):
        return t.reshape(t.shape[0], t.shape[1], DF_HEADS, DF_VD)

    lam = (jnp.exp(jnp.sum(lq1.astype(F32) * lk1.astype(F32)))
           - jnp.exp(jnp.sum(lq2.astype(F32) * lk2.astype(F32))) + lam_init)
    k_all = jnp.concatenate([axial_rope(qk(k)), qk(kc)], axis=1)
    v_all = jnp.concatenate([vs(v), vs(vc)], axis=1)
    y = diff_attend(axial_rope(qk(q)), k_all, v_all, lam, lam_init, subln_g)
    yc = diff_attend(qk(qc), qk(kc), vs(vc), lam, lam_init, subln_g) if need_ctx else None
    return y, yc


def _ln_rows(x):
    mu = jnp.mean(x, -1, keepdims=True)
    xc = x - mu
    return xc * lax.rsqrt(jnp.mean(xc * xc, -1, keepdims=True) + LN_EPS)


def _ffn_kernel(tm, tiles_per_seq, n_chunks,
                h_ref, hprev_ref, hnext_ref, sh_ref, sc_ref, gate_ref,
                wa_ref, wb_ref, cwa_ref, cwb_ref, cba_ref, cbb_ref, wd_ref, lng_ref, lnb_ref,
                o_ref, u_ref, acc_ref):
    i = pl.program_id(0)
    j = pl.program_id(1)
    n = tm + 2 * HALO

    @pl.when(j == 0)
    def _():
        scale = 1.0 + sc_ref[0]
        shift = sh_ref[0]
        first = (i % tiles_per_seq) == 0
        last = (i % tiles_per_seq) == tiles_per_seq - 1
        u_ref[0:tm, :] = (_ln_rows(h_ref[...]) * scale + shift).astype(BF16)
        un = _ln_rows(hnext_ref[...]) * scale + shift
        up = _ln_rows(hprev_ref[...]) * scale + shift
        u_ref[tm:tm + HALO, :] = jnp.where(last, 0.0, un).astype(BF16)
        u_ref[tm + HALO:n, :] = jnp.where(first, 0.0, up).astype(BF16)
        acc_ref[...] = jnp.zeros_like(acc_ref)

    u = u_ref[...]

    def conv(w_ref, cw_ref, cb_ref):
        x = jnp.dot(u, w_ref[...], preferred_element_type=F32)
        cw = cw_ref[...]
        y = (pltpu.roll(x, 1, 0)[0:tm] * cw[0:1] + x[0:tm] * cw[1:2]
             + pltpu.roll(x, n - 1, 0)[0:tm] * cw[2:3] + cb_ref[...])
        return y

    a = conv(wa_ref, cwa_ref, cba_ref)
    b = conv(wb_ref, cwb_ref, cbb_ref)
    g = (a * jax.nn.sigmoid(a) * b).astype(BF16)
    acc_ref[...] += jnp.dot(g, wd_ref[...], preferred_element_type=F32)

    @pl.when(j == n_chunks - 1)
    def _():
        y = DN_ALPHA * h_ref[...] + gate_ref[0] * acc_ref[...]
        o_ref[...] = _ln_rows(y) * lng_ref[...] + lnb_ref[...]


def ffn_block(h, shift, scale, gate, w_up, conv_w, conv_b, w_down, ln_g, ln_b):
    B, L, D = h.shape
    tm = min(FFN_TM, L)
    tiles_per_seq = L // tm
    n_tok = B * L
    n_chunks = D_FF // FFN_CHUNK
    hb = tm // HALO

    def bcast(t):
        return jnp.broadcast_to(t.astype(F32), (B, 1, D))

    wu = w_up.astype(BF16)
    wd = w_down.astype(BF16)
    cb = conv_b.reshape(1, 2 * D_FF)
    h2 = h.reshape(n_tok, D)
    mod_spec = pl.BlockSpec((1, 1, D), lambda i, j: (i // tiles_per_seq, 0, 0))
    vec_spec = pl.BlockSpec((1, D), lambda i, j: (0, 0))
    out = pl.pallas_call(
        functools.partial(_ffn_kernel, tm, tiles_per_seq, n_chunks),
        out_shape=jax.ShapeDtypeStruct((n_tok, D), F32),
        grid=(n_tok // tm, n_chunks),
        in_specs=[pl.BlockSpec((tm, D), lambda i, j: (i, 0)),
                  pl.BlockSpec((HALO, D), lambda i, j: (jnp.maximum(i * hb - 1, 0), 0)),
                  pl.BlockSpec((HALO, D), lambda i, j: (jnp.minimum((i + 1) * hb, n_tok // HALO - 1), 0)),
                  mod_spec, mod_spec, mod_spec,
                  pl.BlockSpec((D, FFN_CHUNK), lambda i, j: (0, j)),
                  pl.BlockSpec((D, FFN_CHUNK), lambda i, j: (0, n_chunks + j)),
                  pl.BlockSpec((3, FFN_CHUNK), lambda i, j: (0, j)),
                  pl.BlockSpec((3, FFN_CHUNK), lambda i, j: (0, n_chunks + j)),
                  pl.BlockSpec((1, FFN_CHUNK), lambda i, j: (0, j)),
                  pl.BlockSpec((1, FFN_CHUNK), lambda i, j: (0, n_chunks + j)),
                  pl.BlockSpec((FFN_CHUNK, D), lambda i, j: (j, 0)),
                  vec_spec, vec_spec],
        out_specs=pl.BlockSpec((tm, D), lambda i, j: (i, 0)),
        scratch_shapes=[pltpu.VMEM((tm + 2 * HALO, D), BF16), pltpu.VMEM((tm, D), F32)],
        compiler_params=pltpu.CompilerParams(
            dimension_semantics=("parallel", "arbitrary"), vmem_limit_bytes=VMEM_LIMIT),
        name="conv_ffn",
    )(h2, h2, h2, bcast(shift), bcast(scale), bcast(gate), wu, wu, conv_w, conv_w, cb, cb, wd,
      ln_g.reshape(1, D), ln_b.reshape(1, D))
    return out.reshape(B, L, D)


P_GATE, P_HY, P_SW, P_RW, P_DF = 0, 4096, 4864, 5376, 6656
P_COLS = 7680
IN_CHUNK = 512


def _in_proj_kernel(x_ref, sh_ref, sc_ref, w_ref, o_ref, u_ref):
    @pl.when(pl.program_id(1) == 0)
    def _():
        u_ref[...] = (_ln_rows(x_ref[...]) * (1.0 + sc_ref[0]) + sh_ref[0]).astype(BF16)

    o_ref[...] = jnp.dot(u_ref[...], w_ref[...], preferred_element_type=F32)


def in_proj(h, shift, scale, w_in_p):
    B, L, D = h.shape
    tm = min(FFN_TM, L)
    tiles_per_seq = L // tm
    n_tok = B * L

    def bcast(t):
        return jnp.broadcast_to(t.astype(F32), (B, 1, D))

    mod_spec = pl.BlockSpec((1, 1, D), lambda i, j: (i // tiles_per_seq, 0, 0))
    return pl.pallas_call(
        _in_proj_kernel,
        out_shape=jax.ShapeDtypeStruct((n_tok, P_COLS), F32),
        grid=(n_tok // tm, P_COLS // IN_CHUNK),
        in_specs=[pl.BlockSpec((tm, D), lambda i, j: (i, 0)), mod_spec, mod_spec,
                  pl.BlockSpec((D, IN_CHUNK), lambda i, j: (0, j))],
        out_specs=pl.BlockSpec((tm, IN_CHUNK), lambda i, j: (i, j)),
        scratch_shapes=[pltpu.VMEM((tm, D), BF16)],
        compiler_params=pltpu.CompilerParams(
            dimension_semantics=("parallel", "arbitrary"), vmem_limit_bytes=VMEM_LIMIT),
        name="in_proj",
    )(h.reshape(n_tok, D), bcast(shift), bcast(scale), w_in_p)


def pack_w_in(w):
    offs = np.cumsum((0,) + IN_SIZES)
    hy, sw, rw, df, gt = [w[:, offs[j]:offs[j + 1]] for j in range(len(IN_SIZES))]

    def padto(t, n):
        return jnp.pad(t, ((0, 0), (0, n - t.shape[1])))

    return jnp.concatenate([gt, hy, sw, padto(rw, P_DF - P_RW), padto(df, P_COLS - P_DF)],
                           axis=1).astype(BF16)


def _merge_kernel(yh_ref, ys_ref, yr_ref, yd_ref, g0_ref, g1_ref, g2_ref, g3_ref, h_ref, gate_ref,
                  wbr_ref, wo_ref, lng_ref, lnb_ref, o_ref):
    acc = None
    for j, (y_ref, g_ref) in enumerate(((yh_ref, g0_ref), (ys_ref, g1_ref), (yr_ref, g2_ref),
                                        (yd_ref, g3_ref))):
        term = jax.nn.sigmoid(g_ref[...]) * jnp.dot(y_ref[...].astype(BF16), wbr_ref[j],
                                                    preferred_element_type=F32)
        acc = term if acc is None else acc + term
    mix = jnp.dot(acc.astype(BF16), wo_ref[...], preferred_element_type=F32)
    y = DN_ALPHA * h_ref[...] + gate_ref[0] * mix
    o_ref[...] = _ln_rows(y) * lng_ref[...] + lnb_ref[...]


def merge_block(ys, p, h, gate, w_br, w_o, ln_g, ln_b):
    B, L, D = h.shape
    n_tok = B * L
    tm = min(MERGE_TM, L)
    tiles_per_seq = L // tm
    y_spec = pl.BlockSpec((tm, BR_W), lambda i: (i, 0))
    vec_spec = pl.BlockSpec((1, D), lambda i: (0, 0))
    g_specs = [pl.BlockSpec((tm, D), functools.partial(lambda j, i: (i, j), j)) for j in range(N_BRANCH)]
    out = pl.pallas_call(
        _merge_kernel,
        out_shape=jax.ShapeDtypeStruct((n_tok, D), F32),
        grid=(n_tok // tm,),
        in_specs=[y_spec] * N_BRANCH + g_specs + [
            pl.BlockSpec((tm, D), lambda i: (i, 0)),
            pl.BlockSpec((1, 1, D), lambda i: (i // tiles_per_seq, 0, 0)),
            pl.BlockSpec((N_BRANCH, BR_W, D), lambda i: (0, 0, 0)),
            pl.BlockSpec((D, D), lambda i: (0, 0)), vec_spec, vec_spec],
        out_specs=pl.BlockSpec((tm, D), lambda i: (i, 0)),
        compiler_params=pltpu.CompilerParams(
            dimension_semantics=("parallel",), vmem_limit_bytes=VMEM_LIMIT),
        name="merge",
    )(*[y.reshape(n_tok, BR_W) for y in ys], p, p, p, p, h.reshape(n_tok, D),
      jnp.broadcast_to(gate.astype(F32), (B, 1, D)), w_br.astype(BF16), w_o.astype(BF16),
      ln_g.reshape(1, D), ln_b.reshape(1, D))
    return out.reshape(B, L, D)


def kernel(x, c, ctx, c_ctx, ada_w, ada_b, w_in, hy_conv_w, hy_conv_b, hy_f_w1, hy_f_b1,
           hy_f_w2, hy_f_b2, hy_f_w3, hy_f_freq, hy_bias, swa_sink, rwkv_mu, rwkv_w0, rwkv_w2,
           rwkv_a0, rwkv_a2, rwkv_g2, rwkv_kk, rwkv_ka, rwkv_rk, rwkv_lnx_g, rwkv_lnx_b,
           diff_lq1, diff_lk1, diff_lq2, diff_lk2, diff_subln_g, w_branch, w_out, ln1_g, ln1_b,
           ffn_w_up, ffn_conv_w, ffn_conv_b, ffn_w_down, ln2_g, ln2_b):
    h, hc = x, ctx
    mats, mats_c = dft_mats(x.shape[1]), dft_mats(ctx.shape[1])
    s_lat = jax.nn.silu(c)
    s_ctx = jax.nn.silu(c_ctx)
    for i in range(DEPTH):
        need_ctx = i < DEPTH - 1
        mod = (s_lat @ ada_w[i] + ada_b[i])[:, None, :]
        mod_c = s_ctx @ ada_w[i] + ada_b[i]
        sh1, sc1, g1, sh2, sc2, g2 = jnp.split(mod, 6, -1)
        csh1, csc1, cg1, csh2, csc2, cg2 = jnp.split(mod_c, 6, -1)

        B, L, D = h.shape
        C = hc.shape[1]
        w_in_p = pack_w_in(w_in[i])
        p = in_proj(h, sh1, sc1, w_in_p)
        pc = in_proj(hc, csh1.reshape(1, 1, D), csc1.reshape(1, 1, D), w_in_p)

        def seg(t, n, off, width):
            return t[:, off:off + width].reshape(B, n, width)

        p_hy, p_sw, p_rw, p_df = [seg(p, L, o, w) for o, w in zip((P_HY, P_SW, P_RW, P_DF), IN_SIZES[:4])]
        pc_hy, pc_sw, pc_rw, pc_df = [seg(pc, C, o, w) for o, w in zip((P_HY, P_SW, P_RW, P_DF), IN_SIZES[:4])]

        y_hy, yc_hy = hyena_branch(p_hy, pc_hy, need_ctx, mats, mats_c, hy_conv_w[i], hy_conv_b[i], hy_f_w1[i],
                                   hy_f_b1[i], hy_f_w2[i], hy_f_b2[i], hy_f_w3[i], hy_f_freq[i],
                                   hy_bias[i])
        y_sw, yc_sw = swa_branch(p_sw, pc_sw, need_ctx, swa_sink[i])
        y_rw, yc_rw = rwkv_branch(p_rw, pc_rw, need_ctx, rwkv_mu[i], rwkv_w0[i], rwkv_w2[i],
                                  rwkv_a0[i], rwkv_a2[i], rwkv_g2[i], rwkv_kk[i], rwkv_ka[i],
                                  rwkv_rk[i], rwkv_lnx_g[i], rwkv_lnx_b[i])
        lam_init = 0.8 - 0.6 * math.exp(-0.3 * i)
        y_df, yc_df = diff_branch(p_df, pc_df, need_ctx, diff_lq1[i], diff_lk1[i], diff_lq2[i],
                                  diff_lk2[i], diff_subln_g[i], lam_init)

        h = merge_block((y_hy, y_sw, y_rw, y_df), p, h, g1, w_branch[i], w_out[i], ln1_g[i], ln1_b[i])
        h = ffn_block(h, sh2, sc2, g2, ffn_w_up[i], ffn_conv_w[i], ffn_conv_b[i], ffn_w_down[i],
                      ln2_g[i], ln2_b[i])

        if need_ctx:
            hc = merge_block((yc_hy, yc_sw, yc_rw, yc_df), pc, hc, cg1.reshape(1, 1, D), w_branch[i],
                             w_out[i], ln1_g[i], ln1_b[i])
            hc = ffn_block(hc, csh2.reshape(1, 1, -1), csc2.reshape(1, 1, -1), cg2.reshape(1, 1, -1),
                           ffn_w_up[i], ffn_conv_w[i], ffn_conv_b[i], ffn_w_down[i], ln2_g[i], ln2_b[i])
    return h
```

```python
import functools
import math

import jax
import jax.numpy as jnp
import numpy as np
from jax import lax
from jax.experimental import pallas as pl
from jax.experimental.pallas import tpu as pltpu

D_MODEL = 1024
BATCH = 8
SEQ = 4096
DEPTH = 2
CTX_LEN = 256
GRID_W = 64
ROPE_BASE = 10000.0
F32 = jnp.float32
BF16 = jnp.bfloat16
NEG_INF = -1e30
LN_EPS = 1e-6

HY_W = 256
HY_ORDER = 2
HY_DIRS = 2
HY_EMB = 33
HY_FFN = 64
HY_MIN_DECAY = math.log(1e-2) / 1.5
HY_MAX_DECAY = math.log(1e-2) / 0.3

SWA_HEADS = 4
SWA_KV = 2
SWA_HD = 64
SWA_WIN = 128
SWA_BLOCK = 128

RW_HEADS = 4
RW_HD = 64
RW_W = RW_HEADS * RW_HD
RW_DECAY_R = 64
RW_AAA_R = 64
RW_GATE_R = 128
RW_GN_EPS = 64e-5

DF_HEADS = 4
DF_HD = 32
DF_VD = 2 * DF_HD
DF_W = DF_HEADS * DF_VD
DF_BLOCK = 128

N_BRANCH = 4
BR_W = 256
D_FF = 2816
DN_ALPHA = (2 * DEPTH) ** 0.25
DN_BETA = (8 * DEPTH) ** -0.25

SWA_SIZES = (SWA_HEADS * SWA_HD, SWA_KV * SWA_HD, SWA_KV * SWA_HD)
RW_SIZES = (RW_W, RW_W, RW_W, RW_DECAY_R, RW_DECAY_R, RW_AAA_R, RW_GATE_R, RW_GATE_R)
DF_SIZES = (DF_W, DF_W, DF_W)
IN_SIZES = (3 * HY_W, sum(SWA_SIZES), sum(RW_SIZES), sum(DF_SIZES), N_BRANCH * D_MODEL)

LANES = 128
MM_TM = 512
MM_TN = 512
RW_TBLK = 64
DF_TQ = 256
HALO = 8
FFN_TM = 1024
FFN_CHUNK = 256
MERGE_TM = 512
HY_TK = 512
RW_PREP_TM = 512
VMEM_LIMIT = 48 * 1024 * 1024


def _mm_kernel(x_ref, w_ref, o_ref):
    o_ref[...] = jnp.dot(x_ref[...].astype(BF16), w_ref[...], preferred_element_type=F32)


def _mm(x, w):
    m, k = x.shape
    n = w.shape[1]
    n_pad = -(-n // LANES) * LANES
    wb = w.astype(BF16)
    if n_pad != n:
        wb = jnp.pad(wb, ((0, 0), (0, n_pad - n)))
    tn = MM_TN if n_pad % MM_TN == 0 else (256 if n_pad % 256 == 0 else LANES)
    tm = MM_TM if m % MM_TM == 0 else m
    out = pl.pallas_call(
        _mm_kernel,
        out_shape=jax.ShapeDtypeStruct((m, n_pad), F32),
        grid=(m // tm, n_pad // tn),
        in_specs=[pl.BlockSpec((tm, k), lambda i, j: (i, 0)),
                  pl.BlockSpec((k, tn), lambda i, j: (0, j))],
        out_specs=pl.BlockSpec((tm, tn), lambda i, j: (i, j)),
        compiler_params=pltpu.CompilerParams(
            dimension_semantics=("parallel", "parallel"), vmem_limit_bytes=VMEM_LIMIT),
        name="matmul",
    )(x, wb)
    return out[:, :n] if n_pad != n else out


def _mm3(x, w):
    b, l, k = x.shape
    return _mm(x.reshape(b * l, k), w).reshape(b, l, w.shape[1])


def split_at(t, sizes):
    return jnp.split(t, np.cumsum(sizes)[:-1].tolist(), axis=-1)


def layer_norm(x, g=None, b=None):
    xf = x.astype(F32)
    mu = jnp.mean(xf, -1, keepdims=True)
    var = jnp.mean(jnp.square(xf - mu), -1, keepdims=True)
    y = (xf - mu) * lax.rsqrt(var + LN_EPS)
    if g is not None:
        y = y * g + b
    return y.astype(x.dtype)


def modulate(x, shift, scale):
    return layer_norm(x) * (1 + scale) + shift


def dwconv3(x, w, b):
    xp = jnp.pad(x, ((0, 0), (1, 1), (0, 0)))
    return xp[:, :-2] * w[0] + x * w[1] + xp[:, 2:] * w[2] + b


def axial_rope(x):
    L, d = x.shape[1], x.shape[-1]
    rows = L // GRID_W
    row = jnp.repeat(jnp.arange(rows), GRID_W)
    col = jnp.tile(jnp.arange(GRID_W), rows)
    nf = d // 4
    inv = ROPE_BASE ** (-jnp.arange(nf, dtype=F32) / nf)
    bshape = (L,) + (1,) * (x.ndim - 3) + (nf,)
    xf = x.astype(F32)
    out = []
    for half, pos in enumerate((row, col)):
        ang = (pos.astype(F32)[:, None] * inv[None, :]).reshape(bshape)
        cos, sin = jnp.cos(ang), jnp.sin(ang)
        xh = xf[..., half * 2 * nf:(half + 1) * 2 * nf]
        x1, x2 = xh[..., :nf], xh[..., nf:]
        out += [x1 * cos - x2 * sin, x1 * sin + x2 * cos]
    return jnp.concatenate(out, -1).astype(x.dtype)


def sink_softmax(logit_list, sink_b):
    lead = logit_list[0].shape[:-1]
    z = jnp.concatenate([t.astype(F32) for t in logit_list]
                        + [jnp.broadcast_to(sink_b.astype(F32), lead + (1,))], -1)
    return jax.nn.softmax(z, -1)[..., :-1]


def dft_mats(L):
    tb = 64
    k = jnp.arange(L, dtype=jnp.int32)[:, None]

    def table(t):
        ang = (((2 * k + 1) * t[None, :]) % (4 * L)).astype(F32) * (math.pi / (2 * L))
        return jnp.cos(ang), jnp.sin(ang)

    ca, sa = table(tb * jnp.arange(L // tb, dtype=jnp.int32))
    cb, sb = table(jnp.arange(tb, dtype=jnp.int32))
    c = (ca[:, :, None] * cb[:, None, :] - sa[:, :, None] * sb[:, None, :]).reshape(L, L)
    s = (sa[:, :, None] * cb[:, None, :] + ca[:, :, None] * sb[:, None, :]).reshape(L, L)
    ct = (ca.T[:, None, :] * cb.T[None, :, :] - sa.T[:, None, :] * sb.T[None, :, :]).reshape(L, L)
    st = (sa.T[:, None, :] * cb.T[None, :, :] + ca.T[:, None, :] * sb.T[None, :, :]).reshape(L, L)
    return c.astype(BF16), s.astype(BF16), ct.astype(BF16), st.astype(BF16)


def hyena_kspec(L, mats, w1, b1, w2, b2, w3, freq):
    t = jnp.linspace(0.0, 1.0, L, dtype=F32)[:, None]
    bands = (HY_EMB - 1) // 2
    w = 2.0 * math.pi * jnp.arange(L, dtype=F32) / L
    fb = jnp.linspace(1e-4, bands - 1, bands, dtype=F32)
    ang = w[:, None] * fb[None, :]
    z = jnp.concatenate([t, jnp.cos(ang), -jnp.sin(ang)], -1)
    fr = freq.astype(F32)
    hdn = jnp.sin(fr * (z @ w1.astype(F32) + b1.astype(F32)))
    hdn = jnp.sin(fr * (hdn @ w2.astype(F32) + b2.astype(F32)))
    filt = (hdn @ w3.astype(F32)).reshape(L, HY_ORDER, HY_DIRS, HY_W)
    deltas = jnp.abs(jnp.linspace(HY_MIN_DECAY, HY_MAX_DECAY, HY_W, dtype=F32))
    filt = filt * jnp.exp(-t * deltas[None, :])[:, None, None, :]
    filt = filt / jnp.sum(jnp.abs(filt), axis=(0, 2), keepdims=True)
    fwd, bwd = filt[:, :, 0], filt[:, :, 1]
    hi = jnp.concatenate([jnp.zeros_like(bwd[:1]), -bwd[:0:-1]], axis=0)
    nw = HY_ORDER * HY_W
    x = jnp.concatenate([fwd.reshape(L, nw), hi.reshape(L, nw)], axis=1)
    cx, sx = _mm(mats[0], x), _mm(mats[1], x)
    sgn = (1 - 2 * (jnp.arange(L) % 2)).astype(F32)[:, None]
    kr = cx[:, :nw] - sgn * sx[:, nw:]
    ki = -sx[:, :nw] - sgn * cx[:, nw:]

    def per_order(a):
        return jnp.transpose(a.reshape(L, HY_ORDER, HY_W), (1, 0, 2))

    return per_order(kr), per_order(ki)


def _hy_fwd_kernel(ac_ref, as_ref, z_ref, kr_ref, ki_ref, yr_ref, yi_ref):
    z = z_ref[0].astype(BF16)
    zr = jnp.dot(ac_ref[...], z, preferred_element_type=F32)
    zi = -jnp.dot(as_ref[...], z, preferred_element_type=F32)
    kr, ki = kr_ref[...], ki_ref[...]
    yr_ref[0] = (zr * kr - zi * ki).astype(BF16)
    yi_ref[0] = (zr * ki + zi * kr).astype(BF16)


def _hy_inv_kernel(inv_l, act_ref, ast_ref, yr_ref, yi_ref, z_ref, x_ref, bias_ref, o_ref):
    y = (jnp.dot(act_ref[...], yr_ref[0], preferred_element_type=F32)
         - jnp.dot(ast_ref[...], yi_ref[0], preferred_element_type=F32))
    o_ref[0] = x_ref[0] * (y * inv_l + bias_ref[...] * z_ref[0])


def hyena_longconv(z, xmul, mats, kr, ki, bias):
    B, L, W = z.shape
    ac, as_, act, ast = mats
    tk = min(HY_TK, L)
    a_spec = pl.BlockSpec((tk, L), lambda i, b: (i, 0))
    full_spec = pl.BlockSpec((1, L, W), lambda i, b: (b, 0, 0))
    tile_spec = pl.BlockSpec((1, tk, W), lambda i, b: (b, i, 0))
    k_spec = pl.BlockSpec((tk, W), lambda i, b: (i, 0))
    params = pltpu.CompilerParams(dimension_semantics=("parallel", "arbitrary"),
                                  vmem_limit_bytes=VMEM_LIMIT)
    yr, yi = pl.pallas_call(
        _hy_fwd_kernel,
        out_shape=[jax.ShapeDtypeStruct((B, L, W), BF16)] * 2,
        grid=(L // tk, B),
        in_specs=[a_spec, a_spec, full_spec, k_spec, k_spec],
        out_specs=[tile_spec, tile_spec],
        compiler_params=params,
        name="hyena_fwd",
    )(ac, as_, z, kr, ki)
    return pl.pallas_call(
        functools.partial(_hy_inv_kernel, 1.0 / L),
        out_shape=jax.ShapeDtypeStruct((B, L, W), F32),
        grid=(L // tk, B),
        in_specs=[a_spec, a_spec, full_spec, full_spec, tile_spec, tile_spec,
                  pl.BlockSpec((1, W), lambda i, b: (0, 0))],
        out_specs=tile_spec,
        compiler_params=params,
        name="hyena_inv",
    )(act, ast, yr, yi, z, xmul, bias.reshape(1, W))


def hyena_mix(p, conv_w, conv_b, mats, kspec, bias):
    v, x1, x2 = jnp.split(dwconv3(p, conv_w, conv_b), 3, -1)
    kr, ki = kspec
    zz = hyena_longconv(v, x1, mats, kr[0], ki[0], bias[0])
    return hyena_longconv(zz, x2, mats, kr[1], ki[1], bias[1])


def hyena_branch(p, pc, need_ctx, mats, mats_c, conv_w, conv_b, fw1, fb1, fw2, fb2, fw3, ffreq, bias):
    kspec = hyena_kspec(p.shape[1], mats, fw1, fb1, fw2, fb2, fw3, ffreq)
    y = hyena_mix(p, conv_w, conv_b, mats, kspec, bias)
    yc = None
    if need_ctx:
        kspec_c = hyena_kspec(pc.shape[1], mats_c, fw1, fb1, fw2, fb2, fw3, ffreq)
        yc = hyena_mix(pc, conv_w, conv_b, mats_c, kspec_c, bias)
    return y, yc


def rope_tables(L, d, width):
    nf = d // 4
    pos = jnp.arange(L)
    inv = ROPE_BASE ** (-jnp.arange(nf, dtype=F32) / nf)
    cs, sn = [], []
    for p in (pos // GRID_W, pos % GRID_W):
        ang = p.astype(F32)[:, None] * inv[None, :]
        cs += [jnp.cos(ang), jnp.cos(ang)]
        sn += [-jnp.sin(ang), jnp.sin(ang)]
    reps = width // d
    return jnp.tile(jnp.concatenate(cs, -1), (1, reps)), jnp.tile(jnp.concatenate(sn, -1), (1, reps))


def _rope(x, cos, sin, d):
    width = x.shape[-1]
    q = d // 4
    lane = lax.broadcasted_iota(jnp.int32, x.shape, 1)
    swapped = jnp.where(lane % (2 * q) < q, pltpu.roll(x, width - q, 1), pltpu.roll(x, q, 1))
    return x * cos + swapped * sin


def _swa_kernel(local, seq_len, q_ref, kp_ref, kc_ref, kn_ref, vp_ref, vc_ref, vn_ref,
                ck_ref, cv_ref, cos_ref, sin_ref, sink_ref, o_ref):
    n = pl.program_id(1)
    blk = SWA_BLOCK
    qscale = SWA_HD ** -0.5 * math.log2(math.e)
    q = q_ref[...]
    if local:
        q0 = pl.multiple_of(n * blk, blk)
        q = _rope(q, cos_ref[pl.ds(q0, blk), :], sin_ref[pl.ds(q0, blk), :], SWA_HD)
        ks, vs = [], []
        for j, (k_ref, v_ref) in enumerate(((kp_ref, vp_ref), (kc_ref, vc_ref), (kn_ref, vn_ref))):
            k0 = pl.multiple_of(jnp.clip(n + j - 1, 0, seq_len // blk - 1) * blk, blk)
            ks.append(_rope(k_ref[...], cos_ref[pl.ds(k0, blk), 0:SWA_KV * SWA_HD],
                            sin_ref[pl.ds(k0, blk), 0:SWA_KV * SWA_HD], SWA_HD).astype(BF16))
            vs.append(v_ref[...].astype(BF16))
        k_loc = jnp.concatenate(ks, axis=0)
        v_loc = jnp.concatenate(vs, axis=0)
        qpos = n * blk + lax.broadcasted_iota(jnp.int32, (blk, 3 * blk), 0)
        kpos = (n - 1) * blk + lax.broadcasted_iota(jnp.int32, (blk, 3 * blk), 1)
        valid = (jnp.abs(kpos - qpos) <= SWA_WIN) & (kpos >= 0) & (kpos < seq_len)
    qb = (q * qscale).astype(BF16)
    ck = ck_ref[...].astype(BF16)
    cv = cv_ref[...].astype(BF16)
    nt = (((1,), (1,)), ((), ()))
    outs = []
    for h in range(SWA_HEADS):
        g = h // (SWA_HEADS // SWA_KV)
        qh = qb[:, h * SWA_HD:(h + 1) * SWA_HD]
        gs = slice(g * SWA_HD, (g + 1) * SWA_HD)
        sink = sink_ref[h:h + 1, 0:1]
        s_ctx = lax.dot_general(qh, ck[:, gs], nt, preferred_element_type=F32)
        m = jnp.maximum(jnp.max(s_ctx, -1, keepdims=True), sink)
        if local:
            s_loc = lax.dot_general(qh, k_loc[:, gs], nt, preferred_element_type=F32)
            s_loc = jnp.where(valid, s_loc, NEG_INF)
            m = jnp.maximum(m, jnp.max(s_loc, -1, keepdims=True))
        p_ctx = jnp.exp2(s_ctx - m)
        den = jnp.sum(p_ctx, -1, keepdims=True) + jnp.exp2(sink - m)
        acc = jnp.dot(p_ctx.astype(BF16), cv[:, gs], preferred_element_type=F32)
        if local:
            p_loc = jnp.exp2(s_loc - m)
            den = den + jnp.sum(p_loc, -1, keepdims=True)
            acc = acc + jnp.dot(p_loc.astype(BF16), v_loc[:, gs], preferred_element_type=F32)
        outs.append(acc / den)
    o_ref[...] = jnp.concatenate(outs, axis=-1)


def swa_attend(p, pc, B, L, C, sink, tables, local):
    blk = SWA_BLOCK
    nb = L // blk
    qcol = P_SW // (SWA_HEADS * SWA_HD)
    kcol = (P_SW + SWA_HEADS * SWA_HD) // (SWA_KV * SWA_HD)
    kvw = SWA_KV * SWA_HD

    def nbr(j, col):
        return lambda b, n: (b * nb + jnp.clip(n + j, 0, nb - 1), col)

    cos, sin = tables
    sink_rows = jnp.broadcast_to((sink.astype(F32) * math.log2(math.e))[:, None], (SWA_HEADS, LANES))
    tab_spec = pl.BlockSpec(cos.shape, lambda b, n: (0, 0))
    out = pl.pallas_call(
        functools.partial(_swa_kernel, local, L),
        out_shape=jax.ShapeDtypeStruct((B * L, SWA_HEADS * SWA_HD), F32),
        grid=(B, nb),
        in_specs=[pl.BlockSpec((blk, SWA_HEADS * SWA_HD), lambda b, n: (b * nb + n, qcol))]
        + [pl.BlockSpec((blk, kvw), nbr(j, kcol)) for j in (-1, 0, 1)]
        + [pl.BlockSpec((blk, kvw), nbr(j, kcol + 1)) for j in (-1, 0, 1)]
        + [pl.BlockSpec((C, kvw), lambda b, n: (b, kcol)),
           pl.BlockSpec((C, kvw), lambda b, n: (b, kcol + 1)),
           tab_spec, tab_spec,
           pl.BlockSpec((SWA_HEADS, LANES), lambda b, n: (0, 0))],
        out_specs=pl.BlockSpec((blk, SWA_HEADS * SWA_HD), lambda b, n: (b * nb + n, 0)),
        compiler_params=pltpu.CompilerParams(
            dimension_semantics=("parallel", "parallel"), vmem_limit_bytes=VMEM_LIMIT),
        name="swa",
    )(p, p, p, p, p, p, p, pc, pc, cos, sin, sink_rows)
    return out.reshape(B, L, SWA_HEADS * SWA_HD)


def swa_branch(p, pc, B, L, C, need_ctx, sink, tables):
    y = swa_attend(p, pc, B, L, C, sink, tables, True)
    yc = swa_attend(pc, pc, B, C, C, sink, tables, False) if need_ctx else None
    return y, yc


def _head_blocks(value):
    row = lax.broadcasted_iota(jnp.int32, (RW_W, RW_W), 0)
    col = lax.broadcasted_iota(jnp.int32, (RW_W, RW_W), 1)
    return jnp.where(row // RW_HD == col // RW_HD, value, 0.0).astype(F32)


def _head_sum(x, blocks):
    return jnp.dot(x, blocks, preferred_element_type=F32, precision=lax.Precision.HIGHEST)


RW_OFF = tuple(int(o) for o in np.cumsum((0,) + RW_SIZES))
RW_PAD = 1280


def _rwkv_prep_kernel(tm, tiles_per_seq, x_ref, xp_ref, xn_ref, mu_ref, w0_ref, w2f_ref, w2b_ref,
                      a0_ref, a2_ref, g2f_ref, g2b_ref, kk_ref, ka_ref,
                      r_o, k_o, v_o, kkn_o, b_o, df_o, db_o, gf_o, gb_o):
    i = pl.program_id(0)
    first = (i % tiles_per_seq) == 0
    last = (i % tiles_per_seq) == tiles_per_seq - 1
    x = x_ref[...]
    row = lax.broadcasted_iota(jnp.int32, x.shape, 0)
    prev_row = jnp.where(first, 0.0, xp_ref[HALO - 1:HALO, :])
    next_row = jnp.where(last, 0.0, xn_ref[0:1, :])
    xm1 = jnp.where(row == 0, prev_row, pltpu.roll(x, 1, 0))
    xp1 = jnp.where(row == tm - 1, next_row, pltpu.roll(x, tm - 1, 0))
    x = x + (0.5 * (xm1 + xp1) - x) * mu_ref[...]

    r = x[:, RW_OFF[0]:RW_OFF[1]]
    k = x[:, RW_OFF[1]:RW_OFF[2]]
    v = x[:, RW_OFF[2]:RW_OFF[3]]
    wd = jnp.tanh(x[:, 768:896]).astype(BF16)
    lo = x[:, 896:1152]
    hi = x[:, 1024:1280]

    def decay(w2_ref, w0):
        wlog = -jax.nn.softplus(-(w0 + jnp.dot(wd, w2_ref[...], preferred_element_type=F32))) - 0.5
        return jnp.exp(-jnp.exp(wlog))

    a = jax.nn.sigmoid(a0_ref[...] + jnp.dot(lo[:, 0:LANES].astype(BF16), a2_ref[...],
                                             preferred_element_type=F32))
    kk = k * kk_ref[...]
    norm = jnp.sqrt(_head_sum(kk * kk, _head_blocks(1.0)))
    kk = kk / jnp.maximum(norm, 1e-12)
    r_o[...] = r
    k_o[...] = k * (1.0 + (a - 1.0) * ka_ref[...])
    v_o[...] = v
    kkn_o[...] = kk
    b_o[...] = kk * a
    df_o[...] = decay(w2f_ref, w0_ref[0:1, :])
    db_o[...] = decay(w2b_ref, w0_ref[1:2, :])
    gf_o[...] = jnp.dot(jax.nn.sigmoid(lo).astype(BF16), g2f_ref[...], preferred_element_type=F32)
    gb_o[...] = jnp.dot(jax.nn.sigmoid(hi).astype(BF16), g2b_ref[...], preferred_element_type=F32)


def rwkv_prep(p, B, L, mu, w0, w2, a0, a2, g2, k_k, k_a):
    tm = min(RW_PREP_TM, L)
    tiles_per_seq = L // tm
    n_tok = B * L
    hb = tm // HALO
    col = P_RW // RW_PAD

    def rows(n, r0, src):
        return jnp.zeros((n, RW_W), F32).at[r0:r0 + src.shape[0]].set(src).astype(BF16)

    w2f = rows(LANES, RW_OFF[3] - 768, w2[0])
    w2b = rows(LANES, RW_OFF[4] - 768, w2[1])
    a2p = rows(LANES, RW_OFF[5] - 896, a2)
    g2f = rows(RW_W, RW_OFF[6] - 896, g2[0])
    g2b = rows(RW_W, RW_OFF[7] - 1024, g2[1])
    mu_p = jnp.pad(mu, (0, RW_PAD - mu.shape[0])).reshape(1, RW_PAD)

    def const(shape):
        return pl.BlockSpec(shape, lambda i: (0,) * len(shape))

    vec = const((1, RW_W))
    out_spec = pl.BlockSpec((tm, RW_W), lambda i: (i, 0))
    return pl.pallas_call(
        functools.partial(_rwkv_prep_kernel, tm, tiles_per_seq),
        out_shape=[jax.ShapeDtypeStruct((n_tok, RW_W), F32)] * 9,
        grid=(n_tok // tm,),
        in_specs=[pl.BlockSpec((tm, RW_PAD), lambda i: (i, col)),
                  pl.BlockSpec((HALO, RW_PAD), lambda i: (jnp.maximum(i * hb - 1, 0), col)),
                  pl.BlockSpec((HALO, RW_PAD), lambda i: (jnp.minimum((i + 1) * hb, n_tok // HALO - 1), col)),
                  const((1, RW_PAD)), const((2, RW_W)), const((LANES, RW_W)), const((LANES, RW_W)),
                  vec, const((LANES, RW_W)), const((RW_W, RW_W)), const((RW_W, RW_W)), vec, vec],
        out_specs=[out_spec] * 9,
        compiler_params=pltpu.CompilerParams(
            dimension_semantics=("parallel",), vmem_limit_bytes=VMEM_LIMIT),
        name="rwkv_prep",
    )(p, p, p, mu_p, w0, w2f, w2b, a0.reshape(1, RW_W), a2p, g2f, g2b,
      k_k.reshape(1, RW_W), k_a.reshape(1, RW_W))


def _rwkv_scan_kernel(tblk, nb,
                      wf_ref, kkf_ref, bf_ref, kf_ref, vf_ref, rf_ref,
                      wb_ref, kkb_ref, bb_ref, kb_ref, vb_ref, rb_ref,
                      of_ref, ob_ref, sf_ref, sb_ref):
    c = pl.program_id(0)

    @pl.when(c == 0)
    def _():
        sf_ref[...] = jnp.zeros_like(sf_ref)
        sb_ref[...] = jnp.zeros_like(sb_ref)

    row = lax.broadcasted_iota(jnp.int32, (RW_W, RW_W), 0)
    col = lax.broadcasted_iota(jnp.int32, (RW_W, RW_W), 1)
    ones_blk = jnp.where(row // RW_HD == col // RW_HD, 1.0, 0.0).astype(BF16)
    r64 = lax.broadcasted_iota(jnp.int32, (RW_HD, RW_W), 0)
    c64 = lax.broadcasted_iota(jnp.int32, (RW_HD, RW_W), 1)
    eye_t = jnp.where(r64 == c64 % RW_HD, 1.0, 0.0).astype(F32)
    r8 = lax.broadcasted_iota(jnp.int32, (8, RW_W), 0)
    c8 = lax.broadcasted_iota(jnp.int32, (8, RW_W), 1)
    head_sel = jnp.where(r8 == c8 // RW_HD, 1.0, 0.0).astype(BF16)

    fwd = (wf_ref, kkf_ref, bf_ref, kf_ref, vf_ref, rf_ref, sf_ref)
    bwd = (wb_ref, kkb_ref, bb_ref, kb_ref, vb_ref, rb_ref, sb_ref)
    nrow = nb * RW_HD

    def reduce_operand(idx, refs):
        w_ref, kk_ref, b_ref, k_ref, v_ref, r_ref, s_ref = refs
        p1 = [(s_ref[b] * kk_ref[idx, b:b + 1, :]).astype(BF16) for b in range(nb)]
        vm = [(eye_t * v_ref[idx, b:b + 1, :]).astype(BF16) for b in range(nb)]
        return jnp.concatenate(p1 + vm, axis=0)

    def update(idx, refs, red):
        w_ref, kk_ref, b_ref, k_ref, v_ref, r_ref, s_ref = refs
        zs = []
        for b in range(nb):
            sa = red[b * RW_HD:(b + 1) * RW_HD]
            vc = red[nrow + b * RW_HD:nrow + (b + 1) * RW_HD]
            s_new = (s_ref[b] * w_ref[idx, b:b + 1, :] - sa * b_ref[idx, b:b + 1, :]
                     + vc * k_ref[idx, b:b + 1, :])
            s_ref[b] = s_new
            zs.append((s_new * r_ref[idx, b:b + 1, :]).astype(BF16))
        return jnp.concatenate(zs, axis=0)

    def emit(idx, z, o_ref):
        o_ref[idx] = lax.dot_general(head_sel, z, (((1,), (1,)), ((), ())),
                                     preferred_element_type=F32)

    def step(i, carry):
        lhs_f, z_f, z_b = carry
        ib = tblk - 1 - i
        red_f = jnp.dot(lhs_f, ones_blk, preferred_element_type=F32)
        red_b = jnp.dot(reduce_operand(ib, bwd), ones_blk, preferred_element_type=F32)
        emit(jnp.maximum(i - 1, 0), z_f, of_ref)
        emit(jnp.minimum(ib + 1, tblk - 1), z_b, ob_ref)
        z_f = update(i, fwd, red_f)
        lhs_f = reduce_operand(jnp.minimum(i + 1, tblk - 1), fwd)
        z_b = update(ib, bwd, red_b)
        return lhs_f, z_f, z_b

    z0 = jnp.zeros((nrow, RW_W), BF16)
    _, z_f, z_b = lax.fori_loop(0, tblk, step, (reduce_operand(0, fwd), z0, z0))
    emit(tblk - 1, z_f, of_ref)
    emit(0, z_b, ob_ref)


def rwkv_scan_pallas(w_f, w_b, kk, bvec, k, v, r, n_ctx):
    n, nb, _ = kk.shape
    tblk = RW_TBLK
    nblk = n // tblk
    nblk_ctx = n_ctx // tblk

    def fmap(c):
        return (c, 0, 0)

    def bmap(c):
        return (jnp.where(c < nblk_ctx, nblk_ctx - 1 - c, nblk + nblk_ctx - 1 - c), 0, 0)

    blk = (tblk, nb, RW_W)
    oblk = (tblk, 8, nb * RW_HD)
    return pl.pallas_call(
        functools.partial(_rwkv_scan_kernel, tblk, nb),
        out_shape=[jax.ShapeDtypeStruct((n, 8, nb * RW_HD), F32)] * 2,
        grid=(nblk,),
        in_specs=[pl.BlockSpec(blk, fmap)] * 6 + [pl.BlockSpec(blk, bmap)] * 6,
        out_specs=[pl.BlockSpec(oblk, fmap), pl.BlockSpec(oblk, bmap)],
        scratch_shapes=[pltpu.VMEM((nb, RW_HD, RW_W), F32), pltpu.VMEM((nb, RW_HD, RW_W), F32)],
        compiler_params=pltpu.CompilerParams(dimension_semantics=("arbitrary",),
                                             vmem_limit_bytes=VMEM_LIMIT),
        name="rwkv_scan",
    )(w_f, kk, bvec, k, v, r, w_b, kk, bvec, k, v, r)


def _rwkv_post_kernel(of_ref, ob_ref, r_ref, k_ref, v_ref, gf_ref, gb_ref, rk_ref, lng_ref, lnb_ref,
                      y_ref):
    mean_blk = _head_blocks(1.0 / RW_HD)

    def gn(o):
        oc = o - _head_sum(o, mean_blk)
        var = _head_sum(oc * oc, mean_blk)
        return oc * lax.rsqrt(var + RW_GN_EPS) * lng_ref[...] + lnb_ref[...]

    bonus = _head_sum(r_ref[...] * k_ref[...] * rk_ref[...], _head_blocks(1.0)) * v_ref[...]
    y_ref[...] = (gn(of_ref[...]) + bonus) * gf_ref[...] + (gn(ob_ref[...]) + bonus) * gb_ref[...]


def rwkv_post(o_f, o_b, r, k, v, g_f, g_b, r_k, lnx_g, lnx_b, L):
    n_tok = r.shape[0]
    tm = min(RW_PREP_TM, L)
    tok = pl.BlockSpec((tm, RW_W), lambda i: (i, 0))
    vec = pl.BlockSpec((1, RW_W), lambda i: (0, 0))
    return pl.pallas_call(
        _rwkv_post_kernel,
        out_shape=jax.ShapeDtypeStruct((n_tok, RW_W), F32),
        grid=(n_tok // tm,),
        in_specs=[tok] * 7 + [vec] * 3,
        out_specs=tok,
        compiler_params=pltpu.CompilerParams(
            dimension_semantics=("parallel",), vmem_limit_bytes=VMEM_LIMIT),
        name="rwkv_post",
    )(o_f, o_b, r, k, v, g_f, g_b, r_k.reshape(1, RW_W), lnx_g.reshape(1, RW_W), lnx_b.reshape(1, RW_W))


def rwkv_branch(p, pc, B, L, C, need_ctx, mu, w0, w2, a0, a2, g2, k_k, k_a, r_k, lnx_g, lnx_b):
    r, k, v, kk, bv, d_f, d_b, g_f, g_b = rwkv_prep(p, B, L, mu, w0, w2, a0, a2, g2, k_k, k_a)
    rc, kc, vc, kkc, bc, dc_f, dc_b, gc_f, gc_b = rwkv_prep(pc, B, C, mu, w0, w2, a0, a2, g2, k_k, k_a)

    def tmajor(xc, xl):
        t = jnp.concatenate([xc.reshape(B, C, RW_W), xl.reshape(B, L, RW_W)], axis=1)
        return jnp.moveaxis(t, 1, 0)

    pf, pb = rwkv_scan_pallas(tmajor(dc_f, d_f), tmajor(dc_b, d_b), tmajor(kkc, kk), tmajor(bc, bv),
                              tmajor(kc, k), tmajor(vc, v), tmajor(rc, r), C)

    def bmajor(t):
        t = t[:, :RW_HEADS].reshape(C + L, RW_HEADS, B, RW_HD)
        return jnp.transpose(t, (2, 0, 1, 3)).reshape(B, C + L, RW_W)

    pf, pb = bmajor(pf), bmajor(pb)
    y = rwkv_post(pf[:, C:].reshape(B * L, RW_W), pb[:, C:].reshape(B * L, RW_W), r, k, v, g_f, g_b,
                  r_k, lnx_g, lnx_b, L).reshape(B, L, RW_W)
    yc = None
    if need_ctx:
        yc = rwkv_post(pf[:, :C].reshape(B * C, RW_W), pb[:, :C].reshape(B * C, RW_W), rc, kc, vc,
                       gc_f, gc_b, r_k, lnx_g, lnx_b, C).reshape(B, C, RW_W)
    return y, yc


def _diff_attn_kernel(q_ref, kt_ref, v_ref, lam_ref, gain_ref, o_ref):
    q = q_ref[0]
    v = v_ref[0]
    lam = lam_ref[...]
    tq = q.shape[0]
    head_of_lane = lax.broadcasted_iota(jnp.int32, (1, DF_W), 1) // DF_VD
    acc = jnp.zeros((tq, DF_W), F32)
    for h in range(DF_HEADS):
        parts = []
        for c in range(2):
            j = 2 * h + c
            s = jnp.dot(q[:, DF_HD * j:DF_HD * (j + 1)], kt_ref[0, j],
                        preferred_element_type=F32)
            p = jnp.exp2(s - jnp.max(s, -1, keepdims=True))
            l = jnp.sum(p, -1, keepdims=True)
            pv = jnp.dot(p.astype(BF16), v, preferred_element_type=F32)
            parts.append(pv / l)
        acc = jnp.where(head_of_lane == h, parts[0] - lam * parts[1], acc)
    row = lax.broadcasted_iota(jnp.int32, (DF_W, DF_W), 0)
    col = lax.broadcasted_iota(jnp.int32, (DF_W, DF_W), 1)
    seg_mean = jnp.where(row // DF_VD == col // DF_VD, 1.0 / DF_VD, 0.0).astype(F32)
    ms = jnp.dot(acc * acc, seg_mean, preferred_element_type=F32, precision=lax.Precision.HIGHEST)
    o_ref[0] = acc * lax.rsqrt(ms + 1e-5) * gain_ref[...]


def diff_attend(q, k, v, lam, lam_init, subln_g):
    B, Lq = q.shape[:2]
    S = k.shape[1]
    tq = min(DF_TQ, Lq)
    qs = (q.astype(F32) * (DF_HD ** -0.5 * math.log2(math.e))).reshape(B, Lq, DF_W).astype(BF16)
    kt = jnp.transpose(k.reshape(B, S, 2 * DF_HEADS, DF_HD), (0, 2, 3, 1)).astype(BF16)
    vb = v.reshape(B, S, DF_W).astype(BF16)
    gain = jnp.tile(subln_g.astype(F32) * (1.0 - lam_init), DF_HEADS).reshape(1, DF_W)
    return pl.pallas_call(
        _diff_attn_kernel,
        out_shape=jax.ShapeDtypeStruct((B, Lq, DF_W), F32),
        grid=(B, Lq // tq),
        in_specs=[pl.BlockSpec((1, tq, DF_W), lambda b, i: (b, i, 0)),
                  pl.BlockSpec((1, 2 * DF_HEADS, DF_HD, S), lambda b, i: (b, 0, 0, 0)),
                  pl.BlockSpec((1, S, DF_W), lambda b, i: (b, 0, 0)),
                  pl.BlockSpec((1, 1), lambda b, i: (0, 0)),
                  pl.BlockSpec((1, DF_W), lambda b, i: (0, 0))],
        out_specs=pl.BlockSpec((1, tq, DF_W), lambda b, i: (b, i, 0)),
        compiler_params=pltpu.CompilerParams(
            dimension_semantics=("parallel", "parallel"), vmem_limit_bytes=VMEM_LIMIT),
        name="diff_attn",
    )(qs, kt, vb, lam.reshape(1, 1).astype(F32), gain)


def diff_branch(p, pc, need_ctx, lq1, lk1, lq2, lk2, subln_g, lam_init):
    q, k, v = split_at(p, DF_SIZES)
    qc, kc, vc = split_at(pc, DF_SIZES)

    def qk(t):
        return t.reshape(t.shape[0], t.shape[1], DF_HEADS, 2, DF_HD)

    def vs(t):
        return t.reshape(t.shape[0], t.shape[1], DF_HEADS, DF_VD)

    lam = (jnp.exp(jnp.sum(lq1.astype(F32) * lk1.astype(F32)))
           - jnp.exp(jnp.sum(lq2.astype(F32) * lk2.astype(F32))) + lam_init)
    k_all = jnp.concatenate([axial_rope(qk(k)), qk(kc)], axis=1)
    v_all = jnp.concatenate([vs(v), vs(vc)], axis=1)
    y = diff_attend(axial_rope(qk(q)), k_all, v_all, lam, lam_init, subln_g)
    yc = diff_attend(qk(qc), qk(kc), vs(vc), lam, lam_init, subln_g) if need_ctx else None
    return y, yc


def _ln_rows(x):
    mu = jnp.mean(x, -1, keepdims=True)
    xc = x - mu
    return xc * lax.rsqrt(jnp.mean(xc * xc, -1, keepdims=True) + LN_EPS)


def _ffn_kernel(tm, tiles_per_seq, n_chunks,
                h_ref, hprev_ref, hnext_ref, sh_ref, sc_ref, gate_ref,
                wa_ref, wb_ref, cwa_ref, cwb_ref, cba_ref, cbb_ref, wd_ref, lng_ref, lnb_ref,
                o_ref, u_ref, acc_ref):
    i = pl.program_id(0)
    j = pl.program_id(1)
    n = tm + 2 * HALO

    @pl.when(j == 0)
    def _():
        scale = 1.0 + sc_ref[0]
        shift = sh_ref[0]
        first = (i % tiles_per_seq) == 0
        last = (i % tiles_per_seq) == tiles_per_seq - 1
        u_ref[0:tm, :] = (_ln_rows(h_ref[...]) * scale + shift).astype(BF16)
        un = _ln_rows(hnext_ref[...]) * scale + shift
        up = _ln_rows(hprev_ref[...]) * scale + shift
        u_ref[tm:tm + HALO, :] = jnp.where(last, 0.0, un).astype(BF16)
        u_ref[tm + HALO:n, :] = jnp.where(first, 0.0, up).astype(BF16)
        acc_ref[...] = jnp.zeros_like(acc_ref)

    u = u_ref[...]

    def conv(w_ref, cw_ref, cb_ref):
        x = jnp.dot(u, w_ref[...], preferred_element_type=F32)
        cw = cw_ref[...]
        y = (pltpu.roll(x, 1, 0)[0:tm] * cw[0:1] + x[0:tm] * cw[1:2]
             + pltpu.roll(x, n - 1, 0)[0:tm] * cw[2:3] + cb_ref[...])
        return y

    a = conv(wa_ref, cwa_ref, cba_ref)
    b = conv(wb_ref, cwb_ref, cbb_ref)
    g = (a * jax.nn.sigmoid(a) * b).astype(BF16)
    acc_ref[...] += jnp.dot(g, wd_ref[...], preferred_element_type=F32)

    @pl.when(j == n_chunks - 1)
    def _():
        y = DN_ALPHA * h_ref[...] + gate_ref[0] * acc_ref[...]
        o_ref[...] = _ln_rows(y) * lng_ref[...] + lnb_ref[...]


def ffn_block(h, shift, scale, gate, w_up, conv_w, conv_b, w_down, ln_g, ln_b):
    B, L, D = h.shape
    tm = min(FFN_TM, L)
    tiles_per_seq = L // tm
    n_tok = B * L
    n_chunks = D_FF // FFN_CHUNK
    hb = tm // HALO

    def bcast(t):
        return jnp.broadcast_to(t.astype(F32), (B, 1, D))

    wu = w_up.astype(BF16)
    wd = w_down.astype(BF16)
    cb = conv_b.reshape(1, 2 * D_FF)
    h2 = h.reshape(n_tok, D)
    mod_spec = pl.BlockSpec((1, 1, D), lambda i, j: (i // tiles_per_seq, 0, 0))
    vec_spec = pl.BlockSpec((1, D), lambda i, j: (0, 0))
    out = pl.pallas_call(
        functools.partial(_ffn_kernel, tm, tiles_per_seq, n_chunks),
        out_shape=jax.ShapeDtypeStruct((n_tok, D), F32),
        grid=(n_tok // tm, n_chunks),
        in_specs=[pl.BlockSpec((tm, D), lambda i, j: (i, 0)),
                  pl.BlockSpec((HALO, D), lambda i, j: (jnp.maximum(i * hb - 1, 0), 0)),
                  pl.BlockSpec((HALO, D), lambda i, j: (jnp.minimum((i + 1) * hb, n_tok // HALO - 1), 0)),
                  mod_spec, mod_spec, mod_spec,
                  pl.BlockSpec((D, FFN_CHUNK), lambda i, j: (0, j)),
                  pl.BlockSpec((D, FFN_CHUNK), lambda i, j: (0, n_chunks + j)),
                  pl.BlockSpec((3, FFN_CHUNK), lambda i, j: (0, j)),
                  pl.BlockSpec((3, FFN_CHUNK), lambda i, j: (0, n_chunks + j)),
                  pl.BlockSpec((1, FFN_CHUNK), lambda i, j: (0, j)),
                  pl.BlockSpec((1, FFN_CHUNK), lambda i, j: (0, n_chunks + j)),
                  pl.BlockSpec((FFN_CHUNK, D), lambda i, j: (j, 0)),
                  vec_spec, vec_spec],
        out_specs=pl.BlockSpec((tm, D), lambda i, j: (i, 0)),
        scratch_shapes=[pltpu.VMEM((tm + 2 * HALO, D), BF16), pltpu.VMEM((tm, D), F32)],
        compiler_params=pltpu.CompilerParams(
            dimension_semantics=("parallel", "arbitrary"), vmem_limit_bytes=VMEM_LIMIT),
        name="conv_ffn",
    )(h2, h2, h2, bcast(shift), bcast(scale), bcast(gate), wu, wu, conv_w, conv_w, cb, cb, wd,
      ln_g.reshape(1, D), ln_b.reshape(1, D))
    return out.reshape(B, L, D)


P_GATE, P_HY, P_RW, P_SW, P_DF = 0, 4096, 5120, 6400, 6912
P_COLS = 7680
IN_CHUNK = 512


def _in_proj_kernel(x_ref, sh_ref, sc_ref, w_ref, o_ref, u_ref):
    @pl.when(pl.program_id(1) == 0)
    def _():
        u_ref[...] = (_ln_rows(x_ref[...]) * (1.0 + sc_ref[0]) + sh_ref[0]).astype(BF16)

    o_ref[...] = jnp.dot(u_ref[...], w_ref[...], preferred_element_type=F32)


def in_proj(h, shift, scale, w_in_p):
    B, L, D = h.shape
    tm = min(FFN_TM, L)
    tiles_per_seq = L // tm
    n_tok = B * L

    def bcast(t):
        return jnp.broadcast_to(t.astype(F32), (B, 1, D))

    mod_spec = pl.BlockSpec((1, 1, D), lambda i, j: (i // tiles_per_seq, 0, 0))
    return pl.pallas_call(
        _in_proj_kernel,
        out_shape=jax.ShapeDtypeStruct((n_tok, P_COLS), F32),
        grid=(n_tok // tm, P_COLS // IN_CHUNK),
        in_specs=[pl.BlockSpec((tm, D), lambda i, j: (i, 0)), mod_spec, mod_spec,
                  pl.BlockSpec((D, IN_CHUNK), lambda i, j: (0, j))],
        out_specs=pl.BlockSpec((tm, IN_CHUNK), lambda i, j: (i, j)),
        scratch_shapes=[pltpu.VMEM((tm, D), BF16)],
        compiler_params=pltpu.CompilerParams(
            dimension_semantics=("parallel", "arbitrary"), vmem_limit_bytes=VMEM_LIMIT),
        name="in_proj",
    )(h.reshape(n_tok, D), bcast(shift), bcast(scale), w_in_p)


def pack_w_in(w):
    offs = np.cumsum((0,) + IN_SIZES)
    hy, sw, rw, df, gt = [w[:, offs[j]:offs[j + 1]] for j in range(len(IN_SIZES))]

    def padto(t, n):
        return jnp.pad(t, ((0, 0), (0, n - t.shape[1])))

    return jnp.concatenate([gt, padto(hy, P_RW - P_HY), padto(rw, P_SW - P_RW), sw, df],
                           axis=1).astype(BF16)


def _merge_kernel(yh_ref, ys_ref, yr_ref, yd_ref, g0_ref, g1_ref, g2_ref, g3_ref, h_ref, gate_ref,
                  wbr_ref, wo_ref, lng_ref, lnb_ref, o_ref):
    acc = None
    for j, (y_ref, g_ref) in enumerate(((yh_ref, g0_ref), (ys_ref, g1_ref), (yr_ref, g2_ref),
                                        (yd_ref, g3_ref))):
        term = jax.nn.sigmoid(g_ref[...]) * jnp.dot(y_ref[...].astype(BF16), wbr_ref[j],
                                                    preferred_element_type=F32)
        acc = term if acc is None else acc + term
    mix = jnp.dot(acc.astype(BF16), wo_ref[...], preferred_element_type=F32)
    y = DN_ALPHA * h_ref[...] + gate_ref[0] * mix
    o_ref[...] = _ln_rows(y) * lng_ref[...] + lnb_ref[...]


def merge_block(ys, p, h, gate, w_br, w_o, ln_g, ln_b):
    B, L, D = h.shape
    n_tok = B * L
    tm = min(MERGE_TM, L)
    tiles_per_seq = L // tm
    y_spec = pl.BlockSpec((tm, BR_W), lambda i: (i, 0))
    vec_spec = pl.BlockSpec((1, D), lambda i: (0, 0))
    g_specs = [pl.BlockSpec((tm, D), functools.partial(lambda j, i: (i, j), j)) for j in range(N_BRANCH)]
    out = pl.pallas_call(
        _merge_kernel,
        out_shape=jax.ShapeDtypeStruct((n_tok, D), F32),
        grid=(n_tok // tm,),
        in_specs=[y_spec] * N_BRANCH + g_specs + [
            pl.BlockSpec((tm, D), lambda i: (i, 0)),
            pl.BlockSpec((1, 1, D), lambda i: (i // tiles_per_seq, 0, 0)),
            pl.BlockSpec((N_BRANCH, BR_W, D), lambda i: (0, 0, 0)),
            pl.BlockSpec((D, D), lambda i: (0, 0)), vec_spec, vec_spec],
        out_specs=pl.BlockSpec((tm, D), lambda i: (i, 0)),
        compiler_params=pltpu.CompilerParams(
            dimension_semantics=("parallel",), vmem_limit_bytes=VMEM_LIMIT),
        name="merge",
    )(*[y.reshape(n_tok, BR_W) for y in ys], p, p, p, p, h.reshape(n_tok, D),
      jnp.broadcast_to(gate.astype(F32), (B, 1, D)), w_br.astype(BF16), w_o.astype(BF16),
      ln_g.reshape(1, D), ln_b.reshape(1, D))
    return out.reshape(B, L, D)


def kernel(x, c, ctx, c_ctx, ada_w, ada_b, w_in, hy_conv_w, hy_conv_b, hy_f_w1, hy_f_b1,
           hy_f_w2, hy_f_b2, hy_f_w3, hy_f_freq, hy_bias, swa_sink, rwkv_mu, rwkv_w0, rwkv_w2,
           rwkv_a0, rwkv_a2, rwkv_g2, rwkv_kk, rwkv_ka, rwkv_rk, rwkv_lnx_g, rwkv_lnx_b,
           diff_lq1, diff_lk1, diff_lq2, diff_lk2, diff_subln_g, w_branch, w_out, ln1_g, ln1_b,
           ffn_w_up, ffn_conv_w, ffn_conv_b, ffn_w_down, ln2_g, ln2_b):
    h, hc = x, ctx
    mats, mats_c = dft_mats(x.shape[1]), dft_mats(ctx.shape[1])
    swa_tables = rope_tables(x.shape[1], SWA_HD, SWA_HEADS * SWA_HD)
    s_lat = jax.nn.silu(c)
    s_ctx = jax.nn.silu(c_ctx)
    for i in range(DEPTH):
        need_ctx = i < DEPTH - 1
        mod = (s_lat @ ada_w[i] + ada_b[i])[:, None, :]
        mod_c = s_ctx @ ada_w[i] + ada_b[i]
        sh1, sc1, g1, sh2, sc2, g2 = jnp.split(mod, 6, -1)
        csh1, csc1, cg1, csh2, csc2, cg2 = jnp.split(mod_c, 6, -1)

        B, L, D = h.shape
        C = hc.shape[1]
        w_in_p = pack_w_in(w_in[i])
        p = in_proj(h, sh1, sc1, w_in_p)
        pc = in_proj(hc, csh1.reshape(1, 1, D), csc1.reshape(1, 1, D), w_in_p)

        def seg(t, n, off, width):
            return t[:, off:off + width].reshape(B, n, width)

        p_hy, p_df = seg(p, L, P_HY, IN_SIZES[0]), seg(p, L, P_DF, IN_SIZES[3])
        pc_hy, pc_df = seg(pc, C, P_HY, IN_SIZES[0]), seg(pc, C, P_DF, IN_SIZES[3])

        y_hy, yc_hy = hyena_branch(p_hy, pc_hy, need_ctx, mats, mats_c, hy_conv_w[i], hy_conv_b[i], hy_f_w1[i],
                                   hy_f_b1[i], hy_f_w2[i], hy_f_b2[i], hy_f_w3[i], hy_f_freq[i],
                                   hy_bias[i])
        y_sw, yc_sw = swa_branch(p, pc, B, L, C, need_ctx, swa_sink[i], swa_tables)
        y_rw, yc_rw = rwkv_branch(p, pc, B, L, C, need_ctx, rwkv_mu[i], rwkv_w0[i], rwkv_w2[i],
                                  rwkv_a0[i], rwkv_a2[i], rwkv_g2[i], rwkv_kk[i], rwkv_ka[i],
                                  rwkv_rk[i], rwkv_lnx_g[i], rwkv_lnx_b[i])
        lam_init = 0.8 - 0.6 * math.exp(-0.3 * i)
        y_df, yc_df = diff_branch(p_df, pc_df, need_ctx, diff_lq1[i], diff_lk1[i], diff_lq2[i],
                                  diff_lk2[i], diff_subln_g[i], lam_init)

        h = merge_block((y_hy, y_sw, y_rw, y_df), p, h, g1, w_branch[i], w_out[i], ln1_g[i], ln1_b[i])
        h = ffn_block(h, sh2, sc2, g2, ffn_w_up[i], ffn_conv_w[i], ffn_conv_b[i], ffn_w_down[i],
                      ln2_g[i], ln2_b[i])

        if need_ctx:
            hc = merge_block((yc_hy, yc_sw, yc_rw, yc_df), pc, hc, cg1.reshape(1, 1, D), w_branch[i],
                             w_out[i], ln1_g[i], ln1_b[i])
            hc = ffn_block(hc, csh2.reshape(1, 1, -1), csc2.reshape(1, 1, -1), cg2.reshape(1, 1, -1),
                           ffn_w_up[i], ffn_conv_w[i], ffn_conv_b[i], ffn_w_down[i], ln2_g[i], ln2_b[i])
    return h
```

```python
import functools
import math

import jax
import jax.numpy as jnp
import numpy as np
from jax import lax
from jax.experimental import pallas as pl
from jax.experimental.pallas import tpu as pltpu

D_MODEL = 1024
BATCH = 8
SEQ = 4096
DEPTH = 2
CTX_LEN = 256
GRID_W = 64
ROPE_BASE = 10000.0
F32 = jnp.float32
BF16 = jnp.bfloat16
NEG_INF = -1e30
LN_EPS = 1e-6

HY_W = 256
HY_ORDER = 2
HY_DIRS = 2
HY_EMB = 33
HY_FFN = 64
HY_MIN_DECAY = math.log(1e-2) / 1.5
HY_MAX_DECAY = math.log(1e-2) / 0.3

SWA_HEADS = 4
SWA_KV = 2
SWA_HD = 64
SWA_WIN = 128
SWA_BLOCK = 128

RW_HEADS = 4
RW_HD = 64
RW_W = RW_HEADS * RW_HD
RW_DECAY_R = 64
RW_AAA_R = 64
RW_GATE_R = 128
RW_GN_EPS = 64e-5

DF_HEADS = 4
DF_HD = 32
DF_VD = 2 * DF_HD
DF_W = DF_HEADS * DF_VD
DF_BLOCK = 128

N_BRANCH = 4
BR_W = 256
D_FF = 2816
DN_ALPHA = (2 * DEPTH) ** 0.25
DN_BETA = (8 * DEPTH) ** -0.25

SWA_SIZES = (SWA_HEADS * SWA_HD, SWA_KV * SWA_HD, SWA_KV * SWA_HD)
RW_SIZES = (RW_W, RW_W, RW_W, RW_DECAY_R, RW_DECAY_R, RW_AAA_R, RW_GATE_R, RW_GATE_R)
DF_SIZES = (DF_W, DF_W, DF_W)
IN_SIZES = (3 * HY_W, sum(SWA_SIZES), sum(RW_SIZES), sum(DF_SIZES), N_BRANCH * D_MODEL)

LANES = 128
MM_TM = 512
MM_TN = 512
RW_TBLK = 64
DF_TQ = 256
HALO = 8
P_HALO = 16
IN_TM = 2048
FFN_TM = 1024
FFN_CHUNK = 256
MERGE_TM = 512
HY_TK = 512
RW_PREP_TM = 512
VMEM_LIMIT = 48 * 1024 * 1024


def _mm_kernel(x_ref, w_ref, o_ref):
    o_ref[...] = jnp.dot(x_ref[...].astype(BF16), w_ref[...], preferred_element_type=F32)


def _mm(x, w):
    m, k = x.shape
    n = w.shape[1]
    n_pad = -(-n // LANES) * LANES
    wb = w.astype(BF16)
    if n_pad != n:
        wb = jnp.pad(wb, ((0, 0), (0, n_pad - n)))
    tn = MM_TN if n_pad % MM_TN == 0 else (256 if n_pad % 256 == 0 else LANES)
    tm = MM_TM if m % MM_TM == 0 else m
    out = pl.pallas_call(
        _mm_kernel,
        out_shape=jax.ShapeDtypeStruct((m, n_pad), F32),
        grid=(m // tm, n_pad // tn),
        in_specs=[pl.BlockSpec((tm, k), lambda i, j: (i, 0)),
                  pl.BlockSpec((k, tn), lambda i, j: (0, j))],
        out_specs=pl.BlockSpec((tm, tn), lambda i, j: (i, j)),
        compiler_params=pltpu.CompilerParams(
            dimension_semantics=("parallel", "parallel"), vmem_limit_bytes=VMEM_LIMIT),
        name="matmul",
    )(x, wb)
    return out[:, :n] if n_pad != n else out


def _mm3(x, w):
    b, l, k = x.shape
    return _mm(x.reshape(b * l, k), w).reshape(b, l, w.shape[1])


def split_at(t, sizes):
    return jnp.split(t, np.cumsum(sizes)[:-1].tolist(), axis=-1)


def layer_norm(x, g=None, b=None):
    xf = x.astype(F32)
    mu = jnp.mean(xf, -1, keepdims=True)
    var = jnp.mean(jnp.square(xf - mu), -1, keepdims=True)
    y = (xf - mu) * lax.rsqrt(var + LN_EPS)
    if g is not None:
        y = y * g + b
    return y.astype(x.dtype)


def modulate(x, shift, scale):
    return layer_norm(x) * (1 + scale) + shift


def dwconv3(x, w, b):
    xp = jnp.pad(x, ((0, 0), (1, 1), (0, 0)))
    return xp[:, :-2] * w[0] + x * w[1] + xp[:, 2:] * w[2] + b


def axial_rope(x):
    L, d = x.shape[1], x.shape[-1]
    rows = L // GRID_W
    row = jnp.repeat(jnp.arange(rows), GRID_W)
    col = jnp.tile(jnp.arange(GRID_W), rows)
    nf = d // 4
    inv = ROPE_BASE ** (-jnp.arange(nf, dtype=F32) / nf)
    bshape = (L,) + (1,) * (x.ndim - 3) + (nf,)
    xf = x.astype(F32)
    out = []
    for half, pos in enumerate((row, col)):
        ang = (pos.astype(F32)[:, None] * inv[None, :]).reshape(bshape)
        cos, sin = jnp.cos(ang), jnp.sin(ang)
        xh = xf[..., half * 2 * nf:(half + 1) * 2 * nf]
        x1, x2 = xh[..., :nf], xh[..., nf:]
        out += [x1 * cos - x2 * sin, x1 * sin + x2 * cos]
    return jnp.concatenate(out, -1).astype(x.dtype)


def sink_softmax(logit_list, sink_b):
    lead = logit_list[0].shape[:-1]
    z = jnp.concatenate([t.astype(F32) for t in logit_list]
                        + [jnp.broadcast_to(sink_b.astype(F32), lead + (1,))], -1)
    return jax.nn.softmax(z, -1)[..., :-1]


def dft_mats(L):
    tb = 64
    k = jnp.arange(L, dtype=jnp.int32)[:, None]

    def table(t):
        ang = (((2 * k + 1) * t[None, :]) % (4 * L)).astype(F32) * (math.pi / (2 * L))
        return jnp.cos(ang), jnp.sin(ang)

    ca, sa = table(tb * jnp.arange(L // tb, dtype=jnp.int32))
    cb, sb = table(jnp.arange(tb, dtype=jnp.int32))
    c = (ca[:, :, None] * cb[:, None, :] - sa[:, :, None] * sb[:, None, :]).reshape(L, L)
    s = (sa[:, :, None] * cb[:, None, :] + ca[:, :, None] * sb[:, None, :]).reshape(L, L)
    ct = (ca.T[:, None, :] * cb.T[None, :, :] - sa.T[:, None, :] * sb.T[None, :, :]).reshape(L, L)
    st = (sa.T[:, None, :] * cb.T[None, :, :] + ca.T[:, None, :] * sb.T[None, :, :]).reshape(L, L)
    return c.astype(BF16), s.astype(BF16), ct.astype(BF16), st.astype(BF16)


def hyena_kspec(L, mats, w1, b1, w2, b2, w3, freq):
    t = jnp.linspace(0.0, 1.0, L, dtype=F32)[:, None]
    bands = (HY_EMB - 1) // 2
    w = 2.0 * math.pi * jnp.arange(L, dtype=F32) / L
    fb = jnp.linspace(1e-4, bands - 1, bands, dtype=F32)
    ang = w[:, None] * fb[None, :]
    z = jnp.concatenate([t, jnp.cos(ang), -jnp.sin(ang)], -1)
    fr = freq.astype(F32)
    hdn = jnp.sin(fr * (z @ w1.astype(F32) + b1.astype(F32)))
    hdn = jnp.sin(fr * (hdn @ w2.astype(F32) + b2.astype(F32)))
    filt = (hdn @ w3.astype(F32)).reshape(L, HY_ORDER, HY_DIRS, HY_W)
    deltas = jnp.abs(jnp.linspace(HY_MIN_DECAY, HY_MAX_DECAY, HY_W, dtype=F32))
    filt = filt * jnp.exp(-t * deltas[None, :])[:, None, None, :]
    filt = filt / jnp.sum(jnp.abs(filt), axis=(0, 2), keepdims=True)
    fwd, bwd = filt[:, :, 0], filt[:, :, 1]
    hi = jnp.concatenate([jnp.zeros_like(bwd[:1]), -bwd[:0:-1]], axis=0)
    nw = HY_ORDER * HY_W
    x = jnp.concatenate([fwd.reshape(L, nw), hi.reshape(L, nw)], axis=1)
    cx, sx = _mm(mats[0], x), _mm(mats[1], x)
    sgn = (1 - 2 * (jnp.arange(L) % 2)).astype(F32)[:, None]
    kr = cx[:, :nw] - sgn * sx[:, nw:]
    ki = -sx[:, :nw] - sgn * cx[:, nw:]

    def per_order(a):
        return jnp.transpose(a.reshape(L, HY_ORDER, HY_W), (1, 0, 2))

    return per_order(kr), per_order(ki)


def _hy_fwd_kernel(ac_ref, as_ref, z_ref, kr_ref, ki_ref, yr_ref, yi_ref):
    z = z_ref[0].astype(BF16)
    zr = jnp.dot(ac_ref[...], z, preferred_element_type=F32)
    zi = -jnp.dot(as_ref[...], z, preferred_element_type=F32)
    kr, ki = kr_ref[...], ki_ref[...]
    yr_ref[0] = (zr * kr - zi * ki).astype(BF16)
    yi_ref[0] = (zr * ki + zi * kr).astype(BF16)


def _hy_inv_kernel(inv_l, act_ref, ast_ref, yr_ref, yi_ref, z_ref, x_ref, bias_ref, o_ref):
    y = (jnp.dot(act_ref[...], yr_ref[0], preferred_element_type=F32)
         - jnp.dot(ast_ref[...], yi_ref[0], preferred_element_type=F32))
    o_ref[0] = x_ref[0] * (y * inv_l + bias_ref[...] * z_ref[0])


def hyena_longconv(z, xmul, mats, kr, ki, bias):
    B, L, W = z.shape
    ac, as_, act, ast = mats
    tk = min(HY_TK, L)
    a_spec = pl.BlockSpec((tk, L), lambda i, b: (i, 0))
    full_spec = pl.BlockSpec((1, L, W), lambda i, b: (b, 0, 0))
    tile_spec = pl.BlockSpec((1, tk, W), lambda i, b: (b, i, 0))
    k_spec = pl.BlockSpec((tk, W), lambda i, b: (i, 0))
    params = pltpu.CompilerParams(dimension_semantics=("parallel", "arbitrary"),
                                  vmem_limit_bytes=VMEM_LIMIT)
    yr, yi = pl.pallas_call(
        _hy_fwd_kernel,
        out_shape=[jax.ShapeDtypeStruct((B, L, W), BF16)] * 2,
        grid=(L // tk, B),
        in_specs=[a_spec, a_spec, full_spec, k_spec, k_spec],
        out_specs=[tile_spec, tile_spec],
        compiler_params=params,
        name="hyena_fwd",
    )(ac, as_, z, kr, ki)
    return pl.pallas_call(
        functools.partial(_hy_inv_kernel, 1.0 / L),
        out_shape=jax.ShapeDtypeStruct((B, L, W), F32),
        grid=(L // tk, B),
        in_specs=[a_spec, a_spec, full_spec, full_spec, tile_spec, tile_spec,
                  pl.BlockSpec((1, W), lambda i, b: (0, 0))],
        out_specs=tile_spec,
        compiler_params=params,
        name="hyena_inv",
    )(act, ast, yr, yi, z, xmul, bias.reshape(1, W))


def hyena_mix(p, conv_w, conv_b, mats, kspec, bias):
    v, x1, x2 = jnp.split(dwconv3(p, conv_w, conv_b), 3, -1)
    kr, ki = kspec
    zz = hyena_longconv(v, x1, mats, kr[0], ki[0], bias[0])
    return hyena_longconv(zz, x2, mats, kr[1], ki[1], bias[1])


def hyena_branch(p, pc, need_ctx, mats, mats_c, conv_w, conv_b, fw1, fb1, fw2, fb2, fw3, ffreq, bias):
    kspec = hyena_kspec(p.shape[1], mats, fw1, fb1, fw2, fb2, fw3, ffreq)
    y = hyena_mix(p, conv_w, conv_b, mats, kspec, bias)
    yc = None
    if need_ctx:
        kspec_c = hyena_kspec(pc.shape[1], mats_c, fw1, fb1, fw2, fb2, fw3, ffreq)
        yc = hyena_mix(pc, conv_w, conv_b, mats_c, kspec_c, bias)
    return y, yc


def rope_tables(L, d, width):
    nf = d // 4
    pos = jnp.arange(L)
    inv = ROPE_BASE ** (-jnp.arange(nf, dtype=F32) / nf)
    cs, sn = [], []
    for p in (pos // GRID_W, pos % GRID_W):
        ang = p.astype(F32)[:, None] * inv[None, :]
        cs += [jnp.cos(ang), jnp.cos(ang)]
        sn += [-jnp.sin(ang), jnp.sin(ang)]
    reps = width // d
    return jnp.tile(jnp.concatenate(cs, -1), (1, reps)), jnp.tile(jnp.concatenate(sn, -1), (1, reps))


def _rope(x, cos, sin, d):
    width = x.shape[-1]
    q = d // 4
    lane = lax.broadcasted_iota(jnp.int32, x.shape, 1)
    swapped = jnp.where(lane % (2 * q) < q, pltpu.roll(x, width - q, 1), pltpu.roll(x, q, 1))
    return x * cos + swapped * sin


def _swa_kernel(local, seq_len, q_ref, kp_ref, kc_ref, kn_ref, vp_ref, vc_ref, vn_ref,
                ck_ref, cv_ref, cos_ref, sin_ref, sink_ref, o_ref):
    n = pl.program_id(1)
    blk = SWA_BLOCK
    qscale = SWA_HD ** -0.5 * math.log2(math.e)
    q = q_ref[...].astype(F32)
    if local:
        q0 = pl.multiple_of(n * blk, blk)
        q = _rope(q, cos_ref[pl.ds(q0, blk), :], sin_ref[pl.ds(q0, blk), :], SWA_HD)
        ks, vs = [], []
        for j, (k_ref, v_ref) in enumerate(((kp_ref, vp_ref), (kc_ref, vc_ref), (kn_ref, vn_ref))):
            k0 = pl.multiple_of(jnp.clip(n + j - 1, 0, seq_len // blk - 1) * blk, blk)
            ks.append(_rope(k_ref[...].astype(F32), cos_ref[pl.ds(k0, blk), 0:SWA_KV * SWA_HD],
                            sin_ref[pl.ds(k0, blk), 0:SWA_KV * SWA_HD], SWA_HD).astype(BF16))
            vs.append(v_ref[...].astype(BF16))
        k_loc = jnp.concatenate(ks, axis=0)
        v_loc = jnp.concatenate(vs, axis=0)
        qpos = n * blk + lax.broadcasted_iota(jnp.int32, (blk, 3 * blk), 0)
        kpos = (n - 1) * blk + lax.broadcasted_iota(jnp.int32, (blk, 3 * blk), 1)
        valid = (jnp.abs(kpos - qpos) <= SWA_WIN) & (kpos >= 0) & (kpos < seq_len)
    qb = (q * qscale).astype(BF16)
    ck = ck_ref[...].astype(BF16)
    cv = cv_ref[...].astype(BF16)
    nt = (((1,), (1,)), ((), ()))
    outs = []
    for h in range(SWA_HEADS):
        g = h // (SWA_HEADS // SWA_KV)
        qh = qb[:, h * SWA_HD:(h + 1) * SWA_HD]
        gs = slice(g * SWA_HD, (g + 1) * SWA_HD)
        sink = sink_ref[h:h + 1, 0:1]
        s_ctx = lax.dot_general(qh, ck[:, gs], nt, preferred_element_type=F32)
        m = jnp.maximum(jnp.max(s_ctx, -1, keepdims=True), sink)
        if local:
            s_loc = lax.dot_general(qh, k_loc[:, gs], nt, preferred_element_type=F32)
            s_loc = jnp.where(valid, s_loc, NEG_INF)
            m = jnp.maximum(m, jnp.max(s_loc, -1, keepdims=True))
        p_ctx = jnp.exp2(s_ctx - m)
        den = jnp.sum(p_ctx, -1, keepdims=True) + jnp.exp2(sink - m)
        acc = jnp.dot(p_ctx.astype(BF16), cv[:, gs], preferred_element_type=F32)
        if local:
            p_loc = jnp.exp2(s_loc - m)
            den = den + jnp.sum(p_loc, -1, keepdims=True)
            acc = acc + jnp.dot(p_loc.astype(BF16), v_loc[:, gs], preferred_element_type=F32)
        outs.append(acc / den)
    o_ref[...] = jnp.concatenate(outs, axis=-1)


def swa_attend(p, pc, B, L, C, sink, tables, local):
    blk = SWA_BLOCK
    nb = L // blk
    qcol = P_SW // (SWA_HEADS * SWA_HD)
    kcol = (P_SW + SWA_HEADS * SWA_HD) // (SWA_KV * SWA_HD)
    kvw = SWA_KV * SWA_HD

    def nbr(j, col):
        return lambda b, n: (b * nb + jnp.clip(n + j, 0, nb - 1), col)

    cos, sin = tables
    sink_rows = jnp.broadcast_to((sink.astype(F32) * math.log2(math.e))[:, None], (SWA_HEADS, LANES))
    tab_spec = pl.BlockSpec(cos.shape, lambda b, n: (0, 0))
    out = pl.pallas_call(
        functools.partial(_swa_kernel, local, L),
        out_shape=jax.ShapeDtypeStruct((B * L, SWA_HEADS * SWA_HD), F32),
        grid=(B, nb),
        in_specs=[pl.BlockSpec((blk, SWA_HEADS * SWA_HD), lambda b, n: (b * nb + n, qcol))]
        + [pl.BlockSpec((blk, kvw), nbr(j, kcol)) for j in (-1, 0, 1)]
        + [pl.BlockSpec((blk, kvw), nbr(j, kcol + 1)) for j in (-1, 0, 1)]
        + [pl.BlockSpec((C, kvw), lambda b, n: (b, kcol)),
           pl.BlockSpec((C, kvw), lambda b, n: (b, kcol + 1)),
           tab_spec, tab_spec,
           pl.BlockSpec((SWA_HEADS, LANES), lambda b, n: (0, 0))],
        out_specs=pl.BlockSpec((blk, SWA_HEADS * SWA_HD), lambda b, n: (b * nb + n, 0)),
        compiler_params=pltpu.CompilerParams(
            dimension_semantics=("parallel", "parallel"), vmem_limit_bytes=VMEM_LIMIT),
        name="swa",
    )(p, p, p, p, p, p, p, pc, pc, cos, sin, sink_rows)
    return out.reshape(B, L, SWA_HEADS * SWA_HD)


def swa_branch(p, pc, B, L, C, need_ctx, sink, tables):
    y = swa_attend(p, pc, B, L, C, sink, tables, True)
    yc = swa_attend(pc, pc, B, C, C, sink, tables, False) if need_ctx else None
    return y, yc


def _head_blocks(value):
    row = lax.broadcasted_iota(jnp.int32, (RW_W, RW_W), 0)
    col = lax.broadcasted_iota(jnp.int32, (RW_W, RW_W), 1)
    return jnp.where(row // RW_HD == col // RW_HD, value, 0.0).astype(F32)


def _head_sum(x, blocks):
    return jnp.dot(x, blocks, preferred_element_type=F32, precision=lax.Precision.HIGHEST)


RW_OFF = tuple(int(o) for o in np.cumsum((0,) + RW_SIZES))
RW_PAD = 1280


def _rwkv_prep_kernel(tm, tiles_per_seq, x_ref, xp_ref, xn_ref, mu_ref, w0_ref, w2f_ref, w2b_ref,
                      a0_ref, a2_ref, g2f_ref, g2b_ref, kk_ref, ka_ref,
                      r_o, k_o, v_o, kkn_o, b_o, df_o, db_o, gf_o, gb_o):
    i = pl.program_id(0)
    first = (i % tiles_per_seq) == 0
    last = (i % tiles_per_seq) == tiles_per_seq - 1
    x = x_ref[...].astype(F32)
    row = lax.broadcasted_iota(jnp.int32, x.shape, 0)
    prev_row = jnp.where(first, 0.0, xp_ref[...].astype(F32)[P_HALO - 1:P_HALO, :])
    next_row = jnp.where(last, 0.0, xn_ref[...].astype(F32)[0:1, :])
    xm1 = jnp.where(row == 0, prev_row, pltpu.roll(x, 1, 0))
    xp1 = jnp.where(row == tm - 1, next_row, pltpu.roll(x, tm - 1, 0))
    x = x + (0.5 * (xm1 + xp1) - x) * mu_ref[...]

    r = x[:, RW_OFF[0]:RW_OFF[1]]
    k = x[:, RW_OFF[1]:RW_OFF[2]]
    v = x[:, RW_OFF[2]:RW_OFF[3]]
    wd = jnp.tanh(x[:, 768:896]).astype(BF16)
    lo = x[:, 896:1152]
    hi = x[:, 1024:1280]

    def decay(w2_ref, w0):
        wlog = -jax.nn.softplus(-(w0 + jnp.dot(wd, w2_ref[...], preferred_element_type=F32))) - 0.5
        return jnp.exp(-jnp.exp(wlog))

    a = jax.nn.sigmoid(a0_ref[...] + jnp.dot(lo[:, 0:LANES].astype(BF16), a2_ref[...],
                                             preferred_element_type=F32))
    kk = k * kk_ref[...]
    norm = jnp.sqrt(_head_sum(kk * kk, _head_blocks(1.0)))
    kk = kk / jnp.maximum(norm, 1e-12)
    r_o[...] = r
    k_o[...] = k * (1.0 + (a - 1.0) * ka_ref[...])
    v_o[...] = v
    kkn_o[...] = kk
    b_o[...] = kk * a
    df_o[...] = decay(w2f_ref, w0_ref[0:1, :])
    db_o[...] = decay(w2b_ref, w0_ref[1:2, :])
    gf_o[...] = jnp.dot(jax.nn.sigmoid(lo).astype(BF16), g2f_ref[...], preferred_element_type=F32)
    gb_o[...] = jnp.dot(jax.nn.sigmoid(hi).astype(BF16), g2b_ref[...], preferred_element_type=F32)


def rwkv_prep(p, B, L, mu, w0, w2, a0, a2, g2, k_k, k_a):
    tm = min(RW_PREP_TM, L)
    tiles_per_seq = L // tm
    n_tok = B * L
    hb = tm // P_HALO
    col = P_RW // RW_PAD

    def rows(n, r0, src):
        return jnp.zeros((n, RW_W), F32).at[r0:r0 + src.shape[0]].set(src).astype(BF16)

    w2f = rows(LANES, RW_OFF[3] - 768, w2[0])
    w2b = rows(LANES, RW_OFF[4] - 768, w2[1])
    a2p = rows(LANES, RW_OFF[5] - 896, a2)
    g2f = rows(RW_W, RW_OFF[6] - 896, g2[0])
    g2b = rows(RW_W, RW_OFF[7] - 1024, g2[1])
    mu_p = jnp.pad(mu, (0, RW_PAD - mu.shape[0])).reshape(1, RW_PAD)

    def const(shape):
        return pl.BlockSpec(shape, lambda i: (0,) * len(shape))

    vec = const((1, RW_W))
    out_spec = pl.BlockSpec((tm, RW_W), lambda i: (i, 0))
    return pl.pallas_call(
        functools.partial(_rwkv_prep_kernel, tm, tiles_per_seq),
        out_shape=[jax.ShapeDtypeStruct((n_tok, RW_W), F32)] * 9,
        grid=(n_tok // tm,),
        in_specs=[pl.BlockSpec((tm, RW_PAD), lambda i: (i, col)),
                  pl.BlockSpec((P_HALO, RW_PAD), lambda i: (jnp.maximum(i * hb - 1, 0), col)),
                  pl.BlockSpec((P_HALO, RW_PAD), lambda i: (jnp.minimum((i + 1) * hb, n_tok // P_HALO - 1), col)),
                  const((1, RW_PAD)), const((2, RW_W)), const((LANES, RW_W)), const((LANES, RW_W)),
                  vec, const((LANES, RW_W)), const((RW_W, RW_W)), const((RW_W, RW_W)), vec, vec],
        out_specs=[out_spec] * 9,
        compiler_params=pltpu.CompilerParams(
            dimension_semantics=("parallel",), vmem_limit_bytes=VMEM_LIMIT),
        name="rwkv_prep",
    )(p, p, p, mu_p, w0, w2f, w2b, a0.reshape(1, RW_W), a2p, g2f, g2b,
      k_k.reshape(1, RW_W), k_a.reshape(1, RW_W))


def _rwkv_scan_kernel(tblk, nb,
                      wf_ref, kkf_ref, bf_ref, kf_ref, vf_ref, rf_ref,
                      wb_ref, kkb_ref, bb_ref, kb_ref, vb_ref, rb_ref,
                      of_ref, ob_ref, sf_ref, sb_ref, sf16_ref, sb16_ref):
    c = pl.program_id(0)

    @pl.when(c == 0)
    def _():
        for ref in (sf_ref, sb_ref, sf16_ref, sb16_ref):
            ref[...] = jnp.zeros_like(ref)

    row = lax.broadcasted_iota(jnp.int32, (RW_W, RW_W), 0)
    col = lax.broadcasted_iota(jnp.int32, (RW_W, RW_W), 1)
    ones_blk = jnp.where(row // RW_HD == col // RW_HD, 1.0, 0.0).astype(BF16)
    r64 = lax.broadcasted_iota(jnp.int32, (RW_HD, RW_W), 0)
    c64 = lax.broadcasted_iota(jnp.int32, (RW_HD, RW_W), 1)
    eye_t = jnp.where(r64 == c64 % RW_HD, 1.0, 0.0).astype(BF16)
    r8 = lax.broadcasted_iota(jnp.int32, (8, RW_W), 0)
    c8 = lax.broadcasted_iota(jnp.int32, (8, RW_W), 1)
    head_sel = jnp.where(r8 == c8 // RW_HD, 1.0, 0.0).astype(BF16)

    fwd = (wf_ref, kkf_ref, bf_ref, kf_ref, vf_ref, rf_ref, sf_ref, sf16_ref)
    bwd = (wb_ref, kkb_ref, bb_ref, kb_ref, vb_ref, rb_ref, sb_ref, sb16_ref)
    nrow = nb * RW_HD

    def one_dir(idx, refs):
        w_ref, kk_ref, b_ref, k_ref, v_ref, r_ref, s_ref, s16_ref = refs
        zs = []
        for b in range(nb):
            kk16 = kk_ref[idx, b:b + 1, :].astype(BF16)
            v16 = v_ref[idx, b:b + 1, :].astype(BF16)
            lhs = jnp.concatenate([s16_ref[b] * kk16, eye_t * v16], axis=0)
            red = jnp.dot(lhs, ones_blk, preferred_element_type=F32)
            s_new = (s_ref[b] * w_ref[idx, b:b + 1, :] - red[0:RW_HD] * b_ref[idx, b:b + 1, :]
                     + red[RW_HD:2 * RW_HD] * k_ref[idx, b:b + 1, :])
            s_ref[b] = s_new
            s16 = s_new.astype(BF16)
            s16_ref[b] = s16
            zs.append(s16 * r_ref[idx, b:b + 1, :].astype(BF16))
        return jnp.concatenate(zs, axis=0)

    def emit(idx, z, o_ref):
        o_ref[idx] = lax.dot_general(head_sel, z, (((1,), (1,)), ((), ())),
                                     preferred_element_type=F32)

    def step(i, carry):
        z_f, z_b = carry
        ib = tblk - 1 - i
        emit(jnp.maximum(i - 1, 0), z_f, of_ref)
        emit(jnp.minimum(ib + 1, tblk - 1), z_b, ob_ref)
        return one_dir(i, fwd), one_dir(ib, bwd)

    z0 = jnp.zeros((nrow, RW_W), BF16)
    z_f, z_b = lax.fori_loop(0, tblk, step, (z0, z0), unroll=4)
    emit(tblk - 1, z_f, of_ref)
    emit(0, z_b, ob_ref)


def rwkv_scan_pallas(w_f, w_b, kk, bvec, k, v, r, n_ctx):
    n, nb, _ = kk.shape
    tblk = RW_TBLK
    nblk = n // tblk
    nblk_ctx = n_ctx // tblk

    def fmap(c):
        return (c, 0, 0)

    def bmap(c):
        return (jnp.where(c < nblk_ctx, nblk_ctx - 1 - c, nblk + nblk_ctx - 1 - c), 0, 0)

    blk = (tblk, nb, RW_W)
    oblk = (tblk, 8, nb * RW_HD)
    return pl.pallas_call(
        functools.partial(_rwkv_scan_kernel, tblk, nb),
        out_shape=[jax.ShapeDtypeStruct((n, 8, nb * RW_HD), F32)] * 2,
        grid=(nblk,),
        in_specs=[pl.BlockSpec(blk, fmap)] * 6 + [pl.BlockSpec(blk, bmap)] * 6,
        out_specs=[pl.BlockSpec(oblk, fmap), pl.BlockSpec(oblk, bmap)],
        scratch_shapes=[pltpu.VMEM((nb, RW_HD, RW_W), F32), pltpu.VMEM((nb, RW_HD, RW_W), F32),
                        pltpu.VMEM((nb, RW_HD, RW_W), BF16), pltpu.VMEM((nb, RW_HD, RW_W), BF16)],
        compiler_params=pltpu.CompilerParams(dimension_semantics=("arbitrary",),
                                             vmem_limit_bytes=VMEM_LIMIT),
        name="rwkv_scan",
    )(w_f, kk, bvec, k, v, r, w_b, kk, bvec, k, v, r)


def _rwkv_post_kernel(of_ref, ob_ref, r_ref, k_ref, v_ref, gf_ref, gb_ref, rk_ref, lng_ref, lnb_ref,
                      y_ref):
    mean_blk = _head_blocks(1.0 / RW_HD)

    def gn(o):
        oc = o - _head_sum(o, mean_blk)
        var = _head_sum(oc * oc, mean_blk)
        return oc * lax.rsqrt(var + RW_GN_EPS) * lng_ref[...] + lnb_ref[...]

    bonus = _head_sum(r_ref[...] * k_ref[...] * rk_ref[...], _head_blocks(1.0)) * v_ref[...]
    y_ref[...] = (gn(of_ref[...]) + bonus) * gf_ref[...] + (gn(ob_ref[...]) + bonus) * gb_ref[...]


def rwkv_post(o_f, o_b, r, k, v, g_f, g_b, r_k, lnx_g, lnx_b, L):
    n_tok = r.shape[0]
    tm = min(RW_PREP_TM, L)
    tok = pl.BlockSpec((tm, RW_W), lambda i: (i, 0))
    vec = pl.BlockSpec((1, RW_W), lambda i: (0, 0))
    return pl.pallas_call(
        _rwkv_post_kernel,
        out_shape=jax.ShapeDtypeStruct((n_tok, RW_W), F32),
        grid=(n_tok // tm,),
        in_specs=[tok] * 7 + [vec] * 3,
        out_specs=tok,
        compiler_params=pltpu.CompilerParams(
            dimension_semantics=("parallel",), vmem_limit_bytes=VMEM_LIMIT),
        name="rwkv_post",
    )(o_f, o_b, r, k, v, g_f, g_b, r_k.reshape(1, RW_W), lnx_g.reshape(1, RW_W), lnx_b.reshape(1, RW_W))


def rwkv_branch(p, pc, B, L, C, need_ctx, mu, w0, w2, a0, a2, g2, k_k, k_a, r_k, lnx_g, lnx_b):
    r, k, v, kk, bv, d_f, d_b, g_f, g_b = rwkv_prep(p, B, L, mu, w0, w2, a0, a2, g2, k_k, k_a)
    rc, kc, vc, kkc, bc, dc_f, dc_b, gc_f, gc_b = rwkv_prep(pc, B, C, mu, w0, w2, a0, a2, g2, k_k, k_a)

    def tmajor(xc, xl):
        t = jnp.concatenate([xc.reshape(B, C, RW_W), xl.reshape(B, L, RW_W)], axis=1)
        return jnp.moveaxis(t, 1, 0)

    pf, pb = rwkv_scan_pallas(tmajor(dc_f, d_f), tmajor(dc_b, d_b), tmajor(kkc, kk), tmajor(bc, bv),
                              tmajor(kc, k), tmajor(vc, v), tmajor(rc, r), C)

    def bmajor(t):
        t = t[:, :RW_HEADS].reshape(C + L, RW_HEADS, B, RW_HD)
        return jnp.transpose(t, (2, 0, 1, 3)).reshape(B, C + L, RW_W)

    pf, pb = bmajor(pf), bmajor(pb)
    y = rwkv_post(pf[:, C:].reshape(B * L, RW_W), pb[:, C:].reshape(B * L, RW_W), r, k, v, g_f, g_b,
                  r_k, lnx_g, lnx_b, L).reshape(B, L, RW_W)
    yc = None
    if need_ctx:
        yc = rwkv_post(pf[:, :C].reshape(B * C, RW_W), pb[:, :C].reshape(B * C, RW_W), rc, kc, vc,
                       gc_f, gc_b, r_k, lnx_g, lnx_b, C).reshape(B, C, RW_W)
    return y, yc


def _diff_attn_kernel(q_ref, kt_ref, v_ref, lam_ref, gain_ref, o_ref):
    q = q_ref[0]
    v = v_ref[0]
    lam = lam_ref[...]
    tq = q.shape[0]
    head_of_lane = lax.broadcasted_iota(jnp.int32, (1, DF_W), 1) // DF_VD
    acc = jnp.zeros((tq, DF_W), F32)
    for h in range(DF_HEADS):
        parts = []
        for c in range(2):
            j = 2 * h + c
            s = jnp.dot(q[:, DF_HD * j:DF_HD * (j + 1)], kt_ref[0, j],
                        preferred_element_type=F32)
            p = jnp.exp2(s - jnp.max(s, -1, keepdims=True))
            l = jnp.sum(p, -1, keepdims=True)
            pv = jnp.dot(p.astype(BF16), v, preferred_element_type=F32)
            parts.append(pv / l)
        acc = jnp.where(head_of_lane == h, parts[0] - lam * parts[1], acc)
    row = lax.broadcasted_iota(jnp.int32, (DF_W, DF_W), 0)
    col = lax.broadcasted_iota(jnp.int32, (DF_W, DF_W), 1)
    seg_mean = jnp.where(row // DF_VD == col // DF_VD, 1.0 / DF_VD, 0.0).astype(F32)
    ms = jnp.dot(acc * acc, seg_mean, preferred_element_type=F32, precision=lax.Precision.HIGHEST)
    o_ref[0] = acc * lax.rsqrt(ms + 1e-5) * gain_ref[...]


def diff_attend(q, k, v, lam, lam_init, subln_g):
    B, Lq = q.shape[:2]
    S = k.shape[1]
    tq = min(DF_TQ, Lq)
    qs = (q.astype(F32) * (DF_HD ** -0.5 * math.log2(math.e))).reshape(B, Lq, DF_W).astype(BF16)
    kt = jnp.transpose(k.reshape(B, S, 2 * DF_HEADS, DF_HD), (0, 2, 3, 1)).astype(BF16)
    vb = v.reshape(B, S, DF_W).astype(BF16)
    gain = jnp.tile(subln_g.astype(F32) * (1.0 - lam_init), DF_HEADS).reshape(1, DF_W)
    return pl.pallas_call(
        _diff_attn_kernel,
        out_shape=jax.ShapeDtypeStruct((B, Lq, DF_W), F32),
        grid=(B, Lq // tq),
        in_specs=[pl.BlockSpec((1, tq, DF_W), lambda b, i: (b, i, 0)),
                  pl.BlockSpec((1, 2 * DF_HEADS, DF_HD, S), lambda b, i: (b, 0, 0, 0)),
                  pl.BlockSpec((1, S, DF_W), lambda b, i: (b, 0, 0)),
                  pl.BlockSpec((1, 1), lambda b, i: (0, 0)),
                  pl.BlockSpec((1, DF_W), lambda b, i: (0, 0))],
        out_specs=pl.BlockSpec((1, tq, DF_W), lambda b, i: (b, i, 0)),
        compiler_params=pltpu.CompilerParams(
            dimension_semantics=("parallel", "parallel"), vmem_limit_bytes=VMEM_LIMIT),
        name="diff_attn",
    )(qs, kt, vb, lam.reshape(1, 1).astype(F32), gain)


def diff_branch(p, pc, need_ctx, lq1, lk1, lq2, lk2, subln_g, lam_init):
    q, k, v = split_at(p, DF_SIZES)
    qc, kc, vc = split_at(pc, DF_SIZES)

    def qk(t):
        return t.reshape(t.shape[0], t.shape[1], DF_HEADS, 2, DF_HD)

    def vs(t):
        return t.reshape(t.shape[0], t.shape[1], DF_HEADS, DF_VD)

    lam = (jnp.exp(jnp.sum(lq1.astype(F32) * lk1.astype(F32)))
           - jnp.exp(jnp.sum(lq2.astype(F32) * lk2.astype(F32))) + lam_init)
    k_all = jnp.concatenate([axial_rope(qk(k)), qk(kc)], axis=1)
    v_all = jnp.concatenate([vs(v), vs(vc)], axis=1)
    y = diff_attend(axial_rope(qk(q)), k_all, v_all, lam, lam_init, subln_g)
    yc = diff_attend(qk(qc), qk(kc), vs(vc), lam, lam_init, subln_g) if need_ctx else None
    return y, yc


def _ln_rows(x):
    mu = jnp.mean(x, -1, keepdims=True)
    xc = x - mu
    return xc * lax.rsqrt(jnp.mean(xc * xc, -1, keepdims=True) + LN_EPS)


def _ffn_kernel(tm, tiles_per_seq, n_chunks,
                h_ref, hprev_ref, hnext_ref, sh_ref, sc_ref, gate_ref,
                wa_ref, wb_ref, cwa_ref, cwb_ref, cba_ref, cbb_ref, wd_ref, lng_ref, lnb_ref,
                o_ref, u_ref, acc_ref):
    i = pl.program_id(0)
    j = pl.program_id(1)
    n = tm + 2 * HALO

    @pl.when(j == 0)
    def _():
        scale = 1.0 + sc_ref[0]
        shift = sh_ref[0]
        first = (i % tiles_per_seq) == 0
        last = (i % tiles_per_seq) == tiles_per_seq - 1
        u_ref[0:tm, :] = (_ln_rows(h_ref[...]) * scale + shift).astype(BF16)
        un = _ln_rows(hnext_ref[...]) * scale + shift
        up = _ln_rows(hprev_ref[...]) * scale + shift
        u_ref[tm:tm + HALO, :] = jnp.where(last, 0.0, un).astype(BF16)
        u_ref[tm + HALO:n, :] = jnp.where(first, 0.0, up).astype(BF16)
        acc_ref[...] = jnp.zeros_like(acc_ref)

    u = u_ref[...]

    def conv(w_ref, cw_ref, cb_ref):
        x = jnp.dot(u, w_ref[...], preferred_element_type=F32)
        cw = cw_ref[...]
        y = (pltpu.roll(x, 1, 0)[0:tm] * cw[0:1] + x[0:tm] * cw[1:2]
             + pltpu.roll(x, n - 1, 0)[0:tm] * cw[2:3] + cb_ref[...])
        return y

    a = conv(wa_ref, cwa_ref, cba_ref)
    b = conv(wb_ref, cwb_ref, cbb_ref)
    g = (a * jax.nn.sigmoid(a) * b).astype(BF16)
    acc_ref[...] += jnp.dot(g, wd_ref[...], preferred_element_type=F32)

    @pl.when(j == n_chunks - 1)
    def _():
        y = DN_ALPHA * h_ref[...] + gate_ref[0] * acc_ref[...]
        o_ref[...] = _ln_rows(y) * lng_ref[...] + lnb_ref[...]


def ffn_block(h, shift, scale, gate, w_up, conv_w, conv_b, w_down, ln_g, ln_b):
    B, L, D = h.shape
    tm = min(FFN_TM, L)
    tiles_per_seq = L // tm
    n_tok = B * L
    n_chunks = D_FF // FFN_CHUNK
    hb = tm // HALO

    def bcast(t):
        return jnp.broadcast_to(t.astype(F32), (B, 1, D))

    wu = w_up.astype(BF16)
    wd = w_down.astype(BF16)
    cb = conv_b.reshape(1, 2 * D_FF)
    h2 = h.reshape(n_tok, D)
    mod_spec = pl.BlockSpec((1, 1, D), lambda i, j: (i // tiles_per_seq, 0, 0))
    vec_spec = pl.BlockSpec((1, D), lambda i, j: (0, 0))
    out = pl.pallas_call(
        functools.partial(_ffn_kernel, tm, tiles_per_seq, n_chunks),
        out_shape=jax.ShapeDtypeStruct((n_tok, D), F32),
        grid=(n_tok // tm, n_chunks),
        in_specs=[pl.BlockSpec((tm, D), lambda i, j: (i, 0)),
                  pl.BlockSpec((HALO, D), lambda i, j: (jnp.maximum(i * hb - 1, 0), 0)),
                  pl.BlockSpec((HALO, D), lambda i, j: (jnp.minimum((i + 1) * hb, n_tok // HALO - 1), 0)),
                  mod_spec, mod_spec, mod_spec,
                  pl.BlockSpec((D, FFN_CHUNK), lambda i, j: (0, j)),
                  pl.BlockSpec((D, FFN_CHUNK), lambda i, j: (0, n_chunks + j)),
                  pl.BlockSpec((3, FFN_CHUNK), lambda i, j: (0, j)),
                  pl.BlockSpec((3, FFN_CHUNK), lambda i, j: (0, n_chunks + j)),
                  pl.BlockSpec((1, FFN_CHUNK), lambda i, j: (0, j)),
                  pl.BlockSpec((1, FFN_CHUNK), lambda i, j: (0, n_chunks + j)),
                  pl.BlockSpec((FFN_CHUNK, D), lambda i, j: (j, 0)),
                  vec_spec, vec_spec],
        out_specs=pl.BlockSpec((tm, D), lambda i, j: (i, 0)),
        scratch_shapes=[pltpu.VMEM((tm + 2 * HALO, D), BF16), pltpu.VMEM((tm, D), F32)],
        compiler_params=pltpu.CompilerParams(
            dimension_semantics=("parallel", "arbitrary"), vmem_limit_bytes=VMEM_LIMIT),
        name="conv_ffn",
    )(h2, h2, h2, bcast(shift), bcast(scale), bcast(gate), wu, wu, conv_w, conv_w, cb, cb, wd,
      ln_g.reshape(1, D), ln_b.reshape(1, D))
    return out.reshape(B, L, D)


P_GATE, P_HY, P_RW, P_SW, P_DF = 0, 4096, 5120, 6400, 6912
P_COLS = 7680
IN_CHUNK = 512


def _in_proj_kernel(x_ref, sh_ref, sc_ref, w_ref, o_ref, u_ref):
    @pl.when(pl.program_id(1) == 0)
    def _():
        u_ref[...] = (_ln_rows(x_ref[...]) * (1.0 + sc_ref[0]) + sh_ref[0]).astype(BF16)

    o_ref[...] = jnp.dot(u_ref[...], w_ref[...], preferred_element_type=F32).astype(BF16)


def in_proj(h, shift, scale, w_in_p):
    B, L, D = h.shape
    tm = min(IN_TM, L)
    tiles_per_seq = L // tm
    n_tok = B * L

    def bcast(t):
        return jnp.broadcast_to(t.astype(F32), (B, 1, D))

    mod_spec = pl.BlockSpec((1, 1, D), lambda i, j: (i // tiles_per_seq, 0, 0))
    return pl.pallas_call(
        _in_proj_kernel,
        out_shape=jax.ShapeDtypeStruct((n_tok, P_COLS), BF16),
        grid=(n_tok // tm, P_COLS // IN_CHUNK),
        in_specs=[pl.BlockSpec((tm, D), lambda i, j: (i, 0)), mod_spec, mod_spec,
                  pl.BlockSpec((D, IN_CHUNK), lambda i, j: (0, j))],
        out_specs=pl.BlockSpec((tm, IN_CHUNK), lambda i, j: (i, j)),
        scratch_shapes=[pltpu.VMEM((tm, D), BF16)],
        compiler_params=pltpu.CompilerParams(
            dimension_semantics=("parallel", "arbitrary"), vmem_limit_bytes=VMEM_LIMIT),
        name="in_proj",
    )(h.reshape(n_tok, D), bcast(shift), bcast(scale), w_in_p)


def pack_w_in(w):
    offs = np.cumsum((0,) + IN_SIZES)
    hy, sw, rw, df, gt = [w[:, offs[j]:offs[j + 1]] for j in range(len(IN_SIZES))]

    def padto(t, n):
        return jnp.pad(t, ((0, 0), (0, n - t.shape[1])))

    return jnp.concatenate([gt, padto(hy, P_RW - P_HY), padto(rw, P_SW - P_RW), sw, df],
                           axis=1).astype(BF16)


def _merge_kernel(yh_ref, ys_ref, yr_ref, yd_ref, g0_ref, g1_ref, g2_ref, g3_ref, h_ref, gate_ref,
                  wbr_ref, wo_ref, lng_ref, lnb_ref, o_ref):
    acc = None
    for j, (y_ref, g_ref) in enumerate(((yh_ref, g0_ref), (ys_ref, g1_ref), (yr_ref, g2_ref),
                                        (yd_ref, g3_ref))):
        term = jax.nn.sigmoid(g_ref[...].astype(F32)) * jnp.dot(y_ref[...].astype(BF16), wbr_ref[j],
                                                    preferred_element_type=F32)
        acc = term if acc is None else acc + term
    mix = jnp.dot(acc.astype(BF16), wo_ref[...], preferred_element_type=F32)
    y = DN_ALPHA * h_ref[...] + gate_ref[0] * mix
    o_ref[...] = _ln_rows(y) * lng_ref[...] + lnb_ref[...]


def merge_block(ys, p, h, gate, w_br, w_o, ln_g, ln_b):
    B, L, D = h.shape
    n_tok = B * L
    tm = min(MERGE_TM, L)
    tiles_per_seq = L // tm
    y_spec = pl.BlockSpec((tm, BR_W), lambda i: (i, 0))
    vec_spec = pl.BlockSpec((1, D), lambda i: (0, 0))
    g_specs = [pl.BlockSpec((tm, D), functools.partial(lambda j, i: (i, j), j)) for j in range(N_BRANCH)]
    out = pl.pallas_call(
        _merge_kernel,
        out_shape=jax.ShapeDtypeStruct((n_tok, D), F32),
        grid=(n_tok // tm,),
        in_specs=[y_spec] * N_BRANCH + g_specs + [
            pl.BlockSpec((tm, D), lambda i: (i, 0)),
            pl.BlockSpec((1, 1, D), lambda i: (i // tiles_per_seq, 0, 0)),
            pl.BlockSpec((N_BRANCH, BR_W, D), lambda i: (0, 0, 0)),
            pl.BlockSpec((D, D), lambda i: (0, 0)), vec_spec, vec_spec],
        out_specs=pl.BlockSpec((tm, D), lambda i: (i, 0)),
        compiler_params=pltpu.CompilerParams(
            dimension_semantics=("parallel",), vmem_limit_bytes=VMEM_LIMIT),
        name="merge",
    )(*[y.reshape(n_tok, BR_W) for y in ys], p, p, p, p, h.reshape(n_tok, D),
      jnp.broadcast_to(gate.astype(F32), (B, 1, D)), w_br.astype(BF16), w_o.astype(BF16),
      ln_g.reshape(1, D), ln_b.reshape(1, D))
    return out.reshape(B, L, D)


def kernel(x, c, ctx, c_ctx, ada_w, ada_b, w_in, hy_conv_w, hy_conv_b, hy_f_w1, hy_f_b1,
           hy_f_w2, hy_f_b2, hy_f_w3, hy_f_freq, hy_bias, swa_sink, rwkv_mu, rwkv_w0, rwkv_w2,
           rwkv_a0, rwkv_a2, rwkv_g2, rwkv_kk, rwkv_ka, rwkv_rk, rwkv_lnx_g, rwkv_lnx_b,
           diff_lq1, diff_lk1, diff_lq2, diff_lk2, diff_subln_g, w_branch, w_out, ln1_g, ln1_b,
           ffn_w_up, ffn_conv_w, ffn_conv_b, ffn_w_down, ln2_g, ln2_b):
    h, hc = x, ctx
    mats, mats_c = dft_mats(x.shape[1]), dft_mats(ctx.shape[1])
    swa_tables = rope_tables(x.shape[1], SWA_HD, SWA_HEADS * SWA_HD)
    s_lat = jax.nn.silu(c)
    s_ctx = jax.nn.silu(c_ctx)
    for i in range(DEPTH):
        need_ctx = i < DEPTH - 1
        mod = (s_lat @ ada_w[i] + ada_b[i])[:, None, :]
        mod_c = s_ctx @ ada_w[i] + ada_b[i]
        sh1, sc1, g1, sh2, sc2, g2 = jnp.split(mod, 6, -1)
        csh1, csc1, cg1, csh2, csc2, cg2 = jnp.split(mod_c, 6, -1)

        B, L, D = h.shape
        C = hc.shape[1]
        w_in_p = pack_w_in(w_in[i])
        p = in_proj(h, sh1, sc1, w_in_p)
        pc = in_proj(hc, csh1.reshape(1, 1, D), csc1.reshape(1, 1, D), w_in_p)

        def seg(t, n, off, width):
            return t[:, off:off + width].astype(F32).reshape(B, n, width)

        p_hy, p_df = seg(p, L, P_HY, IN_SIZES[0]), seg(p, L, P_DF, IN_SIZES[3])
        pc_hy, pc_df = seg(pc, C, P_HY, IN_SIZES[0]), seg(pc, C, P_DF, IN_SIZES[3])

        y_hy, yc_hy = hyena_branch(p_hy, pc_hy, need_ctx, mats, mats_c, hy_conv_w[i], hy_conv_b[i], hy_f_w1[i],
                                   hy_f_b1[i], hy_f_w2[i], hy_f_b2[i], hy_f_w3[i], hy_f_freq[i],
                                   hy_bias[i])
        y_sw, yc_sw = swa_branch(p, pc, B, L, C, need_ctx, swa_sink[i], swa_tables)
        y_rw, yc_rw = rwkv_branch(p, pc, B, L, C, need_ctx, rwkv_mu[i], rwkv_w0[i], rwkv_w2[i],
                                  rwkv_a0[i], rwkv_a2[i], rwkv_g2[i], rwkv_kk[i], rwkv_ka[i],
                                  rwkv_rk[i], rwkv_lnx_g[i], rwkv_lnx_b[i])
        lam_init = 0.8 - 0.6 * math.exp(-0.3 * i)
        y_df, yc_df = diff_branch(p_df, pc_df, need_ctx, diff_lq1[i], diff_lk1[i], diff_lq2[i],
                                  diff_lk2[i], diff_subln_g[i], lam_init)

        h = merge_block((y_hy, y_sw, y_rw, y_df), p, h, g1, w_branch[i], w_out[i], ln1_g[i], ln1_b[i])
        h = ffn_block(h, sh2, sc2, g2, ffn_w_up[i], ffn_conv_w[i], ffn_conv_b[i], ffn_w_down[i],
                      ln2_g[i], ln2_b[i])

        if need_ctx:
            hc = merge_block((yc_hy, yc_sw, yc_rw, yc_df), pc, hc, cg1.reshape(1, 1, D), w_branch[i],
                             w_out[i], ln1_g[i], ln1_b[i])
            hc = ffn_block(hc, csh2.reshape(1, 1, -1), csc2.reshape(1, 1, -1), cg2.reshape(1, 1, -1),
                           ffn_w_up[i], ffn_conv_w[i], ffn_conv_b[i], ffn_w_down[i], ln2_g[i], ln2_b[i])
    return h
```

```python
import functools
import math

import jax
import jax.numpy as jnp
import numpy as np
from jax import lax
from jax.experimental import pallas as pl
from jax.experimental.pallas import tpu as pltpu

D_MODEL = 1024
DEPTH = 2
GRID_W = 64
ROPE_BASE = 10000.0
F32 = jnp.float32
BF16 = jnp.bfloat16
NEG_INF = -1e30
LN_EPS = 1e-6

HY_W = 256
HY_ORDER = 2
HY_DIRS = 2
HY_EMB = 33
HY_MIN_DECAY = math.log(1e-2) / 1.5
HY_MAX_DECAY = math.log(1e-2) / 0.3

SWA_HEADS = 4
SWA_KV = 2
SWA_HD = 64
SWA_WIN = 128
SWA_BLOCK = 128

RW_HEADS = 4
RW_HD = 64
RW_W = RW_HEADS * RW_HD
RW_DECAY_R = 64
RW_AAA_R = 64
RW_GATE_R = 128
RW_GN_EPS = 64e-5

DF_HEADS = 4
DF_HD = 32
DF_VD = 2 * DF_HD
DF_W = DF_HEADS * DF_VD

N_BRANCH = 4
BR_W = 256
D_FF = 2816
DN_ALPHA = (2 * DEPTH) ** 0.25

SWA_SIZES = (SWA_HEADS * SWA_HD, SWA_KV * SWA_HD, SWA_KV * SWA_HD)
RW_SIZES = (RW_W, RW_W, RW_W, RW_DECAY_R, RW_DECAY_R, RW_AAA_R, RW_GATE_R, RW_GATE_R)
DF_SIZES = (DF_W, DF_W, DF_W)
IN_SIZES = (3 * HY_W, sum(SWA_SIZES), sum(RW_SIZES), sum(DF_SIZES), N_BRANCH * D_MODEL)

LANES = 128
MM_TM = 512
MM_TN = 512
RW_TBLK = 64
DF_TQ = 256
HALO = 8
P_HALO = 16
IN_TM = 2048
FFN_TM = 1024
FFN_CHUNK = 256
MERGE_TM = 512
HY_TK = 512
RW_PREP_TM = 512
VMEM_LIMIT = 48 * 1024 * 1024


def _mm_kernel(x_ref, w_ref, o_ref):
    o_ref[...] = jnp.dot(x_ref[...].astype(BF16), w_ref[...], preferred_element_type=F32)


def _mm(x, w):
    m, k = x.shape
    n = w.shape[1]
    n_pad = -(-n // LANES) * LANES
    wb = w.astype(BF16)
    if n_pad != n:
        wb = jnp.pad(wb, ((0, 0), (0, n_pad - n)))
    tn = MM_TN if n_pad % MM_TN == 0 else (256 if n_pad % 256 == 0 else LANES)
    tm = MM_TM if m % MM_TM == 0 else m
    out = pl.pallas_call(
        _mm_kernel,
        out_shape=jax.ShapeDtypeStruct((m, n_pad), F32),
        grid=(m // tm, n_pad // tn),
        in_specs=[pl.BlockSpec((tm, k), lambda i, j: (i, 0)),
                  pl.BlockSpec((k, tn), lambda i, j: (0, j))],
        out_specs=pl.BlockSpec((tm, tn), lambda i, j: (i, j)),
        compiler_params=pltpu.CompilerParams(
            dimension_semantics=("parallel", "parallel"), vmem_limit_bytes=VMEM_LIMIT),
        name="matmul",
    )(x, wb)
    return out[:, :n] if n_pad != n else out


def split_at(t, sizes):
    return jnp.split(t, np.cumsum(sizes)[:-1].tolist(), axis=-1)


def dwconv3(x, w, b):
    xp = jnp.pad(x, ((0, 0), (1, 1), (0, 0)))
    return xp[:, :-2] * w[0] + x * w[1] + xp[:, 2:] * w[2] + b


def axial_rope(x):
    L, d = x.shape[1], x.shape[-1]
    rows = L // GRID_W
    row = jnp.repeat(jnp.arange(rows), GRID_W)
    col = jnp.tile(jnp.arange(GRID_W), rows)
    nf = d // 4
    inv = ROPE_BASE ** (-jnp.arange(nf, dtype=F32) / nf)
    bshape = (L,) + (1,) * (x.ndim - 3) + (nf,)
    xf = x.astype(F32)
    out = []
    for half, pos in enumerate((row, col)):
        ang = (pos.astype(F32)[:, None] * inv[None, :]).reshape(bshape)
        cos, sin = jnp.cos(ang), jnp.sin(ang)
        xh = xf[..., half * 2 * nf:(half + 1) * 2 * nf]
        x1, x2 = xh[..., :nf], xh[..., nf:]
        out += [x1 * cos - x2 * sin, x1 * sin + x2 * cos]
    return jnp.concatenate(out, -1).astype(x.dtype)


def dft_mats(L):
    tb = 64
    k = jnp.arange(L, dtype=jnp.int32)[:, None]

    def table(t):
        ang = (((2 * k + 1) * t[None, :]) % (4 * L)).astype(F32) * (math.pi / (2 * L))
        return jnp.cos(ang), jnp.sin(ang)

    ca, sa = table(tb * jnp.arange(L // tb, dtype=jnp.int32))
    cb, sb = table(jnp.arange(tb, dtype=jnp.int32))
    c = (ca[:, :, None] * cb[:, None, :] - sa[:, :, None] * sb[:, None, :]).reshape(L, L)
    s = (sa[:, :, None] * cb[:, None, :] + ca[:, :, None] * sb[:, None, :]).reshape(L, L)
    ct = (ca.T[:, None, :] * cb.T[None, :, :] - sa.T[:, None, :] * sb.T[None, :, :]).reshape(L, L)
    st = (sa.T[:, None, :] * cb.T[None, :, :] + ca.T[:, None, :] * sb.T[None, :, :]).reshape(L, L)
    return c.astype(BF16), s.astype(BF16), ct.astype(BF16), st.astype(BF16)


def hyena_kspec(L, mats, w1, b1, w2, b2, w3, freq):
    t = jnp.linspace(0.0, 1.0, L, dtype=F32)[:, None]
    bands = (HY_EMB - 1) // 2
    w = 2.0 * math.pi * jnp.arange(L, dtype=F32) / L
    fb = jnp.linspace(1e-4, bands - 1, bands, dtype=F32)
    ang = w[:, None] * fb[None, :]
    z = jnp.concatenate([t, jnp.cos(ang), -jnp.sin(ang)], -1)
    fr = freq.astype(F32)
    hdn = jnp.sin(fr * (z @ w1.astype(F32) + b1.astype(F32)))
    hdn = jnp.sin(fr * (hdn @ w2.astype(F32) + b2.astype(F32)))
    filt = (hdn @ w3.astype(F32)).reshape(L, HY_ORDER, HY_DIRS, HY_W)
    deltas = jnp.abs(jnp.linspace(HY_MIN_DECAY, HY_MAX_DECAY, HY_W, dtype=F32))
    filt = filt * jnp.exp(-t * deltas[None, :])[:, None, None, :]
    filt = filt / jnp.sum(jnp.abs(filt), axis=(0, 2), keepdims=True)
    fwd, bwd = filt[:, :, 0], filt[:, :, 1]
    hi = jnp.concatenate([jnp.zeros_like(bwd[:1]), -bwd[:0:-1]], axis=0)
    nw = HY_ORDER * HY_W
    x = jnp.concatenate([fwd.reshape(L, nw), hi.reshape(L, nw)], axis=1)
    cx, sx = _mm(mats[0], x), _mm(mats[1], x)
    sgn = (1 - 2 * (jnp.arange(L) % 2)).astype(F32)[:, None]
    kr = cx[:, :nw] - sgn * sx[:, nw:]
    ki = -sx[:, :nw] - sgn * cx[:, nw:]

    def per_order(a):
        return jnp.transpose(a.reshape(L, HY_ORDER, HY_W), (1, 0, 2))

    return per_order(kr), per_order(ki)


def _hy_fwd_kernel(ac_ref, as_ref, z_ref, kr_ref, ki_ref, yr_ref, yi_ref):
    z = z_ref[0].astype(BF16)
    zr = jnp.dot(ac_ref[...], z, preferred_element_type=F32)
    zi = -jnp.dot(as_ref[...], z, preferred_element_type=F32)
    kr, ki = kr_ref[...], ki_ref[...]
    yr_ref[0] = (zr * kr - zi * ki).astype(BF16)
    yi_ref[0] = (zr * ki + zi * kr).astype(BF16)


def _hy_inv_kernel(inv_l, act_ref, ast_ref, yr_ref, yi_ref, z_ref, x_ref, bias_ref, o_ref):
    y = (jnp.dot(act_ref[...], yr_ref[0], preferred_element_type=F32)
         - jnp.dot(ast_ref[...], yi_ref[0], preferred_element_type=F32))
    o_ref[0] = x_ref[0] * (y * inv_l + bias_ref[...] * z_ref[0])


def hyena_longconv(z, xmul, mats, kr, ki, bias):
    B, L, W = z.shape
    ac, as_, act, ast = mats
    tk = min(HY_TK, L)
    a_spec = pl.BlockSpec((tk, L), lambda i, b: (i, 0))
    full_spec = pl.BlockSpec((1, L, W), lambda i, b: (b, 0, 0))
    tile_spec = pl.BlockSpec((1, tk, W), lambda i, b: (b, i, 0))
    k_spec = pl.BlockSpec((tk, W), lambda i, b: (i, 0))
    params = pltpu.CompilerParams(dimension_semantics=("parallel", "arbitrary"),
                                  vmem_limit_bytes=VMEM_LIMIT)
    yr, yi = pl.pallas_call(
        _hy_fwd_kernel,
        out_shape=[jax.ShapeDtypeStruct((B, L, W), BF16)] * 2,
        grid=(L // tk, B),
        in_specs=[a_spec, a_spec, full_spec, k_spec, k_spec],
        out_specs=[tile_spec, tile_spec],
        compiler_params=params,
        name="hyena_fwd",
    )(ac, as_, z, kr, ki)
    return pl.pallas_call(
        functools.partial(_hy_inv_kernel, 1.0 / L),
        out_shape=jax.ShapeDtypeStruct((B, L, W), F32),
        grid=(L // tk, B),
        in_specs=[a_spec, a_spec, full_spec, full_spec, tile_spec, tile_spec,
                  pl.BlockSpec((1, W), lambda i, b: (0, 0))],
        out_specs=tile_spec,
        compiler_params=params,
        name="hyena_inv",
    )(act, ast, yr, yi, z, xmul, bias.reshape(1, W))


def hyena_mix(p, conv_w, conv_b, mats, kspec, bias):
    v, x1, x2 = jnp.split(dwconv3(p, conv_w, conv_b), 3, -1)
    kr, ki = kspec
    zz = hyena_longconv(v, x1, mats, kr[0], ki[0], bias[0])
    return hyena_longconv(zz, x2, mats, kr[1], ki[1], bias[1])


def hyena_branch(p, pc, need_ctx, mats, mats_c, conv_w, conv_b, fw1, fb1, fw2, fb2, fw3, ffreq, bias):
    kspec = hyena_kspec(p.shape[1], mats, fw1, fb1, fw2, fb2, fw3, ffreq)
    y = hyena_mix(p, conv_w, conv_b, mats, kspec, bias)
    yc = None
    if need_ctx:
        kspec_c = hyena_kspec(pc.shape[1], mats_c, fw1, fb1, fw2, fb2, fw3, ffreq)
        yc = hyena_mix(pc, conv_w, conv_b, mats_c, kspec_c, bias)
    return y, yc


def rope_tables(L, d, width):
    nf = d // 4
    pos = jnp.arange(L)
    inv = ROPE_BASE ** (-jnp.arange(nf, dtype=F32) / nf)
    cs, sn = [], []
    for p in (pos // GRID_W, pos % GRID_W):
        ang = p.astype(F32)[:, None] * inv[None, :]
        cs += [jnp.cos(ang), jnp.cos(ang)]
        sn += [-jnp.sin(ang), jnp.sin(ang)]
    reps = width // d
    return jnp.tile(jnp.concatenate(cs, -1), (1, reps)), jnp.tile(jnp.concatenate(sn, -1), (1, reps))


def _rope(x, cos, sin, d):
    width = x.shape[-1]
    q = d // 4
    lane = lax.broadcasted_iota(jnp.int32, x.shape, 1)
    swapped = jnp.where(lane % (2 * q) < q, pltpu.roll(x, width - q, 1), pltpu.roll(x, q, 1))
    return x * cos + swapped * sin


def _swa_kernel(local, seq_len, q_ref, kp_ref, kc_ref, kn_ref, vp_ref, vc_ref, vn_ref,
                ck_ref, cv_ref, cos_ref, sin_ref, sink_ref, o_ref):
    n = pl.program_id(1)
    blk = SWA_BLOCK
    qscale = SWA_HD ** -0.5 * math.log2(math.e)
    q = q_ref[...].astype(F32)
    if local:
        q0 = pl.multiple_of(n * blk, blk)
        q = _rope(q, cos_ref[pl.ds(q0, blk), :], sin_ref[pl.ds(q0, blk), :], SWA_HD)
        ks, vs = [], []
        for j, (k_ref, v_ref) in enumerate(((kp_ref, vp_ref), (kc_ref, vc_ref), (kn_ref, vn_ref))):
            k0 = pl.multiple_of(jnp.clip(n + j - 1, 0, seq_len // blk - 1) * blk, blk)
            ks.append(_rope(k_ref[...].astype(F32), cos_ref[pl.ds(k0, blk), 0:SWA_KV * SWA_HD],
                            sin_ref[pl.ds(k0, blk), 0:SWA_KV * SWA_HD], SWA_HD).astype(BF16))
            vs.append(v_ref[...].astype(BF16))
        k_loc = jnp.concatenate(ks, axis=0)
        v_loc = jnp.concatenate(vs, axis=0)
        qpos = n * blk + lax.broadcasted_iota(jnp.int32, (blk, 3 * blk), 0)
        kpos = (n - 1) * blk + lax.broadcasted_iota(jnp.int32, (blk, 3 * blk), 1)
        valid = (jnp.abs(kpos - qpos) <= SWA_WIN) & (kpos >= 0) & (kpos < seq_len)
    qb = (q * qscale).astype(BF16)
    ck = ck_ref[...].astype(BF16)
    cv = cv_ref[...].astype(BF16)
    nt = (((1,), (1,)), ((), ()))
    outs = []
    for h in range(SWA_HEADS):
        g = h // (SWA_HEADS // SWA_KV)
        qh = qb[:, h * SWA_HD:(h + 1) * SWA_HD]
        gs = slice(g * SWA_HD, (g + 1) * SWA_HD)
        sink = sink_ref[h:h + 1, 0:1]
        s_ctx = lax.dot_general(qh, ck[:, gs], nt, preferred_element_type=F32)
        m = jnp.maximum(jnp.max(s_ctx, -1, keepdims=True), sink)
        if local:
            s_loc = lax.dot_general(qh, k_loc[:, gs], nt, preferred_element_type=F32)
            s_loc = jnp.where(valid, s_loc, NEG_INF)
            m = jnp.maximum(m, jnp.max(s_loc, -1, keepdims=True))
        p_ctx = jnp.exp2(s_ctx - m)
        den = jnp.sum(p_ctx, -1, keepdims=True) + jnp.exp2(sink - m)
        acc = jnp.dot(p_ctx.astype(BF16), cv[:, gs], preferred_element_type=F32)
        if local:
            p_loc = jnp.exp2(s_loc - m)
            den = den + jnp.sum(p_loc, -1, keepdims=True)
            acc = acc + jnp.dot(p_loc.astype(BF16), v_loc[:, gs], preferred_element_type=F32)
        outs.append(acc / den)
    o_ref[...] = jnp.concatenate(outs, axis=-1)


def swa_attend(p, pc, B, L, C, sink, tables, local):
    blk = SWA_BLOCK
    nb = L // blk
    qcol = P_SW // (SWA_HEADS * SWA_HD)
    kcol = (P_SW + SWA_HEADS * SWA_HD) // (SWA_KV * SWA_HD)
    kvw = SWA_KV * SWA_HD

    def nbr(j, col):
        return lambda b, n: (b * nb + jnp.clip(n + j, 0, nb - 1), col)

    cos, sin = tables
    sink_rows = jnp.broadcast_to((sink.astype(F32) * math.log2(math.e))[:, None], (SWA_HEADS, LANES))
    tab_spec = pl.BlockSpec(cos.shape, lambda b, n: (0, 0))
    out = pl.pallas_call(
        functools.partial(_swa_kernel, local, L),
        out_shape=jax.ShapeDtypeStruct((B * L, SWA_HEADS * SWA_HD), F32),
        grid=(B, nb),
        in_specs=[pl.BlockSpec((blk, SWA_HEADS * SWA_HD), lambda b, n: (b * nb + n, qcol))]
        + [pl.BlockSpec((blk, kvw), nbr(j, kcol)) for j in (-1, 0, 1)]
        + [pl.BlockSpec((blk, kvw), nbr(j, kcol + 1)) for j in (-1, 0, 1)]
        + [pl.BlockSpec((C, kvw), lambda b, n: (b, kcol)),
           pl.BlockSpec((C, kvw), lambda b, n: (b, kcol + 1)),
           tab_spec, tab_spec,
           pl.BlockSpec((SWA_HEADS, LANES), lambda b, n: (0, 0))],
        out_specs=pl.BlockSpec((blk, SWA_HEADS * SWA_HD), lambda b, n: (b * nb + n, 0)),
        compiler_params=pltpu.CompilerParams(
            dimension_semantics=("parallel", "parallel"), vmem_limit_bytes=VMEM_LIMIT),
        name="swa",
    )(p, p, p, p, p, p, p, pc, pc, cos, sin, sink_rows)
    return out.reshape(B, L, SWA_HEADS * SWA_HD)


def swa_branch(p, pc, B, L, C, need_ctx, sink, tables):
    y = swa_attend(p, pc, B, L, C, sink, tables, True)
    yc = swa_attend(pc, pc, B, C, C, sink, tables, False) if need_ctx else None
    return y, yc


def _head_blocks(value):
    row = lax.broadcasted_iota(jnp.int32, (RW_W, RW_W), 0)
    col = lax.broadcasted_iota(jnp.int32, (RW_W, RW_W), 1)
    return jnp.where(row // RW_HD == col // RW_HD, value, 0.0).astype(F32)


def _head_sum(x, blocks):
    return jnp.dot(x, blocks, preferred_element_type=F32, precision=lax.Precision.HIGHEST)


RW_OFF = tuple(int(o) for o in np.cumsum((0,) + RW_SIZES))
RW_PAD = 1280


def _rwkv_prep_kernel(tm, tiles_per_seq, x_ref, xp_ref, xn_ref, mu_ref, w0_ref, w2f_ref, w2b_ref,
                      a0_ref, a2_ref, g2f_ref, g2b_ref, kk_ref, ka_ref,
                      r_o, k_o, v_o, kkn_o, b_o, df_o, db_o, gf_o, gb_o):
    i = pl.program_id(0)
    first = (i % tiles_per_seq) == 0
    last = (i % tiles_per_seq) == tiles_per_seq - 1
    x = x_ref[...].astype(F32)
    row = lax.broadcasted_iota(jnp.int32, x.shape, 0)
    prev_row = jnp.where(first, 0.0, xp_ref[...].astype(F32)[P_HALO - 1:P_HALO, :])
    next_row = jnp.where(last, 0.0, xn_ref[...].astype(F32)[0:1, :])
    xm1 = jnp.where(row == 0, prev_row, pltpu.roll(x, 1, 0))
    xp1 = jnp.where(row == tm - 1, next_row, pltpu.roll(x, tm - 1, 0))
    x = x + (0.5 * (xm1 + xp1) - x) * mu_ref[...]

    r = x[:, RW_OFF[0]:RW_OFF[1]]
    k = x[:, RW_OFF[1]:RW_OFF[2]]
    v = x[:, RW_OFF[2]:RW_OFF[3]]
    wd = jnp.tanh(x[:, 768:896]).astype(BF16)
    lo = x[:, 896:1152]
    hi = x[:, 1024:1280]

    def decay(w2_ref, w0):
        wlog = -jax.nn.softplus(-(w0 + jnp.dot(wd, w2_ref[...], preferred_element_type=F32))) - 0.5
        return jnp.exp(-jnp.exp(wlog))

    a = jax.nn.sigmoid(a0_ref[...] + jnp.dot(lo[:, 0:LANES].astype(BF16), a2_ref[...],
                                             preferred_element_type=F32))
    kk = k * kk_ref[...]
    norm = jnp.sqrt(_head_sum(kk * kk, _head_blocks(1.0)))
    kk = kk / jnp.maximum(norm, 1e-12)
    r_o[...] = r
    k_o[...] = k * (1.0 + (a - 1.0) * ka_ref[...])
    v_o[...] = v
    kkn_o[...] = kk
    b_o[...] = kk * a
    df_o[...] = decay(w2f_ref, w0_ref[0:1, :])
    db_o[...] = decay(w2b_ref, w0_ref[1:2, :])
    gf_o[...] = jnp.dot(jax.nn.sigmoid(lo).astype(BF16), g2f_ref[...], preferred_element_type=F32)
    gb_o[...] = jnp.dot(jax.nn.sigmoid(hi).astype(BF16), g2b_ref[...], preferred_element_type=F32)


def rwkv_prep(p, B, L, mu, w0, w2, a0, a2, g2, k_k, k_a):
    tm = min(RW_PREP_TM, L)
    tiles_per_seq = L // tm
    n_tok = B * L
    hb = tm // P_HALO
    col = P_RW // RW_PAD

    def rows(n, r0, src):
        return jnp.zeros((n, RW_W), F32).at[r0:r0 + src.shape[0]].set(src).astype(BF16)

    w2f = rows(LANES, RW_OFF[3] - 768, w2[0])
    w2b = rows(LANES, RW_OFF[4] - 768, w2[1])
    a2p = rows(LANES, RW_OFF[5] - 896, a2)
    g2f = rows(RW_W, RW_OFF[6] - 896, g2[0])
    g2b = rows(RW_W, RW_OFF[7] - 1024, g2[1])
    mu_p = jnp.pad(mu, (0, RW_PAD - mu.shape[0])).reshape(1, RW_PAD)

    def const(shape):
        return pl.BlockSpec(shape, lambda i: (0,) * len(shape))

    vec = const((1, RW_W))
    out_spec = pl.BlockSpec((tm, RW_W), lambda i: (i, 0))
    return pl.pallas_call(
        functools.partial(_rwkv_prep_kernel, tm, tiles_per_seq),
        out_shape=[jax.ShapeDtypeStruct((n_tok, RW_W), F32)] * 9,
        grid=(n_tok // tm,),
        in_specs=[pl.BlockSpec((tm, RW_PAD), lambda i: (i, col)),
                  pl.BlockSpec((P_HALO, RW_PAD), lambda i: (jnp.maximum(i * hb - 1, 0), col)),
                  pl.BlockSpec((P_HALO, RW_PAD), lambda i: (jnp.minimum((i + 1) * hb, n_tok // P_HALO - 1), col)),
                  const((1, RW_PAD)), const((2, RW_W)), const((LANES, RW_W)), const((LANES, RW_W)),
                  vec, const((LANES, RW_W)), const((RW_W, RW_W)), const((RW_W, RW_W)), vec, vec],
        out_specs=[out_spec] * 9,
        compiler_params=pltpu.CompilerParams(
            dimension_semantics=("parallel",), vmem_limit_bytes=VMEM_LIMIT),
        name="rwkv_prep",
    )(p, p, p, mu_p, w0, w2f, w2b, a0.reshape(1, RW_W), a2p, g2f, g2b,
      k_k.reshape(1, RW_W), k_a.reshape(1, RW_W))


def _rwkv_scan_kernel(tblk, nb,
                      wf_ref, kkf_ref, bf_ref, kf_ref, vf_ref, rf_ref,
                      wb_ref, kkb_ref, bb_ref, kb_ref, vb_ref, rb_ref,
                      of_ref, ob_ref, sf_ref, sb_ref, sf16_ref, sb16_ref):
    c = pl.program_id(0)

    @pl.when(c == 0)
    def _():
        for ref in (sf_ref, sb_ref, sf16_ref, sb16_ref):
            ref[...] = jnp.zeros_like(ref)

    row = lax.broadcasted_iota(jnp.int32, (RW_W, RW_W), 0)
    col = lax.broadcasted_iota(jnp.int32, (RW_W, RW_W), 1)
    ones_blk = jnp.where(row // RW_HD == col // RW_HD, 1.0, 0.0).astype(BF16)
    r64 = lax.broadcasted_iota(jnp.int32, (RW_HD, RW_W), 0)
    c64 = lax.broadcasted_iota(jnp.int32, (RW_HD, RW_W), 1)
    eye_t = jnp.where(r64 == c64 % RW_HD, 1.0, 0.0).astype(BF16)
    r8 = lax.broadcasted_iota(jnp.int32, (8, RW_W), 0)
    c8 = lax.broadcasted_iota(jnp.int32, (8, RW_W), 1)
    head_sel = jnp.where(r8 == c8 // RW_HD, 1.0, 0.0).astype(BF16)

    fwd = (wf_ref, kkf_ref, bf_ref, kf_ref, vf_ref, rf_ref, sf_ref, sf16_ref)
    bwd = (wb_ref, kkb_ref, bb_ref, kb_ref, vb_ref, rb_ref, sb_ref, sb16_ref)
    nrow = nb * RW_HD

    def one_dir(idx, refs):
        w_ref, kk_ref, b_ref, k_ref, v_ref, r_ref, s_ref, s16_ref = refs
        zs = []
        for b in range(nb):
            kk16 = kk_ref[idx, b:b + 1, :].astype(BF16)
            v16 = v_ref[idx, b:b + 1, :].astype(BF16)
            lhs = jnp.concatenate([s16_ref[b] * kk16, eye_t * v16], axis=0)
            red = jnp.dot(lhs, ones_blk, preferred_element_type=F32)
            s_new = (s_ref[b] * w_ref[idx, b:b + 1, :] - red[0:RW_HD] * b_ref[idx, b:b + 1, :]
                     + red[RW_HD:2 * RW_HD] * k_ref[idx, b:b + 1, :])
            s_ref[b] = s_new
            s16 = s_new.astype(BF16)
            s16_ref[b] = s16
            zs.append(s16 * r_ref[idx, b:b + 1, :].astype(BF16))
        return jnp.concatenate(zs, axis=0)

    def emit(idx, z, o_ref):
        o_ref[idx] = lax.dot_general(head_sel, z, (((1,), (1,)), ((), ())),
                                     preferred_element_type=F32)

    def step(i, carry):
        z_f, z_b = carry
        ib = tblk - 1 - i
        emit(jnp.maximum(i - 1, 0), z_f, of_ref)
        emit(jnp.minimum(ib + 1, tblk - 1), z_b, ob_ref)
        return one_dir(i, fwd), one_dir(ib, bwd)

    z0 = jnp.zeros((nrow, RW_W), BF16)
    z_f, z_b = lax.fori_loop(0, tblk, step, (z0, z0), unroll=8)
    emit(tblk - 1, z_f, of_ref)
    emit(0, z_b, ob_ref)


def rwkv_scan_pallas(w_f, w_b, kk, bvec, k, v, r, n_ctx):
    n, nb, _ = kk.shape
    tblk = RW_TBLK
    nblk = n // tblk
    nblk_ctx = n_ctx // tblk

    def fmap(c):
        return (c, 0, 0)

    def bmap(c):
        return (jnp.where(c < nblk_ctx, nblk_ctx - 1 - c, nblk + nblk_ctx - 1 - c), 0, 0)

    blk = (tblk, nb, RW_W)
    oblk = (tblk, 8, nb * RW_HD)
    return pl.pallas_call(
        functools.partial(_rwkv_scan_kernel, tblk, nb),
        out_shape=[jax.ShapeDtypeStruct((n, 8, nb * RW_HD), F32)] * 2,
        grid=(nblk,),
        in_specs=[pl.BlockSpec(blk, fmap)] * 6 + [pl.BlockSpec(blk, bmap)] * 6,
        out_specs=[pl.BlockSpec(oblk, fmap), pl.BlockSpec(oblk, bmap)],
        scratch_shapes=[pltpu.VMEM((nb, RW_HD, RW_W), F32), pltpu.VMEM((nb, RW_HD, RW_W), F32),
                        pltpu.VMEM((nb, RW_HD, RW_W), BF16), pltpu.VMEM((nb, RW_HD, RW_W), BF16)],
        compiler_params=pltpu.CompilerParams(dimension_semantics=("arbitrary",),
                                             vmem_limit_bytes=VMEM_LIMIT),
        name="rwkv_scan",
    )(w_f, kk, bvec, k, v, r, w_b, kk, bvec, k, v, r)


def _rwkv_post_kernel(of_ref, ob_ref, r_ref, k_ref, v_ref, gf_ref, gb_ref, rk_ref, lng_ref, lnb_ref,
                      y_ref):
    mean_blk = _head_blocks(1.0 / RW_HD)

    def gn(o):
        oc = o - _head_sum(o, mean_blk)
        var = _head_sum(oc * oc, mean_blk)
        return oc * lax.rsqrt(var + RW_GN_EPS) * lng_ref[...] + lnb_ref[...]

    bonus = _head_sum(r_ref[...] * k_ref[...] * rk_ref[...], _head_blocks(1.0)) * v_ref[...]
    y_ref[...] = (gn(of_ref[...]) + bonus) * gf_ref[...] + (gn(ob_ref[...]) + bonus) * gb_ref[...]


def rwkv_post(o_f, o_b, r, k, v, g_f, g_b, r_k, lnx_g, lnx_b, L):
    n_tok = r.shape[0]
    tm = min(RW_PREP_TM, L)
    tok = pl.BlockSpec((tm, RW_W), lambda i: (i, 0))
    vec = pl.BlockSpec((1, RW_W), lambda i: (0, 0))
    return pl.pallas_call(
        _rwkv_post_kernel,
        out_shape=jax.ShapeDtypeStruct((n_tok, RW_W), F32),
        grid=(n_tok // tm,),
        in_specs=[tok] * 7 + [vec] * 3,
        out_specs=tok,
        compiler_params=pltpu.CompilerParams(
            dimension_semantics=("parallel",), vmem_limit_bytes=VMEM_LIMIT),
        name="rwkv_post",
    )(o_f, o_b, r, k, v, g_f, g_b, r_k.reshape(1, RW_W), lnx_g.reshape(1, RW_W), lnx_b.reshape(1, RW_W))


def rwkv_branch(p, pc, B, L, C, need_ctx, mu, w0, w2, a0, a2, g2, k_k, k_a, r_k, lnx_g, lnx_b):
    r, k, v, kk, bv, d_f, d_b, g_f, g_b = rwkv_prep(p, B, L, mu, w0, w2, a0, a2, g2, k_k, k_a)
    rc, kc, vc, kkc, bc, dc_f, dc_b, gc_f, gc_b = rwkv_prep(pc, B, C, mu, w0, w2, a0, a2, g2, k_k, k_a)

    def tmajor(xc, xl):
        t = jnp.concatenate([xc.reshape(B, C, RW_W), xl.reshape(B, L, RW_W)], axis=1)
        return jnp.moveaxis(t, 1, 0)

    pf, pb = rwkv_scan_pallas(tmajor(dc_f, d_f), tmajor(dc_b, d_b), tmajor(kkc, kk), tmajor(bc, bv),
                              tmajor(kc, k), tmajor(vc, v), tmajor(rc, r), C)

    def bmajor(t):
        t = t[:, :RW_HEADS].reshape(C + L, RW_HEADS, B, RW_HD)
        return jnp.transpose(t, (2, 0, 1, 3)).reshape(B, C + L, RW_W)

    pf, pb = bmajor(pf), bmajor(pb)
    y = rwkv_post(pf[:, C:].reshape(B * L, RW_W), pb[:, C:].reshape(B * L, RW_W), r, k, v, g_f, g_b,
                  r_k, lnx_g, lnx_b, L).reshape(B, L, RW_W)
    yc = None
    if need_ctx:
        yc = rwkv_post(pf[:, :C].reshape(B * C, RW_W), pb[:, :C].reshape(B * C, RW_W), rc, kc, vc,
                       gc_f, gc_b, r_k, lnx_g, lnx_b, C).reshape(B, C, RW_W)
    return y, yc


def _diff_attn_kernel(q_ref, kt_ref, v_ref, lam_ref, gain_ref, o_ref):
    q = q_ref[0]
    v = v_ref[0]
    lam = lam_ref[...]
    tq = q.shape[0]
    head_of_lane = lax.broadcasted_iota(jnp.int32, (1, DF_W), 1) // DF_VD
    acc = jnp.zeros((tq, DF_W), F32)
    for h in range(DF_HEADS):
        parts = []
        for c in range(2):
            j = 2 * h + c
            s = jnp.dot(q[:, DF_HD * j:DF_HD * (j + 1)], kt_ref[0, j],
                        preferred_element_type=F32)
            p = jnp.exp2(s - jnp.max(s, -1, keepdims=True))
            l = jnp.sum(p, -1, keepdims=True)
            pv = jnp.dot(p.astype(BF16), v, preferred_element_type=F32)
            parts.append(pv / l)
        acc = jnp.where(head_of_lane == h, parts[0] - lam * parts[1], acc)
    row = lax.broadcasted_iota(jnp.int32, (DF_W, DF_W), 0)
    col = lax.broadcasted_iota(jnp.int32, (DF_W, DF_W), 1)
    seg_mean = jnp.where(row // DF_VD == col // DF_VD, 1.0 / DF_VD, 0.0).astype(F32)
    ms = jnp.dot(acc * acc, seg_mean, preferred_element_type=F32, precision=lax.Precision.HIGHEST)
    o_ref[0] = acc * lax.rsqrt(ms + 1e-5) * gain_ref[...]


def diff_attend(q, k, v, lam, lam_init, subln_g):
    B, Lq = q.shape[:2]
    S = k.shape[1]
    tq = min(DF_TQ, Lq)
    qs = (q.astype(F32) * (DF_HD ** -0.5 * math.log2(math.e))).reshape(B, Lq, DF_W).astype(BF16)
    kt = jnp.transpose(k.reshape(B, S, 2 * DF_HEADS, DF_HD), (0, 2, 3, 1)).astype(BF16)
    vb = v.reshape(B, S, DF_W).astype(BF16)
    gain = jnp.tile(subln_g.astype(F32) * (1.0 - lam_init), DF_HEADS).reshape(1, DF_W)
    return pl.pallas_call(
        _diff_attn_kernel,
        out_shape=jax.ShapeDtypeStruct((B, Lq, DF_W), F32),
        grid=(B, Lq // tq),
        in_specs=[pl.BlockSpec((1, tq, DF_W), lambda b, i: (b, i, 0)),
                  pl.BlockSpec((1, 2 * DF_HEADS, DF_HD, S), lambda b, i: (b, 0, 0, 0)),
                  pl.BlockSpec((1, S, DF_W), lambda b, i: (b, 0, 0)),
                  pl.BlockSpec((1, 1), lambda b, i: (0, 0)),
                  pl.BlockSpec((1, DF_W), lambda b, i: (0, 0))],
        out_specs=pl.BlockSpec((1, tq, DF_W), lambda b, i: (b, i, 0)),
        compiler_params=pltpu.CompilerParams(
            dimension_semantics=("parallel", "parallel"), vmem_limit_bytes=VMEM_LIMIT),
        name="diff_attn",
    )(qs, kt, vb, lam.reshape(1, 1).astype(F32), gain)


def diff_branch(p, pc, need_ctx, lq1, lk1, lq2, lk2, subln_g, lam_init):
    q, k, v = split_at(p, DF_SIZES)
    qc, kc, vc = split_at(pc, DF_SIZES)

    def qk(t):
        return t.reshape(t.shape[0], t.shape[1], DF_HEADS, 2, DF_HD)

    def vs(t):
        return t.reshape(t.shape[0], t.shape[1], DF_HEADS, DF_VD)

    lam = (jnp.exp(jnp.sum(lq1.astype(F32) * lk1.astype(F32)))
           - jnp.exp(jnp.sum(lq2.astype(F32) * lk2.astype(F32))) + lam_init)
    k_all = jnp.concatenate([axial_rope(qk(k)), qk(kc)], axis=1)
    v_all = jnp.concatenate([vs(v), vs(vc)], axis=1)
    y = diff_attend(axial_rope(qk(q)), k_all, v_all, lam, lam_init, subln_g)
    yc = diff_attend(qk(qc), qk(kc), vs(vc), lam, lam_init, subln_g) if need_ctx else None
    return y, yc


def _ln_rows(x):
    mu = jnp.mean(x, -1, keepdims=True)
    xc = x - mu
    return xc * lax.rsqrt(jnp.mean(xc * xc, -1, keepdims=True) + LN_EPS)


def _ffn_kernel(tm, tiles_per_seq, n_chunks,
                h_ref, hprev_ref, hnext_ref, sh_ref, sc_ref, gate_ref,
                wa_ref, wb_ref, cwa_ref, cwb_ref, cba_ref, cbb_ref, wd_ref, lng_ref, lnb_ref,
                o_ref, u_ref, acc_ref):
    i = pl.program_id(0)
    j = pl.program_id(1)
    n = tm + 2 * HALO

    @pl.when(j == 0)
    def _():
        scale = 1.0 + sc_ref[0]
        shift = sh_ref[0]
        first = (i % tiles_per_seq) == 0
        last = (i % tiles_per_seq) == tiles_per_seq - 1
        u_ref[0:tm, :] = (_ln_rows(h_ref[...]) * scale + shift).astype(BF16)
        un = _ln_rows(hnext_ref[...]) * scale + shift
        up = _ln_rows(hprev_ref[...]) * scale + shift
        u_ref[tm:tm + HALO, :] = jnp.where(last, 0.0, un).astype(BF16)
        u_ref[tm + HALO:n, :] = jnp.where(first, 0.0, up).astype(BF16)
        acc_ref[...] = jnp.zeros_like(acc_ref)

    u = u_ref[...]

    def conv(w_ref, cw_ref, cb_ref):
        x = jnp.dot(u, w_ref[...], preferred_element_type=F32)
        cw = cw_ref[...]
        y = (pltpu.roll(x, 1, 0)[0:tm] * cw[0:1] + x[0:tm] * cw[1:2]
             + pltpu.roll(x, n - 1, 0)[0:tm] * cw[2:3] + cb_ref[...])
        return y

    a = conv(wa_ref, cwa_ref, cba_ref)
    b = conv(wb_ref, cwb_ref, cbb_ref)
    g = (a * jax.nn.sigmoid(a) * b).astype(BF16)
    acc_ref[...] += jnp.dot(g, wd_ref[...], preferred_element_type=F32)

    @pl.when(j == n_chunks - 1)
    def _():
        y = DN_ALPHA * h_ref[...] + gate_ref[0] * acc_ref[...]
        o_ref[...] = _ln_rows(y) * lng_ref[...] + lnb_ref[...]


def ffn_block(h, shift, scale, gate, w_up, conv_w, conv_b, w_down, ln_g, ln_b):
    B, L, D = h.shape
    tm = min(FFN_TM, L)
    tiles_per_seq = L // tm
    n_tok = B * L
    n_chunks = D_FF // FFN_CHUNK
    hb = tm // HALO

    def bcast(t):
        return jnp.broadcast_to(t.astype(F32), (B, 1, D))

    wu = w_up.astype(BF16)
    wd = w_down.astype(BF16)
    cb = conv_b.reshape(1, 2 * D_FF)
    h2 = h.reshape(n_tok, D)
    mod_spec = pl.BlockSpec((1, 1, D), lambda i, j: (i // tiles_per_seq, 0, 0))
    vec_spec = pl.BlockSpec((1, D), lambda i, j: (0, 0))
    out = pl.pallas_call(
        functools.partial(_ffn_kernel, tm, tiles_per_seq, n_chunks),
        out_shape=jax.ShapeDtypeStruct((n_tok, D), F32),
        grid=(n_tok // tm, n_chunks),
        in_specs=[pl.BlockSpec((tm, D), lambda i, j: (i, 0)),
                  pl.BlockSpec((HALO, D), lambda i, j: (jnp.maximum(i * hb - 1, 0), 0)),
                  pl.BlockSpec((HALO, D), lambda i, j: (jnp.minimum((i + 1) * hb, n_tok // HALO - 1), 0)),
                  mod_spec, mod_spec, mod_spec,
                  pl.BlockSpec((D, FFN_CHUNK), lambda i, j: (0, j)),
                  pl.BlockSpec((D, FFN_CHUNK), lambda i, j: (0, n_chunks + j)),
                  pl.BlockSpec((3, FFN_CHUNK), lambda i, j: (0, j)),
                  pl.BlockSpec((3, FFN_CHUNK), lambda i, j: (0, n_chunks + j)),
                  pl.BlockSpec((1, FFN_CHUNK), lambda i, j: (0, j)),
                  pl.BlockSpec((1, FFN_CHUNK), lambda i, j: (0, n_chunks + j)),
                  pl.BlockSpec((FFN_CHUNK, D), lambda i, j: (j, 0)),
                  vec_spec, vec_spec],
        out_specs=pl.BlockSpec((tm, D), lambda i, j: (i, 0)),
        scratch_shapes=[pltpu.VMEM((tm + 2 * HALO, D), BF16), pltpu.VMEM((tm, D), F32)],
        compiler_params=pltpu.CompilerParams(
            dimension_semantics=("parallel", "arbitrary"), vmem_limit_bytes=VMEM_LIMIT),
        name="conv_ffn",
    )(h2, h2, h2, bcast(shift), bcast(scale), bcast(gate), wu, wu, conv_w, conv_w, cb, cb, wd,
      ln_g.reshape(1, D), ln_b.reshape(1, D))
    return out.reshape(B, L, D)


P_GATE, P_HY, P_RW, P_SW, P_DF = 0, 4096, 5120, 6400, 6912
P_COLS = 7680
IN_CHUNK = 512


def _in_proj_kernel(x_ref, sh_ref, sc_ref, w_ref, o_ref, u_ref):
    @pl.when(pl.program_id(1) == 0)
    def _():
        u_ref[...] = (_ln_rows(x_ref[...]) * (1.0 + sc_ref[0]) + sh_ref[0]).astype(BF16)

    o_ref[...] = jnp.dot(u_ref[...], w_ref[...], preferred_element_type=F32).astype(BF16)


def in_proj(h, shift, scale, w_in_p):
    B, L, D = h.shape
    tm = min(IN_TM, L)
    tiles_per_seq = L // tm
    n_tok = B * L

    def bcast(t):
        return jnp.broadcast_to(t.astype(F32), (B, 1, D))

    mod_spec = pl.BlockSpec((1, 1, D), lambda i, j: (i // tiles_per_seq, 0, 0))
    return pl.pallas_call(
        _in_proj_kernel,
        out_shape=jax.ShapeDtypeStruct((n_tok, P_COLS), BF16),
        grid=(n_tok // tm, P_COLS // IN_CHUNK),
        in_specs=[pl.BlockSpec((tm, D), lambda i, j: (i, 0)), mod_spec, mod_spec,
                  pl.BlockSpec((D, IN_CHUNK), lambda i, j: (0, j))],
        out_specs=pl.BlockSpec((tm, IN_CHUNK), lambda i, j: (i, j)),
        scratch_shapes=[pltpu.VMEM((tm, D), BF16)],
        compiler_params=pltpu.CompilerParams(
            dimension_semantics=("parallel", "arbitrary"), vmem_limit_bytes=VMEM_LIMIT),
        name="in_proj",
    )(h.reshape(n_tok, D), bcast(shift), bcast(scale), w_in_p)


def pack_w_in(w):
    offs = np.cumsum((0,) + IN_SIZES)
    hy, sw, rw, df, gt = [w[:, offs[j]:offs[j + 1]] for j in range(len(IN_SIZES))]

    def padto(t, n):
        return jnp.pad(t, ((0, 0), (0, n - t.shape[1])))

    return jnp.concatenate([gt, padto(hy, P_RW - P_HY), padto(rw, P_SW - P_RW), sw, df],
                           axis=1).astype(BF16)


def _merge_kernel(yh_ref, ys_ref, yr_ref, yd_ref, g0_ref, g1_ref, g2_ref, g3_ref, h_ref, gate_ref,
                  wbr_ref, wo_ref, lng_ref, lnb_ref, o_ref):
    acc = None
    for j, (y_ref, g_ref) in enumerate(((yh_ref, g0_ref), (ys_ref, g1_ref), (yr_ref, g2_ref),
                                        (yd_ref, g3_ref))):
        term = jax.nn.sigmoid(g_ref[...].astype(F32)) * jnp.dot(y_ref[...].astype(BF16), wbr_ref[j],
                                                    preferred_element_type=F32)
        acc = term if acc is None else acc + term
    mix = jnp.dot(acc.astype(BF16), wo_ref[...], preferred_element_type=F32)
    y = DN_ALPHA * h_ref[...] + gate_ref[0] * mix
    o_ref[...] = _ln_rows(y) * lng_ref[...] + lnb_ref[...]


def merge_block(ys, p, h, gate, w_br, w_o, ln_g, ln_b):
    B, L, D = h.shape
    n_tok = B * L
    tm = min(MERGE_TM, L)
    tiles_per_seq = L // tm
    y_spec = pl.BlockSpec((tm, BR_W), lambda i: (i, 0))
    vec_spec = pl.BlockSpec((1, D), lambda i: (0, 0))
    g_specs = [pl.BlockSpec((tm, D), functools.partial(lambda j, i: (i, j), j)) for j in range(N_BRANCH)]
    out = pl.pallas_call(
        _merge_kernel,
        out_shape=jax.ShapeDtypeStruct((n_tok, D), F32),
        grid=(n_tok // tm,),
        in_specs=[y_spec] * N_BRANCH + g_specs + [
            pl.BlockSpec((tm, D), lambda i: (i, 0)),
            pl.BlockSpec((1, 1, D), lambda i: (i // tiles_per_seq, 0, 0)),
            pl.BlockSpec((N_BRANCH, BR_W, D), lambda i: (0, 0, 0)),
            pl.BlockSpec((D, D), lambda i: (0, 0)), vec_spec, vec_spec],
        out_specs=pl.BlockSpec((tm, D), lambda i: (i, 0)),
        compiler_params=pltpu.CompilerParams(
            dimension_semantics=("parallel",), vmem_limit_bytes=VMEM_LIMIT),
        name="merge",
    )(*[y.reshape(n_tok, BR_W) for y in ys], p, p, p, p, h.reshape(n_tok, D),
      jnp.broadcast_to(gate.astype(F32), (B, 1, D)), w_br.astype(BF16), w_o.astype(BF16),
      ln_g.reshape(1, D), ln_b.reshape(1, D))
    return out.reshape(B, L, D)


def kernel(x, c, ctx, c_ctx, ada_w, ada_b, w_in, hy_conv_w, hy_conv_b, hy_f_w1, hy_f_b1,
           hy_f_w2, hy_f_b2, hy_f_w3, hy_f_freq, hy_bias, swa_sink, rwkv_mu, rwkv_w0, rwkv_w2,
           rwkv_a0, rwkv_a2, rwkv_g2, rwkv_kk, rwkv_ka, rwkv_rk, rwkv_lnx_g, rwkv_lnx_b,
           diff_lq1, diff_lk1, diff_lq2, diff_lk2, diff_subln_g, w_branch, w_out, ln1_g, ln1_b,
           ffn_w_up, ffn_conv_w, ffn_conv_b, ffn_w_down, ln2_g, ln2_b):
    h, hc = x, ctx
    mats, mats_c = dft_mats(x.shape[1]), dft_mats(ctx.shape[1])
    swa_tables = rope_tables(x.shape[1], SWA_HD, SWA_HEADS * SWA_HD)
    s_lat = jax.nn.silu(c)
    s_ctx = jax.nn.silu(c_ctx)
    for i in range(DEPTH):
        need_ctx = i < DEPTH - 1
        mod = (s_lat @ ada_w[i] + ada_b[i])[:, None, :]
        mod_c = s_ctx @ ada_w[i] + ada_b[i]
        sh1, sc1, g1, sh2, sc2, g2 = jnp.split(mod, 6, -1)
        csh1, csc1, cg1, csh2, csc2, cg2 = jnp.split(mod_c, 6, -1)

        B, L, D = h.shape
        C = hc.shape[1]
        w_in_p = pack_w_in(w_in[i])
        p = in_proj(h, sh1, sc1, w_in_p)
        pc = in_proj(hc, csh1.reshape(1, 1, D), csc1.reshape(1, 1, D), w_in_p)

        def seg(t, n, off, width):
            return t[:, off:off + width].astype(F32).reshape(B, n, width)

        p_hy, p_df = seg(p, L, P_HY, IN_SIZES[0]), seg(p, L, P_DF, IN_SIZES[3])
        pc_hy, pc_df = seg(pc, C, P_HY, IN_SIZES[0]), seg(pc, C, P_DF, IN_SIZES[3])

        y_hy, yc_hy = hyena_branch(p_hy, pc_hy, need_ctx, mats, mats_c, hy_conv_w[i], hy_conv_b[i], hy_f_w1[i],
                                   hy_f_b1[i], hy_f_w2[i], hy_f_b2[i], hy_f_w3[i], hy_f_freq[i],
                                   hy_bias[i])
        y_sw, yc_sw = swa_branch(p, pc, B, L, C, need_ctx, swa_sink[i], swa_tables)
        y_rw, yc_rw = rwkv_branch(p, pc, B, L, C, need_ctx, rwkv_mu[i], rwkv_w0[i], rwkv_w2[i],
                                  rwkv_a0[i], rwkv_a2[i], rwkv_g2[i], rwkv_kk[i], rwkv_ka[i],
                                  rwkv_rk[i], rwkv_lnx_g[i], rwkv_lnx_b[i])
        lam_init = 0.8 - 0.6 * math.exp(-0.3 * i)
        y_df, yc_df = diff_branch(p_df, pc_df, need_ctx, diff_lq1[i], diff_lk1[i], diff_lq2[i],
                                  diff_lk2[i], diff_subln_g[i], lam_init)

        h = merge_block((y_hy, y_sw, y_rw, y_df), p, h, g1, w_branch[i], w_out[i], ln1_g[i], ln1_b[i])
        h = ffn_block(h, sh2, sc2, g2, ffn_w_up[i], ffn_conv_w[i], ffn_conv_b[i], ffn_w_down[i],
                      ln2_g[i], ln2_b[i])

        if need_ctx:
            hc = merge_block((yc_hy, yc_sw, yc_rw, yc_df), pc, hc, cg1.reshape(1, 1, D), w_branch[i],
                             w_out[i], ln1_g[i], ln1_b[i])
            hc = ffn_block(hc, csh2.reshape(1, 1, -1), csc2.reshape(1, 1, -1), cg2.reshape(1, 1, -1),
                           ffn_w_up[i], ffn_conv_w[i], ffn_conv_b[i], ffn_w_down[i], ln2_g[i], ln2_b[i])
    return h
```

```python
import functools
import math

import jax
import jax.numpy as jnp
import numpy as np
from jax import lax
from jax.experimental import pallas as pl
from jax.experimental.pallas import tpu as pltpu

D_MODEL = 1024
DEPTH = 2
GRID_W = 64
ROPE_BASE = 10000.0
F32 = jnp.float32
BF16 = jnp.bfloat16
NEG_INF = -1e30
LN_EPS = 1e-6

HY_W = 256
HY_ORDER = 2
HY_DIRS = 2
HY_EMB = 33
HY_MIN_DECAY = math.log(1e-2) / 1.5
HY_MAX_DECAY = math.log(1e-2) / 0.3

SWA_HEADS = 4
SWA_KV = 2
SWA_HD = 64
SWA_WIN = 128
SWA_BLOCK = 128

RW_HEADS = 4
RW_HD = 64
RW_W = RW_HEADS * RW_HD
RW_DECAY_R = 64
RW_AAA_R = 64
RW_GATE_R = 128
RW_GN_EPS = 64e-5

DF_HEADS = 4
DF_HD = 32
DF_VD = 2 * DF_HD
DF_W = DF_HEADS * DF_VD

N_BRANCH = 4
BR_W = 256
D_FF = 2816
DN_ALPHA = (2 * DEPTH) ** 0.25

SWA_SIZES = (SWA_HEADS * SWA_HD, SWA_KV * SWA_HD, SWA_KV * SWA_HD)
RW_SIZES = (RW_W, RW_W, RW_W, RW_DECAY_R, RW_DECAY_R, RW_AAA_R, RW_GATE_R, RW_GATE_R)
DF_SIZES = (DF_W, DF_W, DF_W)
IN_SIZES = (3 * HY_W, sum(SWA_SIZES), sum(RW_SIZES), sum(DF_SIZES), N_BRANCH * D_MODEL)

V7X_VMEM_BYTES = 64 * 1024 * 1024
LANES = 128
MM_TM = 512
MM_TN = 512
RW_TBLK = 64
DF_TQ = 256
SWA_QROWS = 256
HALO = 8
P_HALO = 16
IN_TM = 2048
FFN_TM = 1024
FFN_CHUNK = 256
MERGE_TM = 512
HY_TK = 512
RW_PREP_TM = 512
VMEM_LIMIT = V7X_VMEM_BYTES * 3 // 4


def _mm_kernel(x_ref, w_ref, o_ref):
    o_ref[...] = jnp.dot(x_ref[...].astype(BF16), w_ref[...], preferred_element_type=F32)


def _mm(x, w):
    m, k = x.shape
    n = w.shape[1]
    n_pad = -(-n // LANES) * LANES
    wb = w.astype(BF16)
    if n_pad != n:
        wb = jnp.pad(wb, ((0, 0), (0, n_pad - n)))
    tn = MM_TN if n_pad % MM_TN == 0 else (256 if n_pad % 256 == 0 else LANES)
    tm = MM_TM if m % MM_TM == 0 else m
    out = pl.pallas_call(
        _mm_kernel,
        out_shape=jax.ShapeDtypeStruct((m, n_pad), F32),
        grid=(m // tm, n_pad // tn),
        in_specs=[pl.BlockSpec((tm, k), lambda i, j: (i, 0)),
                  pl.BlockSpec((k, tn), lambda i, j: (0, j))],
        out_specs=pl.BlockSpec((tm, tn), lambda i, j: (i, j)),
        compiler_params=pltpu.CompilerParams(
            dimension_semantics=("parallel", "parallel"), vmem_limit_bytes=VMEM_LIMIT),
        name="matmul",
    )(x, wb)
    return out[:, :n] if n_pad != n else out


def dft_mats(L):
    tb = 64
    k = jnp.arange(L, dtype=jnp.int32)[:, None]

    def table(t):
        ang = (((2 * k + 1) * t[None, :]) % (4 * L)).astype(F32) * (math.pi / (2 * L))
        return jnp.cos(ang), jnp.sin(ang)

    ca, sa = table(tb * jnp.arange(L // tb, dtype=jnp.int32))
    cb, sb = table(jnp.arange(tb, dtype=jnp.int32))
    c = (ca[:, :, None] * cb[:, None, :] - sa[:, :, None] * sb[:, None, :]).reshape(L, L)
    s = (sa[:, :, None] * cb[:, None, :] + ca[:, :, None] * sb[:, None, :]).reshape(L, L)
    ct = (ca.T[:, None, :] * cb.T[None, :, :] - sa.T[:, None, :] * sb.T[None, :, :]).reshape(L, L)
    st = (sa.T[:, None, :] * cb.T[None, :, :] + ca.T[:, None, :] * sb.T[None, :, :]).reshape(L, L)
    return c.astype(BF16), s.astype(BF16), ct.astype(BF16), st.astype(BF16)


def hyena_kspec(L, mats, w1, b1, w2, b2, w3, freq):
    t = jnp.linspace(0.0, 1.0, L, dtype=F32)[:, None]
    bands = (HY_EMB - 1) // 2
    w = 2.0 * math.pi * jnp.arange(L, dtype=F32) / L
    fb = jnp.linspace(1e-4, bands - 1, bands, dtype=F32)
    ang = w[:, None] * fb[None, :]
    z = jnp.concatenate([t, jnp.cos(ang), -jnp.sin(ang)], -1)
    fr = freq.astype(F32)
    hdn = jnp.sin(fr * (z @ w1.astype(F32) + b1.astype(F32)))
    hdn = jnp.sin(fr * (hdn @ w2.astype(F32) + b2.astype(F32)))
    filt = (hdn @ w3.astype(F32)).reshape(L, HY_ORDER, HY_DIRS, HY_W)
    deltas = jnp.abs(jnp.linspace(HY_MIN_DECAY, HY_MAX_DECAY, HY_W, dtype=F32))
    filt = filt * jnp.exp(-t * deltas[None, :])[:, None, None, :]
    filt = filt / jnp.sum(jnp.abs(filt), axis=(0, 2), keepdims=True)
    fwd, bwd = filt[:, :, 0], filt[:, :, 1]
    hi = jnp.concatenate([jnp.zeros_like(bwd[:1]), -bwd[:0:-1]], axis=0)
    nw = HY_ORDER * HY_W
    x = jnp.concatenate([fwd.reshape(L, nw), hi.reshape(L, nw)], axis=1)
    cx, sx = _mm(mats[0], x), _mm(mats[1], x)
    sgn = (1 - 2 * (jnp.arange(L) % 2)).astype(F32)[:, None]
    kr = cx[:, :nw] - sgn * sx[:, nw:]
    ki = -sx[:, :nw] - sgn * cx[:, nw:]

    def per_order(a):
        return jnp.transpose(a.reshape(L, HY_ORDER, HY_W), (1, 0, 2))

    return per_order(kr), per_order(ki)


def _hy_fwd_kernel(ac_ref, as_ref, z_ref, kr_ref, ki_ref, yr_ref, yi_ref):
    z = z_ref[0].astype(BF16)
    zr = jnp.dot(ac_ref[...], z, preferred_element_type=F32)
    zi = -jnp.dot(as_ref[...], z, preferred_element_type=F32)
    kr, ki = kr_ref[...], ki_ref[...]
    yr_ref[0] = (zr * kr - zi * ki).astype(BF16)
    yi_ref[0] = (zr * ki + zi * kr).astype(BF16)


def _hy_inv_kernel(inv_l, act_ref, ast_ref, yr_ref, yi_ref, z_ref, x_ref, bias_ref, o_ref):
    y = (jnp.dot(act_ref[...], yr_ref[0], preferred_element_type=F32)
         - jnp.dot(ast_ref[...], yi_ref[0], preferred_element_type=F32))
    o_ref[0] = x_ref[0] * (y * inv_l + bias_ref[...] * z_ref[0])


def hyena_longconv(z, xmul, mats, kr, ki, bias):
    B, L, W = z.shape
    ac, as_, act, ast = mats
    tk = min(HY_TK, L)
    a_spec = pl.BlockSpec((tk, L), lambda i, b: (i, 0))
    full_spec = pl.BlockSpec((1, L, W), lambda i, b: (b, 0, 0))
    tile_spec = pl.BlockSpec((1, tk, W), lambda i, b: (b, i, 0))
    k_spec = pl.BlockSpec((tk, W), lambda i, b: (i, 0))
    params = pltpu.CompilerParams(dimension_semantics=("parallel", "arbitrary"),
                                  vmem_limit_bytes=VMEM_LIMIT)
    yr, yi = pl.pallas_call(
        _hy_fwd_kernel,
        out_shape=[jax.ShapeDtypeStruct((B, L, W), BF16)] * 2,
        grid=(L // tk, B),
        in_specs=[a_spec, a_spec, full_spec, k_spec, k_spec],
        out_specs=[tile_spec, tile_spec],
        compiler_params=params,
        name="hyena_fwd",
    )(ac, as_, z, kr, ki)
    return pl.pallas_call(
        functools.partial(_hy_inv_kernel, 1.0 / L),
        out_shape=jax.ShapeDtypeStruct((B, L, W), F32),
        grid=(L // tk, B),
        in_specs=[a_spec, a_spec, full_spec, full_spec, tile_spec, tile_spec,
                  pl.BlockSpec((1, W), lambda i, b: (0, 0))],
        out_specs=tile_spec,
        compiler_params=params,
        name="hyena_inv",
    )(act, ast, yr, yi, z, xmul, bias.reshape(1, W))


def _hy_pre_kernel(tm, tiles_per_seq, x_ref, xp_ref, xn_ref, cw_ref, cb_ref, v_ref, x1_ref, x2_ref):
    i = pl.program_id(0)
    first = (i % tiles_per_seq) == 0
    last = (i % tiles_per_seq) == tiles_per_seq - 1
    x = x_ref[...].astype(F32)
    row = lax.broadcasted_iota(jnp.int32, x.shape, 0)
    prev_row = jnp.where(first, 0.0, xp_ref[...].astype(F32)[P_HALO - 1:P_HALO, :])
    next_row = jnp.where(last, 0.0, xn_ref[...].astype(F32)[0:1, :])
    xm1 = jnp.where(row == 0, prev_row, pltpu.roll(x, 1, 0))
    xp1 = jnp.where(row == tm - 1, next_row, pltpu.roll(x, tm - 1, 0))
    cw = cw_ref[...]
    y = xm1 * cw[0:1] + x * cw[1:2] + xp1 * cw[2:3] + cb_ref[...]
    v_ref[...] = y[:, 0:HY_W]
    x1_ref[...] = y[:, HY_W:2 * HY_W]
    x2_ref[...] = y[:, 2 * HY_W:3 * HY_W]


def hyena_pre(p, B, L, conv_w, conv_b):
    tm = min(RW_PREP_TM, L)
    tiles_per_seq = L // tm
    n_tok = B * L
    hb = tm // P_HALO
    width = P_RW - P_HY
    col = P_HY // width
    cw = jnp.pad(conv_w, ((0, 0), (0, width - conv_w.shape[1])))
    cb = jnp.pad(conv_b, (0, width - conv_b.shape[0])).reshape(1, width)
    out_spec = pl.BlockSpec((tm, HY_W), lambda i: (i, 0))
    outs = pl.pallas_call(
        functools.partial(_hy_pre_kernel, tm, tiles_per_seq),
        out_shape=[jax.ShapeDtypeStruct((n_tok, HY_W), F32)] * 3,
        grid=(n_tok // tm,),
        in_specs=[pl.BlockSpec((tm, width), lambda i: (i, col)),
                  pl.BlockSpec((P_HALO, width), lambda i: (jnp.maximum(i * hb - 1, 0), col)),
                  pl.BlockSpec((P_HALO, width), lambda i: (jnp.minimum((i + 1) * hb, n_tok // P_HALO - 1), col)),
                  pl.BlockSpec((3, width), lambda i: (0, 0)),
                  pl.BlockSpec((1, width), lambda i: (0, 0))],
        out_specs=[out_spec] * 3,
        compiler_params=pltpu.CompilerParams(
            dimension_semantics=("parallel",), vmem_limit_bytes=VMEM_LIMIT),
        name="hyena_pre",
    )(p, p, p, cw, cb)
    return [o.reshape(B, L, HY_W) for o in outs]


def hyena_mix(p, B, L, conv_w, conv_b, mats, kspec, bias):
    v, x1, x2 = hyena_pre(p, B, L, conv_w, conv_b)
    kr, ki = kspec
    zz = hyena_longconv(v, x1, mats, kr[0], ki[0], bias[0])
    return hyena_longconv(zz, x2, mats, kr[1], ki[1], bias[1])


def hyena_branch(p, pc, B, L, C, need_ctx, mats, mats_c, conv_w, conv_b, fw1, fb1, fw2, fb2, fw3, ffreq,
                 bias):
    kspec = hyena_kspec(L, mats, fw1, fb1, fw2, fb2, fw3, ffreq)
    y = hyena_mix(p, B, L, conv_w, conv_b, mats, kspec, bias)
    yc = None
    if need_ctx:
        kspec_c = hyena_kspec(C, mats_c, fw1, fb1, fw2, fb2, fw3, ffreq)
        yc = hyena_mix(pc, B, C, conv_w, conv_b, mats_c, kspec_c, bias)
    return y, yc


def rope_tables(L, d, width):
    nf = d // 4
    pos = jnp.arange(L)
    inv = ROPE_BASE ** (-jnp.arange(nf, dtype=F32) / nf)
    cs, sn = [], []
    for p in (pos // GRID_W, pos % GRID_W):
        ang = p.astype(F32)[:, None] * inv[None, :]
        cs += [jnp.cos(ang), jnp.cos(ang)]
        sn += [-jnp.sin(ang), jnp.sin(ang)]
    reps = width // d
    return jnp.tile(jnp.concatenate(cs, -1), (1, reps)), jnp.tile(jnp.concatenate(sn, -1), (1, reps))


def _rope(x, cos, sin, d):
    width = x.shape[-1]
    q = d // 4
    lane = lax.broadcasted_iota(jnp.int32, x.shape, 1)
    swapped = jnp.where(lane % (2 * q) < q, pltpu.roll(x, width - q, 1), pltpu.roll(x, q, 1))
    return x * cos + swapped * sin


def _swa_kernel(local, seq_len, qrows, q_ref, *refs):
    nk = qrows // SWA_BLOCK + 2
    k_refs, v_refs = refs[0:nk], refs[nk:2 * nk]
    ck_ref, cv_ref, cos_ref, sin_ref, sink_ref, o_ref = refs[2 * nk:]
    n = pl.program_id(1)
    blk = SWA_BLOCK
    qscale = SWA_HD ** -0.5 * math.log2(math.e)
    q = q_ref[...].astype(F32)
    if local:
        q0 = pl.multiple_of(n * qrows, qrows)
        q = _rope(q, cos_ref[pl.ds(q0, qrows), :], sin_ref[pl.ds(q0, qrows), :], SWA_HD)
        first = n * (qrows // blk) - 1
        ks, vs = [], []
        for j, (k_ref, v_ref) in enumerate(zip(k_refs, v_refs)):
            k0 = pl.multiple_of(jnp.clip(first + j, 0, seq_len // blk - 1) * blk, blk)
            ks.append(_rope(k_ref[...].astype(F32), cos_ref[pl.ds(k0, blk), 0:SWA_KV * SWA_HD],
                            sin_ref[pl.ds(k0, blk), 0:SWA_KV * SWA_HD], SWA_HD).astype(BF16))
            vs.append(v_ref[...].astype(BF16))
        k_loc = jnp.concatenate(ks, axis=0)
        v_loc = jnp.concatenate(vs, axis=0)
        qpos = n * qrows + lax.broadcasted_iota(jnp.int32, (qrows, nk * blk), 0)
        kpos = first * blk + lax.broadcasted_iota(jnp.int32, (qrows, nk * blk), 1)
        valid = (jnp.abs(kpos - qpos) <= SWA_WIN) & (kpos >= 0) & (kpos < seq_len)
    qb = (q * qscale).astype(BF16)
    ck = ck_ref[...].astype(BF16)
    cv = cv_ref[...].astype(BF16)
    nt = (((1,), (1,)), ((), ()))
    outs = []
    for h in range(SWA_HEADS):
        g = h // (SWA_HEADS // SWA_KV)
        qh = qb[:, h * SWA_HD:(h + 1) * SWA_HD]
        gs = slice(g * SWA_HD, (g + 1) * SWA_HD)
        sink = sink_ref[h:h + 1, 0:1]
        s_ctx = lax.dot_general(qh, ck[:, gs], nt, preferred_element_type=F32)
        m = jnp.maximum(jnp.max(s_ctx, -1, keepdims=True), sink)
        if local:
            s_loc = lax.dot_general(qh, k_loc[:, gs], nt, preferred_element_type=F32)
            s_loc = jnp.where(valid, s_loc, NEG_INF)
            m = jnp.maximum(m, jnp.max(s_loc, -1, keepdims=True))
        p_ctx = jnp.exp2(s_ctx - m)
        den = jnp.sum(p_ctx, -1, keepdims=True) + jnp.exp2(sink - m)
        acc = jnp.dot(p_ctx.astype(BF16), cv[:, gs], preferred_element_type=F32)
        if local:
            p_loc = jnp.exp2(s_loc - m)
            den = den + jnp.sum(p_loc, -1, keepdims=True)
            acc = acc + jnp.dot(p_loc.astype(BF16), v_loc[:, gs], preferred_element_type=F32)
        outs.append(acc / den)
    o_ref[...] = jnp.concatenate(outs, axis=-1)


def swa_attend(p, pc, B, L, C, sink, tables, local):
    blk = SWA_BLOCK
    nb = L // blk
    qrows = min(SWA_QROWS, L)
    nq = L // qrows
    qpb = qrows // blk
    qcol = P_SW // (SWA_HEADS * SWA_HD)
    kcol = (P_SW + SWA_HEADS * SWA_HD) // (SWA_KV * SWA_HD)
    kvw = SWA_KV * SWA_HD

    def nbr(j, col):
        return lambda b, n: (b * nb + jnp.clip(n * qpb + j, 0, nb - 1), col)

    nbrs = range(-1, qpb + 1)

    cos, sin = tables
    sink_rows = jnp.broadcast_to((sink.astype(F32) * math.log2(math.e))[:, None], (SWA_HEADS, LANES))
    tab_spec = pl.BlockSpec(cos.shape, lambda b, n: (0, 0))
    out = pl.pallas_call(
        functools.partial(_swa_kernel, local, L, qrows),
        out_shape=jax.ShapeDtypeStruct((B * L, SWA_HEADS * SWA_HD), F32),
        grid=(B, nq),
        in_specs=[pl.BlockSpec((qrows, SWA_HEADS * SWA_HD), lambda b, n: (b * nq + n, qcol))]
        + [pl.BlockSpec((blk, kvw), nbr(j, kcol)) for j in nbrs]
        + [pl.BlockSpec((blk, kvw), nbr(j, kcol + 1)) for j in nbrs]
        + [pl.BlockSpec((C, kvw), lambda b, n: (b, kcol)),
           pl.BlockSpec((C, kvw), lambda b, n: (b, kcol + 1)),
           tab_spec, tab_spec,
           pl.BlockSpec((SWA_HEADS, LANES), lambda b, n: (0, 0))],
        out_specs=pl.BlockSpec((qrows, SWA_HEADS * SWA_HD), lambda b, n: (b * nq + n, 0)),
        compiler_params=pltpu.CompilerParams(
            dimension_semantics=("parallel", "parallel"), vmem_limit_bytes=VMEM_LIMIT),
        name="swa",
    )(p, *([p] * (2 * len(nbrs))), pc, pc, cos, sin, sink_rows)
    return out.reshape(B, L, SWA_HEADS * SWA_HD)


def swa_branch(p, pc, B, L, C, need_ctx, sink, tables):
    y = swa_attend(p, pc, B, L, C, sink, tables, True)
    yc = swa_attend(pc, pc, B, C, C, sink, tables, False) if need_ctx else None
    return y, yc


def _head_blocks(value):
    row = lax.broadcasted_iota(jnp.int32, (RW_W, RW_W), 0)
    col = lax.broadcasted_iota(jnp.int32, (RW_W, RW_W), 1)
    return jnp.where(row // RW_HD == col // RW_HD, value, 0.0).astype(F32)


def _head_sum(x, blocks):
    return jnp.dot(x, blocks, preferred_element_type=F32, precision=lax.Precision.HIGHEST)


RW_OFF = tuple(int(o) for o in np.cumsum((0,) + RW_SIZES))
RW_PAD = 1280
RW_WIN_WD = RW_OFF[3]
RW_WIN_LO = RW_OFF[5]
RW_WIN_HI = RW_WIN_LO + LANES


def _rwkv_prep_kernel(tm, tiles_per_seq, x_ref, xp_ref, xn_ref, mu_ref, w0_ref, w2f_ref, w2b_ref,
                      a0_ref, a2_ref, g2f_ref, g2b_ref, kk_ref, ka_ref,
                      r_o, k_o, v_o, kkn_o, b_o, df_o, db_o, gf_o, gb_o):
    i = pl.program_id(0)
    first = (i % tiles_per_seq) == 0
    last = (i % tiles_per_seq) == tiles_per_seq - 1
    x = x_ref[...].astype(F32)
    row = lax.broadcasted_iota(jnp.int32, x.shape, 0)
    prev_row = jnp.where(first, 0.0, xp_ref[...].astype(F32)[P_HALO - 1:P_HALO, :])
    next_row = jnp.where(last, 0.0, xn_ref[...].astype(F32)[0:1, :])
    xm1 = jnp.where(row == 0, prev_row, pltpu.roll(x, 1, 0))
    xp1 = jnp.where(row == tm - 1, next_row, pltpu.roll(x, tm - 1, 0))
    x = x + (0.5 * (xm1 + xp1) - x) * mu_ref[...]

    r = x[:, RW_OFF[0]:RW_OFF[1]]
    k = x[:, RW_OFF[1]:RW_OFF[2]]
    v = x[:, RW_OFF[2]:RW_OFF[3]]
    wd = jnp.tanh(x[:, RW_WIN_WD:RW_WIN_WD + LANES]).astype(BF16)
    lo = x[:, RW_WIN_LO:RW_WIN_LO + RW_W]
    hi = x[:, RW_WIN_HI:RW_WIN_HI + RW_W]

    def decay(w2_ref, w0):
        wlog = -jax.nn.softplus(-(w0 + jnp.dot(wd, w2_ref[...], preferred_element_type=F32))) - 0.5
        return jnp.exp(-jnp.exp(wlog))

    a = jax.nn.sigmoid(a0_ref[...] + jnp.dot(lo[:, 0:LANES].astype(BF16), a2_ref[...],
                                             preferred_element_type=F32))
    kk = k * kk_ref[...]
    norm = jnp.sqrt(_head_sum(kk * kk, _head_blocks(1.0)))
    kk = kk / jnp.maximum(norm, 1e-12)
    r_o[...] = r
    k_o[...] = k * (1.0 + (a - 1.0) * ka_ref[...])
    v_o[...] = v
    kkn_o[...] = kk
    b_o[...] = kk * a
    df_o[...] = decay(w2f_ref, w0_ref[0:1, :])
    db_o[...] = decay(w2b_ref, w0_ref[1:2, :])
    gf_o[...] = jnp.dot(jax.nn.sigmoid(lo).astype(BF16), g2f_ref[...], preferred_element_type=F32)
    gb_o[...] = jnp.dot(jax.nn.sigmoid(hi).astype(BF16), g2b_ref[...], preferred_element_type=F32)


def rwkv_prep(p, B, L, mu, w0, w2, a0, a2, g2, k_k, k_a):
    tm = min(RW_PREP_TM, L)
    tiles_per_seq = L // tm
    n_tok = B * L
    hb = tm // P_HALO
    col = P_RW // RW_PAD

    def rows(n, r0, src):
        return jnp.zeros((n, RW_W), F32).at[r0:r0 + src.shape[0]].set(src).astype(BF16)

    w2f = rows(LANES, RW_OFF[3] - RW_WIN_WD, w2[0])
    w2b = rows(LANES, RW_OFF[4] - RW_WIN_WD, w2[1])
    a2p = rows(LANES, RW_OFF[5] - RW_WIN_LO, a2)
    g2f = rows(RW_W, RW_OFF[6] - RW_WIN_LO, g2[0])
    g2b = rows(RW_W, RW_OFF[7] - RW_WIN_HI, g2[1])
    mu_p = jnp.pad(mu, (0, RW_PAD - mu.shape[0])).reshape(1, RW_PAD)

    def const(shape):
        return pl.BlockSpec(shape, lambda i: (0,) * len(shape))

    vec = const((1, RW_W))
    out_spec = pl.BlockSpec((tm, RW_W), lambda i: (i, 0))
    return pl.pallas_call(
        functools.partial(_rwkv_prep_kernel, tm, tiles_per_seq),
        out_shape=[jax.ShapeDtypeStruct((n_tok, RW_W), F32)] * 9,
        grid=(n_tok // tm,),
        in_specs=[pl.BlockSpec((tm, RW_PAD), lambda i: (i, col)),
                  pl.BlockSpec((P_HALO, RW_PAD), lambda i: (jnp.maximum(i * hb - 1, 0), col)),
                  pl.BlockSpec((P_HALO, RW_PAD), lambda i: (jnp.minimum((i + 1) * hb, n_tok // P_HALO - 1), col)),
                  const((1, RW_PAD)), const((2, RW_W)), const((LANES, RW_W)), const((LANES, RW_W)),
                  vec, const((LANES, RW_W)), const((RW_W, RW_W)), const((RW_W, RW_W)), vec, vec],
        out_specs=[out_spec] * 9,
        compiler_params=pltpu.CompilerParams(
            dimension_semantics=("parallel",), vmem_limit_bytes=VMEM_LIMIT),
        name="rwkv_prep",
    )(p, p, p, mu_p, w0, w2f, w2b, a0.reshape(1, RW_W), a2p, g2f, g2b,
      k_k.reshape(1, RW_W), k_a.reshape(1, RW_W))


def _rwkv_scan_kernel(tblk, nb,
                      wf_ref, kkf_ref, bf_ref, kf_ref, vf_ref, rf_ref,
                      wb_ref, kkb_ref, bb_ref, kb_ref, vb_ref, rb_ref,
                      of_ref, ob_ref, sf_ref, sb_ref, sf16_ref, sb16_ref):
    c = pl.program_id(0)

    @pl.when(c == 0)
    def _():
        for ref in (sf_ref, sb_ref, sf16_ref, sb16_ref):
            ref[...] = jnp.zeros_like(ref)

    row = lax.broadcasted_iota(jnp.int32, (RW_W, RW_W), 0)
    col = lax.broadcasted_iota(jnp.int32, (RW_W, RW_W), 1)
    ones_blk = jnp.where(row // RW_HD == col // RW_HD, 1.0, 0.0).astype(BF16)
    r64 = lax.broadcasted_iota(jnp.int32, (RW_HD, RW_W), 0)
    c64 = lax.broadcasted_iota(jnp.int32, (RW_HD, RW_W), 1)
    eye_t = jnp.where(r64 == c64 % RW_HD, 1.0, 0.0).astype(BF16)
    r8 = lax.broadcasted_iota(jnp.int32, (8, RW_W), 0)
    c8 = lax.broadcasted_iota(jnp.int32, (8, RW_W), 1)
    head_sel = jnp.where(r8 == c8 // RW_HD, 1.0, 0.0).astype(BF16)

    fwd = (wf_ref, kkf_ref, bf_ref, kf_ref, vf_ref, rf_ref, sf_ref, sf16_ref)
    bwd = (wb_ref, kkb_ref, bb_ref, kb_ref, vb_ref, rb_ref, sb_ref, sb16_ref)
    nrow = nb * RW_HD

    def one_dir(idx, refs):
        w_ref, kk_ref, b_ref, k_ref, v_ref, r_ref, s_ref, s16_ref = refs
        zs = []
        for b in range(nb):
            kk16 = kk_ref[idx, b:b + 1, :].astype(BF16)
            v16 = v_ref[idx, b:b + 1, :].astype(BF16)
            lhs = jnp.concatenate([s16_ref[b] * kk16, eye_t * v16], axis=0)
            red = jnp.dot(lhs, ones_blk, preferred_element_type=F32)
            s_new = (s_ref[b] * w_ref[idx, b:b + 1, :] - red[0:RW_HD] * b_ref[idx, b:b + 1, :]
                     + red[RW_HD:2 * RW_HD] * k_ref[idx, b:b + 1, :])
            s_ref[b] = s_new
            s16 = s_new.astype(BF16)
            s16_ref[b] = s16
            zs.append(s16 * r_ref[idx, b:b + 1, :].astype(BF16))
        return jnp.concatenate(zs, axis=0)

    def emit(idx, z, o_ref):
        o_ref[idx] = lax.dot_general(head_sel, z, (((1,), (1,)), ((), ())),
                                     preferred_element_type=F32)

    def step(i, carry):
        z_f, z_b = carry
        ib = tblk - 1 - i
        emit(jnp.maximum(i - 1, 0), z_f, of_ref)
        emit(jnp.minimum(ib + 1, tblk - 1), z_b, ob_ref)
        return one_dir(i, fwd), one_dir(ib, bwd)

    z0 = jnp.zeros((nrow, RW_W), BF16)
    z_f, z_b = lax.fori_loop(0, tblk, step, (z0, z0), unroll=8)
    emit(tblk - 1, z_f, of_ref)
    emit(0, z_b, ob_ref)


def rwkv_scan_pallas(w_f, w_b, kk, bvec, k, v, r, n_ctx):
    n, nb, _ = kk.shape
    tblk = RW_TBLK
    nblk = n // tblk
    nblk_ctx = n_ctx // tblk

    def fmap(c):
        return (c, 0, 0)

    def bmap(c):
        return (jnp.where(c < nblk_ctx, nblk_ctx - 1 - c, nblk + nblk_ctx - 1 - c), 0, 0)

    blk = (tblk, nb, RW_W)
    oblk = (tblk, 8, nb * RW_HD)
    return pl.pallas_call(
        functools.partial(_rwkv_scan_kernel, tblk, nb),
        out_shape=[jax.ShapeDtypeStruct((n, 8, nb * RW_HD), F32)] * 2,
        grid=(nblk,),
        in_specs=[pl.BlockSpec(blk, fmap)] * 6 + [pl.BlockSpec(blk, bmap)] * 6,
        out_specs=[pl.BlockSpec(oblk, fmap), pl.BlockSpec(oblk, bmap)],
        scratch_shapes=[pltpu.VMEM((nb, RW_HD, RW_W), F32), pltpu.VMEM((nb, RW_HD, RW_W), F32),
                        pltpu.VMEM((nb, RW_HD, RW_W), BF16), pltpu.VMEM((nb, RW_HD, RW_W), BF16)],
        compiler_params=pltpu.CompilerParams(dimension_semantics=("arbitrary",),
                                             vmem_limit_bytes=VMEM_LIMIT),
        name="rwkv_scan",
    )(w_f, kk, bvec, k, v, r, w_b, kk, bvec, k, v, r)


def _rwkv_post_kernel(of_ref, ob_ref, r_ref, k_ref, v_ref, gf_ref, gb_ref, rk_ref, lng_ref, lnb_ref,
                      y_ref):
    mean_blk = _head_blocks(1.0 / RW_HD)

    def gn(o):
        oc = o - _head_sum(o, mean_blk)
        var = _head_sum(oc * oc, mean_blk)
        return oc * lax.rsqrt(var + RW_GN_EPS) * lng_ref[...] + lnb_ref[...]

    bonus = _head_sum(r_ref[...] * k_ref[...] * rk_ref[...], _head_blocks(1.0)) * v_ref[...]
    y_ref[...] = (gn(of_ref[...]) + bonus) * gf_ref[...] + (gn(ob_ref[...]) + bonus) * gb_ref[...]


def rwkv_post(o_f, o_b, r, k, v, g_f, g_b, r_k, lnx_g, lnx_b, L):
    n_tok = r.shape[0]
    tm = min(RW_PREP_TM, L)
    tok = pl.BlockSpec((tm, RW_W), lambda i: (i, 0))
    vec = pl.BlockSpec((1, RW_W), lambda i: (0, 0))
    return pl.pallas_call(
        _rwkv_post_kernel,
        out_shape=jax.ShapeDtypeStruct((n_tok, RW_W), F32),
        grid=(n_tok // tm,),
        in_specs=[tok] * 7 + [vec] * 3,
        out_specs=tok,
        compiler_params=pltpu.CompilerParams(
            dimension_semantics=("parallel",), vmem_limit_bytes=VMEM_LIMIT),
        name="rwkv_post",
    )(o_f, o_b, r, k, v, g_f, g_b, r_k.reshape(1, RW_W), lnx_g.reshape(1, RW_W), lnx_b.reshape(1, RW_W))


def rwkv_branch(p, pc, B, L, C, need_ctx, mu, w0, w2, a0, a2, g2, k_k, k_a, r_k, lnx_g, lnx_b):
    r, k, v, kk, bv, d_f, d_b, g_f, g_b = rwkv_prep(p, B, L, mu, w0, w2, a0, a2, g2, k_k, k_a)
    rc, kc, vc, kkc, bc, dc_f, dc_b, gc_f, gc_b = rwkv_prep(pc, B, C, mu, w0, w2, a0, a2, g2, k_k, k_a)

    def tmajor(xc, xl):
        t = jnp.concatenate([xc.reshape(B, C, RW_W), xl.reshape(B, L, RW_W)], axis=1)
        return jnp.moveaxis(t, 1, 0)

    pf, pb = rwkv_scan_pallas(tmajor(dc_f, d_f), tmajor(dc_b, d_b), tmajor(kkc, kk), tmajor(bc, bv),
                              tmajor(kc, k), tmajor(vc, v), tmajor(rc, r), C)

    def bmajor(t):
        t = t[:, :RW_HEADS].reshape(C + L, RW_HEADS, B, RW_HD)
        return jnp.transpose(t, (2, 0, 1, 3)).reshape(B, C + L, RW_W)

    pf, pb = bmajor(pf), bmajor(pb)
    y = rwkv_post(pf[:, C:].reshape(B * L, RW_W), pb[:, C:].reshape(B * L, RW_W), r, k, v, g_f, g_b,
                  r_k, lnx_g, lnx_b, L).reshape(B, L, RW_W)
    yc = None
    if need_ctx:
        yc = rwkv_post(pf[:, :C].reshape(B * C, RW_W), pb[:, :C].reshape(B * C, RW_W), rc, kc, vc,
                       gc_f, gc_b, r_k, lnx_g, lnx_b, C).reshape(B, C, RW_W)
    return y, yc


def _diff_attn_kernel(q_ref, kt_ref, v_ref, lam_ref, gain_ref, o_ref):
    q = q_ref[0]
    v = v_ref[0]
    lam = lam_ref[...]
    tq = q.shape[0]
    head_of_lane = lax.broadcasted_iota(jnp.int32, (1, DF_W), 1) // DF_VD
    acc = jnp.zeros((tq, DF_W), F32)
    for h in range(DF_HEADS):
        parts = []
        for c in range(2):
            j = 2 * h + c
            s = jnp.dot(q[:, DF_HD * j:DF_HD * (j + 1)], kt_ref[0, j],
                        preferred_element_type=F32)
            p = jnp.exp2(s - jnp.max(s, -1, keepdims=True))
            l = jnp.sum(p, -1, keepdims=True)
            pv = jnp.dot(p.astype(BF16), v, preferred_element_type=F32)
            parts.append(pv / l)
        acc = jnp.where(head_of_lane == h, parts[0] - lam * parts[1], acc)
    row = lax.broadcasted_iota(jnp.int32, (DF_W, DF_W), 0)
    col = lax.broadcasted_iota(jnp.int32, (DF_W, DF_W), 1)
    seg_mean = jnp.where(row // DF_VD == col // DF_VD, 1.0 / DF_VD, 0.0).astype(F32)
    ms = jnp.dot(acc * acc, seg_mean, preferred_element_type=F32, precision=lax.Precision.HIGHEST)
    o_ref[0] = acc * lax.rsqrt(ms + 1e-5) * gain_ref[...]


def _diff_pre_kernel(rope, x_ref, cos_ref, sin_ref, q_ref, kt_ref, v_ref):
    x = x_ref[...].astype(F32)
    q, k, v = x[:, 0:DF_W], x[:, DF_W:2 * DF_W], x[:, 2 * DF_W:3 * DF_W]
    if rope:
        cos, sin = cos_ref[...], sin_ref[...]
        q = _rope(q, cos, sin, DF_HD)
        k = _rope(k, cos, sin, DF_HD)
    q_ref[...] = (q * (DF_HD ** -0.5 * math.log2(math.e))).astype(BF16)
    kt_ref[0] = k.T.astype(BF16)
    v_ref[...] = v.astype(BF16)


def diff_pre(p, B, L, tables, rope):
    tm = min(RW_PREP_TM, L)
    tiles_per_seq = L // tm
    n_tok = B * L
    width = 3 * DF_W
    col = P_DF // width
    cos, sin = tables
    tok = pl.BlockSpec((tm, DF_W), lambda i: (i, 0))
    tab = pl.BlockSpec((tm, DF_W), lambda i: (i % tiles_per_seq, 0))
    q, kt, v = pl.pallas_call(
        functools.partial(_diff_pre_kernel, rope),
        out_shape=[jax.ShapeDtypeStruct((n_tok, DF_W), BF16),
                   jax.ShapeDtypeStruct((B, DF_W, L), BF16),
                   jax.ShapeDtypeStruct((n_tok, DF_W), BF16)],
        grid=(n_tok // tm,),
        in_specs=[pl.BlockSpec((tm, width), lambda i: (i, col)), tab, tab],
        out_specs=[tok, pl.BlockSpec((1, DF_W, tm), lambda i: (i // tiles_per_seq, 0, i % tiles_per_seq)),
                   tok],
        compiler_params=pltpu.CompilerParams(
            dimension_semantics=("parallel",), vmem_limit_bytes=VMEM_LIMIT),
        name="diff_pre",
    )(p, cos, sin)
    return q.reshape(B, L, DF_W), kt, v.reshape(B, L, DF_W)


def diff_attend(qs, kt, vb, lam, lam_init, subln_g):
    B, Lq = qs.shape[:2]
    S = vb.shape[1]
    tq = min(DF_TQ, Lq)
    kt = kt.reshape(B, 2 * DF_HEADS, DF_HD, S)
    gain = jnp.tile(subln_g.astype(F32) * (1.0 - lam_init), DF_HEADS).reshape(1, DF_W)
    return pl.pallas_call(
        _diff_attn_kernel,
        out_shape=jax.ShapeDtypeStruct((B, Lq, DF_W), F32),
        grid=(B, Lq // tq),
        in_specs=[pl.BlockSpec((1, tq, DF_W), lambda b, i: (b, i, 0)),
                  pl.BlockSpec((1, 2 * DF_HEADS, DF_HD, S), lambda b, i: (b, 0, 0, 0)),
                  pl.BlockSpec((1, S, DF_W), lambda b, i: (b, 0, 0)),
                  pl.BlockSpec((1, 1), lambda b, i: (0, 0)),
                  pl.BlockSpec((1, DF_W), lambda b, i: (0, 0))],
        out_specs=pl.BlockSpec((1, tq, DF_W), lambda b, i: (b, i, 0)),
        compiler_params=pltpu.CompilerParams(
            dimension_semantics=("parallel", "parallel"), vmem_limit_bytes=VMEM_LIMIT),
        name="diff_attn",
    )(qs, kt, vb, lam.reshape(1, 1).astype(F32), gain)


def diff_branch(p, pc, B, L, C, need_ctx, tables, lq1, lk1, lq2, lk2, subln_g, lam_init):
    lam = (jnp.exp(jnp.sum(lq1.astype(F32) * lk1.astype(F32)))
           - jnp.exp(jnp.sum(lq2.astype(F32) * lk2.astype(F32))) + lam_init)
    q, kt, v = diff_pre(p, B, L, tables, True)
    qc, ktc, vc = diff_pre(pc, B, C, tables, False)
    y = diff_attend(q, jnp.concatenate([kt, ktc], axis=2), jnp.concatenate([v, vc], axis=1),
                    lam, lam_init, subln_g)
    yc = diff_attend(qc, ktc, vc, lam, lam_init, subln_g) if need_ctx else None
    return y, yc


def _ln_rows(x):
    mu = jnp.mean(x, -1, keepdims=True)
    xc = x - mu
    return xc * lax.rsqrt(jnp.mean(xc * xc, -1, keepdims=True) + LN_EPS)


def _ffn_kernel(tm, tiles_per_seq, n_chunks,
                h_ref, hprev_ref, hnext_ref, sh_ref, sc_ref, gate_ref,
                wa_ref, wb_ref, cwa_ref, cwb_ref, cba_ref, cbb_ref, wd_ref, lng_ref, lnb_ref,
                o_ref, u_ref, acc_ref):
    i = pl.program_id(0)
    j = pl.program_id(1)
    n = tm + 2 * HALO

    @pl.when(j == 0)
    def _():
        scale = 1.0 + sc_ref[0]
        shift = sh_ref[0]
        first = (i % tiles_per_seq) == 0
        last = (i % tiles_per_seq) == tiles_per_seq - 1
        u_ref[0:tm, :] = (_ln_rows(h_ref[...]) * scale + shift).astype(BF16)
        un = _ln_rows(hnext_ref[...]) * scale + shift
        up = _ln_rows(hprev_ref[...]) * scale + shift
        u_ref[tm:tm + HALO, :] = jnp.where(last, 0.0, un).astype(BF16)
        u_ref[tm + HALO:n, :] = jnp.where(first, 0.0, up).astype(BF16)
        acc_ref[...] = jnp.zeros_like(acc_ref)

    u = u_ref[...]

    def conv(w_ref, cw_ref, cb_ref):
        x = jnp.dot(u, w_ref[...], preferred_element_type=F32)
        cw = cw_ref[...]
        y = (pltpu.roll(x, 1, 0)[0:tm] * cw[0:1] + x[0:tm] * cw[1:2]
             + pltpu.roll(x, n - 1, 0)[0:tm] * cw[2:3] + cb_ref[...])
        return y

    a = conv(wa_ref, cwa_ref, cba_ref)
    b = conv(wb_ref, cwb_ref, cbb_ref)
    g = (a * jax.nn.sigmoid(a) * b).astype(BF16)
    acc_ref[...] += jnp.dot(g, wd_ref[...], preferred_element_type=F32)

    @pl.when(j == n_chunks - 1)
    def _():
        y = DN_ALPHA * h_ref[...] + gate_ref[0] * acc_ref[...]
        o_ref[...] = _ln_rows(y) * lng_ref[...] + lnb_ref[...]


def ffn_block(h, shift, scale, gate, w_up, conv_w, conv_b, w_down, ln_g, ln_b):
    B, L, D = h.shape
    tm = min(FFN_TM, L)
    tiles_per_seq = L // tm
    n_tok = B * L
    n_chunks = D_FF // FFN_CHUNK
    hb = tm // HALO

    def bcast(t):
        return jnp.broadcast_to(t.astype(F32), (B, 1, D))

    wu = w_up.astype(BF16)
    wd = w_down.astype(BF16)
    cb = conv_b.reshape(1, 2 * D_FF)
    h2 = h.reshape(n_tok, D)
    mod_spec = pl.BlockSpec((1, 1, D), lambda i, j: (i // tiles_per_seq, 0, 0))
    vec_spec = pl.BlockSpec((1, D), lambda i, j: (0, 0))
    out = pl.pallas_call(
        functools.partial(_ffn_kernel, tm, tiles_per_seq, n_chunks),
        out_shape=jax.ShapeDtypeStruct((n_tok, D), F32),
        grid=(n_tok // tm, n_chunks),
        in_specs=[pl.BlockSpec((tm, D), lambda i, j: (i, 0)),
                  pl.BlockSpec((HALO, D), lambda i, j: (jnp.maximum(i * hb - 1, 0), 0)),
                  pl.BlockSpec((HALO, D), lambda i, j: (jnp.minimum((i + 1) * hb, n_tok // HALO - 1), 0)),
                  mod_spec, mod_spec, mod_spec,
                  pl.BlockSpec((D, FFN_CHUNK), lambda i, j: (0, j)),
                  pl.BlockSpec((D, FFN_CHUNK), lambda i, j: (0, n_chunks + j)),
                  pl.BlockSpec((3, FFN_CHUNK), lambda i, j: (0, j)),
                  pl.BlockSpec((3, FFN_CHUNK), lambda i, j: (0, n_chunks + j)),
                  pl.BlockSpec((1, FFN_CHUNK), lambda i, j: (0, j)),
                  pl.BlockSpec((1, FFN_CHUNK), lambda i, j: (0, n_chunks + j)),
                  pl.BlockSpec((FFN_CHUNK, D), lambda i, j: (j, 0)),
                  vec_spec, vec_spec],
        out_specs=pl.BlockSpec((tm, D), lambda i, j: (i, 0)),
        scratch_shapes=[pltpu.VMEM((tm + 2 * HALO, D), BF16), pltpu.VMEM((tm, D), F32)],
        compiler_params=pltpu.CompilerParams(
            dimension_semantics=("parallel", "arbitrary"), vmem_limit_bytes=VMEM_LIMIT),
        name="conv_ffn",
    )(h2, h2, h2, bcast(shift), bcast(scale), bcast(gate), wu, wu, conv_w, conv_w, cb, cb, wd,
      ln_g.reshape(1, D), ln_b.reshape(1, D))
    return out.reshape(B, L, D)


P_GATE, P_HY, P_RW, P_SW, P_DF = 0, 4096, 5120, 6400, 6912
P_COLS = 7680
IN_CHUNK = 512


def _in_proj_kernel(x_ref, sh_ref, sc_ref, w_ref, o_ref, u_ref):
    @pl.when(pl.program_id(1) == 0)
    def _():
        u_ref[...] = (_ln_rows(x_ref[...]) * (1.0 + sc_ref[0]) + sh_ref[0]).astype(BF16)

    o_ref[...] = jnp.dot(u_ref[...], w_ref[...], preferred_element_type=F32).astype(BF16)


def in_proj(h, shift, scale, w_in_p):
    B, L, D = h.shape
    tm = min(IN_TM, L)
    tiles_per_seq = L // tm
    n_tok = B * L

    def bcast(t):
        return jnp.broadcast_to(t.astype(F32), (B, 1, D))

    mod_spec = pl.BlockSpec((1, 1, D), lambda i, j: (i // tiles_per_seq, 0, 0))
    return pl.pallas_call(
        _in_proj_kernel,
        out_shape=jax.ShapeDtypeStruct((n_tok, P_COLS), BF16),
        grid=(n_tok // tm, P_COLS // IN_CHUNK),
        in_specs=[pl.BlockSpec((tm, D), lambda i, j: (i, 0)), mod_spec, mod_spec,
                  pl.BlockSpec((D, IN_CHUNK), lambda i, j: (0, j))],
        out_specs=pl.BlockSpec((tm, IN_CHUNK), lambda i, j: (i, j)),
        scratch_shapes=[pltpu.VMEM((tm, D), BF16)],
        compiler_params=pltpu.CompilerParams(
            dimension_semantics=("parallel", "arbitrary"), vmem_limit_bytes=VMEM_LIMIT),
        name="in_proj",
    )(h.reshape(n_tok, D), bcast(shift), bcast(scale), w_in_p)


def pack_w_in(w):
    offs = np.cumsum((0,) + IN_SIZES)
    hy, sw, rw, df, gt = [w[:, offs[j]:offs[j + 1]] for j in range(len(IN_SIZES))]

    def padto(t, n):
        return jnp.pad(t, ((0, 0), (0, n - t.shape[1])))

    return jnp.concatenate([gt, padto(hy, P_RW - P_HY), padto(rw, P_SW - P_RW), sw, df],
                           axis=1).astype(BF16)


def _merge_kernel(yh_ref, ys_ref, yr_ref, yd_ref, g0_ref, g1_ref, g2_ref, g3_ref, h_ref, gate_ref,
                  wbr_ref, wo_ref, lng_ref, lnb_ref, o_ref):
    acc = None
    for j, (y_ref, g_ref) in enumerate(((yh_ref, g0_ref), (ys_ref, g1_ref), (yr_ref, g2_ref),
                                        (yd_ref, g3_ref))):
        term = jax.nn.sigmoid(g_ref[...].astype(F32)) * jnp.dot(y_ref[...].astype(BF16), wbr_ref[j],
                                                    preferred_element_type=F32)
        acc = term if acc is None else acc + term
    mix = jnp.dot(acc.astype(BF16), wo_ref[...], preferred_element_type=F32)
    y = DN_ALPHA * h_ref[...] + gate_ref[0] * mix
    o_ref[...] = _ln_rows(y) * lng_ref[...] + lnb_ref[...]


def merge_block(ys, p, h, gate, w_br, w_o, ln_g, ln_b):
    B, L, D = h.shape
    n_tok = B * L
    tm = min(MERGE_TM, L)
    tiles_per_seq = L // tm
    y_spec = pl.BlockSpec((tm, BR_W), lambda i: (i, 0))
    vec_spec = pl.BlockSpec((1, D), lambda i: (0, 0))
    g_specs = [pl.BlockSpec((tm, D), functools.partial(lambda j, i: (i, j), j)) for j in range(N_BRANCH)]
    out = pl.pallas_call(
        _merge_kernel,
        out_shape=jax.ShapeDtypeStruct((n_tok, D), F32),
        grid=(n_tok // tm,),
        in_specs=[y_spec] * N_BRANCH + g_specs + [
            pl.BlockSpec((tm, D), lambda i: (i, 0)),
            pl.BlockSpec((1, 1, D), lambda i: (i // tiles_per_seq, 0, 0)),
            pl.BlockSpec((N_BRANCH, BR_W, D), lambda i: (0, 0, 0)),
            pl.BlockSpec((D, D), lambda i: (0, 0)), vec_spec, vec_spec],
        out_specs=pl.BlockSpec((tm, D), lambda i: (i, 0)),
        compiler_params=pltpu.CompilerParams(
            dimension_semantics=("parallel",), vmem_limit_bytes=VMEM_LIMIT),
        name="merge",
    )(*[y.reshape(n_tok, BR_W) for y in ys], p, p, p, p, h.reshape(n_tok, D),
      jnp.broadcast_to(gate.astype(F32), (B, 1, D)), w_br.astype(BF16), w_o.astype(BF16),
      ln_g.reshape(1, D), ln_b.reshape(1, D))
    return out.reshape(B, L, D)


def kernel(x, c, ctx, c_ctx, ada_w, ada_b, w_in, hy_conv_w, hy_conv_b, hy_f_w1, hy_f_b1,
           hy_f_w2, hy_f_b2, hy_f_w3, hy_f_freq, hy_bias, swa_sink, rwkv_mu, rwkv_w0, rwkv_w2,
           rwkv_a0, rwkv_a2, rwkv_g2, rwkv_kk, rwkv_ka, rwkv_rk, rwkv_lnx_g, rwkv_lnx_b,
           diff_lq1, diff_lk1, diff_lq2, diff_lk2, diff_subln_g, w_branch, w_out, ln1_g, ln1_b,
           ffn_w_up, ffn_conv_w, ffn_conv_b, ffn_w_down, ln2_g, ln2_b):
    h, hc = x, ctx
    mats, mats_c = dft_mats(x.shape[1]), dft_mats(ctx.shape[1])
    swa_tables = rope_tables(x.shape[1], SWA_HD, SWA_HEADS * SWA_HD)
    df_tables = rope_tables(x.shape[1], DF_HD, DF_W)
    s_lat = jax.nn.silu(c)
    s_ctx = jax.nn.silu(c_ctx)
    for i in range(DEPTH):
        need_ctx = i < DEPTH - 1
        mod = (s_lat @ ada_w[i] + ada_b[i])[:, None, :]
        mod_c = s_ctx @ ada_w[i] + ada_b[i]
        sh1, sc1, g1, sh2, sc2, g2 = jnp.split(mod, 6, -1)
        csh1, csc1, cg1, csh2, csc2, cg2 = jnp.split(mod_c, 6, -1)

        B, L, D = h.shape
        C = hc.shape[1]
        w_in_p = pack_w_in(w_in[i])
        p = in_proj(h, sh1, sc1, w_in_p)
        pc = in_proj(hc, csh1.reshape(1, 1, D), csc1.reshape(1, 1, D), w_in_p)

        y_hy, yc_hy = hyena_branch(p, pc, B, L, C, need_ctx, mats, mats_c, hy_conv_w[i], hy_conv_b[i],
                                   hy_f_w1[i], hy_f_b1[i], hy_f_w2[i], hy_f_b2[i], hy_f_w3[i],
                                   hy_f_freq[i], hy_bias[i])
        y_sw, yc_sw = swa_branch(p, pc, B, L, C, need_ctx, swa_sink[i], swa_tables)
        y_rw, yc_rw = rwkv_branch(p, pc, B, L, C, need_ctx, rwkv_mu[i], rwkv_w0[i], rwkv_w2[i],
                                  rwkv_a0[i], rwkv_a2[i], rwkv_g2[i], rwkv_kk[i], rwkv_ka[i],
                                  rwkv_rk[i], rwkv_lnx_g[i], rwkv_lnx_b[i])
        lam_init = 0.8 - 0.6 * math.exp(-0.3 * i)
        y_df, yc_df = diff_branch(p, pc, B, L, C, need_ctx, df_tables, diff_lq1[i], diff_lk1[i],
                                  diff_lq2[i], diff_lk2[i], diff_subln_g[i], lam_init)

        h = merge_block((y_hy, y_sw, y_rw, y_df), p, h, g1, w_branch[i], w_out[i], ln1_g[i], ln1_b[i])
        h = ffn_block(h, sh2, sc2, g2, ffn_w_up[i], ffn_conv_w[i], ffn_conv_b[i], ffn_w_down[i],
                      ln2_g[i], ln2_b[i])

        if need_ctx:
            hc = merge_block((yc_hy, yc_sw, yc_rw, yc_df), pc, hc, cg1.reshape(1, 1, D), w_branch[i],
                             w_out[i], ln1_g[i], ln1_b[i])
            hc = ffn_block(hc, csh2.reshape(1, 1, -1), csc2.reshape(1, 1, -1), cg2.reshape(1, 1, -1),
                           ffn_w_up[i], ffn_conv_w[i], ffn_conv_b[i], ffn_w_down[i], ln2_g[i], ln2_b[i])
    return h
```

```python
import functools
import math

import jax
import jax.numpy as jnp
import numpy as np
from jax import lax
from jax.experimental import pallas as pl
from jax.experimental.pallas import tpu as pltpu

D_MODEL = 1024
DEPTH = 2
GRID_W = 64
ROPE_BASE = 10000.0
F32 = jnp.float32
BF16 = jnp.bfloat16
NEG_INF = -1e30
LN_EPS = 1e-6

HY_W = 256
HY_ORDER = 2
HY_DIRS = 2
HY_EMB = 33
HY_MIN_DECAY = math.log(1e-2) / 1.5
HY_MAX_DECAY = math.log(1e-2) / 0.3

SWA_HEADS = 4
SWA_KV = 2
SWA_HD = 64
SWA_WIN = 128
SWA_BLOCK = 128

RW_HEADS = 4
RW_HD = 64
RW_W = RW_HEADS * RW_HD
RW_DECAY_R = 64
RW_AAA_R = 64
RW_GATE_R = 128
RW_GN_EPS = 64e-5

DF_HEADS = 4
DF_HD = 32
DF_VD = 2 * DF_HD
DF_W = DF_HEADS * DF_VD

N_BRANCH = 4
BR_W = 256
D_FF = 2816
DN_ALPHA = (2 * DEPTH) ** 0.25

SWA_SIZES = (SWA_HEADS * SWA_HD, SWA_KV * SWA_HD, SWA_KV * SWA_HD)
RW_SIZES = (RW_W, RW_W, RW_W, RW_DECAY_R, RW_DECAY_R, RW_AAA_R, RW_GATE_R, RW_GATE_R)
DF_SIZES = (DF_W, DF_W, DF_W)
IN_SIZES = (3 * HY_W, sum(SWA_SIZES), sum(RW_SIZES), sum(DF_SIZES), N_BRANCH * D_MODEL)

V7X_VMEM_BYTES = 64 * 1024 * 1024
LANES = 128
MM_TM = 512
MM_TN = 512
RW_TBLK = 64
DF_TQ = 256
SWA_QROWS = 256
HALO = 8
P_HALO = 16
IN_TM = 2048
FFN_TM = 1024
FFN_CHUNK = 256
MERGE_TM = 512
HY_TK = 512
RW_PREP_TM = 512
RW_POST_TM = 256
VMEM_LIMIT = V7X_VMEM_BYTES * 3 // 4


def _mm_kernel(x_ref, w_ref, o_ref):
    o_ref[...] = jnp.dot(x_ref[...].astype(BF16), w_ref[...], preferred_element_type=F32)


def _mm(x, w):
    m, k = x.shape
    n = w.shape[1]
    n_pad = -(-n // LANES) * LANES
    wb = w.astype(BF16)
    if n_pad != n:
        wb = jnp.pad(wb, ((0, 0), (0, n_pad - n)))
    tn = MM_TN if n_pad % MM_TN == 0 else (256 if n_pad % 256 == 0 else LANES)
    tm = MM_TM if m % MM_TM == 0 else m
    out = pl.pallas_call(
        _mm_kernel,
        out_shape=jax.ShapeDtypeStruct((m, n_pad), F32),
        grid=(m // tm, n_pad // tn),
        in_specs=[pl.BlockSpec((tm, k), lambda i, j: (i, 0)),
                  pl.BlockSpec((k, tn), lambda i, j: (0, j))],
        out_specs=pl.BlockSpec((tm, tn), lambda i, j: (i, j)),
        compiler_params=pltpu.CompilerParams(
            dimension_semantics=("parallel", "parallel"), vmem_limit_bytes=VMEM_LIMIT),
        name="matmul",
    )(x, wb)
    return out[:, :n] if n_pad != n else out


def dft_mats(L):
    tb = 64
    k = jnp.arange(L, dtype=jnp.int32)[:, None]

    def table(t):
        ang = (((2 * k + 1) * t[None, :]) % (4 * L)).astype(F32) * (math.pi / (2 * L))
        return jnp.cos(ang), jnp.sin(ang)

    ca, sa = table(tb * jnp.arange(L // tb, dtype=jnp.int32))
    cb, sb = table(jnp.arange(tb, dtype=jnp.int32))
    c = (ca[:, :, None] * cb[:, None, :] - sa[:, :, None] * sb[:, None, :]).reshape(L, L)
    s = (sa[:, :, None] * cb[:, None, :] + ca[:, :, None] * sb[:, None, :]).reshape(L, L)
    ct = (ca.T[:, None, :] * cb.T[None, :, :] - sa.T[:, None, :] * sb.T[None, :, :]).reshape(L, L)
    st = (sa.T[:, None, :] * cb.T[None, :, :] + ca.T[:, None, :] * sb.T[None, :, :]).reshape(L, L)
    return c.astype(BF16), s.astype(BF16), ct.astype(BF16), st.astype(BF16)


def hyena_kspec(L, mats, w1, b1, w2, b2, w3, freq):
    t = jnp.linspace(0.0, 1.0, L, dtype=F32)[:, None]
    bands = (HY_EMB - 1) // 2
    w = 2.0 * math.pi * jnp.arange(L, dtype=F32) / L
    fb = jnp.linspace(1e-4, bands - 1, bands, dtype=F32)
    ang = w[:, None] * fb[None, :]
    z = jnp.concatenate([t, jnp.cos(ang), -jnp.sin(ang)], -1)
    fr = freq.astype(F32)
    hdn = jnp.sin(fr * (z @ w1.astype(F32) + b1.astype(F32)))
    hdn = jnp.sin(fr * (hdn @ w2.astype(F32) + b2.astype(F32)))
    filt = (hdn @ w3.astype(F32)).reshape(L, HY_ORDER, HY_DIRS, HY_W)
    deltas = jnp.abs(jnp.linspace(HY_MIN_DECAY, HY_MAX_DECAY, HY_W, dtype=F32))
    filt = filt * jnp.exp(-t * deltas[None, :])[:, None, None, :]
    filt = filt / jnp.sum(jnp.abs(filt), axis=(0, 2), keepdims=True)
    fwd, bwd = filt[:, :, 0], filt[:, :, 1]
    nw = HY_ORDER * HY_W
    bwd = bwd.at[0].set(0.0)
    kr = _mm(mats[0], (fwd + bwd).reshape(L, nw))
    ki = -_mm(mats[1], (fwd - bwd).reshape(L, nw))

    def per_order(a):
        return jnp.transpose(a.reshape(L, HY_ORDER, HY_W), (1, 0, 2))

    return per_order(kr), per_order(ki)


def _hy_fwd_kernel(ac_ref, as_ref, z_ref, kr_ref, ki_ref, yr_ref, yi_ref):
    z = z_ref[0].astype(BF16)
    zr = jnp.dot(ac_ref[...], z, preferred_element_type=F32)
    zi = -jnp.dot(as_ref[...], z, preferred_element_type=F32)
    kr, ki = kr_ref[...], ki_ref[...]
    yr_ref[0] = (zr * kr - zi * ki).astype(BF16)
    yi_ref[0] = (zr * ki + zi * kr).astype(BF16)


def _hy_inv_kernel(inv_l, act_ref, ast_ref, yr_ref, yi_ref, z_ref, x_ref, bias_ref, o_ref):
    y = (jnp.dot(act_ref[...], yr_ref[0], preferred_element_type=F32)
         - jnp.dot(ast_ref[...], yi_ref[0], preferred_element_type=F32))
    o_ref[0] = x_ref[0] * (y * inv_l + bias_ref[...] * z_ref[0])


def hyena_longconv(z, xmul, mats, kr, ki, bias):
    B, L, W = z.shape
    ac, as_, act, ast = mats
    tk = min(HY_TK, L)
    a_spec = pl.BlockSpec((tk, L), lambda i, b: (i, 0))
    full_spec = pl.BlockSpec((1, L, W), lambda i, b: (b, 0, 0))
    tile_spec = pl.BlockSpec((1, tk, W), lambda i, b: (b, i, 0))
    k_spec = pl.BlockSpec((tk, W), lambda i, b: (i, 0))
    params = pltpu.CompilerParams(dimension_semantics=("parallel", "arbitrary"),
                                  vmem_limit_bytes=VMEM_LIMIT)
    yr, yi = pl.pallas_call(
        _hy_fwd_kernel,
        out_shape=[jax.ShapeDtypeStruct((B, L, W), BF16)] * 2,
        grid=(L // tk, B),
        in_specs=[a_spec, a_spec, full_spec, k_spec, k_spec],
        out_specs=[tile_spec, tile_spec],
        compiler_params=params,
        name="hyena_fwd",
    )(ac, as_, z, kr, ki)
    return pl.pallas_call(
        functools.partial(_hy_inv_kernel, 1.0 / L),
        out_shape=jax.ShapeDtypeStruct((B, L, W), F32),
        grid=(L // tk, B),
        in_specs=[a_spec, a_spec, full_spec, full_spec, tile_spec, tile_spec,
                  pl.BlockSpec((1, W), lambda i, b: (0, 0))],
        out_specs=tile_spec,
        compiler_params=params,
        name="hyena_inv",
    )(act, ast, yr, yi, z, xmul, bias.reshape(1, W))


def _hy_pre_kernel(tm, tiles_per_seq, x_ref, xp_ref, xn_ref, cw_ref, cb_ref, v_ref, x1_ref, x2_ref):
    i = pl.program_id(0)
    first = (i % tiles_per_seq) == 0
    last = (i % tiles_per_seq) == tiles_per_seq - 1
    x = x_ref[...].astype(F32)
    row = lax.broadcasted_iota(jnp.int32, x.shape, 0)
    prev_row = jnp.where(first, 0.0, xp_ref[...].astype(F32)[P_HALO - 1:P_HALO, :])
    next_row = jnp.where(last, 0.0, xn_ref[...].astype(F32)[0:1, :])
    xm1 = jnp.where(row == 0, prev_row, pltpu.roll(x, 1, 0))
    xp1 = jnp.where(row == tm - 1, next_row, pltpu.roll(x, tm - 1, 0))
    cw = cw_ref[...]
    y = xm1 * cw[0:1] + x * cw[1:2] + xp1 * cw[2:3] + cb_ref[...]
    v_ref[...] = y[:, 0:HY_W]
    x1_ref[...] = y[:, HY_W:2 * HY_W]
    x2_ref[...] = y[:, 2 * HY_W:3 * HY_W]


def hyena_pre(p, B, L, conv_w, conv_b):
    tm = min(RW_PREP_TM, L)
    tiles_per_seq = L // tm
    n_tok = B * L
    hb = tm // P_HALO
    width = P_RW - P_HY
    col = P_HY // width
    cw = jnp.pad(conv_w, ((0, 0), (0, width - conv_w.shape[1])))
    cb = jnp.pad(conv_b, (0, width - conv_b.shape[0])).reshape(1, width)
    out_spec = pl.BlockSpec((tm, HY_W), lambda i: (i, 0))
    outs = pl.pallas_call(
        functools.partial(_hy_pre_kernel, tm, tiles_per_seq),
        out_shape=[jax.ShapeDtypeStruct((n_tok, HY_W), F32)] * 3,
        grid=(n_tok // tm,),
        in_specs=[pl.BlockSpec((tm, width), lambda i: (i, col)),
                  pl.BlockSpec((P_HALO, width), lambda i: (jnp.maximum(i * hb - 1, 0), col)),
                  pl.BlockSpec((P_HALO, width), lambda i: (jnp.minimum((i + 1) * hb, n_tok // P_HALO - 1), col)),
                  pl.BlockSpec((3, width), lambda i: (0, 0)),
                  pl.BlockSpec((1, width), lambda i: (0, 0))],
        out_specs=[out_spec] * 3,
        compiler_params=pltpu.CompilerParams(
            dimension_semantics=("parallel",), vmem_limit_bytes=VMEM_LIMIT),
        name="hyena_pre",
    )(p, p, p, cw, cb)
    return [o.reshape(B, L, HY_W) for o in outs]


def hyena_mix(p, B, L, conv_w, conv_b, mats, kspec, bias):
    v, x1, x2 = hyena_pre(p, B, L, conv_w, conv_b)
    kr, ki = kspec
    zz = hyena_longconv(v, x1, mats, kr[0], ki[0], bias[0])
    return hyena_longconv(zz, x2, mats, kr[1], ki[1], bias[1])


def hyena_branch(p, pc, B, L, C, need_ctx, mats, mats_c, conv_w, conv_b, fw1, fb1, fw2, fb2, fw3, ffreq,
                 bias):
    kspec = hyena_kspec(L, mats, fw1, fb1, fw2, fb2, fw3, ffreq)
    y = hyena_mix(p, B, L, conv_w, conv_b, mats, kspec, bias)
    yc = None
    if need_ctx:
        kspec_c = hyena_kspec(C, mats_c, fw1, fb1, fw2, fb2, fw3, ffreq)
        yc = hyena_mix(pc, B, C, conv_w, conv_b, mats_c, kspec_c, bias)
    return y, yc


def rope_tables(L, d, width):
    nf = d // 4
    pos = jnp.arange(L)
    inv = ROPE_BASE ** (-jnp.arange(nf, dtype=F32) / nf)
    cs, sn = [], []
    for p in (pos // GRID_W, pos % GRID_W):
        ang = p.astype(F32)[:, None] * inv[None, :]
        cs += [jnp.cos(ang), jnp.cos(ang)]
        sn += [-jnp.sin(ang), jnp.sin(ang)]
    reps = width // d
    return jnp.tile(jnp.concatenate(cs, -1), (1, reps)), jnp.tile(jnp.concatenate(sn, -1), (1, reps))


def _rope(x, cos, sin, d):
    width = x.shape[-1]
    q = d // 4
    lane = lax.broadcasted_iota(jnp.int32, x.shape, 1)
    swapped = jnp.where(lane % (2 * q) < q, pltpu.roll(x, width - q, 1), pltpu.roll(x, q, 1))
    return x * cos + swapped * sin


def _swa_kernel(local, seq_len, qrows, q_ref, *refs):
    nk = qrows // SWA_BLOCK + 2
    k_refs, v_refs = refs[0:nk], refs[nk:2 * nk]
    ck_ref, cv_ref, cos_ref, sin_ref, sink_ref, o_ref = refs[2 * nk:]
    n = pl.program_id(1)
    blk = SWA_BLOCK
    qscale = SWA_HD ** -0.5 * math.log2(math.e)
    q = q_ref[...].astype(F32)
    if local:
        q0 = pl.multiple_of(n * qrows, qrows)
        q = _rope(q, cos_ref[pl.ds(q0, qrows), :], sin_ref[pl.ds(q0, qrows), :], SWA_HD)
        first = n * (qrows // blk) - 1
        ks, vs = [], []
        for j, (k_ref, v_ref) in enumerate(zip(k_refs, v_refs)):
            k0 = pl.multiple_of(jnp.clip(first + j, 0, seq_len // blk - 1) * blk, blk)
            ks.append(_rope(k_ref[...].astype(F32), cos_ref[pl.ds(k0, blk), 0:SWA_KV * SWA_HD],
                            sin_ref[pl.ds(k0, blk), 0:SWA_KV * SWA_HD], SWA_HD).astype(BF16))
            vs.append(v_ref[...].astype(BF16))
        k_loc = jnp.concatenate(ks, axis=0)
        v_loc = jnp.concatenate(vs, axis=0)
        qpos = n * qrows + lax.broadcasted_iota(jnp.int32, (qrows, nk * blk), 0)
        kpos = first * blk + lax.broadcasted_iota(jnp.int32, (qrows, nk * blk), 1)
        valid = (jnp.abs(kpos - qpos) <= SWA_WIN) & (kpos >= 0) & (kpos < seq_len)
    qb = (q * qscale).astype(BF16)
    ck = ck_ref[...].astype(BF16)
    cv = cv_ref[...].astype(BF16)
    nt = (((1,), (1,)), ((), ()))
    outs = []
    for h in range(SWA_HEADS):
        g = h // (SWA_HEADS // SWA_KV)
        qh = qb[:, h * SWA_HD:(h + 1) * SWA_HD]
        gs = slice(g * SWA_HD, (g + 1) * SWA_HD)
        sink = sink_ref[h:h + 1, 0:1]
        s_ctx = lax.dot_general(qh, ck[:, gs], nt, preferred_element_type=F32)
        m = jnp.maximum(jnp.max(s_ctx, -1, keepdims=True), sink)
        if local:
            s_loc = lax.dot_general(qh, k_loc[:, gs], nt, preferred_element_type=F32)
            s_loc = jnp.where(valid, s_loc, NEG_INF)
            m = jnp.maximum(m, jnp.max(s_loc, -1, keepdims=True))
        p_ctx = jnp.exp2(s_ctx - m)
        den = jnp.sum(p_ctx, -1, keepdims=True) + jnp.exp2(sink - m)
        acc = jnp.dot(p_ctx.astype(BF16), cv[:, gs], preferred_element_type=F32)
        if local:
            p_loc = jnp.exp2(s_loc - m)
            den = den + jnp.sum(p_loc, -1, keepdims=True)
            acc = acc + jnp.dot(p_loc.astype(BF16), v_loc[:, gs], preferred_element_type=F32)
        outs.append(acc / den)
    o_ref[...] = jnp.concatenate(outs, axis=-1)


def swa_attend(p, pc, B, L, C, sink, tables, local):
    blk = SWA_BLOCK
    nb = L // blk
    qrows = min(SWA_QROWS, L)
    nq = L // qrows
    qpb = qrows // blk
    qcol = P_SW // (SWA_HEADS * SWA_HD)
    kcol = (P_SW + SWA_HEADS * SWA_HD) // (SWA_KV * SWA_HD)
    kvw = SWA_KV * SWA_HD

    def nbr(j, col):
        return lambda b, n: (b * nb + jnp.clip(n * qpb + j, 0, nb - 1), col)

    nbrs = range(-1, qpb + 1)

    cos, sin = tables
    sink_rows = jnp.broadcast_to((sink.astype(F32) * math.log2(math.e))[:, None], (SWA_HEADS, LANES))
    tab_spec = pl.BlockSpec(cos.shape, lambda b, n: (0, 0))
    out = pl.pallas_call(
        functools.partial(_swa_kernel, local, L, qrows),
        out_shape=jax.ShapeDtypeStruct((B * L, SWA_HEADS * SWA_HD), F32),
        grid=(B, nq),
        in_specs=[pl.BlockSpec((qrows, SWA_HEADS * SWA_HD), lambda b, n: (b * nq + n, qcol))]
        + [pl.BlockSpec((blk, kvw), nbr(j, kcol)) for j in nbrs]
        + [pl.BlockSpec((blk, kvw), nbr(j, kcol + 1)) for j in nbrs]
        + [pl.BlockSpec((C, kvw), lambda b, n: (b, kcol)),
           pl.BlockSpec((C, kvw), lambda b, n: (b, kcol + 1)),
           tab_spec, tab_spec,
           pl.BlockSpec((SWA_HEADS, LANES), lambda b, n: (0, 0))],
        out_specs=pl.BlockSpec((qrows, SWA_HEADS * SWA_HD), lambda b, n: (b * nq + n, 0)),
        compiler_params=pltpu.CompilerParams(
            dimension_semantics=("parallel", "parallel"), vmem_limit_bytes=VMEM_LIMIT),
        name="swa",
    )(p, *([p] * (2 * len(nbrs))), pc, pc, cos, sin, sink_rows)
    return out.reshape(B, L, SWA_HEADS * SWA_HD)


def swa_branch(p, pc, B, L, C, need_ctx, sink, tables):
    y = swa_attend(p, pc, B, L, C, sink, tables, True)
    yc = swa_attend(pc, pc, B, C, C, sink, tables, False) if need_ctx else None
    return y, yc


def _head_blocks(value):
    row = lax.broadcasted_iota(jnp.int32, (RW_W, RW_W), 0)
    col = lax.broadcasted_iota(jnp.int32, (RW_W, RW_W), 1)
    return jnp.where(row // RW_HD == col // RW_HD, value, 0.0).astype(F32)


def _head_sum(x, blocks):
    return jnp.dot(x, blocks, preferred_element_type=F32, precision=lax.Precision.HIGHEST)


RW_OFF = tuple(int(o) for o in np.cumsum((0,) + RW_SIZES))
RW_PAD = 1280
RW_WIN_WD = RW_OFF[3]
RW_WIN_LO = RW_OFF[5]
RW_WIN_HI = RW_WIN_LO + LANES


def _rwkv_prep_kernel(tm, tiles_per_seq, x_ref, xp_ref, xn_ref, mu_ref, w0_ref, w2f_ref, w2b_ref,
                      a0_ref, a2_ref, g2f_ref, g2b_ref, kk_ref, ka_ref,
                      r_o, k_o, v_o, kkn_o, b_o, df_o, db_o, gf_o, gb_o):
    i = pl.program_id(0)
    first = (i % tiles_per_seq) == 0
    last = (i % tiles_per_seq) == tiles_per_seq - 1
    x = x_ref[...].astype(F32)
    row = lax.broadcasted_iota(jnp.int32, x.shape, 0)
    prev_row = jnp.where(first, 0.0, xp_ref[...].astype(F32)[P_HALO - 1:P_HALO, :])
    next_row = jnp.where(last, 0.0, xn_ref[...].astype(F32)[0:1, :])
    xm1 = jnp.where(row == 0, prev_row, pltpu.roll(x, 1, 0))
    xp1 = jnp.where(row == tm - 1, next_row, pltpu.roll(x, tm - 1, 0))
    x = x + (0.5 * (xm1 + xp1) - x) * mu_ref[...]

    r = x[:, RW_OFF[0]:RW_OFF[1]]
    k = x[:, RW_OFF[1]:RW_OFF[2]]
    v = x[:, RW_OFF[2]:RW_OFF[3]]
    wd = jnp.tanh(x[:, RW_WIN_WD:RW_WIN_WD + LANES]).astype(BF16)
    lo = x[:, RW_WIN_LO:RW_WIN_LO + RW_W]
    hi = x[:, RW_WIN_HI:RW_WIN_HI + RW_W]

    def decay(w2_ref, w0):
        wlog = -jax.nn.softplus(-(w0 + jnp.dot(wd, w2_ref[...], preferred_element_type=F32))) - 0.5
        return jnp.exp(-jnp.exp(wlog))

    a = jax.nn.sigmoid(a0_ref[...] + jnp.dot(lo[:, 0:LANES].astype(BF16), a2_ref[...],
                                             preferred_element_type=F32))
    kk = k * kk_ref[...]
    norm = jnp.sqrt(_head_sum(kk * kk, _head_blocks(1.0)))
    kk = kk / jnp.maximum(norm, 1e-12)
    r_o[...] = r
    k_o[...] = k * (1.0 + (a - 1.0) * ka_ref[...])
    v_o[...] = v
    kkn_o[...] = kk
    b_o[...] = kk * a
    df_o[...] = decay(w2f_ref, w0_ref[0:1, :])
    db_o[...] = decay(w2b_ref, w0_ref[1:2, :])
    gf_o[...] = jnp.dot(jax.nn.sigmoid(lo).astype(BF16), g2f_ref[...], preferred_element_type=F32)
    gb_o[...] = jnp.dot(jax.nn.sigmoid(hi).astype(BF16), g2b_ref[...], preferred_element_type=F32)


def rwkv_prep(p, B, L, mu, w0, w2, a0, a2, g2, k_k, k_a):
    tm = min(RW_PREP_TM, L)
    tiles_per_seq = L // tm
    n_tok = B * L
    hb = tm // P_HALO
    col = P_RW // RW_PAD

    def rows(n, r0, src):
        return jnp.zeros((n, RW_W), F32).at[r0:r0 + src.shape[0]].set(src).astype(BF16)

    w2f = rows(LANES, RW_OFF[3] - RW_WIN_WD, w2[0])
    w2b = rows(LANES, RW_OFF[4] - RW_WIN_WD, w2[1])
    a2p = rows(LANES, RW_OFF[5] - RW_WIN_LO, a2)
    g2f = rows(RW_W, RW_OFF[6] - RW_WIN_LO, g2[0])
    g2b = rows(RW_W, RW_OFF[7] - RW_WIN_HI, g2[1])
    mu_p = jnp.pad(mu, (0, RW_PAD - mu.shape[0])).reshape(1, RW_PAD)

    def const(shape):
        return pl.BlockSpec(shape, lambda i: (0,) * len(shape))

    vec = const((1, RW_W))
    out_spec = pl.BlockSpec((tm, RW_W), lambda i: (i, 0))
    return pl.pallas_call(
        functools.partial(_rwkv_prep_kernel, tm, tiles_per_seq),
        out_shape=[jax.ShapeDtypeStruct((n_tok, RW_W), F32)] * 9,
        grid=(n_tok // tm,),
        in_specs=[pl.BlockSpec((tm, RW_PAD), lambda i: (i, col)),
                  pl.BlockSpec((P_HALO, RW_PAD), lambda i: (jnp.maximum(i * hb - 1, 0), col)),
                  pl.BlockSpec((P_HALO, RW_PAD), lambda i: (jnp.minimum((i + 1) * hb, n_tok // P_HALO - 1), col)),
                  const((1, RW_PAD)), const((2, RW_W)), const((LANES, RW_W)), const((LANES, RW_W)),
                  vec, const((LANES, RW_W)), const((RW_W, RW_W)), const((RW_W, RW_W)), vec, vec],
        out_specs=[out_spec] * 9,
        compiler_params=pltpu.CompilerParams(
            dimension_semantics=("parallel",), vmem_limit_bytes=VMEM_LIMIT),
        name="rwkv_prep",
    )(p, p, p, mu_p, w0, w2f, w2b, a0.reshape(1, RW_W), a2p, g2f, g2b,
      k_k.reshape(1, RW_W), k_a.reshape(1, RW_W))


def _rwkv_scan_kernel(tblk, nb,
                      wf_ref, kkf_ref, bf_ref, kf_ref, vf_ref, rf_ref,
                      wb_ref, kkb_ref, bb_ref, kb_ref, vb_ref, rb_ref,
                      of_ref, ob_ref, sf_ref, sb_ref, sf16_ref, sb16_ref):
    c = pl.program_id(0)

    @pl.when(c == 0)
    def _():
        for ref in (sf_ref, sb_ref, sf16_ref, sb16_ref):
            ref[...] = jnp.zeros_like(ref)

    row = lax.broadcasted_iota(jnp.int32, (RW_W, RW_W), 0)
    col = lax.broadcasted_iota(jnp.int32, (RW_W, RW_W), 1)
    ones_blk = jnp.where(row // RW_HD == col // RW_HD, 1.0, 0.0).astype(BF16)
    r64 = lax.broadcasted_iota(jnp.int32, (RW_HD, RW_W), 0)
    c64 = lax.broadcasted_iota(jnp.int32, (RW_HD, RW_W), 1)
    eye_t = jnp.where(r64 == c64 % RW_HD, 1.0, 0.0).astype(BF16)
    r8 = lax.broadcasted_iota(jnp.int32, (8, RW_W), 0)
    c8 = lax.broadcasted_iota(jnp.int32, (8, RW_W), 1)
    head_sel = jnp.where(r8 == c8 // RW_HD, 1.0, 0.0).astype(BF16)

    fwd = (wf_ref, kkf_ref, bf_ref, kf_ref, vf_ref, rf_ref, sf_ref, sf16_ref)
    bwd = (wb_ref, kkb_ref, bb_ref, kb_ref, vb_ref, rb_ref, sb_ref, sb16_ref)
    nrow = nb * RW_HD

    def one_dir(idx, refs):
        w_ref, kk_ref, b_ref, k_ref, v_ref, r_ref, s_ref, s16_ref = refs
        zs = []
        for b in range(nb):
            kk16 = kk_ref[idx, b:b + 1, :].astype(BF16)
            v16 = v_ref[idx, b:b + 1, :].astype(BF16)
            lhs = jnp.concatenate([s16_ref[b] * kk16, eye_t * v16], axis=0)
            red = jnp.dot(lhs, ones_blk, preferred_element_type=F32)
            s_new = (s_ref[b] * w_ref[idx, b:b + 1, :] - red[0:RW_HD] * b_ref[idx, b:b + 1, :]
                     + red[RW_HD:2 * RW_HD] * k_ref[idx, b:b + 1, :])
            s_ref[b] = s_new
            s16 = s_new.astype(BF16)
            s16_ref[b] = s16
            zs.append(s16 * r_ref[idx, b:b + 1, :].astype(BF16))
        return jnp.concatenate(zs, axis=0)

    def emit(idx, z, o_ref):
        o = lax.dot_general(head_sel, z, (((1,), (1,)), ((), ())), preferred_element_type=F32)
        o_ref[idx] = o[0:RW_HEADS]

    def step(i, carry):
        z_f, z_b = carry
        ib = tblk - 1 - i
        emit(jnp.maximum(i - 1, 0), z_f, of_ref)
        emit(jnp.minimum(ib + 1, tblk - 1), z_b, ob_ref)
        return one_dir(i, fwd), one_dir(ib, bwd)

    z0 = jnp.zeros((nrow, RW_W), BF16)
    z_f, z_b = lax.fori_loop(0, tblk, step, (z0, z0), unroll=8)
    emit(tblk - 1, z_f, of_ref)
    emit(0, z_b, ob_ref)


def rwkv_scan_pallas(w_f, w_b, kk, bvec, k, v, r, n_ctx):
    n, nb, _ = kk.shape
    tblk = RW_TBLK
    nblk = n // tblk
    nblk_ctx = n_ctx // tblk

    def fmap(c):
        return (c, 0, 0)

    def bmap(c):
        return (jnp.where(c < nblk_ctx, nblk_ctx - 1 - c, nblk + nblk_ctx - 1 - c), 0, 0)

    blk = (tblk, nb, RW_W)
    oblk = (tblk, RW_HEADS, nb * RW_HD)
    return pl.pallas_call(
        functools.partial(_rwkv_scan_kernel, tblk, nb),
        out_shape=[jax.ShapeDtypeStruct((n, RW_HEADS, nb * RW_HD), F32)] * 2,
        grid=(nblk,),
        in_specs=[pl.BlockSpec(blk, fmap)] * 6 + [pl.BlockSpec(blk, bmap)] * 6,
        out_specs=[pl.BlockSpec(oblk, fmap), pl.BlockSpec(oblk, bmap)],
        scratch_shapes=[pltpu.VMEM((nb, RW_HD, RW_W), F32), pltpu.VMEM((nb, RW_HD, RW_W), F32),
                        pltpu.VMEM((nb, RW_HD, RW_W), BF16), pltpu.VMEM((nb, RW_HD, RW_W), BF16)],
        compiler_params=pltpu.CompilerParams(dimension_semantics=("arbitrary",),
                                             vmem_limit_bytes=VMEM_LIMIT),
        name="rwkv_scan",
    )(w_f, kk, bvec, k, v, r, w_b, kk, bvec, k, v, r)


def _rwkv_post_kernel(of_ref, ob_ref, r_ref, k_ref, v_ref, gf_ref, gb_ref, rk_ref, lng_ref, lnb_ref,
                      y_ref):
    mean_blk = _head_blocks(1.0 / RW_HD)

    def gn(o):
        oc = o - _head_sum(o, mean_blk)
        var = _head_sum(oc * oc, mean_blk)
        return oc * lax.rsqrt(var + RW_GN_EPS) * lng_ref[...] + lnb_ref[...]

    bonus = _head_sum(r_ref[...] * k_ref[...] * rk_ref[...], _head_blocks(1.0)) * v_ref[...]
    y_ref[...] = (gn(of_ref[...]) + bonus) * gf_ref[...] + (gn(ob_ref[...]) + bonus) * gb_ref[...]


def rwkv_post(o_f, o_b, r, k, v, g_f, g_b, r_k, lnx_g, lnx_b, L, n_all, start):
    n_tok = r.shape[0]
    tm = min(RW_POST_TM, L)
    assert start % tm == 0 and n_all % tm == 0 and L % tm == 0
    tiles_per_seq = L // tm
    tok = pl.BlockSpec((tm, RW_W), lambda i: (i, 0))
    o_spec = pl.BlockSpec(
        (tm, RW_W), lambda i: ((i // tiles_per_seq) * (n_all // tm) + start // tm + i % tiles_per_seq, 0))
    vec = pl.BlockSpec((1, RW_W), lambda i: (0, 0))
    return pl.pallas_call(
        _rwkv_post_kernel,
        out_shape=jax.ShapeDtypeStruct((n_tok, RW_W), F32),
        grid=(n_tok // tm,),
        in_specs=[o_spec] * 2 + [tok] * 5 + [vec] * 3,
        out_specs=tok,
        compiler_params=pltpu.CompilerParams(
            dimension_semantics=("parallel",), vmem_limit_bytes=VMEM_LIMIT),
        name="rwkv_post",
    )(o_f, o_b, r, k, v, g_f, g_b, r_k.reshape(1, RW_W), lnx_g.reshape(1, RW_W), lnx_b.reshape(1, RW_W))


def rwkv_branch(p, pc, B, L, C, need_ctx, mu, w0, w2, a0, a2, g2, k_k, k_a, r_k, lnx_g, lnx_b):
    r, k, v, kk, bv, d_f, d_b, g_f, g_b = rwkv_prep(p, B, L, mu, w0, w2, a0, a2, g2, k_k, k_a)
    rc, kc, vc, kkc, bc, dc_f, dc_b, gc_f, gc_b = rwkv_prep(pc, B, C, mu, w0, w2, a0, a2, g2, k_k, k_a)

    def tmajor(xc, xl):
        t = jnp.concatenate([xc.reshape(B, C, RW_W), xl.reshape(B, L, RW_W)], axis=1)
        return jnp.moveaxis(t, 1, 0)

    pf, pb = rwkv_scan_pallas(tmajor(dc_f, d_f), tmajor(dc_b, d_b), tmajor(kkc, kk), tmajor(bc, bv),
                              tmajor(kc, k), tmajor(vc, v), tmajor(rc, r), C)

    def bmajor(t):
        t = t.reshape(C + L, RW_HEADS, B, RW_HD)
        return jnp.transpose(t, (2, 0, 1, 3)).reshape(B * (C + L), RW_W)

    pf, pb = bmajor(pf), bmajor(pb)
    y = rwkv_post(pf, pb, r, k, v, g_f, g_b, r_k, lnx_g, lnx_b, L, C + L, C).reshape(B, L, RW_W)
    yc = None
    if need_ctx:
        yc = rwkv_post(pf, pb, rc, kc, vc, gc_f, gc_b, r_k, lnx_g, lnx_b, C, C + L, 0).reshape(B, C, RW_W)
    return y, yc


def _diff_attn_kernel(q_ref, kt_ref, v_ref, lam_ref, gain_ref, o_ref):
    q = q_ref[0]
    v = v_ref[0]
    lam = lam_ref[...]
    tq = q.shape[0]
    head_of_lane = lax.broadcasted_iota(jnp.int32, (1, DF_W), 1) // DF_VD
    acc = jnp.zeros((tq, DF_W), F32)
    for h in range(DF_HEADS):
        parts = []
        for c in range(2):
            j = 2 * h + c
            s = jnp.dot(q[:, DF_HD * j:DF_HD * (j + 1)], kt_ref[0, j],
                        preferred_element_type=F32)
            p = jnp.exp2(s - jnp.max(s, -1, keepdims=True))
            l = jnp.sum(p, -1, keepdims=True)
            pv = jnp.dot(p.astype(BF16), v, preferred_element_type=F32)
            parts.append(pv / l)
        acc = jnp.where(head_of_lane == h, parts[0] - lam * parts[1], acc)
    row = lax.broadcasted_iota(jnp.int32, (DF_W, DF_W), 0)
    col = lax.broadcasted_iota(jnp.int32, (DF_W, DF_W), 1)
    seg_mean = jnp.where(row // DF_VD == col // DF_VD, 1.0 / DF_VD, 0.0).astype(F32)
    ms = jnp.dot(acc * acc, seg_mean, preferred_element_type=F32, precision=lax.Precision.HIGHEST)
    o_ref[0] = acc * lax.rsqrt(ms + 1e-5) * gain_ref[...]


def _diff_pre_kernel(rope, x_ref, cos_ref, sin_ref, q_ref, kt_ref, v_ref):
    x = x_ref[...].astype(F32)
    q, k, v = x[:, 0:DF_W], x[:, DF_W:2 * DF_W], x[:, 2 * DF_W:3 * DF_W]
    if rope:
        cos, sin = cos_ref[...], sin_ref[...]
        q = _rope(q, cos, sin, DF_HD)
        k = _rope(k, cos, sin, DF_HD)
    q_ref[...] = (q * (DF_HD ** -0.5 * math.log2(math.e))).astype(BF16)
    kt_ref[0] = k.T.astype(BF16)
    v_ref[...] = v.astype(BF16)


def diff_pre(p, B, L, tables, rope):
    tm = min(RW_PREP_TM, L)
    tiles_per_seq = L // tm
    n_tok = B * L
    width = 3 * DF_W
    col = P_DF // width
    cos, sin = tables
    tok = pl.BlockSpec((tm, DF_W), lambda i: (i, 0))
    tab = pl.BlockSpec((tm, DF_W), lambda i: (i % tiles_per_seq, 0))
    q, kt, v = pl.pallas_call(
        functools.partial(_diff_pre_kernel, rope),
        out_shape=[jax.ShapeDtypeStruct((n_tok, DF_W), BF16),
                   jax.ShapeDtypeStruct((B, DF_W, L), BF16),
                   jax.ShapeDtypeStruct((n_tok, DF_W), BF16)],
        grid=(n_tok // tm,),
        in_specs=[pl.BlockSpec((tm, width), lambda i: (i, col)), tab, tab],
        out_specs=[tok, pl.BlockSpec((1, DF_W, tm), lambda i: (i // tiles_per_seq, 0, i % tiles_per_seq)),
                   tok],
        compiler_params=pltpu.CompilerParams(
            dimension_semantics=("parallel",), vmem_limit_bytes=VMEM_LIMIT),
        name="diff_pre",
    )(p, cos, sin)
    return q.reshape(B, L, DF_W), kt, v.reshape(B, L, DF_W)


def diff_attend(qs, kt, vb, lam, lam_init, subln_g):
    B, Lq = qs.shape[:2]
    S = vb.shape[1]
    tq = min(DF_TQ, Lq)
    kt = kt.reshape(B, 2 * DF_HEADS, DF_HD, S)
    gain = jnp.tile(subln_g.astype(F32) * (1.0 - lam_init), DF_HEADS).reshape(1, DF_W)
    return pl.pallas_call(
        _diff_attn_kernel,
        out_shape=jax.ShapeDtypeStruct((B, Lq, DF_W), F32),
        grid=(B, Lq // tq),
        in_specs=[pl.BlockSpec((1, tq, DF_W), lambda b, i: (b, i, 0)),
                  pl.BlockSpec((1, 2 * DF_HEADS, DF_HD, S), lambda b, i: (b, 0, 0, 0)),
                  pl.BlockSpec((1, S, DF_W), lambda b, i: (b, 0, 0)),
                  pl.BlockSpec((1, 1), lambda b, i: (0, 0)),
                  pl.BlockSpec((1, DF_W), lambda b, i: (0, 0))],
        out_specs=pl.BlockSpec((1, tq, DF_W), lambda b, i: (b, i, 0)),
        compiler_params=pltpu.CompilerParams(
            dimension_semantics=("parallel", "parallel"), vmem_limit_bytes=VMEM_LIMIT),
        name="diff_attn",
    )(qs, kt, vb, lam.reshape(1, 1).astype(F32), gain)


def diff_branch(p, pc, B, L, C, need_ctx, tables, lq1, lk1, lq2, lk2, subln_g, lam_init):
    lam = (jnp.exp(jnp.sum(lq1.astype(F32) * lk1.astype(F32)))
           - jnp.exp(jnp.sum(lq2.astype(F32) * lk2.astype(F32))) + lam_init)
    q, kt, v = diff_pre(p, B, L, tables, True)
    qc, ktc, vc = diff_pre(pc, B, C, tables, False)
    y = diff_attend(q, jnp.concatenate([kt, ktc], axis=2), jnp.concatenate([v, vc], axis=1),
                    lam, lam_init, subln_g)
    yc = diff_attend(qc, ktc, vc, lam, lam_init, subln_g) if need_ctx else None
    return y, yc


def _ln_rows(x):
    mu = jnp.mean(x, -1, keepdims=True)
    xc = x - mu
    return xc * lax.rsqrt(jnp.mean(xc * xc, -1, keepdims=True) + LN_EPS)


def _ffn_kernel(tm, tiles_per_seq, n_chunks,
                h_ref, hprev_ref, hnext_ref, sh_ref, sc_ref, gate_ref,
                wa_ref, wb_ref, cwa_ref, cwb_ref, cba_ref, cbb_ref, wd_ref, lng_ref, lnb_ref,
                o_ref, u_ref, acc_ref):
    i = pl.program_id(0)
    j = pl.program_id(1)
    n = tm + 2 * HALO

    @pl.when(j == 0)
    def _():
        scale = 1.0 + sc_ref[0]
        shift = sh_ref[0]
        first = (i % tiles_per_seq) == 0
        last = (i % tiles_per_seq) == tiles_per_seq - 1
        u_ref[0:tm, :] = (_ln_rows(h_ref[...]) * scale + shift).astype(BF16)
        un = _ln_rows(hnext_ref[...]) * scale + shift
        up = _ln_rows(hprev_ref[...]) * scale + shift
        u_ref[tm:tm + HALO, :] = jnp.where(last, 0.0, un).astype(BF16)
        u_ref[tm + HALO:n, :] = jnp.where(first, 0.0, up).astype(BF16)
        acc_ref[...] = jnp.zeros_like(acc_ref)

    u = u_ref[...]

    def conv(w_ref, cw_ref, cb_ref):
        x = jnp.dot(u, w_ref[...], preferred_element_type=F32)
        cw = cw_ref[...]
        y = (pltpu.roll(x, 1, 0)[0:tm] * cw[0:1] + x[0:tm] * cw[1:2]
             + pltpu.roll(x, n - 1, 0)[0:tm] * cw[2:3] + cb_ref[...])
        return y

    a = conv(wa_ref, cwa_ref, cba_ref)
    b = conv(wb_ref, cwb_ref, cbb_ref)
    g = (a * jax.nn.sigmoid(a) * b).astype(BF16)
    acc_ref[...] += jnp.dot(g, wd_ref[...], preferred_element_type=F32)

    @pl.when(j == n_chunks - 1)
    def _():
        y = DN_ALPHA * h_ref[...] + gate_ref[0] * acc_ref[...]
        o_ref[...] = _ln_rows(y) * lng_ref[...] + lnb_ref[...]


def ffn_block(h, shift, scale, gate, w_up, conv_w, conv_b, w_down, ln_g, ln_b):
    B, L, D = h.shape
    tm = min(FFN_TM, L)
    tiles_per_seq = L // tm
    n_tok = B * L
    n_chunks = D_FF // FFN_CHUNK
    hb = tm // HALO

    def bcast(t):
        return jnp.broadcast_to(t.astype(F32), (B, 1, D))

    wu = w_up.astype(BF16)
    wd = w_down.astype(BF16)
    cb = conv_b.reshape(1, 2 * D_FF)
    h2 = h.reshape(n_tok, D)
    mod_spec = pl.BlockSpec((1, 1, D), lambda i, j: (i // tiles_per_seq, 0, 0))
    vec_spec = pl.BlockSpec((1, D), lambda i, j: (0, 0))
    out = pl.pallas_call(
        functools.partial(_ffn_kernel, tm, tiles_per_seq, n_chunks),
        out_shape=jax.ShapeDtypeStruct((n_tok, D), F32),
        grid=(n_tok // tm, n_chunks),
        in_specs=[pl.BlockSpec((tm, D), lambda i, j: (i, 0)),
                  pl.BlockSpec((HALO, D), lambda i, j: (jnp.maximum(i * hb - 1, 0), 0)),
                  pl.BlockSpec((HALO, D), lambda i, j: (jnp.minimum((i + 1) * hb, n_tok // HALO - 1), 0)),
                  mod_spec, mod_spec, mod_spec,
                  pl.BlockSpec((D, FFN_CHUNK), lambda i, j: (0, j)),
                  pl.BlockSpec((D, FFN_CHUNK), lambda i, j: (0, n_chunks + j)),
                  pl.BlockSpec((3, FFN_CHUNK), lambda i, j: (0, j)),
                  pl.BlockSpec((3, FFN_CHUNK), lambda i, j: (0, n_chunks + j)),
                  pl.BlockSpec((1, FFN_CHUNK), lambda i, j: (0, j)),
                  pl.BlockSpec((1, FFN_CHUNK), lambda i, j: (0, n_chunks + j)),
                  pl.BlockSpec((FFN_CHUNK, D), lambda i, j: (j, 0)),
                  vec_spec, vec_spec],
        out_specs=pl.BlockSpec((tm, D), lambda i, j: (i, 0)),
        scratch_shapes=[pltpu.VMEM((tm + 2 * HALO, D), BF16), pltpu.VMEM((tm, D), F32)],
        compiler_params=pltpu.CompilerParams(
            dimension_semantics=("parallel", "arbitrary"), vmem_limit_bytes=VMEM_LIMIT),
        name="conv_ffn",
    )(h2, h2, h2, bcast(shift), bcast(scale), bcast(gate), wu, wu, conv_w, conv_w, cb, cb, wd,
      ln_g.reshape(1, D), ln_b.reshape(1, D))
    return out.reshape(B, L, D)


P_GATE, P_HY, P_RW, P_SW, P_DF = 0, 4096, 5120, 6400, 6912
P_COLS = 7680
IN_CHUNK = 512


def _in_proj_kernel(x_ref, sh_ref, sc_ref, w_ref, o_ref, u_ref):
    @pl.when(pl.program_id(1) == 0)
    def _():
        u_ref[...] = (_ln_rows(x_ref[...]) * (1.0 + sc_ref[0]) + sh_ref[0]).astype(BF16)

    o_ref[...] = jnp.dot(u_ref[...], w_ref[...], preferred_element_type=F32).astype(BF16)


def in_proj(h, shift, scale, w_in_p):
    B, L, D = h.shape
    tm = min(IN_TM, L)
    tiles_per_seq = L // tm
    n_tok = B * L

    def bcast(t):
        return jnp.broadcast_to(t.astype(F32), (B, 1, D))

    mod_spec = pl.BlockSpec((1, 1, D), lambda i, j: (i // tiles_per_seq, 0, 0))
    return pl.pallas_call(
        _in_proj_kernel,
        out_shape=jax.ShapeDtypeStruct((n_tok, P_COLS), BF16),
        grid=(n_tok // tm, P_COLS // IN_CHUNK),
        in_specs=[pl.BlockSpec((tm, D), lambda i, j: (i, 0)), mod_spec, mod_spec,
                  pl.BlockSpec((D, IN_CHUNK), lambda i, j: (0, j))],
        out_specs=pl.BlockSpec((tm, IN_CHUNK), lambda i, j: (i, j)),
        scratch_shapes=[pltpu.VMEM((tm, D), BF16)],
        compiler_params=pltpu.CompilerParams(
            dimension_semantics=("parallel", "arbitrary"), vmem_limit_bytes=VMEM_LIMIT),
        name="in_proj",
    )(h.reshape(n_tok, D), bcast(shift), bcast(scale), w_in_p)


def pack_w_in(w):
    offs = np.cumsum((0,) + IN_SIZES)
    hy, sw, rw, df, gt = [w[:, offs[j]:offs[j + 1]] for j in range(len(IN_SIZES))]

    def padto(t, n):
        return jnp.pad(t, ((0, 0), (0, n - t.shape[1])))

    return jnp.concatenate([gt, padto(hy, P_RW - P_HY), padto(rw, P_SW - P_RW), sw, df],
                           axis=1).astype(BF16)


def _merge_kernel(yh_ref, ys_ref, yr_ref, yd_ref, g0_ref, g1_ref, g2_ref, g3_ref, h_ref, gate_ref,
                  wbr_ref, wo_ref, lng_ref, lnb_ref, o_ref):
    acc = None
    for j, (y_ref, g_ref) in enumerate(((yh_ref, g0_ref), (ys_ref, g1_ref), (yr_ref, g2_ref),
                                        (yd_ref, g3_ref))):
        term = jax.nn.sigmoid(g_ref[...].astype(F32)) * jnp.dot(y_ref[...].astype(BF16), wbr_ref[j],
                                                    preferred_element_type=F32)
        acc = term if acc is None else acc + term
    mix = jnp.dot(acc.astype(BF16), wo_ref[...], preferred_element_type=F32)
    y = DN_ALPHA * h_ref[...] + gate_ref[0] * mix
    o_ref[...] = _ln_rows(y) * lng_ref[...] + lnb_ref[...]


def merge_block(ys, p, h, gate, w_br, w_o, ln_g, ln_b):
    B, L, D = h.shape
    n_tok = B * L
    tm = min(MERGE_TM, L)
    tiles_per_seq = L // tm
    y_spec = pl.BlockSpec((tm, BR_W), lambda i: (i, 0))
    vec_spec = pl.BlockSpec((1, D), lambda i: (0, 0))
    g_specs = [pl.BlockSpec((tm, D), functools.partial(lambda j, i: (i, j), j)) for j in range(N_BRANCH)]
    out = pl.pallas_call(
        _merge_kernel,
        out_shape=jax.ShapeDtypeStruct((n_tok, D), F32),
        grid=(n_tok // tm,),
        in_specs=[y_spec] * N_BRANCH + g_specs + [
            pl.BlockSpec((tm, D), lambda i: (i, 0)),
            pl.BlockSpec((1, 1, D), lambda i: (i // tiles_per_seq, 0, 0)),
            pl.BlockSpec((N_BRANCH, BR_W, D), lambda i: (0, 0, 0)),
            pl.BlockSpec((D, D), lambda i: (0, 0)), vec_spec, vec_spec],
        out_specs=pl.BlockSpec((tm, D), lambda i: (i, 0)),
        compiler_params=pltpu.CompilerParams(
            dimension_semantics=("parallel",), vmem_limit_bytes=VMEM_LIMIT),
        name="merge",
    )(*[y.reshape(n_tok, BR_W) for y in ys], p, p, p, p, h.reshape(n_tok, D),
      jnp.broadcast_to(gate.astype(F32), (B, 1, D)), w_br.astype(BF16), w_o.astype(BF16),
      ln_g.reshape(1, D), ln_b.reshape(1, D))
    return out.reshape(B, L, D)


def kernel(x, c, ctx, c_ctx, ada_w, ada_b, w_in, hy_conv_w, hy_conv_b, hy_f_w1, hy_f_b1,
           hy_f_w2, hy_f_b2, hy_f_w3, hy_f_freq, hy_bias, swa_sink, rwkv_mu, rwkv_w0, rwkv_w2,
           rwkv_a0, rwkv_a2, rwkv_g2, rwkv_kk, rwkv_ka, rwkv_rk, rwkv_lnx_g, rwkv_lnx_b,
           diff_lq1, diff_lk1, diff_lq2, diff_lk2, diff_subln_g, w_branch, w_out, ln1_g, ln1_b,
           ffn_w_up, ffn_conv_w, ffn_conv_b, ffn_w_down, ln2_g, ln2_b):
    h, hc = x, ctx
    mats, mats_c = dft_mats(x.shape[1]), dft_mats(ctx.shape[1])
    swa_tables = rope_tables(x.shape[1], SWA_HD, SWA_HEADS * SWA_HD)
    df_tables = rope_tables(x.shape[1], DF_HD, DF_W)
    s_lat = jax.nn.silu(c)
    s_ctx = jax.nn.silu(c_ctx)
    for i in range(DEPTH):
        need_ctx = i < DEPTH - 1
        mod = (s_lat @ ada_w[i] + ada_b[i])[:, None, :]
        mod_c = s_ctx @ ada_w[i] + ada_b[i]
        sh1, sc1, g1, sh2, sc2, g2 = jnp.split(mod, 6, -1)
        csh1, csc1, cg1, csh2, csc2, cg2 = jnp.split(mod_c, 6, -1)

        B, L, D = h.shape
        C = hc.shape[1]
        w_in_p = pack_w_in(w_in[i])
        p = in_proj(h, sh1, sc1, w_in_p)
        pc = in_proj(hc, csh1.reshape(1, 1, D), csc1.reshape(1, 1, D), w_in_p)

        y_hy, yc_hy = hyena_branch(p, pc, B, L, C, need_ctx, mats, mats_c, hy_conv_w[i], hy_conv_b[i],
                                   hy_f_w1[i], hy_f_b1[i], hy_f_w2[i], hy_f_b2[i], hy_f_w3[i],
                                   hy_f_freq[i], hy_bias[i])
        y_sw, yc_sw = swa_branch(p, pc, B, L, C, need_ctx, swa_sink[i], swa_tables)
        y_rw, yc_rw = rwkv_branch(p, pc, B, L, C, need_ctx, rwkv_mu[i], rwkv_w0[i], rwkv_w2[i],
                                  rwkv_a0[i], rwkv_a2[i], rwkv_g2[i], rwkv_kk[i], rwkv_ka[i],
                                  rwkv_rk[i], rwkv_lnx_g[i], rwkv_lnx_b[i])
        lam_init = 0.8 - 0.6 * math.exp(-0.3 * i)
        y_df, yc_df = diff_branch(p, pc, B, L, C, need_ctx, df_tables, diff_lq1[i], diff_lk1[i],
                                  diff_lq2[i], diff_lk2[i], diff_subln_g[i], lam_init)

        h = merge_block((y_hy, y_sw, y_rw, y_df), p, h, g1, w_branch[i], w_out[i], ln1_g[i], ln1_b[i])
        h = ffn_block(h, sh2, sc2, g2, ffn_w_up[i], ffn_conv_w[i], ffn_conv_b[i], ffn_w_down[i],
                      ln2_g[i], ln2_b[i])

        if need_ctx:
            hc = merge_block((yc_hy, yc_sw, yc_rw, yc_df), pc, hc, cg1.reshape(1, 1, D), w_branch[i],
                             w_out[i], ln1_g[i], ln1_b[i])
            hc = ffn_block(hc, csh2.reshape(1, 1, -1), csc2.reshape(1, 1, -1), cg2.reshape(1, 1, -1),
                           ffn_w_up[i], ffn_conv_w[i], ffn_conv_b[i], ffn_w_down[i], ln2_g[i], ln2_b[i])
    return h
```

```python
import functools
import math

import jax
import jax.numpy as jnp
import numpy as np
from jax import lax
from jax.experimental import pallas as pl
from jax.experimental.pallas import tpu as pltpu

D_MODEL = 1024
DEPTH = 2
GRID_W = 64
ROPE_BASE = 10000.0
F32 = jnp.float32
BF16 = jnp.bfloat16
NEG_INF = -1e30
LN_EPS = 1e-6

HY_W = 256
HY_ORDER = 2
HY_DIRS = 2
HY_EMB = 33
HY_MIN_DECAY = math.log(1e-2) / 1.5
HY_MAX_DECAY = math.log(1e-2) / 0.3

SWA_HEADS = 4
SWA_KV = 2
SWA_HD = 64
SWA_WIN = 128
SWA_BLOCK = 128

RW_HEADS = 4
RW_HD = 64
RW_W = RW_HEADS * RW_HD
RW_DECAY_R = 64
RW_AAA_R = 64
RW_GATE_R = 128
RW_GN_EPS = 64e-5

DF_HEADS = 4
DF_HD = 32
DF_VD = 2 * DF_HD
DF_W = DF_HEADS * DF_VD

N_BRANCH = 4
BR_W = 256
D_FF = 2816
DN_ALPHA = (2 * DEPTH) ** 0.25

SWA_SIZES = (SWA_HEADS * SWA_HD, SWA_KV * SWA_HD, SWA_KV * SWA_HD)
RW_SIZES = (RW_W, RW_W, RW_W, RW_DECAY_R, RW_DECAY_R, RW_AAA_R, RW_GATE_R, RW_GATE_R)
DF_SIZES = (DF_W, DF_W, DF_W)
IN_SIZES = (3 * HY_W, sum(SWA_SIZES), sum(RW_SIZES), sum(DF_SIZES), N_BRANCH * D_MODEL)

V7X_VMEM_BYTES = 64 * 1024 * 1024
LANES = 128
MM_TM = 512
MM_TN = 512
RW_TBLK = 64
DF_TQ = 256
SWA_QROWS = 256
HALO = 8
P_HALO = 16
IN_TM = 2048
FFN_TM = 1024
FFN_CHUNK = 256
MERGE_TM = 512
HY_TK = 512
RW_PREP_TM = 512
RW_POST_TM = 256
VMEM_LIMIT = V7X_VMEM_BYTES * 3 // 4


def _mm_kernel(x_ref, w_ref, o_ref):
    o_ref[...] = jnp.dot(x_ref[...].astype(BF16), w_ref[...], preferred_element_type=F32)


def _mm(x, w):
    m, k = x.shape
    n = w.shape[1]
    n_pad = -(-n // LANES) * LANES
    wb = w.astype(BF16)
    if n_pad != n:
        wb = jnp.pad(wb, ((0, 0), (0, n_pad - n)))
    tn = MM_TN if n_pad % MM_TN == 0 else (256 if n_pad % 256 == 0 else LANES)
    tm = MM_TM if m % MM_TM == 0 else m
    out = pl.pallas_call(
        _mm_kernel,
        out_shape=jax.ShapeDtypeStruct((m, n_pad), F32),
        grid=(m // tm, n_pad // tn),
        in_specs=[pl.BlockSpec((tm, k), lambda i, j: (i, 0)),
                  pl.BlockSpec((k, tn), lambda i, j: (0, j))],
        out_specs=pl.BlockSpec((tm, tn), lambda i, j: (i, j)),
        compiler_params=pltpu.CompilerParams(
            dimension_semantics=("parallel", "parallel"), vmem_limit_bytes=VMEM_LIMIT),
        name="matmul",
    )(x, wb)
    return out[:, :n] if n_pad != n else out


def dft_mats(L):
    tb = 64
    k = jnp.arange(L, dtype=jnp.int32)[:, None]

    def table(t):
        ang = (((2 * k + 1) * t[None, :]) % (4 * L)).astype(F32) * (math.pi / (2 * L))
        return jnp.cos(ang), jnp.sin(ang)

    ca, sa = table(tb * jnp.arange(L // tb, dtype=jnp.int32))
    cb, sb = table(jnp.arange(tb, dtype=jnp.int32))
    c = (ca[:, :, None] * cb[:, None, :] - sa[:, :, None] * sb[:, None, :]).reshape(L, L)
    s = (sa[:, :, None] * cb[:, None, :] + ca[:, :, None] * sb[:, None, :]).reshape(L, L)
    ct = (ca.T[:, None, :] * cb.T[None, :, :] - sa.T[:, None, :] * sb.T[None, :, :]).reshape(L, L)
    st = (sa.T[:, None, :] * cb.T[None, :, :] + ca.T[:, None, :] * sb.T[None, :, :]).reshape(L, L)
    return c.astype(BF16), s.astype(BF16), ct.astype(BF16), st.astype(BF16)


def hyena_kspec(L, mats, w1, b1, w2, b2, w3, freq):
    t = jnp.linspace(0.0, 1.0, L, dtype=F32)[:, None]
    bands = (HY_EMB - 1) // 2
    w = 2.0 * math.pi * jnp.arange(L, dtype=F32) / L
    fb = jnp.linspace(1e-4, bands - 1, bands, dtype=F32)
    ang = w[:, None] * fb[None, :]
    z = jnp.concatenate([t, jnp.cos(ang), -jnp.sin(ang)], -1)
    fr = freq.astype(F32)
    hdn = jnp.sin(fr * (z @ w1.astype(F32) + b1.astype(F32)))
    hdn = jnp.sin(fr * (hdn @ w2.astype(F32) + b2.astype(F32)))
    filt = (hdn @ w3.astype(F32)).reshape(L, HY_ORDER, HY_DIRS, HY_W)
    deltas = jnp.abs(jnp.linspace(HY_MIN_DECAY, HY_MAX_DECAY, HY_W, dtype=F32))
    filt = filt * jnp.exp(-t * deltas[None, :])[:, None, None, :]
    filt = filt / jnp.sum(jnp.abs(filt), axis=(0, 2), keepdims=True)
    fwd, bwd = filt[:, :, 0], filt[:, :, 1]
    nw = HY_ORDER * HY_W
    bwd = bwd.at[0].set(0.0)
    kr = _mm(mats[0], (fwd + bwd).reshape(L, nw))
    ki = -_mm(mats[1], (fwd - bwd).reshape(L, nw))

    def per_order(a):
        return jnp.transpose(a.reshape(L, HY_ORDER, HY_W), (1, 0, 2))

    return per_order(kr), per_order(ki)


def _hy_fwd_kernel(ac_ref, as_ref, z_ref, kr_ref, ki_ref, yr_ref, yi_ref):
    z = z_ref[0].astype(BF16)
    zr = jnp.dot(ac_ref[...], z, preferred_element_type=F32)
    zi = -jnp.dot(as_ref[...], z, preferred_element_type=F32)
    kr, ki = kr_ref[...], ki_ref[...]
    yr_ref[0] = (zr * kr - zi * ki).astype(BF16)
    yi_ref[0] = (zr * ki + zi * kr).astype(BF16)


def _hy_inv_kernel(inv_l, act_ref, ast_ref, yr_ref, yi_ref, z_ref, x_ref, bias_ref, o_ref, *o16_ref):
    y = (jnp.dot(act_ref[...], yr_ref[0], preferred_element_type=F32)
         - jnp.dot(ast_ref[...], yi_ref[0], preferred_element_type=F32))
    o = x_ref[0] * (y * inv_l + bias_ref[...] * z_ref[0])
    o_ref[0] = o
    for ref in o16_ref:
        ref[0] = o.astype(BF16)


def hyena_longconv(z, z16, xmul, mats, kr, ki, bias, want16):
    B, L, W = z.shape
    ac, as_, act, ast = mats
    tk = min(HY_TK, L)
    a_spec = pl.BlockSpec((tk, L), lambda i, b: (i, 0))
    full_spec = pl.BlockSpec((1, L, W), lambda i, b: (b, 0, 0))
    tile_spec = pl.BlockSpec((1, tk, W), lambda i, b: (b, i, 0))
    k_spec = pl.BlockSpec((tk, W), lambda i, b: (i, 0))
    params = pltpu.CompilerParams(dimension_semantics=("parallel", "arbitrary"),
                                  vmem_limit_bytes=VMEM_LIMIT)
    yr, yi = pl.pallas_call(
        _hy_fwd_kernel,
        out_shape=[jax.ShapeDtypeStruct((B, L, W), BF16)] * 2,
        grid=(L // tk, B),
        in_specs=[a_spec, a_spec, full_spec, k_spec, k_spec],
        out_specs=[tile_spec, tile_spec],
        compiler_params=params,
        name="hyena_fwd",
    )(ac, as_, z16, kr, ki)
    n_out = 2 if want16 else 1
    return pl.pallas_call(
        functools.partial(_hy_inv_kernel, 1.0 / L),
        out_shape=[jax.ShapeDtypeStruct((B, L, W), F32), jax.ShapeDtypeStruct((B, L, W), BF16)][:n_out],
        grid=(L // tk, B),
        in_specs=[a_spec, a_spec, full_spec, full_spec, tile_spec, tile_spec,
                  pl.BlockSpec((1, W), lambda i, b: (0, 0))],
        out_specs=[tile_spec] * n_out,
        compiler_params=params,
        name="hyena_inv",
    )(act, ast, yr, yi, z, xmul, bias.reshape(1, W))


def _hy_pre_kernel(tm, tiles_per_seq, x_ref, xp_ref, xn_ref, cw_ref, cb_ref, v_ref, v16_ref, x1_ref,
                   x2_ref):
    i = pl.program_id(0)
    first = (i % tiles_per_seq) == 0
    last = (i % tiles_per_seq) == tiles_per_seq - 1
    x = x_ref[...].astype(F32)
    row = lax.broadcasted_iota(jnp.int32, x.shape, 0)
    prev_row = jnp.where(first, 0.0, xp_ref[...].astype(F32)[P_HALO - 1:P_HALO, :])
    next_row = jnp.where(last, 0.0, xn_ref[...].astype(F32)[0:1, :])
    xm1 = jnp.where(row == 0, prev_row, pltpu.roll(x, 1, 0))
    xp1 = jnp.where(row == tm - 1, next_row, pltpu.roll(x, tm - 1, 0))
    cw = cw_ref[...]
    y = xm1 * cw[0:1] + x * cw[1:2] + xp1 * cw[2:3] + cb_ref[...]
    v_ref[...] = y[:, 0:HY_W]
    v16_ref[...] = y[:, 0:HY_W].astype(BF16)
    x1_ref[...] = y[:, HY_W:2 * HY_W]
    x2_ref[...] = y[:, 2 * HY_W:3 * HY_W]


def hyena_pre(p, B, L, conv_w, conv_b):
    tm = min(RW_PREP_TM, L)
    tiles_per_seq = L // tm
    n_tok = B * L
    hb = tm // P_HALO
    width = P_RW - P_HY
    col = P_HY // width
    cw = jnp.pad(conv_w, ((0, 0), (0, width - conv_w.shape[1])))
    cb = jnp.pad(conv_b, (0, width - conv_b.shape[0])).reshape(1, width)
    out_spec = pl.BlockSpec((tm, HY_W), lambda i: (i, 0))
    outs = pl.pallas_call(
        functools.partial(_hy_pre_kernel, tm, tiles_per_seq),
        out_shape=[jax.ShapeDtypeStruct((n_tok, HY_W), dt) for dt in (F32, BF16, F32, F32)],
        grid=(n_tok // tm,),
        in_specs=[pl.BlockSpec((tm, width), lambda i: (i, col)),
                  pl.BlockSpec((P_HALO, width), lambda i: (jnp.maximum(i * hb - 1, 0), col)),
                  pl.BlockSpec((P_HALO, width), lambda i: (jnp.minimum((i + 1) * hb, n_tok // P_HALO - 1), col)),
                  pl.BlockSpec((3, width), lambda i: (0, 0)),
                  pl.BlockSpec((1, width), lambda i: (0, 0))],
        out_specs=[out_spec] * 4,
        compiler_params=pltpu.CompilerParams(
            dimension_semantics=("parallel",), vmem_limit_bytes=VMEM_LIMIT),
        name="hyena_pre",
    )(p, p, p, cw, cb)
    return [o.reshape(B, L, HY_W) for o in outs]


def hyena_mix(p, B, L, conv_w, conv_b, mats, kspec, bias):
    v, v16, x1, x2 = hyena_pre(p, B, L, conv_w, conv_b)
    kr, ki = kspec
    zz, zz16 = hyena_longconv(v, v16, x1, mats, kr[0], ki[0], bias[0], True)
    return hyena_longconv(zz, zz16, x2, mats, kr[1], ki[1], bias[1], False)[0]


def hyena_branch(p, pc, B, L, C, need_ctx, mats, mats_c, conv_w, conv_b, fw1, fb1, fw2, fb2, fw3, ffreq,
                 bias):
    kspec = hyena_kspec(L, mats, fw1, fb1, fw2, fb2, fw3, ffreq)
    y = hyena_mix(p, B, L, conv_w, conv_b, mats, kspec, bias)
    yc = None
    if need_ctx:
        kspec_c = hyena_kspec(C, mats_c, fw1, fb1, fw2, fb2, fw3, ffreq)
        yc = hyena_mix(pc, B, C, conv_w, conv_b, mats_c, kspec_c, bias)
    return y, yc


def rope_tables(L, d, width):
    nf = d // 4
    pos = jnp.arange(L)
    inv = ROPE_BASE ** (-jnp.arange(nf, dtype=F32) / nf)
    cs, sn = [], []
    for p in (pos // GRID_W, pos % GRID_W):
        ang = p.astype(F32)[:, None] * inv[None, :]
        cs += [jnp.cos(ang), jnp.cos(ang)]
        sn += [-jnp.sin(ang), jnp.sin(ang)]
    reps = width // d
    return jnp.tile(jnp.concatenate(cs, -1), (1, reps)), jnp.tile(jnp.concatenate(sn, -1), (1, reps))


def _rope(x, cos, sin, d):
    width = x.shape[-1]
    q = d // 4
    lane = lax.broadcasted_iota(jnp.int32, x.shape, 1)
    swapped = jnp.where(lane % (2 * q) < q, pltpu.roll(x, width - q, 1), pltpu.roll(x, q, 1))
    return x * cos + swapped * sin


def _swa_kernel(local, seq_len, qrows, q_ref, *refs):
    nk = qrows // SWA_BLOCK + 2
    k_refs, v_refs = refs[0:nk], refs[nk:2 * nk]
    ck_ref, cv_ref, cos_ref, sin_ref, sink_ref, o_ref = refs[2 * nk:]
    n = pl.program_id(1)
    blk = SWA_BLOCK
    qscale = SWA_HD ** -0.5 * math.log2(math.e)
    q = q_ref[...].astype(F32)
    if local:
        q0 = pl.multiple_of(n * qrows, qrows)
        q = _rope(q, cos_ref[pl.ds(q0, qrows), :], sin_ref[pl.ds(q0, qrows), :], SWA_HD)
        first = n * (qrows // blk) - 1
        ks, vs = [], []
        for j, (k_ref, v_ref) in enumerate(zip(k_refs, v_refs)):
            k0 = pl.multiple_of(jnp.clip(first + j, 0, seq_len // blk - 1) * blk, blk)
            ks.append(_rope(k_ref[...].astype(F32), cos_ref[pl.ds(k0, blk), 0:SWA_KV * SWA_HD],
                            sin_ref[pl.ds(k0, blk), 0:SWA_KV * SWA_HD], SWA_HD).astype(BF16))
            vs.append(v_ref[...].astype(BF16))
        k_loc = jnp.concatenate(ks, axis=0)
        v_loc = jnp.concatenate(vs, axis=0)
        qpos = n * qrows + lax.broadcasted_iota(jnp.int32, (qrows, nk * blk), 0)
        kpos = first * blk + lax.broadcasted_iota(jnp.int32, (qrows, nk * blk), 1)
        valid = (jnp.abs(kpos - qpos) <= SWA_WIN) & (kpos >= 0) & (kpos < seq_len)
    qb = (q * qscale).astype(BF16)
    ck = ck_ref[...].astype(BF16)
    cv = cv_ref[...].astype(BF16)
    nt = (((1,), (1,)), ((), ()))
    outs = []
    for h in range(SWA_HEADS):
        g = h // (SWA_HEADS // SWA_KV)
        qh = qb[:, h * SWA_HD:(h + 1) * SWA_HD]
        gs = slice(g * SWA_HD, (g + 1) * SWA_HD)
        sink = sink_ref[h:h + 1, 0:1]
        s_ctx = lax.dot_general(qh, ck[:, gs], nt, preferred_element_type=F32)
        m = jnp.maximum(jnp.max(s_ctx, -1, keepdims=True), sink)
        if local:
            s_loc = lax.dot_general(qh, k_loc[:, gs], nt, preferred_element_type=F32)
            s_loc = jnp.where(valid, s_loc, NEG_INF)
            m = jnp.maximum(m, jnp.max(s_loc, -1, keepdims=True))
        p_ctx = jnp.exp2(s_ctx - m)
        den = jnp.sum(p_ctx, -1, keepdims=True) + jnp.exp2(sink - m)
        acc = jnp.dot(p_ctx.astype(BF16), cv[:, gs], preferred_element_type=F32)
        if local:
            p_loc = jnp.exp2(s_loc - m)
            den = den + jnp.sum(p_loc, -1, keepdims=True)
            acc = acc + jnp.dot(p_loc.astype(BF16), v_loc[:, gs], preferred_element_type=F32)
        outs.append(acc / den)
    o_ref[...] = jnp.concatenate(outs, axis=-1)


def swa_attend(p, pc, B, L, C, sink, tables, local):
    blk = SWA_BLOCK
    nb = L // blk
    qrows = min(SWA_QROWS, L)
    nq = L // qrows
    qpb = qrows // blk
    qcol = P_SW // (SWA_HEADS * SWA_HD)
    kcol = (P_SW + SWA_HEADS * SWA_HD) // (SWA_KV * SWA_HD)
    kvw = SWA_KV * SWA_HD

    def nbr(j, col):
        return lambda b, n: (b * nb + jnp.clip(n * qpb + j, 0, nb - 1), col)

    nbrs = range(-1, qpb + 1)

    cos, sin = tables
    sink_rows = jnp.broadcast_to((sink.astype(F32) * math.log2(math.e))[:, None], (SWA_HEADS, LANES))
    tab_spec = pl.BlockSpec(cos.shape, lambda b, n: (0, 0))
    out = pl.pallas_call(
        functools.partial(_swa_kernel, local, L, qrows),
        out_shape=jax.ShapeDtypeStruct((B * L, SWA_HEADS * SWA_HD), F32),
        grid=(B, nq),
        in_specs=[pl.BlockSpec((qrows, SWA_HEADS * SWA_HD), lambda b, n: (b * nq + n, qcol))]
        + [pl.BlockSpec((blk, kvw), nbr(j, kcol)) for j in nbrs]
        + [pl.BlockSpec((blk, kvw), nbr(j, kcol + 1)) for j in nbrs]
        + [pl.BlockSpec((C, kvw), lambda b, n: (b, kcol)),
           pl.BlockSpec((C, kvw), lambda b, n: (b, kcol + 1)),
           tab_spec, tab_spec,
           pl.BlockSpec((SWA_HEADS, LANES), lambda b, n: (0, 0))],
        out_specs=pl.BlockSpec((qrows, SWA_HEADS * SWA_HD), lambda b, n: (b * nq + n, 0)),
        compiler_params=pltpu.CompilerParams(
            dimension_semantics=("parallel", "parallel"), vmem_limit_bytes=VMEM_LIMIT),
        name="swa",
    )(p, *([p] * (2 * len(nbrs))), pc, pc, cos, sin, sink_rows)
    return out.reshape(B, L, SWA_HEADS * SWA_HD)


def swa_branch(p, pc, B, L, C, need_ctx, sink, tables):
    y = swa_attend(p, pc, B, L, C, sink, tables, True)
    yc = swa_attend(pc, pc, B, C, C, sink, tables, False) if need_ctx else None
    return y, yc


def _head_blocks(value):
    row = lax.broadcasted_iota(jnp.int32, (RW_W, RW_W), 0)
    col = lax.broadcasted_iota(jnp.int32, (RW_W, RW_W), 1)
    return jnp.where(row // RW_HD == col // RW_HD, value, 0.0).astype(F32)


def _head_sum(x, blocks):
    return jnp.dot(x, blocks, preferred_element_type=F32, precision=lax.Precision.HIGHEST)


RW_OFF = tuple(int(o) for o in np.cumsum((0,) + RW_SIZES))
RW_PAD = 1280
RW_WIN_WD = RW_OFF[3]
RW_WIN_LO = RW_OFF[5]
RW_WIN_HI = RW_WIN_LO + LANES


def _rwkv_prep_kernel(tm, tiles_per_seq, x_ref, xp_ref, xn_ref, mu_ref, w0_ref, w2f_ref, w2b_ref,
                      a0_ref, a2_ref, g2f_ref, g2b_ref, kk_ref, ka_ref,
                      r_o, k_o, v_o, kkn_o, b_o, df_o, db_o, gf_o, gb_o):
    i = pl.program_id(0)
    first = (i % tiles_per_seq) == 0
    last = (i % tiles_per_seq) == tiles_per_seq - 1
    x = x_ref[...].astype(F32)
    row = lax.broadcasted_iota(jnp.int32, x.shape, 0)
    prev_row = jnp.where(first, 0.0, xp_ref[...].astype(F32)[P_HALO - 1:P_HALO, :])
    next_row = jnp.where(last, 0.0, xn_ref[...].astype(F32)[0:1, :])
    xm1 = jnp.where(row == 0, prev_row, pltpu.roll(x, 1, 0))
    xp1 = jnp.where(row == tm - 1, next_row, pltpu.roll(x, tm - 1, 0))
    x = x + (0.5 * (xm1 + xp1) - x) * mu_ref[...]

    r = x[:, RW_OFF[0]:RW_OFF[1]]
    k = x[:, RW_OFF[1]:RW_OFF[2]]
    v = x[:, RW_OFF[2]:RW_OFF[3]]
    wd = jnp.tanh(x[:, RW_WIN_WD:RW_WIN_WD + LANES]).astype(BF16)
    lo = x[:, RW_WIN_LO:RW_WIN_LO + RW_W]
    hi = x[:, RW_WIN_HI:RW_WIN_HI + RW_W]

    def decay(w2_ref, w0):
        wlog = -jax.nn.softplus(-(w0 + jnp.dot(wd, w2_ref[...], preferred_element_type=F32))) - 0.5
        return jnp.exp(-jnp.exp(wlog))

    a = jax.nn.sigmoid(a0_ref[...] + jnp.dot(lo[:, 0:LANES].astype(BF16), a2_ref[...],
                                             preferred_element_type=F32))
    kk = k * kk_ref[...]
    norm = jnp.sqrt(_head_sum(kk * kk, _head_blocks(1.0)))
    kk = kk / jnp.maximum(norm, 1e-12)
    r_o[...] = r
    k_o[...] = k * (1.0 + (a - 1.0) * ka_ref[...])
    v_o[...] = v
    kkn_o[...] = kk
    b_o[...] = kk * a
    df_o[...] = decay(w2f_ref, w0_ref[0:1, :])
    db_o[...] = decay(w2b_ref, w0_ref[1:2, :])
    gf_o[...] = jnp.dot(jax.nn.sigmoid(lo).astype(BF16), g2f_ref[...], preferred_element_type=F32)
    gb_o[...] = jnp.dot(jax.nn.sigmoid(hi).astype(BF16), g2b_ref[...], preferred_element_type=F32)


def rwkv_prep(p, B, L, mu, w0, w2, a0, a2, g2, k_k, k_a):
    tm = min(RW_PREP_TM, L)
    tiles_per_seq = L // tm
    n_tok = B * L
    hb = tm // P_HALO
    col = P_RW // RW_PAD

    def rows(n, r0, src):
        return jnp.zeros((n, RW_W), F32).at[r0:r0 + src.shape[0]].set(src).astype(BF16)

    w2f = rows(LANES, RW_OFF[3] - RW_WIN_WD, w2[0])
    w2b = rows(LANES, RW_OFF[4] - RW_WIN_WD, w2[1])
    a2p = rows(LANES, RW_OFF[5] - RW_WIN_LO, a2)
    g2f = rows(RW_W, RW_OFF[6] - RW_WIN_LO, g2[0])
    g2b = rows(RW_W, RW_OFF[7] - RW_WIN_HI, g2[1])
    mu_p = jnp.pad(mu, (0, RW_PAD - mu.shape[0])).reshape(1, RW_PAD)

    def const(shape):
        return pl.BlockSpec(shape, lambda i: (0,) * len(shape))

    vec = const((1, RW_W))
    out_spec = pl.BlockSpec((tm, RW_W), lambda i: (i, 0))
    return pl.pallas_call(
        functools.partial(_rwkv_prep_kernel, tm, tiles_per_seq),
        out_shape=[jax.ShapeDtypeStruct((n_tok, RW_W), F32)] * 9,
        grid=(n_tok // tm,),
        in_specs=[pl.BlockSpec((tm, RW_PAD), lambda i: (i, col)),
                  pl.BlockSpec((P_HALO, RW_PAD), lambda i: (jnp.maximum(i * hb - 1, 0), col)),
                  pl.BlockSpec((P_HALO, RW_PAD), lambda i: (jnp.minimum((i + 1) * hb, n_tok // P_HALO - 1), col)),
                  const((1, RW_PAD)), const((2, RW_W)), const((LANES, RW_W)), const((LANES, RW_W)),
                  vec, const((LANES, RW_W)), const((RW_W, RW_W)), const((RW_W, RW_W)), vec, vec],
        out_specs=[out_spec] * 9,
        compiler_params=pltpu.CompilerParams(
            dimension_semantics=("parallel",), vmem_limit_bytes=VMEM_LIMIT),
        name="rwkv_prep",
    )(p, p, p, mu_p, w0, w2f, w2b, a0.reshape(1, RW_W), a2p, g2f, g2b,
      k_k.reshape(1, RW_W), k_a.reshape(1, RW_W))


def _rwkv_scan_kernel(tblk, nb,
                      wf_ref, kkf_ref, bf_ref, kf_ref, vf_ref, rf_ref,
                      wb_ref, kkb_ref, bb_ref, kb_ref, vb_ref, rb_ref,
                      of_ref, ob_ref, sf_ref, sb_ref, sf16_ref, sb16_ref):
    c = pl.program_id(0)

    @pl.when(c == 0)
    def _():
        for ref in (sf_ref, sb_ref, sf16_ref, sb16_ref):
            ref[...] = jnp.zeros_like(ref)

    row = lax.broadcasted_iota(jnp.int32, (RW_W, RW_W), 0)
    col = lax.broadcasted_iota(jnp.int32, (RW_W, RW_W), 1)
    ones_blk = jnp.where(row // RW_HD == col // RW_HD, 1.0, 0.0).astype(BF16)
    r64 = lax.broadcasted_iota(jnp.int32, (RW_HD, RW_W), 0)
    c64 = lax.broadcasted_iota(jnp.int32, (RW_HD, RW_W), 1)
    eye_t = jnp.where(r64 == c64 % RW_HD, 1.0, 0.0).astype(BF16)
    r8 = lax.broadcasted_iota(jnp.int32, (8, RW_W), 0)
    c8 = lax.broadcasted_iota(jnp.int32, (8, RW_W), 1)
    head_sel = jnp.where(r8 == c8 // RW_HD, 1.0, 0.0).astype(BF16)

    fwd = (wf_ref, kkf_ref, bf_ref, kf_ref, vf_ref, rf_ref, sf_ref, sf16_ref)
    bwd = (wb_ref, kkb_ref, bb_ref, kb_ref, vb_ref, rb_ref, sb_ref, sb16_ref)
    nrow = nb * RW_HD

    def one_dir(idx, refs):
        w_ref, kk_ref, b_ref, k_ref, v_ref, r_ref, s_ref, s16_ref = refs
        zs = []
        for b in range(nb):
            kk16 = kk_ref[idx, b:b + 1, :].astype(BF16)
            v16 = v_ref[idx, b:b + 1, :].astype(BF16)
            lhs = jnp.concatenate([s16_ref[b] * kk16, eye_t * v16], axis=0)
            red = jnp.dot(lhs, ones_blk, preferred_element_type=F32)
            s_new = (s_ref[b] * w_ref[idx, b:b + 1, :] - red[0:RW_HD] * b_ref[idx, b:b + 1, :]
                     + red[RW_HD:2 * RW_HD] * k_ref[idx, b:b + 1, :])
            s_ref[b] = s_new
            s16 = s_new.astype(BF16)
            s16_ref[b] = s16
            zs.append(s16 * r_ref[idx, b:b + 1, :].astype(BF16))
        return jnp.concatenate(zs, axis=0)

    def emit(idx, z, o_ref):
        o = lax.dot_general(head_sel, z, (((1,), (1,)), ((), ())), preferred_element_type=F32)
        o_ref[idx] = o[0:RW_HEADS]

    def step(i, carry):
        z_f, z_b = carry
        ib = tblk - 1 - i
        emit(jnp.maximum(i - 1, 0), z_f, of_ref)
        emit(jnp.minimum(ib + 1, tblk - 1), z_b, ob_ref)
        return one_dir(i, fwd), one_dir(ib, bwd)

    z0 = jnp.zeros((nrow, RW_W), BF16)
    z_f, z_b = lax.fori_loop(0, tblk, step, (z0, z0), unroll=8)
    emit(tblk - 1, z_f, of_ref)
    emit(0, z_b, ob_ref)


def rwkv_scan_pallas(w_f, w_b, kk, bvec, k, v, r, n_ctx):
    n, nb, _ = kk.shape
    tblk = RW_TBLK
    nblk = n // tblk
    nblk_ctx = n_ctx // tblk

    def fmap(c):
        return (c, 0, 0)

    def bmap(c):
        return (jnp.where(c < nblk_ctx, nblk_ctx - 1 - c, nblk + nblk_ctx - 1 - c), 0, 0)

    blk = (tblk, nb, RW_W)
    oblk = (tblk, RW_HEADS, nb * RW_HD)
    return pl.pallas_call(
        functools.partial(_rwkv_scan_kernel, tblk, nb),
        out_shape=[jax.ShapeDtypeStruct((n, RW_HEADS, nb * RW_HD), F32)] * 2,
        grid=(nblk,),
        in_specs=[pl.BlockSpec(blk, fmap)] * 6 + [pl.BlockSpec(blk, bmap)] * 6,
        out_specs=[pl.BlockSpec(oblk, fmap), pl.BlockSpec(oblk, bmap)],
        scratch_shapes=[pltpu.VMEM((nb, RW_HD, RW_W), F32), pltpu.VMEM((nb, RW_HD, RW_W), F32),
                        pltpu.VMEM((nb, RW_HD, RW_W), BF16), pltpu.VMEM((nb, RW_HD, RW_W), BF16)],
        compiler_params=pltpu.CompilerParams(dimension_semantics=("arbitrary",),
                                             vmem_limit_bytes=VMEM_LIMIT),
        name="rwkv_scan",
    )(w_f, kk, bvec, k, v, r, w_b, kk, bvec, k, v, r)


def _rwkv_post_kernel(of_ref, ob_ref, r_ref, k_ref, v_ref, gf_ref, gb_ref, rk_ref, lng_ref, lnb_ref,
                      y_ref):
    mean_blk = _head_blocks(1.0 / RW_HD)

    def gn(o):
        oc = o - _head_sum(o, mean_blk)
        var = _head_sum(oc * oc, mean_blk)
        return oc * lax.rsqrt(var + RW_GN_EPS) * lng_ref[...] + lnb_ref[...]

    bonus = _head_sum(r_ref[...] * k_ref[...] * rk_ref[...], _head_blocks(1.0)) * v_ref[...]
    y_ref[...] = (gn(of_ref[...]) + bonus) * gf_ref[...] + (gn(ob_ref[...]) + bonus) * gb_ref[...]


def rwkv_post(o_f, o_b, r, k, v, g_f, g_b, r_k, lnx_g, lnx_b, L, n_all, start):
    n_tok = r.shape[0]
    tm = min(RW_POST_TM, L)
    assert start % tm == 0 and n_all % tm == 0 and L % tm == 0
    tiles_per_seq = L // tm
    tok = pl.BlockSpec((tm, RW_W), lambda i: (i, 0))
    o_spec = pl.BlockSpec(
        (tm, RW_W), lambda i: ((i // tiles_per_seq) * (n_all // tm) + start // tm + i % tiles_per_seq, 0))
    vec = pl.BlockSpec((1, RW_W), lambda i: (0, 0))
    return pl.pallas_call(
        _rwkv_post_kernel,
        out_shape=jax.ShapeDtypeStruct((n_tok, RW_W), F32),
        grid=(n_tok // tm,),
        in_specs=[o_spec] * 2 + [tok] * 5 + [vec] * 3,
        out_specs=tok,
        compiler_params=pltpu.CompilerParams(
            dimension_semantics=("parallel",), vmem_limit_bytes=VMEM_LIMIT),
        name="rwkv_post",
    )(o_f, o_b, r, k, v, g_f, g_b, r_k.reshape(1, RW_W), lnx_g.reshape(1, RW_W), lnx_b.reshape(1, RW_W))


def rwkv_branch(p, pc, B, L, C, need_ctx, mu, w0, w2, a0, a2, g2, k_k, k_a, r_k, lnx_g, lnx_b):
    r, k, v, kk, bv, d_f, d_b, g_f, g_b = rwkv_prep(p, B, L, mu, w0, w2, a0, a2, g2, k_k, k_a)
    rc, kc, vc, kkc, bc, dc_f, dc_b, gc_f, gc_b = rwkv_prep(pc, B, C, mu, w0, w2, a0, a2, g2, k_k, k_a)

    def tmajor(xc, xl):
        t = jnp.concatenate([xc.reshape(B, C, RW_W), xl.reshape(B, L, RW_W)], axis=1)
        return jnp.moveaxis(t, 1, 0)

    pf, pb = rwkv_scan_pallas(tmajor(dc_f, d_f), tmajor(dc_b, d_b), tmajor(kkc, kk), tmajor(bc, bv),
                              tmajor(kc, k), tmajor(vc, v), tmajor(rc, r), C)

    def bmajor(t):
        t = t.reshape(C + L, RW_HEADS, B, RW_HD)
        return jnp.transpose(t, (2, 0, 1, 3)).reshape(B * (C + L), RW_W)

    pf, pb = bmajor(pf), bmajor(pb)
    y = rwkv_post(pf, pb, r, k, v, g_f, g_b, r_k, lnx_g, lnx_b, L, C + L, C).reshape(B, L, RW_W)
    yc = None
    if need_ctx:
        yc = rwkv_post(pf, pb, rc, kc, vc, gc_f, gc_b, r_k, lnx_g, lnx_b, C, C + L, 0).reshape(B, C, RW_W)
    return y, yc


def _diff_attn_kernel(q_ref, kt_ref, v_ref, lam_ref, gain_ref, o_ref):
    q = q_ref[0]
    v = v_ref[0]
    lam = lam_ref[...]
    tq = q.shape[0]
    head_of_lane = lax.broadcasted_iota(jnp.int32, (1, DF_W), 1) // DF_VD
    acc = jnp.zeros((tq, DF_W), F32)
    for h in range(DF_HEADS):
        parts = []
        for c in range(2):
            j = 2 * h + c
            s = jnp.dot(q[:, DF_HD * j:DF_HD * (j + 1)], kt_ref[0, j],
                        preferred_element_type=F32)
            p = jnp.exp2(s - jnp.max(s, -1, keepdims=True))
            l = jnp.sum(p, -1, keepdims=True)
            pv = jnp.dot(p.astype(BF16), v, preferred_element_type=F32)
            parts.append(pv / l)
        acc = jnp.where(head_of_lane == h, parts[0] - lam * parts[1], acc)
    row = lax.broadcasted_iota(jnp.int32, (DF_W, DF_W), 0)
    col = lax.broadcasted_iota(jnp.int32, (DF_W, DF_W), 1)
    seg_mean = jnp.where(row // DF_VD == col // DF_VD, 1.0 / DF_VD, 0.0).astype(F32)
    ms = jnp.dot(acc * acc, seg_mean, preferred_element_type=F32, precision=lax.Precision.HIGHEST)
    o_ref[0] = acc * lax.rsqrt(ms + 1e-5) * gain_ref[...]


def _diff_pre_kernel(rope, x_ref, cos_ref, sin_ref, q_ref, kt_ref, v_ref):
    x = x_ref[...].astype(F32)
    q, k, v = x[:, 0:DF_W], x[:, DF_W:2 * DF_W], x[:, 2 * DF_W:3 * DF_W]
    if rope:
        cos, sin = cos_ref[...], sin_ref[...]
        q = _rope(q, cos, sin, DF_HD)
        k = _rope(k, cos, sin, DF_HD)
    q_ref[...] = (q * (DF_HD ** -0.5 * math.log2(math.e))).astype(BF16)
    kt_ref[0] = k.T.astype(BF16)
    v_ref[...] = v.astype(BF16)


def diff_pre(p, B, L, tables, rope):
    tm = min(RW_PREP_TM, L)
    tiles_per_seq = L // tm
    n_tok = B * L
    width = 3 * DF_W
    col = P_DF // width
    cos, sin = tables
    tok = pl.BlockSpec((tm, DF_W), lambda i: (i, 0))
    tab = pl.BlockSpec((tm, DF_W), lambda i: (i % tiles_per_seq, 0))
    q, kt, v = pl.pallas_call(
        functools.partial(_diff_pre_kernel, rope),
        out_shape=[jax.ShapeDtypeStruct((n_tok, DF_W), BF16),
                   jax.ShapeDtypeStruct((B, DF_W, L), BF16),
                   jax.ShapeDtypeStruct((n_tok, DF_W), BF16)],
        grid=(n_tok // tm,),
        in_specs=[pl.BlockSpec((tm, width), lambda i: (i, col)), tab, tab],
        out_specs=[tok, pl.BlockSpec((1, DF_W, tm), lambda i: (i // tiles_per_seq, 0, i % tiles_per_seq)),
                   tok],
        compiler_params=pltpu.CompilerParams(
            dimension_semantics=("parallel",), vmem_limit_bytes=VMEM_LIMIT),
        name="diff_pre",
    )(p, cos, sin)
    return q.reshape(B, L, DF_W), kt, v.reshape(B, L, DF_W)


def diff_attend(qs, kt, vb, lam, lam_init, subln_g):
    B, Lq = qs.shape[:2]
    S = vb.shape[1]
    tq = min(DF_TQ, Lq)
    kt = kt.reshape(B, 2 * DF_HEADS, DF_HD, S)
    gain = jnp.tile(subln_g.astype(F32) * (1.0 - lam_init), DF_HEADS).reshape(1, DF_W)
    return pl.pallas_call(
        _diff_attn_kernel,
        out_shape=jax.ShapeDtypeStruct((B, Lq, DF_W), F32),
        grid=(B, Lq // tq),
        in_specs=[pl.BlockSpec((1, tq, DF_W), lambda b, i: (b, i, 0)),
                  pl.BlockSpec((1, 2 * DF_HEADS, DF_HD, S), lambda b, i: (b, 0, 0, 0)),
                  pl.BlockSpec((1, S, DF_W), lambda b, i: (b, 0, 0)),
                  pl.BlockSpec((1, 1), lambda b, i: (0, 0)),
                  pl.BlockSpec((1, DF_W), lambda b, i: (0, 0))],
        out_specs=pl.BlockSpec((1, tq, DF_W), lambda b, i: (b, i, 0)),
        compiler_params=pltpu.CompilerParams(
            dimension_semantics=("parallel", "parallel"), vmem_limit_bytes=VMEM_LIMIT),
        name="diff_attn",
    )(qs, kt, vb, lam.reshape(1, 1).astype(F32), gain)


def diff_branch(p, pc, B, L, C, need_ctx, tables, lq1, lk1, lq2, lk2, subln_g, lam_init):
    lam = (jnp.exp(jnp.sum(lq1.astype(F32) * lk1.astype(F32)))
           - jnp.exp(jnp.sum(lq2.astype(F32) * lk2.astype(F32))) + lam_init)
    q, kt, v = diff_pre(p, B, L, tables, True)
    qc, ktc, vc = diff_pre(pc, B, C, tables, False)
    y = diff_attend(q, jnp.concatenate([kt, ktc], axis=2), jnp.concatenate([v, vc], axis=1),
                    lam, lam_init, subln_g)
    yc = diff_attend(qc, ktc, vc, lam, lam_init, subln_g) if need_ctx else None
    return y, yc


def _ln_rows(x):
    mu = jnp.mean(x, -1, keepdims=True)
    xc = x - mu
    return xc * lax.rsqrt(jnp.mean(xc * xc, -1, keepdims=True) + LN_EPS)


def _ffn_kernel(tm, tiles_per_seq, n_chunks,
                h_ref, hprev_ref, hnext_ref, sh_ref, sc_ref, gate_ref,
                wa_ref, wb_ref, cwa_ref, cwb_ref, cba_ref, cbb_ref, wd_ref, lng_ref, lnb_ref,
                o_ref, u_ref, acc_ref):
    i = pl.program_id(0)
    j = pl.program_id(1)
    n = tm + 2 * HALO

    @pl.when(j == 0)
    def _():
        scale = 1.0 + sc_ref[0]
        shift = sh_ref[0]
        first = (i % tiles_per_seq) == 0
        last = (i % tiles_per_seq) == tiles_per_seq - 1
        u_ref[0:tm, :] = (_ln_rows(h_ref[...]) * scale + shift).astype(BF16)
        un = _ln_rows(hnext_ref[...]) * scale + shift
        up = _ln_rows(hprev_ref[...]) * scale + shift
        u_ref[tm:tm + HALO, :] = jnp.where(last, 0.0, un).astype(BF16)
        u_ref[tm + HALO:n, :] = jnp.where(first, 0.0, up).astype(BF16)
        acc_ref[...] = jnp.zeros_like(acc_ref)

    u = u_ref[...]

    def conv(w_ref, cw_ref, cb_ref):
        x = jnp.dot(u, w_ref[...], preferred_element_type=F32)
        cw = cw_ref[...]
        y = (pltpu.roll(x, 1, 0)[0:tm] * cw[0:1] + x[0:tm] * cw[1:2]
             + pltpu.roll(x, n - 1, 0)[0:tm] * cw[2:3] + cb_ref[...])
        return y

    a = conv(wa_ref, cwa_ref, cba_ref)
    b = conv(wb_ref, cwb_ref, cbb_ref)
    g = (a * jax.nn.sigmoid(a) * b).astype(BF16)
    acc_ref[...] += jnp.dot(g, wd_ref[...], preferred_element_type=F32)

    @pl.when(j == n_chunks - 1)
    def _():
        y = DN_ALPHA * h_ref[...] + gate_ref[0] * acc_ref[...]
        o_ref[...] = _ln_rows(y) * lng_ref[...] + lnb_ref[...]


def ffn_block(h, shift, scale, gate, w_up, conv_w, conv_b, w_down, ln_g, ln_b):
    B, L, D = h.shape
    tm = min(FFN_TM, L)
    tiles_per_seq = L // tm
    n_tok = B * L
    n_chunks = D_FF // FFN_CHUNK
    hb = tm // HALO

    def bcast(t):
        return jnp.broadcast_to(t.astype(F32), (B, 1, D))

    wu = w_up.astype(BF16)
    wd = w_down.astype(BF16)
    cb = conv_b.reshape(1, 2 * D_FF)
    h2 = h.reshape(n_tok, D)
    mod_spec = pl.BlockSpec((1, 1, D), lambda i, j: (i // tiles_per_seq, 0, 0))
    vec_spec = pl.BlockSpec((1, D), lambda i, j: (0, 0))
    out = pl.pallas_call(
        functools.partial(_ffn_kernel, tm, tiles_per_seq, n_chunks),
        out_shape=jax.ShapeDtypeStruct((n_tok, D), F32),
        grid=(n_tok // tm, n_chunks),
        in_specs=[pl.BlockSpec((tm, D), lambda i, j: (i, 0)),
                  pl.BlockSpec((HALO, D), lambda i, j: (jnp.maximum(i * hb - 1, 0), 0)),
                  pl.BlockSpec((HALO, D), lambda i, j: (jnp.minimum((i + 1) * hb, n_tok // HALO - 1), 0)),
                  mod_spec, mod_spec, mod_spec,
                  pl.BlockSpec((D, FFN_CHUNK), lambda i, j: (0, j)),
                  pl.BlockSpec((D, FFN_CHUNK), lambda i, j: (0, n_chunks + j)),
                  pl.BlockSpec((3, FFN_CHUNK), lambda i, j: (0, j)),
                  pl.BlockSpec((3, FFN_CHUNK), lambda i, j: (0, n_chunks + j)),
                  pl.BlockSpec((1, FFN_CHUNK), lambda i, j: (0, j)),
                  pl.BlockSpec((1, FFN_CHUNK), lambda i, j: (0, n_chunks + j)),
                  pl.BlockSpec((FFN_CHUNK, D), lambda i, j: (j, 0)),
                  vec_spec, vec_spec],
        out_specs=pl.BlockSpec((tm, D), lambda i, j: (i, 0)),
        scratch_shapes=[pltpu.VMEM((tm + 2 * HALO, D), BF16), pltpu.VMEM((tm, D), F32)],
        compiler_params=pltpu.CompilerParams(
            dimension_semantics=("parallel", "arbitrary"), vmem_limit_bytes=VMEM_LIMIT),
        name="conv_ffn",
    )(h2, h2, h2, bcast(shift), bcast(scale), bcast(gate), wu, wu, conv_w, conv_w, cb, cb, wd,
      ln_g.reshape(1, D), ln_b.reshape(1, D))
    return out.reshape(B, L, D)


P_GATE, P_HY, P_RW, P_SW, P_DF = 0, 4096, 5120, 6400, 6912
P_COLS = 7680
IN_CHUNK = 768


def _in_proj_kernel(x_ref, sh_ref, sc_ref, w_ref, o_ref, u_ref):
    @pl.when(pl.program_id(1) == 0)
    def _():
        u_ref[...] = (_ln_rows(x_ref[...]) * (1.0 + sc_ref[0]) + sh_ref[0]).astype(BF16)

    o_ref[...] = jnp.dot(u_ref[...], w_ref[...], preferred_element_type=F32).astype(BF16)


def in_proj(h, shift, scale, w_in_p):
    B, L, D = h.shape
    tm = min(IN_TM, L)
    tiles_per_seq = L // tm
    n_tok = B * L

    def bcast(t):
        return jnp.broadcast_to(t.astype(F32), (B, 1, D))

    mod_spec = pl.BlockSpec((1, 1, D), lambda i, j: (i // tiles_per_seq, 0, 0))
    return pl.pallas_call(
        _in_proj_kernel,
        out_shape=jax.ShapeDtypeStruct((n_tok, P_COLS), BF16),
        grid=(n_tok // tm, P_COLS // IN_CHUNK),
        in_specs=[pl.BlockSpec((tm, D), lambda i, j: (i, 0)), mod_spec, mod_spec,
                  pl.BlockSpec((D, IN_CHUNK), lambda i, j: (0, j))],
        out_specs=pl.BlockSpec((tm, IN_CHUNK), lambda i, j: (i, j)),
        scratch_shapes=[pltpu.VMEM((tm, D), BF16)],
        compiler_params=pltpu.CompilerParams(
            dimension_semantics=("parallel", "arbitrary"), vmem_limit_bytes=VMEM_LIMIT),
        name="in_proj",
    )(h.reshape(n_tok, D), bcast(shift), bcast(scale), w_in_p)


def pack_w_in(w):
    offs = np.cumsum((0,) + IN_SIZES)
    hy, sw, rw, df, gt = [w[:, offs[j]:offs[j + 1]] for j in range(len(IN_SIZES))]

    def padto(t, n):
        return jnp.pad(t, ((0, 0), (0, n - t.shape[1])))

    return jnp.concatenate([gt, padto(hy, P_RW - P_HY), padto(rw, P_SW - P_RW), sw, df],
                           axis=1).astype(BF16)


def _merge_kernel(yh_ref, ys_ref, yr_ref, yd_ref, g0_ref, g1_ref, g2_ref, g3_ref, h_ref, gate_ref,
                  wbr_ref, wo_ref, lng_ref, lnb_ref, o_ref):
    acc = None
    for j, (y_ref, g_ref) in enumerate(((yh_ref, g0_ref), (ys_ref, g1_ref), (yr_ref, g2_ref),
                                        (yd_ref, g3_ref))):
        term = jax.nn.sigmoid(g_ref[...].astype(F32)) * jnp.dot(y_ref[...].astype(BF16), wbr_ref[j],
                                                    preferred_element_type=F32)
        acc = term if acc is None else acc + term
    mix = jnp.dot(acc.astype(BF16), wo_ref[...], preferred_element_type=F32)
    y = DN_ALPHA * h_ref[...] + gate_ref[0] * mix
    o_ref[...] = _ln_rows(y) * lng_ref[...] + lnb_ref[...]


def merge_block(ys, p, h, gate, w_br, w_o, ln_g, ln_b):
    B, L, D = h.shape
    n_tok = B * L
    tm = min(MERGE_TM, L)
    tiles_per_seq = L // tm
    y_spec = pl.BlockSpec((tm, BR_W), lambda i: (i, 0))
    vec_spec = pl.BlockSpec((1, D), lambda i: (0, 0))
    g_specs = [pl.BlockSpec((tm, D), functools.partial(lambda j, i: (i, j), j)) for j in range(N_BRANCH)]
    out = pl.pallas_call(
        _merge_kernel,
        out_shape=jax.ShapeDtypeStruct((n_tok, D), F32),
        grid=(n_tok // tm,),
        in_specs=[y_spec] * N_BRANCH + g_specs + [
            pl.BlockSpec((tm, D), lambda i: (i, 0)),
            pl.BlockSpec((1, 1, D), lambda i: (i // tiles_per_seq, 0, 0)),
            pl.BlockSpec((N_BRANCH, BR_W, D), lambda i: (0, 0, 0)),
            pl.BlockSpec((D, D), lambda i: (0, 0)), vec_spec, vec_spec],
        out_specs=pl.BlockSpec((tm, D), lambda i: (i, 0)),
        compiler_params=pltpu.CompilerParams(
            dimension_semantics=("parallel",), vmem_limit_bytes=VMEM_LIMIT),
        name="merge",
    )(*[y.reshape(n_tok, BR_W) for y in ys], p, p, p, p, h.reshape(n_tok, D),
      jnp.broadcast_to(gate.astype(F32), (B, 1, D)), w_br.astype(BF16), w_o.astype(BF16),
      ln_g.reshape(1, D), ln_b.reshape(1, D))
    return out.reshape(B, L, D)


def kernel(x, c, ctx, c_ctx, ada_w, ada_b, w_in, hy_conv_w, hy_conv_b, hy_f_w1, hy_f_b1,
           hy_f_w2, hy_f_b2, hy_f_w3, hy_f_freq, hy_bias, swa_sink, rwkv_mu, rwkv_w0, rwkv_w2,
           rwkv_a0, rwkv_a2, rwkv_g2, rwkv_kk, rwkv_ka, rwkv_rk, rwkv_lnx_g, rwkv_lnx_b,
           diff_lq1, diff_lk1, diff_lq2, diff_lk2, diff_subln_g, w_branch, w_out, ln1_g, ln1_b,
           ffn_w_up, ffn_conv_w, ffn_conv_b, ffn_w_down, ln2_g, ln2_b):
    h, hc = x, ctx
    mats, mats_c = dft_mats(x.shape[1]), dft_mats(ctx.shape[1])
    swa_tables = rope_tables(x.shape[1], SWA_HD, SWA_HEADS * SWA_HD)
    df_tables = rope_tables(x.shape[1], DF_HD, DF_W)
    s_lat = jax.nn.silu(c)
    s_ctx = jax.nn.silu(c_ctx)
    for i in range(DEPTH):
        need_ctx = i < DEPTH - 1
        mod = (s_lat @ ada_w[i] + ada_b[i])[:, None, :]
        mod_c = s_ctx @ ada_w[i] + ada_b[i]
        sh1, sc1, g1, sh2, sc2, g2 = jnp.split(mod, 6, -1)
        csh1, csc1, cg1, csh2, csc2, cg2 = jnp.split(mod_c, 6, -1)

        B, L, D = h.shape
        C = hc.shape[1]
        w_in_p = pack_w_in(w_in[i])
        p = in_proj(h, sh1, sc1, w_in_p)
        pc = in_proj(hc, csh1.reshape(1, 1, D), csc1.reshape(1, 1, D), w_in_p)

        y_hy, yc_hy = hyena_branch(p, pc, B, L, C, need_ctx, mats, mats_c, hy_conv_w[i], hy_conv_b[i],
                                   hy_f_w1[i], hy_f_b1[i], hy_f_w2[i], hy_f_b2[i], hy_f_w3[i],
                                   hy_f_freq[i], hy_bias[i])
        y_sw, yc_sw = swa_branch(p, pc, B, L, C, need_ctx, swa_sink[i], swa_tables)
        y_rw, yc_rw = rwkv_branch(p, pc, B, L, C, need_ctx, rwkv_mu[i], rwkv_w0[i], rwkv_w2[i],
                                  rwkv_a0[i], rwkv_a2[i], rwkv_g2[i], rwkv_kk[i], rwkv_ka[i],
                                  rwkv_rk[i], rwkv_lnx_g[i], rwkv_lnx_b[i])
        lam_init = 0.8 - 0.6 * math.exp(-0.3 * i)
        y_df, yc_df = diff_branch(p, pc, B, L, C, need_ctx, df_tables, diff_lq1[i], diff_lk1[i],
                                  diff_lq2[i], diff_lk2[i], diff_subln_g[i], lam_init)

        h = merge_block((y_hy, y_sw, y_rw, y_df), p, h, g1, w_branch[i], w_out[i], ln1_g[i], ln1_b[i])
        h = ffn_block(h, sh2, sc2, g2, ffn_w_up[i], ffn_conv_w[i], ffn_conv_b[i], ffn_w_down[i],
                      ln2_g[i], ln2_b[i])

        if need_ctx:
            hc = merge_block((yc_hy, yc_sw, yc_rw, yc_df), pc, hc, cg1.reshape(1, 1, D), w_branch[i],
                             w_out[i], ln1_g[i], ln1_b[i])
            hc = ffn_block(hc, csh2.reshape(1, 1, -1), csc2.reshape(1, 1, -1), cg2.reshape(1, 1, -1),
                           ffn_w_up[i], ffn_conv_w[i], ffn_conv_b[i], ffn_w_down[i], ln2_g[i], ln2_b[i])
    return h
```

```python
import functools
import math

import jax
import jax.numpy as jnp
import numpy as np
from jax import lax
from jax.experimental import pallas as pl
from jax.experimental.pallas import tpu as pltpu

D_MODEL = 1024
DEPTH = 2
GRID_W = 64
ROPE_BASE = 10000.0
F32 = jnp.float32
BF16 = jnp.bfloat16
NEG_INF = -1e30
LN_EPS = 1e-6

HY_W = 256
HY_ORDER = 2
HY_DIRS = 2
HY_EMB = 33
HY_MIN_DECAY = math.log(1e-2) / 1.5
HY_MAX_DECAY = math.log(1e-2) / 0.3

SWA_HEADS = 4
SWA_KV = 2
SWA_HD = 64
SWA_WIN = 128
SWA_BLOCK = 128

RW_HEADS = 4
RW_HD = 64
RW_W = RW_HEADS * RW_HD
RW_DECAY_R = 64
RW_AAA_R = 64
RW_GATE_R = 128
RW_GN_EPS = 64e-5

DF_HEADS = 4
DF_HD = 32
DF_VD = 2 * DF_HD
DF_W = DF_HEADS * DF_VD

N_BRANCH = 4
BR_W = 256
D_FF = 2816
DN_ALPHA = (2 * DEPTH) ** 0.25

SWA_SIZES = (SWA_HEADS * SWA_HD, SWA_KV * SWA_HD, SWA_KV * SWA_HD)
RW_SIZES = (RW_W, RW_W, RW_W, RW_DECAY_R, RW_DECAY_R, RW_AAA_R, RW_GATE_R, RW_GATE_R)
DF_SIZES = (DF_W, DF_W, DF_W)
IN_SIZES = (3 * HY_W, sum(SWA_SIZES), sum(RW_SIZES), sum(DF_SIZES), N_BRANCH * D_MODEL)

V7X_VMEM_BYTES = 64 * 1024 * 1024
LANES = 128
MM_TM = 512
MM_TN = 512
RW_TBLK = 64
DF_TQ = 256
SWA_QROWS = 256
HALO = 8
P_HALO = 16
IN_TM = 2048
FFN_TM = 1024
FFN_CHUNK = 256
MERGE_TM = 512
HY_TK = 512
RW_PREP_TM = 512
RW_POST_TM = 256
VMEM_LIMIT = V7X_VMEM_BYTES * 3 // 4


def _mm_kernel(x_ref, w_ref, o_ref):
    o_ref[...] = jnp.dot(x_ref[...].astype(BF16), w_ref[...], preferred_element_type=F32)


def _mm(x, w):
    m, k = x.shape
    n = w.shape[1]
    n_pad = -(-n // LANES) * LANES
    wb = w.astype(BF16)
    if n_pad != n:
        wb = jnp.pad(wb, ((0, 0), (0, n_pad - n)))
    tn = MM_TN if n_pad % MM_TN == 0 else (256 if n_pad % 256 == 0 else LANES)
    tm = MM_TM if m % MM_TM == 0 else m
    out = pl.pallas_call(
        _mm_kernel,
        out_shape=jax.ShapeDtypeStruct((m, n_pad), F32),
        grid=(m // tm, n_pad // tn),
        in_specs=[pl.BlockSpec((tm, k), lambda i, j: (i, 0)),
                  pl.BlockSpec((k, tn), lambda i, j: (0, j))],
        out_specs=pl.BlockSpec((tm, tn), lambda i, j: (i, j)),
        compiler_params=pltpu.CompilerParams(
            dimension_semantics=("parallel", "parallel"), vmem_limit_bytes=VMEM_LIMIT),
        name="matmul",
    )(x, wb)
    return out[:, :n] if n_pad != n else out


def dft_mats(L):
    tb = 64
    k = jnp.arange(L, dtype=jnp.int32)[:, None]

    def table(t):
        ang = (((2 * k + 1) * t[None, :]) % (4 * L)).astype(F32) * (math.pi / (2 * L))
        return jnp.cos(ang), jnp.sin(ang)

    ca, sa = table(tb * jnp.arange(L // tb, dtype=jnp.int32))
    cb, sb = table(jnp.arange(tb, dtype=jnp.int32))
    c = (ca[:, :, None] * cb[:, None, :] - sa[:, :, None] * sb[:, None, :]).reshape(L, L)
    s = (sa[:, :, None] * cb[:, None, :] + ca[:, :, None] * sb[:, None, :]).reshape(L, L)
    ct = (ca.T[:, None, :] * cb.T[None, :, :] - sa.T[:, None, :] * sb.T[None, :, :]).reshape(L, L)
    st = (sa.T[:, None, :] * cb.T[None, :, :] + ca.T[:, None, :] * sb.T[None, :, :]).reshape(L, L)
    return c.astype(BF16), s.astype(BF16), ct.astype(BF16), st.astype(BF16)


def hyena_kspec(L, mats, w1, b1, w2, b2, w3, freq):
    t = jnp.linspace(0.0, 1.0, L, dtype=F32)[:, None]
    bands = (HY_EMB - 1) // 2
    w = 2.0 * math.pi * jnp.arange(L, dtype=F32) / L
    fb = jnp.linspace(1e-4, bands - 1, bands, dtype=F32)
    ang = w[:, None] * fb[None, :]
    z = jnp.concatenate([t, jnp.cos(ang), -jnp.sin(ang)], -1)
    fr = freq.astype(F32)
    hdn = jnp.sin(fr * (z @ w1.astype(F32) + b1.astype(F32)))
    hdn = jnp.sin(fr * (hdn @ w2.astype(F32) + b2.astype(F32)))
    filt = (hdn @ w3.astype(F32)).reshape(L, HY_ORDER, HY_DIRS, HY_W)
    deltas = jnp.abs(jnp.linspace(HY_MIN_DECAY, HY_MAX_DECAY, HY_W, dtype=F32))
    filt = filt * jnp.exp(-t * deltas[None, :])[:, None, None, :]
    filt = filt / jnp.sum(jnp.abs(filt), axis=(0, 2), keepdims=True)
    fwd, bwd = filt[:, :, 0], filt[:, :, 1]
    nw = HY_ORDER * HY_W
    bwd = bwd.at[0].set(0.0)
    kr = _mm(mats[0], (fwd + bwd).reshape(L, nw))
    ki = -_mm(mats[1], (fwd - bwd).reshape(L, nw))

    def per_order(a):
        return jnp.transpose(a.reshape(L, HY_ORDER, HY_W), (1, 0, 2))

    return per_order(kr), per_order(ki)


def _hy_fwd_kernel(ac_ref, as_ref, z_ref, kr_ref, ki_ref, yr_ref, yi_ref):
    z = z_ref[0].astype(BF16)
    zr = jnp.dot(ac_ref[...], z, preferred_element_type=F32)
    zi = -jnp.dot(as_ref[...], z, preferred_element_type=F32)
    kr, ki = kr_ref[...], ki_ref[...]
    yr_ref[0] = (zr * kr - zi * ki).astype(BF16)
    yi_ref[0] = (zr * ki + zi * kr).astype(BF16)


def _hy_inv_kernel(inv_l, act_ref, ast_ref, yr_ref, yi_ref, z_ref, x_ref, bias_ref, o_ref, *o16_ref):
    y = (jnp.dot(act_ref[...], yr_ref[0], preferred_element_type=F32)
         - jnp.dot(ast_ref[...], yi_ref[0], preferred_element_type=F32))
    o = x_ref[0] * (y * inv_l + bias_ref[...] * z_ref[0])
    o_ref[0] = o
    for ref in o16_ref:
        ref[0] = o.astype(BF16)


def hyena_longconv(z, z16, xmul, mats, kr, ki, bias, want16):
    B, L, W = z.shape
    ac, as_, act, ast = mats
    tk = min(HY_TK, L)
    a_spec = pl.BlockSpec((tk, L), lambda i, b: (i, 0))
    full_spec = pl.BlockSpec((1, L, W), lambda i, b: (b, 0, 0))
    tile_spec = pl.BlockSpec((1, tk, W), lambda i, b: (b, i, 0))
    k_spec = pl.BlockSpec((tk, W), lambda i, b: (i, 0))
    params = pltpu.CompilerParams(dimension_semantics=("parallel", "arbitrary"),
                                  vmem_limit_bytes=VMEM_LIMIT)
    yr, yi = pl.pallas_call(
        _hy_fwd_kernel,
        out_shape=[jax.ShapeDtypeStruct((B, L, W), BF16)] * 2,
        grid=(L // tk, B),
        in_specs=[a_spec, a_spec, full_spec, k_spec, k_spec],
        out_specs=[tile_spec, tile_spec],
        compiler_params=params,
        name="hyena_fwd",
    )(ac, as_, z16, kr, ki)
    n_out = 2 if want16 else 1
    return pl.pallas_call(
        functools.partial(_hy_inv_kernel, 1.0 / L),
        out_shape=[jax.ShapeDtypeStruct((B, L, W), F32), jax.ShapeDtypeStruct((B, L, W), BF16)][:n_out],
        grid=(L // tk, B),
        in_specs=[a_spec, a_spec, full_spec, full_spec, tile_spec, tile_spec,
                  pl.BlockSpec((1, W), lambda i, b: (0, 0))],
        out_specs=[tile_spec] * n_out,
        compiler_params=params,
        name="hyena_inv",
    )(act, ast, yr, yi, z, xmul, bias.reshape(1, W))


def _hy_pre_kernel(tm, tiles_per_seq, x_ref, xp_ref, xn_ref, cw_ref, cb_ref, v_ref, v16_ref, x1_ref,
                   x2_ref):
    i = pl.program_id(0)
    first = (i % tiles_per_seq) == 0
    last = (i % tiles_per_seq) == tiles_per_seq - 1
    x = x_ref[...].astype(F32)
    row = lax.broadcasted_iota(jnp.int32, x.shape, 0)
    prev_row = jnp.where(first, 0.0, xp_ref[...].astype(F32)[P_HALO - 1:P_HALO, :])
    next_row = jnp.where(last, 0.0, xn_ref[...].astype(F32)[0:1, :])
    xm1 = jnp.where(row == 0, prev_row, pltpu.roll(x, 1, 0))
    xp1 = jnp.where(row == tm - 1, next_row, pltpu.roll(x, tm - 1, 0))
    cw = cw_ref[...]
    y = xm1 * cw[0:1] + x * cw[1:2] + xp1 * cw[2:3] + cb_ref[...]
    v_ref[...] = y[:, 0:HY_W]
    v16_ref[...] = y[:, 0:HY_W].astype(BF16)
    x1_ref[...] = y[:, HY_W:2 * HY_W]
    x2_ref[...] = y[:, 2 * HY_W:3 * HY_W]


def hyena_pre(p, B, L, conv_w, conv_b):
    tm = min(RW_PREP_TM, L)
    tiles_per_seq = L // tm
    n_tok = B * L
    hb = tm // P_HALO
    width = P_RW - P_HY
    col = P_HY // width
    cw = jnp.pad(conv_w, ((0, 0), (0, width - conv_w.shape[1])))
    cb = jnp.pad(conv_b, (0, width - conv_b.shape[0])).reshape(1, width)
    out_spec = pl.BlockSpec((tm, HY_W), lambda i: (i, 0))
    outs = pl.pallas_call(
        functools.partial(_hy_pre_kernel, tm, tiles_per_seq),
        out_shape=[jax.ShapeDtypeStruct((n_tok, HY_W), dt) for dt in (F32, BF16, F32, F32)],
        grid=(n_tok // tm,),
        in_specs=[pl.BlockSpec((tm, width), lambda i: (i, col)),
                  pl.BlockSpec((P_HALO, width), lambda i: (jnp.maximum(i * hb - 1, 0), col)),
                  pl.BlockSpec((P_HALO, width), lambda i: (jnp.minimum((i + 1) * hb, n_tok // P_HALO - 1), col)),
                  pl.BlockSpec((3, width), lambda i: (0, 0)),
                  pl.BlockSpec((1, width), lambda i: (0, 0))],
        out_specs=[out_spec] * 4,
        compiler_params=pltpu.CompilerParams(
            dimension_semantics=("parallel",), vmem_limit_bytes=VMEM_LIMIT),
        name="hyena_pre",
    )(p, p, p, cw, cb)
    return [o.reshape(B, L, HY_W) for o in outs]


def hyena_mix(p, B, L, conv_w, conv_b, mats, kspec, bias):
    v, v16, x1, x2 = hyena_pre(p, B, L, conv_w, conv_b)
    kr, ki = kspec
    zz, zz16 = hyena_longconv(v, v16, x1, mats, kr[0], ki[0], bias[0], True)
    return hyena_longconv(zz, zz16, x2, mats, kr[1], ki[1], bias[1], False)[0]


def hyena_branch(p, pc, B, L, C, need_ctx, mats, mats_c, conv_w, conv_b, fw1, fb1, fw2, fb2, fw3, ffreq,
                 bias):
    kspec = hyena_kspec(L, mats, fw1, fb1, fw2, fb2, fw3, ffreq)
    y = hyena_mix(p, B, L, conv_w, conv_b, mats, kspec, bias)
    yc = None
    if need_ctx:
        kspec_c = hyena_kspec(C, mats_c, fw1, fb1, fw2, fb2, fw3, ffreq)
        yc = hyena_mix(pc, B, C, conv_w, conv_b, mats_c, kspec_c, bias)
    return y, yc


def rope_tables(L, d, width):
    nf = d // 4
    pos = jnp.arange(L)
    inv = ROPE_BASE ** (-jnp.arange(nf, dtype=F32) / nf)
    cs, sn = [], []
    for p in (pos // GRID_W, pos % GRID_W):
        ang = p.astype(F32)[:, None] * inv[None, :]
        cs += [jnp.cos(ang), jnp.cos(ang)]
        sn += [-jnp.sin(ang), jnp.sin(ang)]
    reps = width // d
    return jnp.tile(jnp.concatenate(cs, -1), (1, reps)), jnp.tile(jnp.concatenate(sn, -1), (1, reps))


def _rope(x, cos, sin, d):
    width = x.shape[-1]
    q = d // 4
    lane = lax.broadcasted_iota(jnp.int32, x.shape, 1)
    swapped = jnp.where(lane % (2 * q) < q, pltpu.roll(x, width - q, 1), pltpu.roll(x, q, 1))
    return x * cos + swapped * sin


def _swa_kernel(local, seq_len, qrows, q_ref, *refs):
    nk = qrows // SWA_BLOCK + 2
    k_refs, v_refs = refs[0:nk], refs[nk:2 * nk]
    ck_ref, cv_ref, cos_ref, sin_ref, sink_ref, o_ref = refs[2 * nk:]
    n = pl.program_id(1)
    blk = SWA_BLOCK
    qscale = SWA_HD ** -0.5 * math.log2(math.e)
    q = q_ref[...].astype(F32)
    if local:
        q0 = pl.multiple_of(n * qrows, qrows)
        q = _rope(q, cos_ref[pl.ds(q0, qrows), :], sin_ref[pl.ds(q0, qrows), :], SWA_HD)
        first = n * (qrows // blk) - 1
        ks, vs = [], []
        for j, (k_ref, v_ref) in enumerate(zip(k_refs, v_refs)):
            k0 = pl.multiple_of(jnp.clip(first + j, 0, seq_len // blk - 1) * blk, blk)
            ks.append(_rope(k_ref[...].astype(F32), cos_ref[pl.ds(k0, blk), 0:SWA_KV * SWA_HD],
                            sin_ref[pl.ds(k0, blk), 0:SWA_KV * SWA_HD], SWA_HD).astype(BF16))
            vs.append(v_ref[...].astype(BF16))
        k_loc = jnp.concatenate(ks, axis=0)
        v_loc = jnp.concatenate(vs, axis=0)
        qpos = n * qrows + lax.broadcasted_iota(jnp.int32, (qrows, nk * blk), 0)
        kpos = first * blk + lax.broadcasted_iota(jnp.int32, (qrows, nk * blk), 1)
        valid = (jnp.abs(kpos - qpos) <= SWA_WIN) & (kpos >= 0) & (kpos < seq_len)
    qb = (q * qscale).astype(BF16)
    ck = ck_ref[...].astype(BF16)
    cv = cv_ref[...].astype(BF16)
    nt = (((1,), (1,)), ((), ()))
    outs = []
    for h in range(SWA_HEADS):
        g = h // (SWA_HEADS // SWA_KV)
        qh = qb[:, h * SWA_HD:(h + 1) * SWA_HD]
        gs = slice(g * SWA_HD, (g + 1) * SWA_HD)
        sink = sink_ref[h:h + 1, 0:1]
        s_ctx = lax.dot_general(qh, ck[:, gs], nt, preferred_element_type=F32)
        m = jnp.maximum(jnp.max(s_ctx, -1, keepdims=True), sink)
        if local:
            s_loc = lax.dot_general(qh, k_loc[:, gs], nt, preferred_element_type=F32)
            s_loc = jnp.where(valid, s_loc, NEG_INF)
            m = jnp.maximum(m, jnp.max(s_loc, -1, keepdims=True))
        p_ctx = jnp.exp2(s_ctx - m)
        den = jnp.sum(p_ctx, -1, keepdims=True) + jnp.exp2(sink - m)
        acc = jnp.dot(p_ctx.astype(BF16), cv[:, gs], preferred_element_type=F32)
        if local:
            p_loc = jnp.exp2(s_loc - m)
            den = den + jnp.sum(p_loc, -1, keepdims=True)
            acc = acc + jnp.dot(p_loc.astype(BF16), v_loc[:, gs], preferred_element_type=F32)
        outs.append(acc / den)
    o_ref[...] = jnp.concatenate(outs, axis=-1)


def swa_attend(p, pc, B, L, C, sink, tables, local):
    blk = SWA_BLOCK
    nb = L // blk
    qrows = min(SWA_QROWS, L)
    nq = L // qrows
    qpb = qrows // blk
    qcol = P_SW // (SWA_HEADS * SWA_HD)
    kcol = (P_SW + SWA_HEADS * SWA_HD) // (SWA_KV * SWA_HD)
    kvw = SWA_KV * SWA_HD

    def nbr(j, col):
        return lambda b, n: (b * nb + jnp.clip(n * qpb + j, 0, nb - 1), col)

    nbrs = range(-1, qpb + 1)

    cos, sin = tables
    sink_rows = jnp.broadcast_to((sink.astype(F32) * math.log2(math.e))[:, None], (SWA_HEADS, LANES))
    tab_spec = pl.BlockSpec(cos.shape, lambda b, n: (0, 0))
    out = pl.pallas_call(
        functools.partial(_swa_kernel, local, L, qrows),
        out_shape=jax.ShapeDtypeStruct((B * L, SWA_HEADS * SWA_HD), F32),
        grid=(B, nq),
        in_specs=[pl.BlockSpec((qrows, SWA_HEADS * SWA_HD), lambda b, n: (b * nq + n, qcol))]
        + [pl.BlockSpec((blk, kvw), nbr(j, kcol)) for j in nbrs]
        + [pl.BlockSpec((blk, kvw), nbr(j, kcol + 1)) for j in nbrs]
        + [pl.BlockSpec((C, kvw), lambda b, n: (b, kcol)),
           pl.BlockSpec((C, kvw), lambda b, n: (b, kcol + 1)),
           tab_spec, tab_spec,
           pl.BlockSpec((SWA_HEADS, LANES), lambda b, n: (0, 0))],
        out_specs=pl.BlockSpec((qrows, SWA_HEADS * SWA_HD), lambda b, n: (b * nq + n, 0)),
        compiler_params=pltpu.CompilerParams(
            dimension_semantics=("parallel", "parallel"), vmem_limit_bytes=VMEM_LIMIT),
        name="swa",
    )(p, *([p] * (2 * len(nbrs))), pc, pc, cos, sin, sink_rows)
    return out.reshape(B, L, SWA_HEADS * SWA_HD)


def swa_branch(p, pc, B, L, C, need_ctx, sink, tables):
    y = swa_attend(p, pc, B, L, C, sink, tables, True)
    yc = swa_attend(pc, pc, B, C, C, sink, tables, False) if need_ctx else None
    return y, yc


def _head_blocks(value):
    row = lax.broadcasted_iota(jnp.int32, (RW_W, RW_W), 0)
    col = lax.broadcasted_iota(jnp.int32, (RW_W, RW_W), 1)
    return jnp.where(row // RW_HD == col // RW_HD, value, 0.0).astype(BF16)


def _head_sum(x, blocks):
    hi = x.astype(BF16)
    lo = (x - hi.astype(F32)).astype(BF16)
    return (jnp.dot(hi, blocks, preferred_element_type=F32)
            + jnp.dot(lo, blocks, preferred_element_type=F32))


RW_OFF = tuple(int(o) for o in np.cumsum((0,) + RW_SIZES))
RW_PAD = 1280
RW_WIN_WD = RW_OFF[3]
RW_WIN_LO = RW_OFF[5]
RW_WIN_HI = RW_WIN_LO + LANES


def _rwkv_prep_kernel(tm, tiles_per_seq, x_ref, xp_ref, xn_ref, mu_ref, w0_ref, w2f_ref, w2b_ref,
                      a0_ref, a2_ref, g2f_ref, g2b_ref, kk_ref, ka_ref,
                      r_o, k_o, v_o, kkn_o, b_o, df_o, db_o, gf_o, gb_o):
    i = pl.program_id(0)
    first = (i % tiles_per_seq) == 0
    last = (i % tiles_per_seq) == tiles_per_seq - 1
    x = x_ref[...].astype(F32)
    row = lax.broadcasted_iota(jnp.int32, x.shape, 0)
    prev_row = jnp.where(first, 0.0, xp_ref[...].astype(F32)[P_HALO - 1:P_HALO, :])
    next_row = jnp.where(last, 0.0, xn_ref[...].astype(F32)[0:1, :])
    xm1 = jnp.where(row == 0, prev_row, pltpu.roll(x, 1, 0))
    xp1 = jnp.where(row == tm - 1, next_row, pltpu.roll(x, tm - 1, 0))
    x = x + (0.5 * (xm1 + xp1) - x) * mu_ref[...]

    r = x[:, RW_OFF[0]:RW_OFF[1]]
    k = x[:, RW_OFF[1]:RW_OFF[2]]
    v = x[:, RW_OFF[2]:RW_OFF[3]]
    wd = jnp.tanh(x[:, RW_WIN_WD:RW_WIN_WD + LANES]).astype(BF16)
    lo = x[:, RW_WIN_LO:RW_WIN_LO + RW_W]
    hi = x[:, RW_WIN_HI:RW_WIN_HI + RW_W]

    def decay(w2_ref, w0):
        wlog = -jax.nn.softplus(-(w0 + jnp.dot(wd, w2_ref[...], preferred_element_type=F32))) - 0.5
        return jnp.exp(-jnp.exp(wlog))

    a = jax.nn.sigmoid(a0_ref[...] + jnp.dot(lo[:, 0:LANES].astype(BF16), a2_ref[...],
                                             preferred_element_type=F32))
    kk = k * kk_ref[...]
    norm = jnp.sqrt(_head_sum(kk * kk, _head_blocks(1.0)))
    kk = kk / jnp.maximum(norm, 1e-12)
    r_o[...] = r
    k_o[...] = k * (1.0 + (a - 1.0) * ka_ref[...])
    v_o[...] = v
    kkn_o[...] = kk
    b_o[...] = kk * a
    df_o[...] = decay(w2f_ref, w0_ref[0:1, :])
    db_o[...] = decay(w2b_ref, w0_ref[1:2, :])
    gf_o[...] = jnp.dot(jax.nn.sigmoid(lo).astype(BF16), g2f_ref[...], preferred_element_type=F32)
    gb_o[...] = jnp.dot(jax.nn.sigmoid(hi).astype(BF16), g2b_ref[...], preferred_element_type=F32)


def rwkv_prep(p, B, L, mu, w0, w2, a0, a2, g2, k_k, k_a):
    tm = min(RW_PREP_TM, L)
    tiles_per_seq = L // tm
    n_tok = B * L
    hb = tm // P_HALO
    col = P_RW // RW_PAD

    def rows(n, r0, src):
        return jnp.zeros((n, RW_W), F32).at[r0:r0 + src.shape[0]].set(src).astype(BF16)

    w2f = rows(LANES, RW_OFF[3] - RW_WIN_WD, w2[0])
    w2b = rows(LANES, RW_OFF[4] - RW_WIN_WD, w2[1])
    a2p = rows(LANES, RW_OFF[5] - RW_WIN_LO, a2)
    g2f = rows(RW_W, RW_OFF[6] - RW_WIN_LO, g2[0])
    g2b = rows(RW_W, RW_OFF[7] - RW_WIN_HI, g2[1])
    mu_p = jnp.pad(mu, (0, RW_PAD - mu.shape[0])).reshape(1, RW_PAD)

    def const(shape):
        return pl.BlockSpec(shape, lambda i: (0,) * len(shape))

    vec = const((1, RW_W))
    out_spec = pl.BlockSpec((tm, RW_W), lambda i: (i, 0))
    return pl.pallas_call(
        functools.partial(_rwkv_prep_kernel, tm, tiles_per_seq),
        out_shape=[jax.ShapeDtypeStruct((n_tok, RW_W), F32)] * 9,
        grid=(n_tok // tm,),
        in_specs=[pl.BlockSpec((tm, RW_PAD), lambda i: (i, col)),
                  pl.BlockSpec((P_HALO, RW_PAD), lambda i: (jnp.maximum(i * hb - 1, 0), col)),
                  pl.BlockSpec((P_HALO, RW_PAD), lambda i: (jnp.minimum((i + 1) * hb, n_tok // P_HALO - 1), col)),
                  const((1, RW_PAD)), const((2, RW_W)), const((LANES, RW_W)), const((LANES, RW_W)),
                  vec, const((LANES, RW_W)), const((RW_W, RW_W)), const((RW_W, RW_W)), vec, vec],
        out_specs=[out_spec] * 9,
        compiler_params=pltpu.CompilerParams(
            dimension_semantics=("parallel",), vmem_limit_bytes=VMEM_LIMIT),
        name="rwkv_prep",
    )(p, p, p, mu_p, w0, w2f, w2b, a0.reshape(1, RW_W), a2p, g2f, g2b,
      k_k.reshape(1, RW_W), k_a.reshape(1, RW_W))


def _rwkv_scan_kernel(tblk, nb,
                      wf_ref, kkf_ref, bf_ref, kf_ref, vf_ref, rf_ref,
                      wb_ref, kkb_ref, bb_ref, kb_ref, vb_ref, rb_ref,
                      of_ref, ob_ref, sf_ref, sb_ref, sf16_ref, sb16_ref):
    c = pl.program_id(0)

    @pl.when(c == 0)
    def _():
        for ref in (sf_ref, sb_ref, sf16_ref, sb16_ref):
            ref[...] = jnp.zeros_like(ref)

    row = lax.broadcasted_iota(jnp.int32, (RW_W, RW_W), 0)
    col = lax.broadcasted_iota(jnp.int32, (RW_W, RW_W), 1)
    ones_blk = jnp.where(row // RW_HD == col // RW_HD, 1.0, 0.0).astype(BF16)
    r64 = lax.broadcasted_iota(jnp.int32, (RW_HD, RW_W), 0)
    c64 = lax.broadcasted_iota(jnp.int32, (RW_HD, RW_W), 1)
    eye_t = jnp.where(r64 == c64 % RW_HD, 1.0, 0.0).astype(BF16)
    r8 = lax.broadcasted_iota(jnp.int32, (8, RW_W), 0)
    c8 = lax.broadcasted_iota(jnp.int32, (8, RW_W), 1)
    head_sel = jnp.where(r8 == c8 // RW_HD, 1.0, 0.0).astype(BF16)

    fwd = (wf_ref, kkf_ref, bf_ref, kf_ref, vf_ref, rf_ref, sf_ref, sf16_ref)
    bwd = (wb_ref, kkb_ref, bb_ref, kb_ref, vb_ref, rb_ref, sb_ref, sb16_ref)
    nrow = nb * RW_HD

    def one_dir(idx, refs):
        w_ref, kk_ref, b_ref, k_ref, v_ref, r_ref, s_ref, s16_ref = refs
        zs = []
        for b in range(nb):
            kk16 = kk_ref[idx, b:b + 1, :].astype(BF16)
            v16 = v_ref[idx, b:b + 1, :].astype(BF16)
            lhs = jnp.concatenate([s16_ref[b] * kk16, eye_t * v16], axis=0)
            red = jnp.dot(lhs, ones_blk, preferred_element_type=F32)
            s_new = (s_ref[b] * w_ref[idx, b:b + 1, :] - red[0:RW_HD] * b_ref[idx, b:b + 1, :]
                     + red[RW_HD:2 * RW_HD] * k_ref[idx, b:b + 1, :])
            s_ref[b] = s_new
            s16 = s_new.astype(BF16)
            s16_ref[b] = s16
            zs.append(s16 * r_ref[idx, b:b + 1, :].astype(BF16))
        return jnp.concatenate(zs, axis=0)

    def emit(idx, z, o_ref):
        o = lax.dot_general(head_sel, z, (((1,), (1,)), ((), ())), preferred_element_type=F32)
        o_ref[idx] = o[0:RW_HEADS]

    def step(i, carry):
        z_f, z_b = carry
        ib = tblk - 1 - i
        emit(jnp.maximum(i - 1, 0), z_f, of_ref)
        emit(jnp.minimum(ib + 1, tblk - 1), z_b, ob_ref)
        return one_dir(i, fwd), one_dir(ib, bwd)

    z0 = jnp.zeros((nrow, RW_W), BF16)
    z_f, z_b = lax.fori_loop(0, tblk, step, (z0, z0), unroll=8)
    emit(tblk - 1, z_f, of_ref)
    emit(0, z_b, ob_ref)


def rwkv_scan_pallas(w_f, w_b, kk, bvec, k, v, r, n_ctx):
    n, nb, _ = kk.shape
    tblk = RW_TBLK
    nblk = n // tblk
    nblk_ctx = n_ctx // tblk

    def fmap(c):
        return (c, 0, 0)

    def bmap(c):
        return (jnp.where(c < nblk_ctx, nblk_ctx - 1 - c, nblk + nblk_ctx - 1 - c), 0, 0)

    blk = (tblk, nb, RW_W)
    oblk = (tblk, RW_HEADS, nb * RW_HD)
    return pl.pallas_call(
        functools.partial(_rwkv_scan_kernel, tblk, nb),
        out_shape=[jax.ShapeDtypeStruct((n, RW_HEADS, nb * RW_HD), F32)] * 2,
        grid=(nblk,),
        in_specs=[pl.BlockSpec(blk, fmap)] * 6 + [pl.BlockSpec(blk, bmap)] * 6,
        out_specs=[pl.BlockSpec(oblk, fmap), pl.BlockSpec(oblk, bmap)],
        scratch_shapes=[pltpu.VMEM((nb, RW_HD, RW_W), F32), pltpu.VMEM((nb, RW_HD, RW_W), F32),
                        pltpu.VMEM((nb, RW_HD, RW_W), BF16), pltpu.VMEM((nb, RW_HD, RW_W), BF16)],
        compiler_params=pltpu.CompilerParams(dimension_semantics=("arbitrary",),
                                             vmem_limit_bytes=VMEM_LIMIT),
        name="rwkv_scan",
    )(w_f, kk, bvec, k, v, r, w_b, kk, bvec, k, v, r)


def _rwkv_post_kernel(of_ref, ob_ref, r_ref, k_ref, v_ref, gf_ref, gb_ref, rk_ref, lng_ref, lnb_ref,
                      y_ref):
    mean_blk = _head_blocks(1.0 / RW_HD)

    def gn(o):
        oc = o - _head_sum(o, mean_blk)
        var = _head_sum(oc * oc, mean_blk)
        return oc * lax.rsqrt(var + RW_GN_EPS) * lng_ref[...] + lnb_ref[...]

    bonus = _head_sum(r_ref[...] * k_ref[...] * rk_ref[...], _head_blocks(1.0)) * v_ref[...]
    y_ref[...] = (gn(of_ref[...]) + bonus) * gf_ref[...] + (gn(ob_ref[...]) + bonus) * gb_ref[...]


def rwkv_post(o_f, o_b, r, k, v, g_f, g_b, r_k, lnx_g, lnx_b, L, n_all, start):
    n_tok = r.shape[0]
    tm = min(RW_POST_TM, L)
    assert start % tm == 0 and n_all % tm == 0 and L % tm == 0
    tiles_per_seq = L // tm
    tok = pl.BlockSpec((tm, RW_W), lambda i: (i, 0))
    o_spec = pl.BlockSpec(
        (tm, RW_W), lambda i: ((i // tiles_per_seq) * (n_all // tm) + start // tm + i % tiles_per_seq, 0))
    vec = pl.BlockSpec((1, RW_W), lambda i: (0, 0))
    return pl.pallas_call(
        _rwkv_post_kernel,
        out_shape=jax.ShapeDtypeStruct((n_tok, RW_W), F32),
        grid=(n_tok // tm,),
        in_specs=[o_spec] * 2 + [tok] * 5 + [vec] * 3,
        out_specs=tok,
        compiler_params=pltpu.CompilerParams(
            dimension_semantics=("parallel",), vmem_limit_bytes=VMEM_LIMIT),
        name="rwkv_post",
    )(o_f, o_b, r, k, v, g_f, g_b, r_k.reshape(1, RW_W), lnx_g.reshape(1, RW_W), lnx_b.reshape(1, RW_W))


def rwkv_branch(p, pc, B, L, C, need_ctx, mu, w0, w2, a0, a2, g2, k_k, k_a, r_k, lnx_g, lnx_b):
    r, k, v, kk, bv, d_f, d_b, g_f, g_b = rwkv_prep(p, B, L, mu, w0, w2, a0, a2, g2, k_k, k_a)
    rc, kc, vc, kkc, bc, dc_f, dc_b, gc_f, gc_b = rwkv_prep(pc, B, C, mu, w0, w2, a0, a2, g2, k_k, k_a)

    def tmajor(xc, xl):
        t = jnp.concatenate([xc.reshape(B, C, RW_W), xl.reshape(B, L, RW_W)], axis=1)
        return jnp.moveaxis(t, 1, 0)

    pf, pb = rwkv_scan_pallas(tmajor(dc_f, d_f), tmajor(dc_b, d_b), tmajor(kkc, kk), tmajor(bc, bv),
                              tmajor(kc, k), tmajor(vc, v), tmajor(rc, r), C)

    def bmajor(t):
        t = t.reshape(C + L, RW_HEADS, B, RW_HD)
        return jnp.transpose(t, (2, 0, 1, 3)).reshape(B * (C + L), RW_W)

    pf, pb = bmajor(pf), bmajor(pb)
    y = rwkv_post(pf, pb, r, k, v, g_f, g_b, r_k, lnx_g, lnx_b, L, C + L, C).reshape(B, L, RW_W)
    yc = None
    if need_ctx:
        yc = rwkv_post(pf, pb, rc, kc, vc, gc_f, gc_b, r_k, lnx_g, lnx_b, C, C + L, 0).reshape(B, C, RW_W)
    return y, yc


def _diff_attn_kernel(q_ref, kt_ref, v_ref, lam_ref, gain_ref, o_ref):
    q = q_ref[0]
    v = v_ref[0]
    lam = lam_ref[...]
    tq = q.shape[0]
    head_of_lane = lax.broadcasted_iota(jnp.int32, (1, DF_W), 1) // DF_VD
    acc = jnp.zeros((tq, DF_W), F32)
    for h in range(DF_HEADS):
        parts = []
        for c in range(2):
            j = 2 * h + c
            s = jnp.dot(q[:, DF_HD * j:DF_HD * (j + 1)], kt_ref[0, j],
                        preferred_element_type=F32)
            p = jnp.exp2(s - jnp.max(s, -1, keepdims=True))
            l = jnp.sum(p, -1, keepdims=True)
            pv = jnp.dot(p.astype(BF16), v, preferred_element_type=F32)
            parts.append(pv / l)
        acc = jnp.where(head_of_lane == h, parts[0] - lam * parts[1], acc)
    row = lax.broadcasted_iota(jnp.int32, (DF_W, DF_W), 0)
    col = lax.broadcasted_iota(jnp.int32, (DF_W, DF_W), 1)
    seg_mean = jnp.where(row // DF_VD == col // DF_VD, 1.0 / DF_VD, 0.0).astype(F32)
    ms = jnp.dot(acc * acc, seg_mean, preferred_element_type=F32, precision=lax.Precision.HIGHEST)
    o_ref[0] = acc * lax.rsqrt(ms + 1e-5) * gain_ref[...]


def _diff_pre_kernel(rope, x_ref, cos_ref, sin_ref, q_ref, kt_ref, v_ref):
    x = x_ref[...].astype(F32)
    q, k, v = x[:, 0:DF_W], x[:, DF_W:2 * DF_W], x[:, 2 * DF_W:3 * DF_W]
    if rope:
        cos, sin = cos_ref[...], sin_ref[...]
        q = _rope(q, cos, sin, DF_HD)
        k = _rope(k, cos, sin, DF_HD)
    q_ref[...] = (q * (DF_HD ** -0.5 * math.log2(math.e))).astype(BF16)
    kt_ref[0] = k.T.astype(BF16)
    v_ref[...] = v.astype(BF16)


def diff_pre(p, B, L, tables, rope):
    tm = min(RW_PREP_TM, L)
    tiles_per_seq = L // tm
    n_tok = B * L
    width = 3 * DF_W
    col = P_DF // width
    cos, sin = tables
    tok = pl.BlockSpec((tm, DF_W), lambda i: (i, 0))
    tab = pl.BlockSpec((tm, DF_W), lambda i: (i % tiles_per_seq, 0))
    q, kt, v = pl.pallas_call(
        functools.partial(_diff_pre_kernel, rope),
        out_shape=[jax.ShapeDtypeStruct((n_tok, DF_W), BF16),
                   jax.ShapeDtypeStruct((B, DF_W, L), BF16),
                   jax.ShapeDtypeStruct((n_tok, DF_W), BF16)],
        grid=(n_tok // tm,),
        in_specs=[pl.BlockSpec((tm, width), lambda i: (i, col)), tab, tab],
        out_specs=[tok, pl.BlockSpec((1, DF_W, tm), lambda i: (i // tiles_per_seq, 0, i % tiles_per_seq)),
                   tok],
        compiler_params=pltpu.CompilerParams(
            dimension_semantics=("parallel",), vmem_limit_bytes=VMEM_LIMIT),
        name="diff_pre",
    )(p, cos, sin)
    return q.reshape(B, L, DF_W), kt, v.reshape(B, L, DF_W)


def diff_attend(qs, kt, vb, lam, lam_init, subln_g):
    B, Lq = qs.shape[:2]
    S = vb.shape[1]
    tq = min(DF_TQ, Lq)
    kt = kt.reshape(B, 2 * DF_HEADS, DF_HD, S)
    gain = jnp.tile(subln_g.astype(F32) * (1.0 - lam_init), DF_HEADS).reshape(1, DF_W)
    return pl.pallas_call(
        _diff_attn_kernel,
        out_shape=jax.ShapeDtypeStruct((B, Lq, DF_W), F32),
        grid=(B, Lq // tq),
        in_specs=[pl.BlockSpec((1, tq, DF_W), lambda b, i: (b, i, 0)),
                  pl.BlockSpec((1, 2 * DF_HEADS, DF_HD, S), lambda b, i: (b, 0, 0, 0)),
                  pl.BlockSpec((1, S, DF_W), lambda b, i: (b, 0, 0)),
                  pl.BlockSpec((1, 1), lambda b, i: (0, 0)),
                  pl.BlockSpec((1, DF_W), lambda b, i: (0, 0))],
        out_specs=pl.BlockSpec((1, tq, DF_W), lambda b, i: (b, i, 0)),
        compiler_params=pltpu.CompilerParams(
            dimension_semantics=("parallel", "parallel"), vmem_limit_bytes=VMEM_LIMIT),
        name="diff_attn",
    )(qs, kt, vb, lam.reshape(1, 1).astype(F32), gain)


def diff_branch(p, pc, B, L, C, need_ctx, tables, lq1, lk1, lq2, lk2, subln_g, lam_init):
    lam = (jnp.exp(jnp.sum(lq1.astype(F32) * lk1.astype(F32)))
           - jnp.exp(jnp.sum(lq2.astype(F32) * lk2.astype(F32))) + lam_init)
    q, kt, v = diff_pre(p, B, L, tables, True)
    qc, ktc, vc = diff_pre(pc, B, C, tables, False)
    y = diff_attend(q, jnp.concatenate([kt, ktc], axis=2), jnp.concatenate([v, vc], axis=1),
                    lam, lam_init, subln_g)
    yc = diff_attend(qc, ktc, vc, lam, lam_init, subln_g) if need_ctx else None
    return y, yc


def _ln_rows(x):
    mu = jnp.mean(x, -1, keepdims=True)
    xc = x - mu
    return xc * lax.rsqrt(jnp.mean(xc * xc, -1, keepdims=True) + LN_EPS)


def _ffn_kernel(tm, tiles_per_seq, n_chunks,
                h_ref, hprev_ref, hnext_ref, sh_ref, sc_ref, gate_ref,
                wa_ref, wb_ref, cwa_ref, cwb_ref, cba_ref, cbb_ref, wd_ref, lng_ref, lnb_ref,
                o_ref, u_ref, acc_ref):
    i = pl.program_id(0)
    j = pl.program_id(1)
    n = tm + 2 * HALO

    @pl.when(j == 0)
    def _():
        scale = 1.0 + sc_ref[0]
        shift = sh_ref[0]
        first = (i % tiles_per_seq) == 0
        last = (i % tiles_per_seq) == tiles_per_seq - 1
        u_ref[0:tm, :] = (_ln_rows(h_ref[...]) * scale + shift).astype(BF16)
        un = _ln_rows(hnext_ref[...]) * scale + shift
        up = _ln_rows(hprev_ref[...]) * scale + shift
        u_ref[tm:tm + HALO, :] = jnp.where(last, 0.0, un).astype(BF16)
        u_ref[tm + HALO:n, :] = jnp.where(first, 0.0, up).astype(BF16)
        acc_ref[...] = jnp.zeros_like(acc_ref)

    u = u_ref[...]

    def conv(w_ref, cw_ref, cb_ref):
        x = jnp.dot(u, w_ref[...], preferred_element_type=F32)
        cw = cw_ref[...]
        y = (pltpu.roll(x, 1, 0)[0:tm] * cw[0:1] + x[0:tm] * cw[1:2]
             + pltpu.roll(x, n - 1, 0)[0:tm] * cw[2:3] + cb_ref[...])
        return y

    a = conv(wa_ref, cwa_ref, cba_ref)
    b = conv(wb_ref, cwb_ref, cbb_ref)
    g = (a * jax.nn.sigmoid(a) * b).astype(BF16)
    acc_ref[...] += jnp.dot(g, wd_ref[...], preferred_element_type=F32)

    @pl.when(j == n_chunks - 1)
    def _():
        y = DN_ALPHA * h_ref[...] + gate_ref[0] * acc_ref[...]
        o_ref[...] = _ln_rows(y) * lng_ref[...] + lnb_ref[...]


def ffn_block(h, shift, scale, gate, w_up, conv_w, conv_b, w_down, ln_g, ln_b):
    B, L, D = h.shape
    tm = min(FFN_TM, L)
    tiles_per_seq = L // tm
    n_tok = B * L
    n_chunks = D_FF // FFN_CHUNK
    hb = tm // HALO

    def bcast(t):
        return jnp.broadcast_to(t.astype(F32), (B, 1, D))

    wu = w_up.astype(BF16)
    wd = w_down.astype(BF16)
    cb = conv_b.reshape(1, 2 * D_FF)
    h2 = h.reshape(n_tok, D)
    mod_spec = pl.BlockSpec((1, 1, D), lambda i, j: (i // tiles_per_seq, 0, 0))
    vec_spec = pl.BlockSpec((1, D), lambda i, j: (0, 0))
    out = pl.pallas_call(
        functools.partial(_ffn_kernel, tm, tiles_per_seq, n_chunks),
        out_shape=jax.ShapeDtypeStruct((n_tok, D), F32),
        grid=(n_tok // tm, n_chunks),
        in_specs=[pl.BlockSpec((tm, D), lambda i, j: (i, 0)),
                  pl.BlockSpec((HALO, D), lambda i, j: (jnp.maximum(i * hb - 1, 0), 0)),
                  pl.BlockSpec((HALO, D), lambda i, j: (jnp.minimum((i + 1) * hb, n_tok // HALO - 1), 0)),
                  mod_spec, mod_spec, mod_spec,
                  pl.BlockSpec((D, FFN_CHUNK), lambda i, j: (0, j)),
                  pl.BlockSpec((D, FFN_CHUNK), lambda i, j: (0, n_chunks + j)),
                  pl.BlockSpec((3, FFN_CHUNK), lambda i, j: (0, j)),
                  pl.BlockSpec((3, FFN_CHUNK), lambda i, j: (0, n_chunks + j)),
                  pl.BlockSpec((1, FFN_CHUNK), lambda i, j: (0, j)),
                  pl.BlockSpec((1, FFN_CHUNK), lambda i, j: (0, n_chunks + j)),
                  pl.BlockSpec((FFN_CHUNK, D), lambda i, j: (j, 0)),
                  vec_spec, vec_spec],
        out_specs=pl.BlockSpec((tm, D), lambda i, j: (i, 0)),
        scratch_shapes=[pltpu.VMEM((tm + 2 * HALO, D), BF16), pltpu.VMEM((tm, D), F32)],
        compiler_params=pltpu.CompilerParams(
            dimension_semantics=("parallel", "arbitrary"), vmem_limit_bytes=VMEM_LIMIT),
        name="conv_ffn",
    )(h2, h2, h2, bcast(shift), bcast(scale), bcast(gate), wu, wu, conv_w, conv_w, cb, cb, wd,
      ln_g.reshape(1, D), ln_b.reshape(1, D))
    return out.reshape(B, L, D)


P_GATE, P_HY, P_RW, P_SW, P_DF = 0, 4096, 5120, 6400, 6912
P_COLS = 7680
IN_CHUNK = 768


def _in_proj_kernel(x_ref, sh_ref, sc_ref, w_ref, o_ref, u_ref):
    @pl.when(pl.program_id(1) == 0)
    def _():
        u_ref[...] = (_ln_rows(x_ref[...]) * (1.0 + sc_ref[0]) + sh_ref[0]).astype(BF16)

    o_ref[...] = jnp.dot(u_ref[...], w_ref[...], preferred_element_type=F32).astype(BF16)


def in_proj(h, shift, scale, w_in_p):
    B, L, D = h.shape
    tm = min(IN_TM, L)
    tiles_per_seq = L // tm
    n_tok = B * L

    def bcast(t):
        return jnp.broadcast_to(t.astype(F32), (B, 1, D))

    mod_spec = pl.BlockSpec((1, 1, D), lambda i, j: (i // tiles_per_seq, 0, 0))
    return pl.pallas_call(
        _in_proj_kernel,
        out_shape=jax.ShapeDtypeStruct((n_tok, P_COLS), BF16),
        grid=(n_tok // tm, P_COLS // IN_CHUNK),
        in_specs=[pl.BlockSpec((tm, D), lambda i, j: (i, 0)), mod_spec, mod_spec,
                  pl.BlockSpec((D, IN_CHUNK), lambda i, j: (0, j))],
        out_specs=pl.BlockSpec((tm, IN_CHUNK), lambda i, j: (i, j)),
        scratch_shapes=[pltpu.VMEM((tm, D), BF16)],
        compiler_params=pltpu.CompilerParams(
            dimension_semantics=("parallel", "arbitrary"), vmem_limit_bytes=VMEM_LIMIT),
        name="in_proj",
    )(h.reshape(n_tok, D), bcast(shift), bcast(scale), w_in_p)


def pack_w_in(w):
    offs = np.cumsum((0,) + IN_SIZES)
    hy, sw, rw, df, gt = [w[:, offs[j]:offs[j + 1]] for j in range(len(IN_SIZES))]

    def padto(t, n):
        return jnp.pad(t, ((0, 0), (0, n - t.shape[1])))

    return jnp.concatenate([gt, padto(hy, P_RW - P_HY), padto(rw, P_SW - P_RW), sw, df],
                           axis=1).astype(BF16)


def _merge_kernel(yh_ref, ys_ref, yr_ref, yd_ref, g0_ref, g1_ref, g2_ref, g3_ref, h_ref, gate_ref,
                  wbr_ref, wo_ref, lng_ref, lnb_ref, o_ref):
    acc = None
    for j, (y_ref, g_ref) in enumerate(((yh_ref, g0_ref), (ys_ref, g1_ref), (yr_ref, g2_ref),
                                        (yd_ref, g3_ref))):
        term = jax.nn.sigmoid(g_ref[...].astype(F32)) * jnp.dot(y_ref[...].astype(BF16), wbr_ref[j],
                                                    preferred_element_type=F32)
        acc = term if acc is None else acc + term
    mix = jnp.dot(acc.astype(BF16), wo_ref[...], preferred_element_type=F32)
    y = DN_ALPHA * h_ref[...] + gate_ref[0] * mix
    o_ref[...] = _ln_rows(y) * lng_ref[...] + lnb_ref[...]


def merge_block(ys, p, h, gate, w_br, w_o, ln_g, ln_b):
    B, L, D = h.shape
    n_tok = B * L
    tm = min(MERGE_TM, L)
    tiles_per_seq = L // tm
    y_spec = pl.BlockSpec((tm, BR_W), lambda i: (i, 0))
    vec_spec = pl.BlockSpec((1, D), lambda i: (0, 0))
    g_specs = [pl.BlockSpec((tm, D), functools.partial(lambda j, i: (i, j), j)) for j in range(N_BRANCH)]
    out = pl.pallas_call(
        _merge_kernel,
        out_shape=jax.ShapeDtypeStruct((n_tok, D), F32),
        grid=(n_tok // tm,),
        in_specs=[y_spec] * N_BRANCH + g_specs + [
            pl.BlockSpec((tm, D), lambda i: (i, 0)),
            pl.BlockSpec((1, 1, D), lambda i: (i // tiles_per_seq, 0, 0)),
            pl.BlockSpec((N_BRANCH, BR_W, D), lambda i: (0, 0, 0)),
            pl.BlockSpec((D, D), lambda i: (0, 0)), vec_spec, vec_spec],
        out_specs=pl.BlockSpec((tm, D), lambda i: (i, 0)),
        compiler_params=pltpu.CompilerParams(
            dimension_semantics=("parallel",), vmem_limit_bytes=VMEM_LIMIT),
        name="merge",
    )(*[y.reshape(n_tok, BR_W) for y in ys], p, p, p, p, h.reshape(n_tok, D),
      jnp.broadcast_to(gate.astype(F32), (B, 1, D)), w_br.astype(BF16), w_o.astype(BF16),
      ln_g.reshape(1, D), ln_b.reshape(1, D))
    return out.reshape(B, L, D)


def kernel(x, c, ctx, c_ctx, ada_w, ada_b, w_in, hy_conv_w, hy_conv_b, hy_f_w1, hy_f_b1,
           hy_f_w2, hy_f_b2, hy_f_w3, hy_f_freq, hy_bias, swa_sink, rwkv_mu, rwkv_w0, rwkv_w2,
           rwkv_a0, rwkv_a2, rwkv_g2, rwkv_kk, rwkv_ka, rwkv_rk, rwkv_lnx_g, rwkv_lnx_b,
           diff_lq1, diff_lk1, diff_lq2, diff_lk2, diff_subln_g, w_branch, w_out, ln1_g, ln1_b,
           ffn_w_up, ffn_conv_w, ffn_conv_b, ffn_w_down, ln2_g, ln2_b):
    h, hc = x, ctx
    mats, mats_c = dft_mats(x.shape[1]), dft_mats(ctx.shape[1])
    swa_tables = rope_tables(x.shape[1], SWA_HD, SWA_HEADS * SWA_HD)
    df_tables = rope_tables(x.shape[1], DF_HD, DF_W)
    s_lat = jax.nn.silu(c)
    s_ctx = jax.nn.silu(c_ctx)
    for i in range(DEPTH):
        need_ctx = i < DEPTH - 1
        mod = (s_lat @ ada_w[i] + ada_b[i])[:, None, :]
        mod_c = s_ctx @ ada_w[i] + ada_b[i]
        sh1, sc1, g1, sh2, sc2, g2 = jnp.split(mod, 6, -1)
        csh1, csc1, cg1, csh2, csc2, cg2 = jnp.split(mod_c, 6, -1)

        B, L, D = h.shape
        C = hc.shape[1]
        w_in_p = pack_w_in(w_in[i])
        p = in_proj(h, sh1, sc1, w_in_p)
        pc = in_proj(hc, csh1.reshape(1, 1, D), csc1.reshape(1, 1, D), w_in_p)

        y_hy, yc_hy = hyena_branch(p, pc, B, L, C, need_ctx, mats, mats_c, hy_conv_w[i], hy_conv_b[i],
                                   hy_f_w1[i], hy_f_b1[i], hy_f_w2[i], hy_f_b2[i], hy_f_w3[i],
                                   hy_f_freq[i], hy_bias[i])
        y_sw, yc_sw = swa_branch(p, pc, B, L, C, need_ctx, swa_sink[i], swa_tables)
        y_rw, yc_rw = rwkv_branch(p, pc, B, L, C, need_ctx, rwkv_mu[i], rwkv_w0[i], rwkv_w2[i],
                                  rwkv_a0[i], rwkv_a2[i], rwkv_g2[i], rwkv_kk[i], rwkv_ka[i],
                                  rwkv_rk[i], rwkv_lnx_g[i], rwkv_lnx_b[i])
        lam_init = 0.8 - 0.6 * math.exp(-0.3 * i)
        y_df, yc_df = diff_branch(p, pc, B, L, C, need_ctx, df_tables, diff_lq1[i], diff_lk1[i],
                                  diff_lq2[i], diff_lk2[i], diff_subln_g[i], lam_init)

        h = merge_block((y_hy, y_sw, y_rw, y_df), p, h, g1, w_branch[i], w_out[i], ln1_g[i], ln1_b[i])
        h = ffn_block(h, sh2, sc2, g2, ffn_w_up[i], ffn_conv_w[i], ffn_conv_b[i], ffn_w_down[i],
                      ln2_g[i], ln2_b[i])

        if need_ctx:
            hc = merge_block((yc_hy, yc_sw, yc_rw, yc_df), pc, hc, cg1.reshape(1, 1, D), w_branch[i],
                             w_out[i], ln1_g[i], ln1_b[i])
            hc = ffn_block(hc, csh2.reshape(1, 1, -1), csc2.reshape(1, 1, -1), cg2.reshape(1, 1, -1),
                           ffn_w_up[i], ffn_conv_w[i], ffn_conv_b[i], ffn_w_down[i], ln2_g[i], ln2_b[i])
    return h
```

```python
import functools
import math

import jax
import jax.numpy as jnp
import numpy as np
from jax import lax
from jax.experimental import pallas as pl
from jax.experimental.pallas import tpu as pltpu

D_MODEL = 1024
DEPTH = 2
GRID_W = 64
ROPE_BASE = 10000.0
F32 = jnp.float32
BF16 = jnp.bfloat16
NEG_INF = -1e30
LN_EPS = 1e-6

HY_W = 256
HY_ORDER = 2
HY_DIRS = 2
HY_EMB = 33
HY_MIN_DECAY = math.log(1e-2) / 1.5
HY_MAX_DECAY = math.log(1e-2) / 0.3

SWA_HEADS = 4
SWA_KV = 2
SWA_HD = 64
SWA_WIN = 128
SWA_BLOCK = 128

RW_HEADS = 4
RW_HD = 64
RW_W = RW_HEADS * RW_HD
RW_DECAY_R = 64
RW_AAA_R = 64
RW_GATE_R = 128
RW_GN_EPS = 64e-5

DF_HEADS = 4
DF_HD = 32
DF_VD = 2 * DF_HD
DF_W = DF_HEADS * DF_VD

N_BRANCH = 4
BR_W = 256
D_FF = 2816
DN_ALPHA = (2 * DEPTH) ** 0.25

SWA_SIZES = (SWA_HEADS * SWA_HD, SWA_KV * SWA_HD, SWA_KV * SWA_HD)
RW_SIZES = (RW_W, RW_W, RW_W, RW_DECAY_R, RW_DECAY_R, RW_AAA_R, RW_GATE_R, RW_GATE_R)
DF_SIZES = (DF_W, DF_W, DF_W)
IN_SIZES = (3 * HY_W, sum(SWA_SIZES), sum(RW_SIZES), sum(DF_SIZES), N_BRANCH * D_MODEL)

V7X_VMEM_BYTES = 64 * 1024 * 1024
LANES = 128
MM_TM = 512
MM_TN = 512
RW_TBLK = 64
DF_TQ = 256
SWA_QROWS = 256
HALO = 8
P_HALO = 16
IN_TM = 2048
FFN_TM = 1024
FFN_CHUNK = 256
MERGE_TM = 512
HY_TK = 512
RW_PREP_TM = 512
RW_POST_TM = 256
VMEM_LIMIT = V7X_VMEM_BYTES * 3 // 4


def _mm_kernel(x_ref, w_ref, o_ref):
    o_ref[...] = jnp.dot(x_ref[...].astype(BF16), w_ref[...], preferred_element_type=F32)


def _mm(x, w):
    m, k = x.shape
    n = w.shape[1]
    n_pad = -(-n // LANES) * LANES
    wb = w.astype(BF16)
    if n_pad != n:
        wb = jnp.pad(wb, ((0, 0), (0, n_pad - n)))
    tn = MM_TN if n_pad % MM_TN == 0 else (256 if n_pad % 256 == 0 else LANES)
    tm = MM_TM if m % MM_TM == 0 else m
    out = pl.pallas_call(
        _mm_kernel,
        out_shape=jax.ShapeDtypeStruct((m, n_pad), F32),
        grid=(m // tm, n_pad // tn),
        in_specs=[pl.BlockSpec((tm, k), lambda i, j: (i, 0)),
                  pl.BlockSpec((k, tn), lambda i, j: (0, j))],
        out_specs=pl.BlockSpec((tm, tn), lambda i, j: (i, j)),
        compiler_params=pltpu.CompilerParams(
            dimension_semantics=("parallel", "parallel"), vmem_limit_bytes=VMEM_LIMIT),
        name="matmul",
    )(x, wb)
    return out[:, :n] if n_pad != n else out


def dft_mats(L):
    tb = 64
    k = jnp.arange(L, dtype=jnp.int32)[:, None]

    def table(t):
        ang = (((2 * k + 1) * t[None, :]) % (4 * L)).astype(F32) * (math.pi / (2 * L))
        return jnp.cos(ang), jnp.sin(ang)

    ca, sa = table(tb * jnp.arange(L // tb, dtype=jnp.int32))
    cb, sb = table(jnp.arange(tb, dtype=jnp.int32))
    c = (ca[:, :, None] * cb[:, None, :] - sa[:, :, None] * sb[:, None, :]).reshape(L, L)
    s = (sa[:, :, None] * cb[:, None, :] + ca[:, :, None] * sb[:, None, :]).reshape(L, L)
    ct = (ca.T[:, None, :] * cb.T[None, :, :] - sa.T[:, None, :] * sb.T[None, :, :]).reshape(L, L)
    st = (sa.T[:, None, :] * cb.T[None, :, :] + ca.T[:, None, :] * sb.T[None, :, :]).reshape(L, L)
    return c.astype(BF16), s.astype(BF16), ct.astype(BF16), st.astype(BF16)


def hyena_kspec(L, mats, w1, b1, w2, b2, w3, freq):
    t = jnp.linspace(0.0, 1.0, L, dtype=F32)[:, None]
    bands = (HY_EMB - 1) // 2
    w = 2.0 * math.pi * jnp.arange(L, dtype=F32) / L
    fb = jnp.linspace(1e-4, bands - 1, bands, dtype=F32)
    ang = w[:, None] * fb[None, :]
    z = jnp.concatenate([t, jnp.cos(ang), -jnp.sin(ang)], -1)
    fr = freq.astype(F32)
    hdn = jnp.sin(fr * (z @ w1.astype(F32) + b1.astype(F32)))
    hdn = jnp.sin(fr * (hdn @ w2.astype(F32) + b2.astype(F32)))
    filt = (hdn @ w3.astype(F32)).reshape(L, HY_ORDER, HY_DIRS, HY_W)
    deltas = jnp.abs(jnp.linspace(HY_MIN_DECAY, HY_MAX_DECAY, HY_W, dtype=F32))
    filt = filt * jnp.exp(-t * deltas[None, :])[:, None, None, :]
    filt = filt / jnp.sum(jnp.abs(filt), axis=(0, 2), keepdims=True)
    fwd, bwd = filt[:, :, 0], filt[:, :, 1]
    nw = HY_ORDER * HY_W
    bwd = bwd.at[0].set(0.0)
    kr = _mm(mats[0], (fwd + bwd).reshape(L, nw))
    ki = -_mm(mats[1], (fwd - bwd).reshape(L, nw))

    def per_order(a):
        return jnp.transpose(a.reshape(L, HY_ORDER, HY_W), (1, 0, 2))

    return per_order(kr), per_order(ki)


def _hy_fwd_kernel(ac_ref, as_ref, z_ref, kr_ref, ki_ref, yr_ref, yi_ref):
    z = z_ref[0].astype(BF16)
    zr = jnp.dot(ac_ref[...], z, preferred_element_type=F32)
    zi = -jnp.dot(as_ref[...], z, preferred_element_type=F32)
    kr, ki = kr_ref[...], ki_ref[...]
    yr_ref[0] = (zr * kr - zi * ki).astype(BF16)
    yi_ref[0] = (zr * ki + zi * kr).astype(BF16)


def _hy_inv_kernel(inv_l, act_ref, ast_ref, yr_ref, yi_ref, z_ref, x_ref, bias_ref, o_ref, *o16_ref):
    y = (jnp.dot(act_ref[...], yr_ref[0], preferred_element_type=F32)
         - jnp.dot(ast_ref[...], yi_ref[0], preferred_element_type=F32))
    o = x_ref[0] * (y * inv_l + bias_ref[...] * z_ref[0])
    o_ref[0] = o
    for ref in o16_ref:
        ref[0] = o.astype(BF16)


def hyena_longconv(z, z16, xmul, mats, kr, ki, bias, want16):
    B, L, W = z.shape
    ac, as_, act, ast = mats
    tk = min(HY_TK, L)
    a_spec = pl.BlockSpec((tk, L), lambda i, b: (i, 0))
    full_spec = pl.BlockSpec((1, L, W), lambda i, b: (b, 0, 0))
    tile_spec = pl.BlockSpec((1, tk, W), lambda i, b: (b, i, 0))
    k_spec = pl.BlockSpec((tk, W), lambda i, b: (i, 0))
    params = pltpu.CompilerParams(dimension_semantics=("parallel", "arbitrary"),
                                  vmem_limit_bytes=VMEM_LIMIT)
    yr, yi = pl.pallas_call(
        _hy_fwd_kernel,
        out_shape=[jax.ShapeDtypeStruct((B, L, W), BF16)] * 2,
        grid=(L // tk, B),
        in_specs=[a_spec, a_spec, full_spec, k_spec, k_spec],
        out_specs=[tile_spec, tile_spec],
        compiler_params=params,
        name="hyena_fwd",
    )(ac, as_, z16, kr, ki)
    n_out = 2 if want16 else 1
    return pl.pallas_call(
        functools.partial(_hy_inv_kernel, 1.0 / L),
        out_shape=[jax.ShapeDtypeStruct((B, L, W), F32), jax.ShapeDtypeStruct((B, L, W), BF16)][:n_out],
        grid=(L // tk, B),
        in_specs=[a_spec, a_spec, full_spec, full_spec, tile_spec, tile_spec,
                  pl.BlockSpec((1, W), lambda i, b: (0, 0))],
        out_specs=[tile_spec] * n_out,
        compiler_params=params,
        name="hyena_inv",
    )(act, ast, yr, yi, z, xmul, bias.reshape(1, W))


def _hy_pre_kernel(tm, tiles_per_seq, x_ref, xp_ref, xn_ref, cw_ref, cb_ref, v_ref, v16_ref, x1_ref,
                   x2_ref):
    i = pl.program_id(0)
    first = (i % tiles_per_seq) == 0
    last = (i % tiles_per_seq) == tiles_per_seq - 1
    x = x_ref[...].astype(F32)
    row = lax.broadcasted_iota(jnp.int32, x.shape, 0)
    prev_row = jnp.where(first, 0.0, xp_ref[...].astype(F32)[P_HALO - 1:P_HALO, :])
    next_row = jnp.where(last, 0.0, xn_ref[...].astype(F32)[0:1, :])
    xm1 = jnp.where(row == 0, prev_row, pltpu.roll(x, 1, 0))
    xp1 = jnp.where(row == tm - 1, next_row, pltpu.roll(x, tm - 1, 0))
    cw = cw_ref[...]
    y = xm1 * cw[0:1] + x * cw[1:2] + xp1 * cw[2:3] + cb_ref[...]
    v_ref[...] = y[:, 0:HY_W]
    v16_ref[...] = y[:, 0:HY_W].astype(BF16)
    x1_ref[...] = y[:, HY_W:2 * HY_W]
    x2_ref[...] = y[:, 2 * HY_W:3 * HY_W]


def hyena_pre(p, B, L, conv_w, conv_b):
    tm = min(RW_PREP_TM, L)
    tiles_per_seq = L // tm
    n_tok = B * L
    hb = tm // P_HALO
    width = P_RW - P_HY
    col = P_HY // width
    cw = jnp.pad(conv_w, ((0, 0), (0, width - conv_w.shape[1])))
    cb = jnp.pad(conv_b, (0, width - conv_b.shape[0])).reshape(1, width)
    out_spec = pl.BlockSpec((tm, HY_W), lambda i: (i, 0))
    outs = pl.pallas_call(
        functools.partial(_hy_pre_kernel, tm, tiles_per_seq),
        out_shape=[jax.ShapeDtypeStruct((n_tok, HY_W), dt) for dt in (F32, BF16, F32, F32)],
        grid=(n_tok // tm,),
        in_specs=[pl.BlockSpec((tm, width), lambda i: (i, col)),
                  pl.BlockSpec((P_HALO, width), lambda i: (jnp.maximum(i * hb - 1, 0), col)),
                  pl.BlockSpec((P_HALO, width), lambda i: (jnp.minimum((i + 1) * hb, n_tok // P_HALO - 1), col)),
                  pl.BlockSpec((3, width), lambda i: (0, 0)),
                  pl.BlockSpec((1, width), lambda i: (0, 0))],
        out_specs=[out_spec] * 4,
        compiler_params=pltpu.CompilerParams(
            dimension_semantics=("parallel",), vmem_limit_bytes=VMEM_LIMIT),
        name="hyena_pre",
    )(p, p, p, cw, cb)
    return [o.reshape(B, L, HY_W) for o in outs]


def hyena_mix(p, B, L, conv_w, conv_b, mats, kspec, bias):
    v, v16, x1, x2 = hyena_pre(p, B, L, conv_w, conv_b)
    kr, ki = kspec
    zz, zz16 = hyena_longconv(v, v16, x1, mats, kr[0], ki[0], bias[0], True)
    return hyena_longconv(zz, zz16, x2, mats, kr[1], ki[1], bias[1], False)[0]


def hyena_branch(p, pc, B, L, C, need_ctx, mats, mats_c, conv_w, conv_b, fw1, fb1, fw2, fb2, fw3, ffreq,
                 bias):
    kspec = hyena_kspec(L, mats, fw1, fb1, fw2, fb2, fw3, ffreq)
    y = hyena_mix(p, B, L, conv_w, conv_b, mats, kspec, bias)
    yc = None
    if need_ctx:
        kspec_c = hyena_kspec(C, mats_c, fw1, fb1, fw2, fb2, fw3, ffreq)
        yc = hyena_mix(pc, B, C, conv_w, conv_b, mats_c, kspec_c, bias)
    return y, yc


def rope_tables(L, d, width):
    nf = d // 4
    pos = jnp.arange(L)
    inv = ROPE_BASE ** (-jnp.arange(nf, dtype=F32) / nf)
    cs, sn = [], []
    for p in (pos // GRID_W, pos % GRID_W):
        ang = p.astype(F32)[:, None] * inv[None, :]
        cs += [jnp.cos(ang), jnp.cos(ang)]
        sn += [-jnp.sin(ang), jnp.sin(ang)]
    reps = width // d
    return jnp.tile(jnp.concatenate(cs, -1), (1, reps)), jnp.tile(jnp.concatenate(sn, -1), (1, reps))


def _rope(x, cos, sin, d):
    width = x.shape[-1]
    q = d // 4
    lane = lax.broadcasted_iota(jnp.int32, x.shape, 1)
    swapped = jnp.where(lane % (2 * q) < q, pltpu.roll(x, width - q, 1), pltpu.roll(x, q, 1))
    return x * cos + swapped * sin


def _swa_kernel(local, seq_len, qrows, q_ref, *refs):
    nk = qrows // SWA_BLOCK + 2
    k_refs, v_refs = refs[0:nk], refs[nk:2 * nk]
    ck_ref, cv_ref, cos_ref, sin_ref, sink_ref, o_ref = refs[2 * nk:]
    n = pl.program_id(1)
    blk = SWA_BLOCK
    qscale = SWA_HD ** -0.5 * math.log2(math.e)
    q = q_ref[...].astype(F32)
    if local:
        q0 = pl.multiple_of(n * qrows, qrows)
        q = _rope(q, cos_ref[pl.ds(q0, qrows), :], sin_ref[pl.ds(q0, qrows), :], SWA_HD)
        first = n * (qrows // blk) - 1
        ks, vs = [], []
        for j, (k_ref, v_ref) in enumerate(zip(k_refs, v_refs)):
            k0 = pl.multiple_of(jnp.clip(first + j, 0, seq_len // blk - 1) * blk, blk)
            ks.append(_rope(k_ref[...].astype(F32), cos_ref[pl.ds(k0, blk), 0:SWA_KV * SWA_HD],
                            sin_ref[pl.ds(k0, blk), 0:SWA_KV * SWA_HD], SWA_HD).astype(BF16))
            vs.append(v_ref[...].astype(BF16))
        k_loc = jnp.concatenate(ks, axis=0)
        v_loc = jnp.concatenate(vs, axis=0)
        qpos = n * qrows + lax.broadcasted_iota(jnp.int32, (qrows, nk * blk), 0)
        kpos = first * blk + lax.broadcasted_iota(jnp.int32, (qrows, nk * blk), 1)
        valid = (jnp.abs(kpos - qpos) <= SWA_WIN) & (kpos >= 0) & (kpos < seq_len)
    qb = (q * qscale).astype(BF16)
    ck = ck_ref[...].astype(BF16)
    cv = cv_ref[...].astype(BF16)
    nt = (((1,), (1,)), ((), ()))
    outs = []
    for h in range(SWA_HEADS):
        g = h // (SWA_HEADS // SWA_KV)
        qh = qb[:, h * SWA_HD:(h + 1) * SWA_HD]
        gs = slice(g * SWA_HD, (g + 1) * SWA_HD)
        sink = sink_ref[h:h + 1, 0:1]
        s_ctx = lax.dot_general(qh, ck[:, gs], nt, preferred_element_type=F32)
        m = jnp.maximum(jnp.max(s_ctx, -1, keepdims=True), sink)
        if local:
            s_loc = lax.dot_general(qh, k_loc[:, gs], nt, preferred_element_type=F32)
            s_loc = jnp.where(valid, s_loc, NEG_INF)
            m = jnp.maximum(m, jnp.max(s_loc, -1, keepdims=True))
        p_ctx = jnp.exp2(s_ctx - m)
        den = jnp.sum(p_ctx, -1, keepdims=True) + jnp.exp2(sink - m)
        acc = jnp.dot(p_ctx.astype(BF16), cv[:, gs], preferred_element_type=F32)
        if local:
            p_loc = jnp.exp2(s_loc - m)
            den = den + jnp.sum(p_loc, -1, keepdims=True)
            acc = acc + jnp.dot(p_loc.astype(BF16), v_loc[:, gs], preferred_element_type=F32)
        outs.append(acc / den)
    o_ref[...] = jnp.concatenate(outs, axis=-1)


def swa_attend(p, pc, B, L, C, sink, tables, local):
    blk = SWA_BLOCK
    nb = L // blk
    qrows = min(SWA_QROWS, L)
    nq = L // qrows
    qpb = qrows // blk
    qcol = P_SW // (SWA_HEADS * SWA_HD)
    kcol = (P_SW + SWA_HEADS * SWA_HD) // (SWA_KV * SWA_HD)
    kvw = SWA_KV * SWA_HD

    def nbr(j, col):
        return lambda b, n: (b * nb + jnp.clip(n * qpb + j, 0, nb - 1), col)

    nbrs = range(-1, qpb + 1)

    cos, sin = tables
    sink_rows = jnp.broadcast_to((sink.astype(F32) * math.log2(math.e))[:, None], (SWA_HEADS, LANES))
    tab_spec = pl.BlockSpec(cos.shape, lambda b, n: (0, 0))
    out = pl.pallas_call(
        functools.partial(_swa_kernel, local, L, qrows),
        out_shape=jax.ShapeDtypeStruct((B * L, SWA_HEADS * SWA_HD), F32),
        grid=(B, nq),
        in_specs=[pl.BlockSpec((qrows, SWA_HEADS * SWA_HD), lambda b, n: (b * nq + n, qcol))]
        + [pl.BlockSpec((blk, kvw), nbr(j, kcol)) for j in nbrs]
        + [pl.BlockSpec((blk, kvw), nbr(j, kcol + 1)) for j in nbrs]
        + [pl.BlockSpec((C, kvw), lambda b, n: (b, kcol)),
           pl.BlockSpec((C, kvw), lambda b, n: (b, kcol + 1)),
           tab_spec, tab_spec,
           pl.BlockSpec((SWA_HEADS, LANES), lambda b, n: (0, 0))],
        out_specs=pl.BlockSpec((qrows, SWA_HEADS * SWA_HD), lambda b, n: (b * nq + n, 0)),
        compiler_params=pltpu.CompilerParams(
            dimension_semantics=("parallel", "parallel"), vmem_limit_bytes=VMEM_LIMIT),
        name="swa",
    )(p, *([p] * (2 * len(nbrs))), pc, pc, cos, sin, sink_rows)
    return out.reshape(B, L, SWA_HEADS * SWA_HD)


def swa_branch(p, pc, B, L, C, need_ctx, sink, tables):
    y = swa_attend(p, pc, B, L, C, sink, tables, True)
    yc = swa_attend(pc, pc, B, C, C, sink, tables, False) if need_ctx else None
    return y, yc


def _head_blocks(value):
    row = lax.broadcasted_iota(jnp.int32, (RW_W, RW_W), 0)
    col = lax.broadcasted_iota(jnp.int32, (RW_W, RW_W), 1)
    return jnp.where(row // RW_HD == col // RW_HD, value, 0.0).astype(BF16)


def _head_sum(x, blocks):
    hi = x.astype(BF16)
    lo = (x - hi.astype(F32)).astype(BF16)
    return (jnp.dot(hi, blocks, preferred_element_type=F32)
            + jnp.dot(lo, blocks, preferred_element_type=F32))


RW_OFF = tuple(int(o) for o in np.cumsum((0,) + RW_SIZES))
RW_PAD = 1280
RW_WIN_WD = RW_OFF[3]
RW_WIN_LO = RW_OFF[5]
RW_WIN_HI = RW_WIN_LO + LANES


def _rwkv_prep_kernel(tm, tiles_per_seq, x_ref, xp_ref, xn_ref, mu_ref, w0_ref, w2f_ref, w2b_ref,
                      a0_ref, a2_ref, g2f_ref, g2b_ref, kk_ref, ka_ref,
                      r_o, k_o, v_o, kkn_o, b_o, df_o, db_o, gf_o, gb_o):
    i = pl.program_id(0)
    first = (i % tiles_per_seq) == 0
    last = (i % tiles_per_seq) == tiles_per_seq - 1
    x = x_ref[...].astype(F32)
    row = lax.broadcasted_iota(jnp.int32, x.shape, 0)
    prev_row = jnp.where(first, 0.0, xp_ref[...].astype(F32)[P_HALO - 1:P_HALO, :])
    next_row = jnp.where(last, 0.0, xn_ref[...].astype(F32)[0:1, :])
    xm1 = jnp.where(row == 0, prev_row, pltpu.roll(x, 1, 0))
    xp1 = jnp.where(row == tm - 1, next_row, pltpu.roll(x, tm - 1, 0))
    x = x + (0.5 * (xm1 + xp1) - x) * mu_ref[...]

    r = x[:, RW_OFF[0]:RW_OFF[1]]
    k = x[:, RW_OFF[1]:RW_OFF[2]]
    v = x[:, RW_OFF[2]:RW_OFF[3]]
    wd = jnp.tanh(x[:, RW_WIN_WD:RW_WIN_WD + LANES]).astype(BF16)
    lo = x[:, RW_WIN_LO:RW_WIN_LO + RW_W]
    hi = x[:, RW_WIN_HI:RW_WIN_HI + RW_W]

    def decay(w2_ref, w0):
        wlog = -jax.nn.softplus(-(w0 + jnp.dot(wd, w2_ref[...], preferred_element_type=F32))) - 0.5
        return jnp.exp(-jnp.exp(wlog))

    a = jax.nn.sigmoid(a0_ref[...] + jnp.dot(lo[:, 0:LANES].astype(BF16), a2_ref[...],
                                             preferred_element_type=F32))
    kk = k * kk_ref[...]
    norm = jnp.sqrt(_head_sum(kk * kk, _head_blocks(1.0)))
    kk = kk / jnp.maximum(norm, 1e-12)
    r_o[...] = r
    k_o[...] = k * (1.0 + (a - 1.0) * ka_ref[...])
    v_o[...] = v
    kkn_o[...] = kk
    b_o[...] = kk * a
    df_o[...] = decay(w2f_ref, w0_ref[0:1, :])
    db_o[...] = decay(w2b_ref, w0_ref[1:2, :])
    gf_o[...] = jnp.dot(jax.nn.sigmoid(lo).astype(BF16), g2f_ref[...], preferred_element_type=F32)
    gb_o[...] = jnp.dot(jax.nn.sigmoid(hi).astype(BF16), g2b_ref[...], preferred_element_type=F32)


def rwkv_prep(p, B, L, mu, w0, w2, a0, a2, g2, k_k, k_a):
    tm = min(RW_PREP_TM, L)
    tiles_per_seq = L // tm
    n_tok = B * L
    hb = tm // P_HALO
    col = P_RW // RW_PAD

    def rows(n, r0, src):
        return jnp.zeros((n, RW_W), F32).at[r0:r0 + src.shape[0]].set(src).astype(BF16)

    w2f = rows(LANES, RW_OFF[3] - RW_WIN_WD, w2[0])
    w2b = rows(LANES, RW_OFF[4] - RW_WIN_WD, w2[1])
    a2p = rows(LANES, RW_OFF[5] - RW_WIN_LO, a2)
    g2f = rows(RW_W, RW_OFF[6] - RW_WIN_LO, g2[0])
    g2b = rows(RW_W, RW_OFF[7] - RW_WIN_HI, g2[1])
    mu_p = jnp.pad(mu, (0, RW_PAD - mu.shape[0])).reshape(1, RW_PAD)

    def const(shape):
        return pl.BlockSpec(shape, lambda i: (0,) * len(shape))

    vec = const((1, RW_W))
    out_spec = pl.BlockSpec((tm, RW_W), lambda i: (i, 0))
    return pl.pallas_call(
        functools.partial(_rwkv_prep_kernel, tm, tiles_per_seq),
        out_shape=[jax.ShapeDtypeStruct((n_tok, RW_W), F32)] * 9,
        grid=(n_tok // tm,),
        in_specs=[pl.BlockSpec((tm, RW_PAD), lambda i: (i, col)),
                  pl.BlockSpec((P_HALO, RW_PAD), lambda i: (jnp.maximum(i * hb - 1, 0), col)),
                  pl.BlockSpec((P_HALO, RW_PAD), lambda i: (jnp.minimum((i + 1) * hb, n_tok // P_HALO - 1), col)),
                  const((1, RW_PAD)), const((2, RW_W)), const((LANES, RW_W)), const((LANES, RW_W)),
                  vec, const((LANES, RW_W)), const((RW_W, RW_W)), const((RW_W, RW_W)), vec, vec],
        out_specs=[out_spec] * 9,
        compiler_params=pltpu.CompilerParams(
            dimension_semantics=("parallel",), vmem_limit_bytes=VMEM_LIMIT),
        name="rwkv_prep",
    )(p, p, p, mu_p, w0, w2f, w2b, a0.reshape(1, RW_W), a2p, g2f, g2b,
      k_k.reshape(1, RW_W), k_a.reshape(1, RW_W))


def _rwkv_scan_kernel(tblk, nb,
                      wf_ref, kkf_ref, bf_ref, kf_ref, vf_ref, rf_ref,
                      wb_ref, kkb_ref, bb_ref, kb_ref, vb_ref, rb_ref,
                      of_ref, ob_ref, sf_ref, sb_ref, sf16_ref, sb16_ref):
    c = pl.program_id(0)

    @pl.when(c == 0)
    def _():
        for ref in (sf_ref, sb_ref, sf16_ref, sb16_ref):
            ref[...] = jnp.zeros_like(ref)

    row = lax.broadcasted_iota(jnp.int32, (RW_W, RW_W), 0)
    col = lax.broadcasted_iota(jnp.int32, (RW_W, RW_W), 1)
    ones_blk = jnp.where(row // RW_HD == col // RW_HD, 1.0, 0.0).astype(BF16)
    r64 = lax.broadcasted_iota(jnp.int32, (RW_HD, RW_W), 0)
    c64 = lax.broadcasted_iota(jnp.int32, (RW_HD, RW_W), 1)
    eye_t = jnp.where(r64 == c64 % RW_HD, 1.0, 0.0).astype(BF16)
    r8 = lax.broadcasted_iota(jnp.int32, (8, RW_W), 0)
    c8 = lax.broadcasted_iota(jnp.int32, (8, RW_W), 1)
    head_sel = jnp.where(r8 == c8 // RW_HD, 1.0, 0.0).astype(BF16)

    fwd = (wf_ref, kkf_ref, bf_ref, kf_ref, vf_ref, rf_ref, sf_ref, sf16_ref)
    bwd = (wb_ref, kkb_ref, bb_ref, kb_ref, vb_ref, rb_ref, sb_ref, sb16_ref)
    nrow = nb * RW_HD

    def one_dir(idx, refs):
        w_ref, kk_ref, b_ref, k_ref, v_ref, r_ref, s_ref, s16_ref = refs
        zs = []
        for b in range(nb):
            kk16 = kk_ref[idx, b:b + 1, :].astype(BF16)
            v16 = v_ref[idx, b:b + 1, :].astype(BF16)
            lhs = jnp.concatenate([s16_ref[b] * kk16, eye_t * v16], axis=0)
            red = jnp.dot(lhs, ones_blk, preferred_element_type=F32)
            s_new = (s_ref[b] * w_ref[idx, b:b + 1, :] - red[0:RW_HD] * b_ref[idx, b:b + 1, :]
                     + red[RW_HD:2 * RW_HD] * k_ref[idx, b:b + 1, :])
            s_ref[b] = s_new
            s16 = s_new.astype(BF16)
            s16_ref[b] = s16
            zs.append(s16 * r_ref[idx, b:b + 1, :].astype(BF16))
        return jnp.concatenate(zs, axis=0)

    def emit(idx, z, o_ref):
        o = lax.dot_general(head_sel, z, (((1,), (1,)), ((), ())), preferred_element_type=F32)
        o_ref[idx] = o[0:RW_HEADS]

    def step(i, carry):
        z_f, z_b = carry
        ib = tblk - 1 - i
        emit(jnp.maximum(i - 1, 0), z_f, of_ref)
        emit(jnp.minimum(ib + 1, tblk - 1), z_b, ob_ref)
        return one_dir(i, fwd), one_dir(ib, bwd)

    z0 = jnp.zeros((nrow, RW_W), BF16)
    z_f, z_b = lax.fori_loop(0, tblk, step, (z0, z0), unroll=8)
    emit(tblk - 1, z_f, of_ref)
    emit(0, z_b, ob_ref)


def rwkv_scan_pallas(w_f, w_b, kk, bvec, k, v, r, n_ctx):
    n, nb, _ = kk.shape
    tblk = RW_TBLK
    nblk = n // tblk
    nblk_ctx = n_ctx // tblk

    def fmap(c):
        return (c, 0, 0)

    def bmap(c):
        return (jnp.where(c < nblk_ctx, nblk_ctx - 1 - c, nblk + nblk_ctx - 1 - c), 0, 0)

    blk = (tblk, nb, RW_W)
    oblk = (tblk, RW_HEADS, nb * RW_HD)
    return pl.pallas_call(
        functools.partial(_rwkv_scan_kernel, tblk, nb),
        out_shape=[jax.ShapeDtypeStruct((n, RW_HEADS, nb * RW_HD), F32)] * 2,
        grid=(nblk,),
        in_specs=[pl.BlockSpec(blk, fmap)] * 6 + [pl.BlockSpec(blk, bmap)] * 6,
        out_specs=[pl.BlockSpec(oblk, fmap), pl.BlockSpec(oblk, bmap)],
        scratch_shapes=[pltpu.VMEM((nb, RW_HD, RW_W), F32), pltpu.VMEM((nb, RW_HD, RW_W), F32),
                        pltpu.VMEM((nb, RW_HD, RW_W), BF16), pltpu.VMEM((nb, RW_HD, RW_W), BF16)],
        compiler_params=pltpu.CompilerParams(dimension_semantics=("arbitrary",),
                                             vmem_limit_bytes=VMEM_LIMIT),
        name="rwkv_scan",
    )(w_f, kk, bvec, k, v, r, w_b, kk, bvec, k, v, r)


def _rwkv_post_kernel(nb, of_ref, ob_ref, r_ref, k_ref, v_ref, gf_ref, gb_ref, rk_ref, lng_ref, lnb_ref,
                      y_ref):
    mean_blk = _head_blocks(1.0 / RW_HD)
    sum_blk = _head_blocks(1.0)

    def gn(o_ref, b):
        o = jnp.concatenate([o_ref[:, h, b * RW_HD:(b + 1) * RW_HD] for h in range(RW_HEADS)], axis=-1)
        oc = o - _head_sum(o, mean_blk)
        var = _head_sum(oc * oc, mean_blk)
        return oc * lax.rsqrt(var + RW_GN_EPS) * lng_ref[...] + lnb_ref[...]

    for b in range(nb):
        bonus = _head_sum(r_ref[b] * k_ref[b] * rk_ref[...], sum_blk) * v_ref[b]
        y_ref[b] = (gn(of_ref, b) + bonus) * gf_ref[b] + (gn(ob_ref, b) + bonus) * gb_ref[b]


def rwkv_post(o_f, o_b, r, k, v, g_f, g_b, r_k, lnx_g, lnx_b, B, L, start):
    tm = min(RW_POST_TM, L)
    assert start % tm == 0 and L % tm == 0
    tok = pl.BlockSpec((B, tm, RW_W), lambda i: (0, i, 0))
    o_spec = pl.BlockSpec((tm, RW_HEADS, B * RW_HD), lambda i: (start // tm + i, 0, 0))
    vec = pl.BlockSpec((1, RW_W), lambda i: (0, 0))
    streams = [t.reshape(B, L, RW_W) for t in (r, k, v, g_f, g_b)]
    return pl.pallas_call(
        functools.partial(_rwkv_post_kernel, B),
        out_shape=jax.ShapeDtypeStruct((B, L, RW_W), F32),
        grid=(L // tm,),
        in_specs=[o_spec] * 2 + [tok] * 5 + [vec] * 3,
        out_specs=tok,
        compiler_params=pltpu.CompilerParams(
            dimension_semantics=("parallel",), vmem_limit_bytes=VMEM_LIMIT),
        name="rwkv_post",
    )(o_f, o_b, *streams, r_k.reshape(1, RW_W), lnx_g.reshape(1, RW_W), lnx_b.reshape(1, RW_W))


def rwkv_branch(p, pc, B, L, C, need_ctx, mu, w0, w2, a0, a2, g2, k_k, k_a, r_k, lnx_g, lnx_b):
    r, k, v, kk, bv, d_f, d_b, g_f, g_b = rwkv_prep(p, B, L, mu, w0, w2, a0, a2, g2, k_k, k_a)
    rc, kc, vc, kkc, bc, dc_f, dc_b, gc_f, gc_b = rwkv_prep(pc, B, C, mu, w0, w2, a0, a2, g2, k_k, k_a)

    def tmajor(xc, xl):
        t = jnp.concatenate([xc.reshape(B, C, RW_W), xl.reshape(B, L, RW_W)], axis=1)
        return jnp.moveaxis(t, 1, 0)

    pf, pb = rwkv_scan_pallas(tmajor(dc_f, d_f), tmajor(dc_b, d_b), tmajor(kkc, kk), tmajor(bc, bv),
                              tmajor(kc, k), tmajor(vc, v), tmajor(rc, r), C)

    y = rwkv_post(pf, pb, r, k, v, g_f, g_b, r_k, lnx_g, lnx_b, B, L, C)
    yc = rwkv_post(pf, pb, rc, kc, vc, gc_f, gc_b, r_k, lnx_g, lnx_b, B, C, 0) if need_ctx else None
    return y, yc


def _diff_attn_kernel(q_ref, kt_ref, v_ref, lam_ref, gain_ref, o_ref):
    q = q_ref[0]
    v = v_ref[0]
    lam = lam_ref[...]
    tq = q.shape[0]
    head_of_lane = lax.broadcasted_iota(jnp.int32, (1, DF_W), 1) // DF_VD
    acc = jnp.zeros((tq, DF_W), F32)
    for h in range(DF_HEADS):
        parts = []
        for c in range(2):
            j = 2 * h + c
            s = jnp.dot(q[:, DF_HD * j:DF_HD * (j + 1)], kt_ref[0, j],
                        preferred_element_type=F32)
            p = jnp.exp2(s - jnp.max(s, -1, keepdims=True))
            l = jnp.sum(p, -1, keepdims=True)
            pv = jnp.dot(p.astype(BF16), v, preferred_element_type=F32)
            parts.append(pv / l)
        acc = jnp.where(head_of_lane == h, parts[0] - lam * parts[1], acc)
    row = lax.broadcasted_iota(jnp.int32, (DF_W, DF_W), 0)
    col = lax.broadcasted_iota(jnp.int32, (DF_W, DF_W), 1)
    seg_mean = jnp.where(row // DF_VD == col // DF_VD, 1.0 / DF_VD, 0.0).astype(F32)
    ms = jnp.dot(acc * acc, seg_mean, preferred_element_type=F32, precision=lax.Precision.HIGHEST)
    o_ref[0] = acc * lax.rsqrt(ms + 1e-5) * gain_ref[...]


def _diff_pre_kernel(rope, x_ref, cos_ref, sin_ref, q_ref, kt_ref, v_ref):
    x = x_ref[...].astype(F32)
    q, k, v = x[:, 0:DF_W], x[:, DF_W:2 * DF_W], x[:, 2 * DF_W:3 * DF_W]
    if rope:
        cos, sin = cos_ref[...], sin_ref[...]
        q = _rope(q, cos, sin, DF_HD)
        k = _rope(k, cos, sin, DF_HD)
    q_ref[...] = (q * (DF_HD ** -0.5 * math.log2(math.e))).astype(BF16)
    kt_ref[0] = k.T.astype(BF16)
    v_ref[...] = v.astype(BF16)


def diff_pre(p, B, L, tables, rope):
    tm = min(RW_PREP_TM, L)
    tiles_per_seq = L // tm
    n_tok = B * L
    width = 3 * DF_W
    col = P_DF // width
    cos, sin = tables
    tok = pl.BlockSpec((tm, DF_W), lambda i: (i, 0))
    tab = pl.BlockSpec((tm, DF_W), lambda i: (i % tiles_per_seq, 0))
    q, kt, v = pl.pallas_call(
        functools.partial(_diff_pre_kernel, rope),
        out_shape=[jax.ShapeDtypeStruct((n_tok, DF_W), BF16),
                   jax.ShapeDtypeStruct((B, DF_W, L), BF16),
                   jax.ShapeDtypeStruct((n_tok, DF_W), BF16)],
        grid=(n_tok // tm,),
        in_specs=[pl.BlockSpec((tm, width), lambda i: (i, col)), tab, tab],
        out_specs=[tok, pl.BlockSpec((1, DF_W, tm), lambda i: (i // tiles_per_seq, 0, i % tiles_per_seq)),
                   tok],
        compiler_params=pltpu.CompilerParams(
            dimension_semantics=("parallel",), vmem_limit_bytes=VMEM_LIMIT),
        name="diff_pre",
    )(p, cos, sin)
    return q.reshape(B, L, DF_W), kt, v.reshape(B, L, DF_W)


def diff_attend(qs, kt, vb, lam, lam_init, subln_g):
    B, Lq = qs.shape[:2]
    S = vb.shape[1]
    tq = min(DF_TQ, Lq)
    kt = kt.reshape(B, 2 * DF_HEADS, DF_HD, S)
    gain = jnp.tile(subln_g.astype(F32) * (1.0 - lam_init), DF_HEADS).reshape(1, DF_W)
    return pl.pallas_call(
        _diff_attn_kernel,
        out_shape=jax.ShapeDtypeStruct((B, Lq, DF_W), F32),
        grid=(B, Lq // tq),
        in_specs=[pl.BlockSpec((1, tq, DF_W), lambda b, i: (b, i, 0)),
                  pl.BlockSpec((1, 2 * DF_HEADS, DF_HD, S), lambda b, i: (b, 0, 0, 0)),
                  pl.BlockSpec((1, S, DF_W), lambda b, i: (b, 0, 0)),
                  pl.BlockSpec((1, 1), lambda b, i: (0, 0)),
                  pl.BlockSpec((1, DF_W), lambda b, i: (0, 0))],
        out_specs=pl.BlockSpec((1, tq, DF_W), lambda b, i: (b, i, 0)),
        compiler_params=pltpu.CompilerParams(
            dimension_semantics=("parallel", "parallel"), vmem_limit_bytes=VMEM_LIMIT),
        name="diff_attn",
    )(qs, kt, vb, lam.reshape(1, 1).astype(F32), gain)


def diff_branch(p, pc, B, L, C, need_ctx, tables, lq1, lk1, lq2, lk2, subln_g, lam_init):
    lam = (jnp.exp(jnp.sum(lq1.astype(F32) * lk1.astype(F32)))
           - jnp.exp(jnp.sum(lq2.astype(F32) * lk2.astype(F32))) + lam_init)
    q, kt, v = diff_pre(p, B, L, tables, True)
    qc, ktc, vc = diff_pre(pc, B, C, tables, False)
    y = diff_attend(q, jnp.concatenate([kt, ktc], axis=2), jnp.concatenate([v, vc], axis=1),
                    lam, lam_init, subln_g)
    yc = diff_attend(qc, ktc, vc, lam, lam_init, subln_g) if need_ctx else None
    return y, yc


def _ln_rows(x):
    mu = jnp.mean(x, -1, keepdims=True)
    xc = x - mu
    return xc * lax.rsqrt(jnp.mean(xc * xc, -1, keepdims=True) + LN_EPS)


def _ffn_kernel(tm, tiles_per_seq, n_chunks,
                h_ref, hprev_ref, hnext_ref, sh_ref, sc_ref, gate_ref,
                wa_ref, wb_ref, cwa_ref, cwb_ref, cba_ref, cbb_ref, wd_ref, lng_ref, lnb_ref,
                o_ref, u_ref, acc_ref):
    i = pl.program_id(0)
    j = pl.program_id(1)
    n = tm + 2 * HALO

    @pl.when(j == 0)
    def _():
        scale = 1.0 + sc_ref[0]
        shift = sh_ref[0]
        first = (i % tiles_per_seq) == 0
        last = (i % tiles_per_seq) == tiles_per_seq - 1
        u_ref[0:tm, :] = (_ln_rows(h_ref[...]) * scale + shift).astype(BF16)
        un = _ln_rows(hnext_ref[...]) * scale + shift
        up = _ln_rows(hprev_ref[...]) * scale + shift
        u_ref[tm:tm + HALO, :] = jnp.where(last, 0.0, un).astype(BF16)
        u_ref[tm + HALO:n, :] = jnp.where(first, 0.0, up).astype(BF16)
        acc_ref[...] = jnp.zeros_like(acc_ref)

    u = u_ref[...]

    def conv(w_ref, cw_ref, cb_ref):
        x = jnp.dot(u, w_ref[...], preferred_element_type=F32)
        cw = cw_ref[...]
        y = (pltpu.roll(x, 1, 0)[0:tm] * cw[0:1] + x[0:tm] * cw[1:2]
             + pltpu.roll(x, n - 1, 0)[0:tm] * cw[2:3] + cb_ref[...])
        return y

    a = conv(wa_ref, cwa_ref, cba_ref)
    b = conv(wb_ref, cwb_ref, cbb_ref)
    g = (a * jax.nn.sigmoid(a) * b).astype(BF16)
    acc_ref[...] += jnp.dot(g, wd_ref[...], preferred_element_type=F32)

    @pl.when(j == n_chunks - 1)
    def _():
        y = DN_ALPHA * h_ref[...] + gate_ref[0] * acc_ref[...]
        o_ref[...] = _ln_rows(y) * lng_ref[...] + lnb_ref[...]


def ffn_block(h, shift, scale, gate, w_up, conv_w, conv_b, w_down, ln_g, ln_b):
    B, L, D = h.shape
    tm = min(FFN_TM, L)
    tiles_per_seq = L // tm
    n_tok = B * L
    n_chunks = D_FF // FFN_CHUNK
    hb = tm // HALO

    def bcast(t):
        return jnp.broadcast_to(t.astype(F32), (B, 1, D))

    wu = w_up.astype(BF16)
    wd = w_down.astype(BF16)
    cb = conv_b.reshape(1, 2 * D_FF)
    h2 = h.reshape(n_tok, D)
    mod_spec = pl.BlockSpec((1, 1, D), lambda i, j: (i // tiles_per_seq, 0, 0))
    vec_spec = pl.BlockSpec((1, D), lambda i, j: (0, 0))
    out = pl.pallas_call(
        functools.partial(_ffn_kernel, tm, tiles_per_seq, n_chunks),
        out_shape=jax.ShapeDtypeStruct((n_tok, D), F32),
        grid=(n_tok // tm, n_chunks),
        in_specs=[pl.BlockSpec((tm, D), lambda i, j: (i, 0)),
                  pl.BlockSpec((HALO, D), lambda i, j: (jnp.maximum(i * hb - 1, 0), 0)),
                  pl.BlockSpec((HALO, D), lambda i, j: (jnp.minimum((i + 1) * hb, n_tok // HALO - 1), 0)),
                  mod_spec, mod_spec, mod_spec,
                  pl.BlockSpec((D, FFN_CHUNK), lambda i, j: (0, j)),
                  pl.BlockSpec((D, FFN_CHUNK), lambda i, j: (0, n_chunks + j)),
                  pl.BlockSpec((3, FFN_CHUNK), lambda i, j: (0, j)),
                  pl.BlockSpec((3, FFN_CHUNK), lambda i, j: (0, n_chunks + j)),
                  pl.BlockSpec((1, FFN_CHUNK), lambda i, j: (0, j)),
                  pl.BlockSpec((1, FFN_CHUNK), lambda i, j: (0, n_chunks + j)),
                  pl.BlockSpec((FFN_CHUNK, D), lambda i, j: (j, 0)),
                  vec_spec, vec_spec],
        out_specs=pl.BlockSpec((tm, D), lambda i, j: (i, 0)),
        scratch_shapes=[pltpu.VMEM((tm + 2 * HALO, D), BF16), pltpu.VMEM((tm, D), F32)],
        compiler_params=pltpu.CompilerParams(
            dimension_semantics=("parallel", "arbitrary"), vmem_limit_bytes=VMEM_LIMIT),
        name="conv_ffn",
    )(h2, h2, h2, bcast(shift), bcast(scale), bcast(gate), wu, wu, conv_w, conv_w, cb, cb, wd,
      ln_g.reshape(1, D), ln_b.reshape(1, D))
    return out.reshape(B, L, D)


P_GATE, P_HY, P_RW, P_SW, P_DF = 0, 4096, 5120, 6400, 6912
P_COLS = 7680
IN_CHUNK = 768


def _in_proj_kernel(x_ref, sh_ref, sc_ref, w_ref, o_ref, u_ref):
    @pl.when(pl.program_id(1) == 0)
    def _():
        u_ref[...] = (_ln_rows(x_ref[...]) * (1.0 + sc_ref[0]) + sh_ref[0]).astype(BF16)

    o_ref[...] = jnp.dot(u_ref[...], w_ref[...], preferred_element_type=F32).astype(BF16)


def in_proj(h, shift, scale, w_in_p):
    B, L, D = h.shape
    tm = min(IN_TM, L)
    tiles_per_seq = L // tm
    n_tok = B * L

    def bcast(t):
        return jnp.broadcast_to(t.astype(F32), (B, 1, D))

    mod_spec = pl.BlockSpec((1, 1, D), lambda i, j: (i // tiles_per_seq, 0, 0))
    return pl.pallas_call(
        _in_proj_kernel,
        out_shape=jax.ShapeDtypeStruct((n_tok, P_COLS), BF16),
        grid=(n_tok // tm, P_COLS // IN_CHUNK),
        in_specs=[pl.BlockSpec((tm, D), lambda i, j: (i, 0)), mod_spec, mod_spec,
                  pl.BlockSpec((D, IN_CHUNK), lambda i, j: (0, j))],
        out_specs=pl.BlockSpec((tm, IN_CHUNK), lambda i, j: (i, j)),
        scratch_shapes=[pltpu.VMEM((tm, D), BF16)],
        compiler_params=pltpu.CompilerParams(
            dimension_semantics=("parallel", "arbitrary"), vmem_limit_bytes=VMEM_LIMIT),
        name="in_proj",
    )(h.reshape(n_tok, D), bcast(shift), bcast(scale), w_in_p)


def pack_w_in(w):
    offs = np.cumsum((0,) + IN_SIZES)
    hy, sw, rw, df, gt = [w[:, offs[j]:offs[j + 1]] for j in range(len(IN_SIZES))]

    def padto(t, n):
        return jnp.pad(t, ((0, 0), (0, n - t.shape[1])))

    return jnp.concatenate([gt, padto(hy, P_RW - P_HY), padto(rw, P_SW - P_RW), sw, df],
                           axis=1).astype(BF16)


def _merge_kernel(yh_ref, ys_ref, yr_ref, yd_ref, g0_ref, g1_ref, g2_ref, g3_ref, h_ref, gate_ref,
                  wbr_ref, wo_ref, lng_ref, lnb_ref, o_ref):
    acc = None
    for j, (y_ref, g_ref) in enumerate(((yh_ref, g0_ref), (ys_ref, g1_ref), (yr_ref, g2_ref),
                                        (yd_ref, g3_ref))):
        term = jax.nn.sigmoid(g_ref[...].astype(F32)) * jnp.dot(y_ref[...].astype(BF16), wbr_ref[j],
                                                    preferred_element_type=F32)
        acc = term if acc is None else acc + term
    mix = jnp.dot(acc.astype(BF16), wo_ref[...], preferred_element_type=F32)
    y = DN_ALPHA * h_ref[...] + gate_ref[0] * mix
    o_ref[...] = _ln_rows(y) * lng_ref[...] + lnb_ref[...]


def merge_block(ys, p, h, gate, w_br, w_o, ln_g, ln_b):
    B, L, D = h.shape
    n_tok = B * L
    tm = min(MERGE_TM, L)
    tiles_per_seq = L // tm
    y_spec = pl.BlockSpec((tm, BR_W), lambda i: (i, 0))
    vec_spec = pl.BlockSpec((1, D), lambda i: (0, 0))
    g_specs = [pl.BlockSpec((tm, D), functools.partial(lambda j, i: (i, j), j)) for j in range(N_BRANCH)]
    out = pl.pallas_call(
        _merge_kernel,
        out_shape=jax.ShapeDtypeStruct((n_tok, D), F32),
        grid=(n_tok // tm,),
        in_specs=[y_spec] * N_BRANCH + g_specs + [
            pl.BlockSpec((tm, D), lambda i: (i, 0)),
            pl.BlockSpec((1, 1, D), lambda i: (i // tiles_per_seq, 0, 0)),
            pl.BlockSpec((N_BRANCH, BR_W, D), lambda i: (0, 0, 0)),
            pl.BlockSpec((D, D), lambda i: (0, 0)), vec_spec, vec_spec],
        out_specs=pl.BlockSpec((tm, D), lambda i: (i, 0)),
        compiler_params=pltpu.CompilerParams(
            dimension_semantics=("parallel",), vmem_limit_bytes=VMEM_LIMIT),
        name="merge",
    )(*[y.reshape(n_tok, BR_W) for y in ys], p, p, p, p, h.reshape(n_tok, D),
      jnp.broadcast_to(gate.astype(F32), (B, 1, D)), w_br.astype(BF16), w_o.astype(BF16),
      ln_g.reshape(1, D), ln_b.reshape(1, D))
    return out.reshape(B, L, D)


def kernel(x, c, ctx, c_ctx, ada_w, ada_b, w_in, hy_conv_w, hy_conv_b, hy_f_w1, hy_f_b1,
           hy_f_w2, hy_f_b2, hy_f_w3, hy_f_freq, hy_bias, swa_sink, rwkv_mu, rwkv_w0, rwkv_w2,
           rwkv_a0, rwkv_a2, rwkv_g2, rwkv_kk, rwkv_ka, rwkv_rk, rwkv_lnx_g, rwkv_lnx_b,
           diff_lq1, diff_lk1, diff_lq2, diff_lk2, diff_subln_g, w_branch, w_out, ln1_g, ln1_b,
           ffn_w_up, ffn_conv_w, ffn_conv_b, ffn_w_down, ln2_g, ln2_b):
    h, hc = x, ctx
    mats, mats_c = dft_mats(x.shape[1]), dft_mats(ctx.shape[1])
    swa_tables = rope_tables(x.shape[1], SWA_HD, SWA_HEADS * SWA_HD)
    df_tables = rope_tables(x.shape[1], DF_HD, DF_W)
    s_lat = jax.nn.silu(c)
    s_ctx = jax.nn.silu(c_ctx)
    for i in range(DEPTH):
        need_ctx = i < DEPTH - 1
        mod = (s_lat @ ada_w[i] + ada_b[i])[:, None, :]
        mod_c = s_ctx @ ada_w[i] + ada_b[i]
        sh1, sc1, g1, sh2, sc2, g2 = jnp.split(mod, 6, -1)
        csh1, csc1, cg1, csh2, csc2, cg2 = jnp.split(mod_c, 6, -1)

        B, L, D = h.shape
        C = hc.shape[1]
        w_in_p = pack_w_in(w_in[i])
        p = in_proj(h, sh1, sc1, w_in_p)
        pc = in_proj(hc, csh1.reshape(1, 1, D), csc1.reshape(1, 1, D), w_in_p)

        y_hy, yc_hy = hyena_branch(p, pc, B, L, C, need_ctx, mats, mats_c, hy_conv_w[i], hy_conv_b[i],
                                   hy_f_w1[i], hy_f_b1[i], hy_f_w2[i], hy_f_b2[i], hy_f_w3[i],
                                   hy_f_freq[i], hy_bias[i])
        y_sw, yc_sw = swa_branch(p, pc, B, L, C, need_ctx, swa_sink[i], swa_tables)
        y_rw, yc_rw = rwkv_branch(p, pc, B, L, C, need_ctx, rwkv_mu[i], rwkv_w0[i], rwkv_w2[i],
                                  rwkv_a0[i], rwkv_a2[i], rwkv_g2[i], rwkv_kk[i], rwkv_ka[i],
                                  rwkv_rk[i], rwkv_lnx_g[i], rwkv_lnx_b[i])
        lam_init = 0.8 - 0.6 * math.exp(-0.3 * i)
        y_df, yc_df = diff_branch(p, pc, B, L, C, need_ctx, df_tables, diff_lq1[i], diff_lk1[i],
                                  diff_lq2[i], diff_lk2[i], diff_subln_g[i], lam_init)

        h = merge_block((y_hy, y_sw, y_rw, y_df), p, h, g1, w_branch[i], w_out[i], ln1_g[i], ln1_b[i])
        h = ffn_block(h, sh2, sc2, g2, ffn_w_up[i], ffn_conv_w[i], ffn_conv_b[i], ffn_w_down[i],
                      ln2_g[i], ln2_b[i])

        if need_ctx:
            hc = merge_block((yc_hy, yc_sw, yc_rw, yc_df), pc, hc, cg1.reshape(1, 1, D), w_branch[i],
                             w_out[i], ln1_g[i], ln1_b[i])
            hc = ffn_block(hc, csh2.reshape(1, 1, -1), csc2.reshape(1, 1, -1), cg2.reshape(1, 1, -1),
                           ffn_w_up[i], ffn_conv_w[i], ffn_conv_b[i], ffn_w_down[i], ln2_g[i], ln2_b[i])
    return h
```

```python
import functools
import math

import jax
import jax.numpy as jnp
import numpy as np
from jax import lax
from jax.experimental import pallas as pl
from jax.experimental.pallas import tpu as pltpu

D_MODEL = 1024
DEPTH = 2
GRID_W = 64
ROPE_BASE = 10000.0
F32 = jnp.float32
BF16 = jnp.bfloat16
NEG_INF = -1e30
LN_EPS = 1e-6

HY_W = 256
HY_ORDER = 2
HY_DIRS = 2
HY_EMB = 33
HY_MIN_DECAY = math.log(1e-2) / 1.5
HY_MAX_DECAY = math.log(1e-2) / 0.3

SWA_HEADS = 4
SWA_KV = 2
SWA_HD = 64
SWA_WIN = 128
SWA_BLOCK = 128

RW_HEADS = 4
RW_HD = 64
RW_W = RW_HEADS * RW_HD
RW_DECAY_R = 64
RW_AAA_R = 64
RW_GATE_R = 128
RW_GN_EPS = 64e-5

DF_HEADS = 4
DF_HD = 32
DF_VD = 2 * DF_HD
DF_W = DF_HEADS * DF_VD

N_BRANCH = 4
BR_W = 256
D_FF = 2816
DN_ALPHA = (2 * DEPTH) ** 0.25

SWA_SIZES = (SWA_HEADS * SWA_HD, SWA_KV * SWA_HD, SWA_KV * SWA_HD)
RW_SIZES = (RW_W, RW_W, RW_W, RW_DECAY_R, RW_DECAY_R, RW_AAA_R, RW_GATE_R, RW_GATE_R)
DF_SIZES = (DF_W, DF_W, DF_W)
IN_SIZES = (3 * HY_W, sum(SWA_SIZES), sum(RW_SIZES), sum(DF_SIZES), N_BRANCH * D_MODEL)

V7X_VMEM_BYTES = 64 * 1024 * 1024
LANES = 128
MM_TM = 512
MM_TN = 512
RW_TBLK = 64
DF_TQ = 256
SWA_QROWS = 256
HALO = 8
P_HALO = 16
IN_TM = 2048
FFN_TM = 1024
FFN_CHUNK = 256
MERGE_TM = 512
HY_TK = 512
RW_PREP_TM = 512
RW_POST_TM = 256
VMEM_LIMIT = V7X_VMEM_BYTES * 3 // 4


def _mm_kernel(x_ref, w_ref, o_ref):
    o_ref[...] = jnp.dot(x_ref[...].astype(BF16), w_ref[...], preferred_element_type=F32)


def _mm(x, w):
    m, k = x.shape
    n = w.shape[1]
    n_pad = -(-n // LANES) * LANES
    wb = w.astype(BF16)
    if n_pad != n:
        wb = jnp.pad(wb, ((0, 0), (0, n_pad - n)))
    tn = MM_TN if n_pad % MM_TN == 0 else (256 if n_pad % 256 == 0 else LANES)
    tm = MM_TM if m % MM_TM == 0 else m
    out = pl.pallas_call(
        _mm_kernel,
        out_shape=jax.ShapeDtypeStruct((m, n_pad), F32),
        grid=(m // tm, n_pad // tn),
        in_specs=[pl.BlockSpec((tm, k), lambda i, j: (i, 0)),
                  pl.BlockSpec((k, tn), lambda i, j: (0, j))],
        out_specs=pl.BlockSpec((tm, tn), lambda i, j: (i, j)),
        compiler_params=pltpu.CompilerParams(
            dimension_semantics=("parallel", "parallel"), vmem_limit_bytes=VMEM_LIMIT),
        name="matmul",
    )(x, wb)
    return out[:, :n] if n_pad != n else out


def dft_mats(L):
    tb = 64
    k = jnp.arange(L, dtype=jnp.int32)[:, None]

    def table(t):
        ang = (((2 * k + 1) * t[None, :]) % (4 * L)).astype(F32) * (math.pi / (2 * L))
        return jnp.cos(ang), jnp.sin(ang)

    ca, sa = table(tb * jnp.arange(L // tb, dtype=jnp.int32))
    cb, sb = table(jnp.arange(tb, dtype=jnp.int32))
    c = (ca[:, :, None] * cb[:, None, :] - sa[:, :, None] * sb[:, None, :]).reshape(L, L)
    s = (sa[:, :, None] * cb[:, None, :] + ca[:, :, None] * sb[:, None, :]).reshape(L, L)
    ct = (ca.T[:, None, :] * cb.T[None, :, :] - sa.T[:, None, :] * sb.T[None, :, :]).reshape(L, L)
    st = (sa.T[:, None, :] * cb.T[None, :, :] + ca.T[:, None, :] * sb.T[None, :, :]).reshape(L, L)
    return c.astype(BF16), s.astype(BF16), ct.astype(BF16), st.astype(BF16)


def hyena_kspec(L, mats, w1, b1, w2, b2, w3, freq):
    t = jnp.linspace(0.0, 1.0, L, dtype=F32)[:, None]
    bands = (HY_EMB - 1) // 2
    w = 2.0 * math.pi * jnp.arange(L, dtype=F32) / L
    fb = jnp.linspace(1e-4, bands - 1, bands, dtype=F32)
    ang = w[:, None] * fb[None, :]
    z = jnp.concatenate([t, jnp.cos(ang), -jnp.sin(ang)], -1)
    fr = freq.astype(F32)
    hdn = jnp.sin(fr * (z @ w1.astype(F32) + b1.astype(F32)))
    hdn = jnp.sin(fr * (hdn @ w2.astype(F32) + b2.astype(F32)))
    filt = (hdn @ w3.astype(F32)).reshape(L, HY_ORDER, HY_DIRS, HY_W)
    deltas = jnp.abs(jnp.linspace(HY_MIN_DECAY, HY_MAX_DECAY, HY_W, dtype=F32))
    filt = filt * jnp.exp(-t * deltas[None, :])[:, None, None, :]
    filt = filt / jnp.sum(jnp.abs(filt), axis=(0, 2), keepdims=True)
    fwd, bwd = filt[:, :, 0], filt[:, :, 1]
    nw = HY_ORDER * HY_W
    bwd = bwd.at[0].set(0.0)
    kr = _mm(mats[0], (fwd + bwd).reshape(L, nw))
    ki = -_mm(mats[1], (fwd - bwd).reshape(L, nw))

    def per_order(a):
        return jnp.transpose(a.reshape(L, HY_ORDER, HY_W), (1, 0, 2))

    return per_order(kr), per_order(ki)


def _hy_fwd_kernel(ac_ref, as_ref, z_ref, kr_ref, ki_ref, yr_ref, yi_ref):
    z = z_ref[0].astype(BF16)
    zr = jnp.dot(ac_ref[...], z, preferred_element_type=F32)
    zi = -jnp.dot(as_ref[...], z, preferred_element_type=F32)
    kr, ki = kr_ref[...], ki_ref[...]
    yr_ref[0] = (zr * kr - zi * ki).astype(BF16)
    yi_ref[0] = (zr * ki + zi * kr).astype(BF16)


def _hy_inv_kernel(inv_l, act_ref, ast_ref, yr_ref, yi_ref, z_ref, x_ref, bias_ref, o_ref, *o16_ref):
    y = (jnp.dot(act_ref[...], yr_ref[0], preferred_element_type=F32)
         - jnp.dot(ast_ref[...], yi_ref[0], preferred_element_type=F32))
    o = x_ref[0] * (y * inv_l + bias_ref[...] * z_ref[0])
    o_ref[0] = o
    for ref in o16_ref:
        ref[0] = o.astype(BF16)


def hyena_longconv(z, z16, xmul, mats, kr, ki, bias, want16):
    B, L, W = z.shape
    ac, as_, act, ast = mats
    tk = min(HY_TK, L)
    a_spec = pl.BlockSpec((tk, L), lambda i, b: (i, 0))
    full_spec = pl.BlockSpec((1, L, W), lambda i, b: (b, 0, 0))
    tile_spec = pl.BlockSpec((1, tk, W), lambda i, b: (b, i, 0))
    k_spec = pl.BlockSpec((tk, W), lambda i, b: (i, 0))
    params = pltpu.CompilerParams(dimension_semantics=("parallel", "arbitrary"),
                                  vmem_limit_bytes=VMEM_LIMIT)
    yr, yi = pl.pallas_call(
        _hy_fwd_kernel,
        out_shape=[jax.ShapeDtypeStruct((B, L, W), BF16)] * 2,
        grid=(L // tk, B),
        in_specs=[a_spec, a_spec, full_spec, k_spec, k_spec],
        out_specs=[tile_spec, tile_spec],
        compiler_params=params,
        name="hyena_fwd",
    )(ac, as_, z16, kr, ki)
    n_out = 2 if want16 else 1
    return pl.pallas_call(
        functools.partial(_hy_inv_kernel, 1.0 / L),
        out_shape=[jax.ShapeDtypeStruct((B, L, W), F32), jax.ShapeDtypeStruct((B, L, W), BF16)][:n_out],
        grid=(L // tk, B),
        in_specs=[a_spec, a_spec, full_spec, full_spec, tile_spec, tile_spec,
                  pl.BlockSpec((1, W), lambda i, b: (0, 0))],
        out_specs=[tile_spec] * n_out,
        compiler_params=params,
        name="hyena_inv",
    )(act, ast, yr, yi, z, xmul, bias.reshape(1, W))


def _hy_pre_kernel(tm, tiles_per_seq, x_ref, xp_ref, xn_ref, cw_ref, cb_ref, v_ref, v16_ref, x1_ref,
                   x2_ref):
    i = pl.program_id(0)
    first = (i % tiles_per_seq) == 0
    last = (i % tiles_per_seq) == tiles_per_seq - 1
    x = x_ref[...].astype(F32)
    row = lax.broadcasted_iota(jnp.int32, x.shape, 0)
    prev_row = jnp.where(first, 0.0, xp_ref[...].astype(F32)[P_HALO - 1:P_HALO, :])
    next_row = jnp.where(last, 0.0, xn_ref[...].astype(F32)[0:1, :])
    xm1 = jnp.where(row == 0, prev_row, pltpu.roll(x, 1, 0))
    xp1 = jnp.where(row == tm - 1, next_row, pltpu.roll(x, tm - 1, 0))
    cw = cw_ref[...]
    y = xm1 * cw[0:1] + x * cw[1:2] + xp1 * cw[2:3] + cb_ref[...]
    v_ref[...] = y[:, 0:HY_W]
    v16_ref[...] = y[:, 0:HY_W].astype(BF16)
    x1_ref[...] = y[:, HY_W:2 * HY_W]
    x2_ref[...] = y[:, 2 * HY_W:3 * HY_W]


def hyena_pre(p, B, L, conv_w, conv_b):
    tm = min(RW_PREP_TM, L)
    tiles_per_seq = L // tm
    n_tok = B * L
    hb = tm // P_HALO
    width = P_RW - P_HY
    col = P_HY // width
    cw = jnp.pad(conv_w, ((0, 0), (0, width - conv_w.shape[1])))
    cb = jnp.pad(conv_b, (0, width - conv_b.shape[0])).reshape(1, width)
    out_spec = pl.BlockSpec((tm, HY_W), lambda i: (i, 0))
    outs = pl.pallas_call(
        functools.partial(_hy_pre_kernel, tm, tiles_per_seq),
        out_shape=[jax.ShapeDtypeStruct((n_tok, HY_W), dt) for dt in (F32, BF16, F32, F32)],
        grid=(n_tok // tm,),
        in_specs=[pl.BlockSpec((tm, width), lambda i: (i, col)),
                  pl.BlockSpec((P_HALO, width), lambda i: (jnp.maximum(i * hb - 1, 0), col)),
                  pl.BlockSpec((P_HALO, width), lambda i: (jnp.minimum((i + 1) * hb, n_tok // P_HALO - 1), col)),
                  pl.BlockSpec((3, width), lambda i: (0, 0)),
                  pl.BlockSpec((1, width), lambda i: (0, 0))],
        out_specs=[out_spec] * 4,
        compiler_params=pltpu.CompilerParams(
            dimension_semantics=("parallel",), vmem_limit_bytes=VMEM_LIMIT),
        name="hyena_pre",
    )(p, p, p, cw, cb)
    return [o.reshape(B, L, HY_W) for o in outs]


def hyena_mix(p, B, L, conv_w, conv_b, mats, kspec, bias):
    v, v16, x1, x2 = hyena_pre(p, B, L, conv_w, conv_b)
    kr, ki = kspec
    zz, zz16 = hyena_longconv(v, v16, x1, mats, kr[0], ki[0], bias[0], True)
    return hyena_longconv(zz, zz16, x2, mats, kr[1], ki[1], bias[1], False)[0]


def hyena_branch(p, pc, B, L, C, need_ctx, mats, mats_c, conv_w, conv_b, fw1, fb1, fw2, fb2, fw3, ffreq,
                 bias):
    kspec = hyena_kspec(L, mats, fw1, fb1, fw2, fb2, fw3, ffreq)
    y = hyena_mix(p, B, L, conv_w, conv_b, mats, kspec, bias)
    yc = None
    if need_ctx:
        kspec_c = hyena_kspec(C, mats_c, fw1, fb1, fw2, fb2, fw3, ffreq)
        yc = hyena_mix(pc, B, C, conv_w, conv_b, mats_c, kspec_c, bias)
    return y, yc


def rope_tables(L, d, width):
    nf = d // 4
    pos = jnp.arange(L)
    inv = ROPE_BASE ** (-jnp.arange(nf, dtype=F32) / nf)
    cs, sn = [], []
    for p in (pos // GRID_W, pos % GRID_W):
        ang = p.astype(F32)[:, None] * inv[None, :]
        cs += [jnp.cos(ang), jnp.cos(ang)]
        sn += [-jnp.sin(ang), jnp.sin(ang)]
    reps = width // d
    return jnp.tile(jnp.concatenate(cs, -1), (1, reps)), jnp.tile(jnp.concatenate(sn, -1), (1, reps))


def _rope(x, cos, sin, d):
    width = x.shape[-1]
    q = d // 4
    lane = lax.broadcasted_iota(jnp.int32, x.shape, 1)
    swapped = jnp.where(lane % (2 * q) < q, pltpu.roll(x, width - q, 1), pltpu.roll(x, q, 1))
    return x * cos + swapped * sin


def _swa_kernel(local, seq_len, qrows, q_ref, *refs):
    nk = qrows // SWA_BLOCK + 2
    k_refs, v_refs = refs[0:nk], refs[nk:2 * nk]
    ck_ref, cv_ref, cos_ref, sin_ref, sink_ref, o_ref = refs[2 * nk:]
    n = pl.program_id(1)
    blk = SWA_BLOCK
    qscale = SWA_HD ** -0.5 * math.log2(math.e)
    q = q_ref[...].astype(F32)
    if local:
        q0 = pl.multiple_of(n * qrows, qrows)
        q = _rope(q, cos_ref[pl.ds(q0, qrows), :], sin_ref[pl.ds(q0, qrows), :], SWA_HD)
        first = n * (qrows // blk) - 1
        ks, vs = [], []
        for j, (k_ref, v_ref) in enumerate(zip(k_refs, v_refs)):
            k0 = pl.multiple_of(jnp.clip(first + j, 0, seq_len // blk - 1) * blk, blk)
            ks.append(_rope(k_ref[...].astype(F32), cos_ref[pl.ds(k0, blk), 0:SWA_KV * SWA_HD],
                            sin_ref[pl.ds(k0, blk), 0:SWA_KV * SWA_HD], SWA_HD).astype(BF16))
            vs.append(v_ref[...].astype(BF16))
        k_loc = jnp.concatenate(ks, axis=0)
        v_loc = jnp.concatenate(vs, axis=0)
        qpos = n * qrows + lax.broadcasted_iota(jnp.int32, (qrows, nk * blk), 0)
        kpos = first * blk + lax.broadcasted_iota(jnp.int32, (qrows, nk * blk), 1)
        valid = (jnp.abs(kpos - qpos) <= SWA_WIN) & (kpos >= 0) & (kpos < seq_len)
    qb = (q * qscale).astype(BF16)
    ck = ck_ref[...].astype(BF16)
    cv = cv_ref[...].astype(BF16)
    nt = (((1,), (1,)), ((), ()))
    outs = []
    for h in range(SWA_HEADS):
        g = h // (SWA_HEADS // SWA_KV)
        qh = qb[:, h * SWA_HD:(h + 1) * SWA_HD]
        gs = slice(g * SWA_HD, (g + 1) * SWA_HD)
        sink = sink_ref[h:h + 1, 0:1]
        s_ctx = lax.dot_general(qh, ck[:, gs], nt, preferred_element_type=F32)
        m = jnp.maximum(jnp.max(s_ctx, -1, keepdims=True), sink)
        if local:
            s_loc = lax.dot_general(qh, k_loc[:, gs], nt, preferred_element_type=F32)
            s_loc = jnp.where(valid, s_loc, NEG_INF)
            m = jnp.maximum(m, jnp.max(s_loc, -1, keepdims=True))
        p_ctx = jnp.exp2(s_ctx - m)
        den = jnp.sum(p_ctx, -1, keepdims=True) + jnp.exp2(sink - m)
        acc = jnp.dot(p_ctx.astype(BF16), cv[:, gs], preferred_element_type=F32)
        if local:
            p_loc = jnp.exp2(s_loc - m)
            den = den + jnp.sum(p_loc, -1, keepdims=True)
            acc = acc + jnp.dot(p_loc.astype(BF16), v_loc[:, gs], preferred_element_type=F32)
        outs.append(acc / den)
    o_ref[...] = jnp.concatenate(outs, axis=-1)


def swa_attend(p, pc, B, L, C, sink, tables, local):
    blk = SWA_BLOCK
    nb = L // blk
    qrows = min(SWA_QROWS, L)
    nq = L // qrows
    qpb = qrows // blk
    qcol = P_SW // (SWA_HEADS * SWA_HD)
    kcol = (P_SW + SWA_HEADS * SWA_HD) // (SWA_KV * SWA_HD)
    kvw = SWA_KV * SWA_HD

    def nbr(j, col):
        return lambda b, n: (b * nb + jnp.clip(n * qpb + j, 0, nb - 1), col)

    nbrs = range(-1, qpb + 1)

    cos, sin = tables
    sink_rows = jnp.broadcast_to((sink.astype(F32) * math.log2(math.e))[:, None], (SWA_HEADS, LANES))
    tab_spec = pl.BlockSpec(cos.shape, lambda b, n: (0, 0))
    out = pl.pallas_call(
        functools.partial(_swa_kernel, local, L, qrows),
        out_shape=jax.ShapeDtypeStruct((B * L, SWA_HEADS * SWA_HD), F32),
        grid=(B, nq),
        in_specs=[pl.BlockSpec((qrows, SWA_HEADS * SWA_HD), lambda b, n: (b * nq + n, qcol))]
        + [pl.BlockSpec((blk, kvw), nbr(j, kcol)) for j in nbrs]
        + [pl.BlockSpec((blk, kvw), nbr(j, kcol + 1)) for j in nbrs]
        + [pl.BlockSpec((C, kvw), lambda b, n: (b, kcol)),
           pl.BlockSpec((C, kvw), lambda b, n: (b, kcol + 1)),
           tab_spec, tab_spec,
           pl.BlockSpec((SWA_HEADS, LANES), lambda b, n: (0, 0))],
        out_specs=pl.BlockSpec((qrows, SWA_HEADS * SWA_HD), lambda b, n: (b * nq + n, 0)),
        compiler_params=pltpu.CompilerParams(
            dimension_semantics=("parallel", "parallel"), vmem_limit_bytes=VMEM_LIMIT),
        name="swa",
    )(p, *([p] * (2 * len(nbrs))), pc, pc, cos, sin, sink_rows)
    return out.reshape(B, L, SWA_HEADS * SWA_HD)


def swa_branch(p, pc, B, L, C, need_ctx, sink, tables):
    y = swa_attend(p, pc, B, L, C, sink, tables, True)
    yc = swa_attend(pc, pc, B, C, C, sink, tables, False) if need_ctx else None
    return y, yc


def _seg_blocks(value, width, seg):
    row = lax.broadcasted_iota(jnp.int32, (width, width), 0)
    col = lax.broadcasted_iota(jnp.int32, (width, width), 1)
    return jnp.where(row // seg == col // seg, value, 0.0).astype(BF16)


def _head_blocks(value):
    return _seg_blocks(value, RW_W, RW_HD)


def _head_sum(x, blocks):
    hi = x.astype(BF16)
    lo = (x - hi.astype(F32)).astype(BF16)
    return (jnp.dot(hi, blocks, preferred_element_type=F32)
            + jnp.dot(lo, blocks, preferred_element_type=F32))


RW_OFF = tuple(int(o) for o in np.cumsum((0,) + RW_SIZES))
RW_PAD = 1280
RW_WIN_WD = RW_OFF[3]
RW_WIN_LO = RW_OFF[5]
RW_WIN_HI = RW_WIN_LO + LANES


def _rwkv_prep_kernel(tm, tiles_per_seq, x_ref, xp_ref, xn_ref, mu_ref, w0_ref, w2f_ref, w2b_ref,
                      a0_ref, a2_ref, g2f_ref, g2b_ref, kk_ref, ka_ref,
                      r_o, k_o, v_o, kkn_o, b_o, df_o, db_o, gf_o, gb_o):
    i = pl.program_id(0)
    first = (i % tiles_per_seq) == 0
    last = (i % tiles_per_seq) == tiles_per_seq - 1
    x = x_ref[...].astype(F32)
    row = lax.broadcasted_iota(jnp.int32, x.shape, 0)
    prev_row = jnp.where(first, 0.0, xp_ref[...].astype(F32)[P_HALO - 1:P_HALO, :])
    next_row = jnp.where(last, 0.0, xn_ref[...].astype(F32)[0:1, :])
    xm1 = jnp.where(row == 0, prev_row, pltpu.roll(x, 1, 0))
    xp1 = jnp.where(row == tm - 1, next_row, pltpu.roll(x, tm - 1, 0))
    x = x + (0.5 * (xm1 + xp1) - x) * mu_ref[...]

    r = x[:, RW_OFF[0]:RW_OFF[1]]
    k = x[:, RW_OFF[1]:RW_OFF[2]]
    v = x[:, RW_OFF[2]:RW_OFF[3]]
    wd = jnp.tanh(x[:, RW_WIN_WD:RW_WIN_WD + LANES]).astype(BF16)
    lo = x[:, RW_WIN_LO:RW_WIN_LO + RW_W]
    hi = x[:, RW_WIN_HI:RW_WIN_HI + RW_W]

    def decay(w2_ref, w0):
        wlog = -jax.nn.softplus(-(w0 + jnp.dot(wd, w2_ref[...], preferred_element_type=F32))) - 0.5
        return jnp.exp(-jnp.exp(wlog))

    a = jax.nn.sigmoid(a0_ref[...] + jnp.dot(lo[:, 0:LANES].astype(BF16), a2_ref[...],
                                             preferred_element_type=F32))
    kk = k * kk_ref[...]
    norm = jnp.sqrt(_head_sum(kk * kk, _head_blocks(1.0)))
    kk = kk / jnp.maximum(norm, 1e-12)
    r_o[...] = r
    k_o[...] = k * (1.0 + (a - 1.0) * ka_ref[...])
    v_o[...] = v
    kkn_o[...] = kk
    b_o[...] = kk * a
    df_o[...] = decay(w2f_ref, w0_ref[0:1, :])
    db_o[...] = decay(w2b_ref, w0_ref[1:2, :])
    gf_o[...] = jnp.dot(jax.nn.sigmoid(lo).astype(BF16), g2f_ref[...], preferred_element_type=F32)
    gb_o[...] = jnp.dot(jax.nn.sigmoid(hi).astype(BF16), g2b_ref[...], preferred_element_type=F32)


def rwkv_prep(p, B, L, mu, w0, w2, a0, a2, g2, k_k, k_a):
    tm = min(RW_PREP_TM, L)
    tiles_per_seq = L // tm
    n_tok = B * L
    hb = tm // P_HALO
    col = P_RW // RW_PAD

    def rows(n, r0, src):
        return jnp.zeros((n, RW_W), F32).at[r0:r0 + src.shape[0]].set(src).astype(BF16)

    w2f = rows(LANES, RW_OFF[3] - RW_WIN_WD, w2[0])
    w2b = rows(LANES, RW_OFF[4] - RW_WIN_WD, w2[1])
    a2p = rows(LANES, RW_OFF[5] - RW_WIN_LO, a2)
    g2f = rows(RW_W, RW_OFF[6] - RW_WIN_LO, g2[0])
    g2b = rows(RW_W, RW_OFF[7] - RW_WIN_HI, g2[1])
    mu_p = jnp.pad(mu, (0, RW_PAD - mu.shape[0])).reshape(1, RW_PAD)

    def const(shape):
        return pl.BlockSpec(shape, lambda i: (0,) * len(shape))

    vec = const((1, RW_W))
    out_spec = pl.BlockSpec((tm, RW_W), lambda i: (i, 0))
    return pl.pallas_call(
        functools.partial(_rwkv_prep_kernel, tm, tiles_per_seq),
        out_shape=[jax.ShapeDtypeStruct((n_tok, RW_W), F32)] * 9,
        grid=(n_tok // tm,),
        in_specs=[pl.BlockSpec((tm, RW_PAD), lambda i: (i, col)),
                  pl.BlockSpec((P_HALO, RW_PAD), lambda i: (jnp.maximum(i * hb - 1, 0), col)),
                  pl.BlockSpec((P_HALO, RW_PAD), lambda i: (jnp.minimum((i + 1) * hb, n_tok // P_HALO - 1), col)),
                  const((1, RW_PAD)), const((2, RW_W)), const((LANES, RW_W)), const((LANES, RW_W)),
                  vec, const((LANES, RW_W)), const((RW_W, RW_W)), const((RW_W, RW_W)), vec, vec],
        out_specs=[out_spec] * 9,
        compiler_params=pltpu.CompilerParams(
            dimension_semantics=("parallel",), vmem_limit_bytes=VMEM_LIMIT),
        name="rwkv_prep",
    )(p, p, p, mu_p, w0, w2f, w2b, a0.reshape(1, RW_W), a2p, g2f, g2b,
      k_k.reshape(1, RW_W), k_a.reshape(1, RW_W))


def _rwkv_scan_kernel(tblk, nb,
                      wf_ref, kkf_ref, bf_ref, kf_ref, vf_ref, rf_ref,
                      wb_ref, kkb_ref, bb_ref, kb_ref, vb_ref, rb_ref,
                      of_ref, ob_ref, sf_ref, sb_ref, sf16_ref, sb16_ref):
    c = pl.program_id(0)

    @pl.when(c == 0)
    def _():
        for ref in (sf_ref, sb_ref, sf16_ref, sb16_ref):
            ref[...] = jnp.zeros_like(ref)

    row = lax.broadcasted_iota(jnp.int32, (RW_W, RW_W), 0)
    col = lax.broadcasted_iota(jnp.int32, (RW_W, RW_W), 1)
    ones_blk = jnp.where(row // RW_HD == col // RW_HD, 1.0, 0.0).astype(BF16)
    r64 = lax.broadcasted_iota(jnp.int32, (RW_HD, RW_W), 0)
    c64 = lax.broadcasted_iota(jnp.int32, (RW_HD, RW_W), 1)
    eye_t = jnp.where(r64 == c64 % RW_HD, 1.0, 0.0).astype(BF16)
    r8 = lax.broadcasted_iota(jnp.int32, (8, RW_W), 0)
    c8 = lax.broadcasted_iota(jnp.int32, (8, RW_W), 1)
    head_sel = jnp.where(r8 == c8 // RW_HD, 1.0, 0.0).astype(BF16)

    fwd = (wf_ref, kkf_ref, bf_ref, kf_ref, vf_ref, rf_ref, sf_ref, sf16_ref)
    bwd = (wb_ref, kkb_ref, bb_ref, kb_ref, vb_ref, rb_ref, sb_ref, sb16_ref)
    nrow = nb * RW_HD

    def one_dir(idx, refs):
        w_ref, kk_ref, b_ref, k_ref, v_ref, r_ref, s_ref, s16_ref = refs
        zs = []
        for b in range(nb):
            kk16 = kk_ref[idx, b:b + 1, :].astype(BF16)
            v16 = v_ref[idx, b:b + 1, :].astype(BF16)
            lhs = jnp.concatenate([s16_ref[b] * kk16, eye_t * v16], axis=0)
            red = jnp.dot(lhs, ones_blk, preferred_element_type=F32)
            s_new = (s_ref[b] * w_ref[idx, b:b + 1, :] - red[0:RW_HD] * b_ref[idx, b:b + 1, :]
                     + red[RW_HD:2 * RW_HD] * k_ref[idx, b:b + 1, :])
            s_ref[b] = s_new
            s16 = s_new.astype(BF16)
            s16_ref[b] = s16
            zs.append(s16 * r_ref[idx, b:b + 1, :].astype(BF16))
        return jnp.concatenate(zs, axis=0)

    def emit(idx, z, o_ref):
        o = lax.dot_general(head_sel, z, (((1,), (1,)), ((), ())), preferred_element_type=F32)
        o_ref[idx] = o[0:RW_HEADS]

    def step(i, carry):
        z_f, z_b = carry
        ib = tblk - 1 - i
        emit(jnp.maximum(i - 1, 0), z_f, of_ref)
        emit(jnp.minimum(ib + 1, tblk - 1), z_b, ob_ref)
        return one_dir(i, fwd), one_dir(ib, bwd)

    z0 = jnp.zeros((nrow, RW_W), BF16)
    z_f, z_b = lax.fori_loop(0, tblk, step, (z0, z0), unroll=8)
    emit(tblk - 1, z_f, of_ref)
    emit(0, z_b, ob_ref)


def rwkv_scan_pallas(w_f, w_b, kk, bvec, k, v, r, n_ctx):
    n, nb, _ = kk.shape
    tblk = RW_TBLK
    nblk = n // tblk
    nblk_ctx = n_ctx // tblk

    def fmap(c):
        return (c, 0, 0)

    def bmap(c):
        return (jnp.where(c < nblk_ctx, nblk_ctx - 1 - c, nblk + nblk_ctx - 1 - c), 0, 0)

    blk = (tblk, nb, RW_W)
    oblk = (tblk, RW_HEADS, nb * RW_HD)
    return pl.pallas_call(
        functools.partial(_rwkv_scan_kernel, tblk, nb),
        out_shape=[jax.ShapeDtypeStruct((n, RW_HEADS, nb * RW_HD), F32)] * 2,
        grid=(nblk,),
        in_specs=[pl.BlockSpec(blk, fmap)] * 6 + [pl.BlockSpec(blk, bmap)] * 6,
        out_specs=[pl.BlockSpec(oblk, fmap), pl.BlockSpec(oblk, bmap)],
        scratch_shapes=[pltpu.VMEM((nb, RW_HD, RW_W), F32), pltpu.VMEM((nb, RW_HD, RW_W), F32),
                        pltpu.VMEM((nb, RW_HD, RW_W), BF16), pltpu.VMEM((nb, RW_HD, RW_W), BF16)],
        compiler_params=pltpu.CompilerParams(dimension_semantics=("arbitrary",),
                                             vmem_limit_bytes=VMEM_LIMIT),
        name="rwkv_scan",
    )(w_f, kk, bvec, k, v, r, w_b, kk, bvec, k, v, r)


def _rwkv_post_kernel(nb, of_ref, ob_ref, r_ref, k_ref, v_ref, gf_ref, gb_ref, rk_ref, lng_ref, lnb_ref,
                      y_ref):
    mean_blk = _head_blocks(1.0 / RW_HD)
    sum_blk = _head_blocks(1.0)

    def gn(o_ref, b):
        o = jnp.concatenate([o_ref[:, h, b * RW_HD:(b + 1) * RW_HD] for h in range(RW_HEADS)], axis=-1)
        oc = o - _head_sum(o, mean_blk)
        var = _head_sum(oc * oc, mean_blk)
        return oc * lax.rsqrt(var + RW_GN_EPS) * lng_ref[...] + lnb_ref[...]

    for b in range(nb):
        bonus = _head_sum(r_ref[b] * k_ref[b] * rk_ref[...], sum_blk) * v_ref[b]
        y_ref[b] = (gn(of_ref, b) + bonus) * gf_ref[b] + (gn(ob_ref, b) + bonus) * gb_ref[b]


def rwkv_post(o_f, o_b, r, k, v, g_f, g_b, r_k, lnx_g, lnx_b, B, L, start):
    tm = min(RW_POST_TM, L)
    assert start % tm == 0 and L % tm == 0
    tok = pl.BlockSpec((B, tm, RW_W), lambda i: (0, i, 0))
    o_spec = pl.BlockSpec((tm, RW_HEADS, B * RW_HD), lambda i: (start // tm + i, 0, 0))
    vec = pl.BlockSpec((1, RW_W), lambda i: (0, 0))
    streams = [t.reshape(B, L, RW_W) for t in (r, k, v, g_f, g_b)]
    return pl.pallas_call(
        functools.partial(_rwkv_post_kernel, B),
        out_shape=jax.ShapeDtypeStruct((B, L, RW_W), F32),
        grid=(L // tm,),
        in_specs=[o_spec] * 2 + [tok] * 5 + [vec] * 3,
        out_specs=tok,
        compiler_params=pltpu.CompilerParams(
            dimension_semantics=("parallel",), vmem_limit_bytes=VMEM_LIMIT),
        name="rwkv_post",
    )(o_f, o_b, *streams, r_k.reshape(1, RW_W), lnx_g.reshape(1, RW_W), lnx_b.reshape(1, RW_W))


def rwkv_branch(p, pc, B, L, C, need_ctx, mu, w0, w2, a0, a2, g2, k_k, k_a, r_k, lnx_g, lnx_b):
    r, k, v, kk, bv, d_f, d_b, g_f, g_b = rwkv_prep(p, B, L, mu, w0, w2, a0, a2, g2, k_k, k_a)
    rc, kc, vc, kkc, bc, dc_f, dc_b, gc_f, gc_b = rwkv_prep(pc, B, C, mu, w0, w2, a0, a2, g2, k_k, k_a)

    def tmajor(xc, xl):
        t = jnp.concatenate([xc.reshape(B, C, RW_W), xl.reshape(B, L, RW_W)], axis=1)
        return jnp.moveaxis(t, 1, 0)

    pf, pb = rwkv_scan_pallas(tmajor(dc_f, d_f), tmajor(dc_b, d_b), tmajor(kkc, kk), tmajor(bc, bv),
                              tmajor(kc, k), tmajor(vc, v), tmajor(rc, r), C)

    y = rwkv_post(pf, pb, r, k, v, g_f, g_b, r_k, lnx_g, lnx_b, B, L, C)
    yc = rwkv_post(pf, pb, rc, kc, vc, gc_f, gc_b, r_k, lnx_g, lnx_b, B, C, 0) if need_ctx else None
    return y, yc


def _diff_attn_kernel(q_ref, kt_ref, v_ref, lam_ref, gain_ref, o_ref):
    q = q_ref[0]
    v = v_ref[0]
    lam = lam_ref[...]
    tq = q.shape[0]
    head_of_lane = lax.broadcasted_iota(jnp.int32, (1, DF_W), 1) // DF_VD
    acc = jnp.zeros((tq, DF_W), F32)
    for h in range(DF_HEADS):
        parts = []
        for c in range(2):
            j = 2 * h + c
            s = jnp.dot(q[:, DF_HD * j:DF_HD * (j + 1)], kt_ref[0, j],
                        preferred_element_type=F32)
            p = jnp.exp2(s - jnp.max(s, -1, keepdims=True))
            l = jnp.sum(p, -1, keepdims=True)
            pv = jnp.dot(p.astype(BF16), v, preferred_element_type=F32)
            parts.append(pv / l)
        acc = jnp.where(head_of_lane == h, parts[0] - lam * parts[1], acc)
    ms = _head_sum(acc * acc, _seg_blocks(1.0 / DF_VD, DF_W, DF_VD))
    o_ref[0] = acc * lax.rsqrt(ms + 1e-5) * gain_ref[...]


def _diff_pre_kernel(rope, x_ref, cos_ref, sin_ref, q_ref, kt_ref, v_ref):
    x = x_ref[...].astype(F32)
    q, k, v = x[:, 0:DF_W], x[:, DF_W:2 * DF_W], x[:, 2 * DF_W:3 * DF_W]
    if rope:
        cos, sin = cos_ref[...], sin_ref[...]
        q = _rope(q, cos, sin, DF_HD)
        k = _rope(k, cos, sin, DF_HD)
    q_ref[...] = (q * (DF_HD ** -0.5 * math.log2(math.e))).astype(BF16)
    kt_ref[0] = k.T.astype(BF16)
    v_ref[...] = v.astype(BF16)


def diff_pre(p, B, L, tables, rope):
    tm = min(RW_PREP_TM, L)
    tiles_per_seq = L // tm
    n_tok = B * L
    width = 3 * DF_W
    col = P_DF // width
    cos, sin = tables
    tok = pl.BlockSpec((tm, DF_W), lambda i: (i, 0))
    tab = pl.BlockSpec((tm, DF_W), lambda i: (i % tiles_per_seq, 0))
    q, kt, v = pl.pallas_call(
        functools.partial(_diff_pre_kernel, rope),
        out_shape=[jax.ShapeDtypeStruct((n_tok, DF_W), BF16),
                   jax.ShapeDtypeStruct((B, DF_W, L), BF16),
                   jax.ShapeDtypeStruct((n_tok, DF_W), BF16)],
        grid=(n_tok // tm,),
        in_specs=[pl.BlockSpec((tm, width), lambda i: (i, col)), tab, tab],
        out_specs=[tok, pl.BlockSpec((1, DF_W, tm), lambda i: (i // tiles_per_seq, 0, i % tiles_per_seq)),
                   tok],
        compiler_params=pltpu.CompilerParams(
            dimension_semantics=("parallel",), vmem_limit_bytes=VMEM_LIMIT),
        name="diff_pre",
    )(p, cos, sin)
    return q.reshape(B, L, DF_W), kt, v.reshape(B, L, DF_W)


def diff_attend(qs, kt, vb, lam, lam_init, subln_g):
    B, Lq = qs.shape[:2]
    S = vb.shape[1]
    tq = min(DF_TQ, Lq)
    kt = kt.reshape(B, 2 * DF_HEADS, DF_HD, S)
    gain = jnp.tile(subln_g.astype(F32) * (1.0 - lam_init), DF_HEADS).reshape(1, DF_W)
    return pl.pallas_call(
        _diff_attn_kernel,
        out_shape=jax.ShapeDtypeStruct((B, Lq, DF_W), F32),
        grid=(B, Lq // tq),
        in_specs=[pl.BlockSpec((1, tq, DF_W), lambda b, i: (b, i, 0)),
                  pl.BlockSpec((1, 2 * DF_HEADS, DF_HD, S), lambda b, i: (b, 0, 0, 0)),
                  pl.BlockSpec((1, S, DF_W), lambda b, i: (b, 0, 0)),
                  pl.BlockSpec((1, 1), lambda b, i: (0, 0)),
                  pl.BlockSpec((1, DF_W), lambda b, i: (0, 0))],
        out_specs=pl.BlockSpec((1, tq, DF_W), lambda b, i: (b, i, 0)),
        compiler_params=pltpu.CompilerParams(
            dimension_semantics=("parallel", "parallel"), vmem_limit_bytes=VMEM_LIMIT),
        name="diff_attn",
    )(qs, kt, vb, lam.reshape(1, 1).astype(F32), gain)


def diff_branch(p, pc, B, L, C, need_ctx, tables, lq1, lk1, lq2, lk2, subln_g, lam_init):
    lam = (jnp.exp(jnp.sum(lq1.astype(F32) * lk1.astype(F32)))
           - jnp.exp(jnp.sum(lq2.astype(F32) * lk2.astype(F32))) + lam_init)
    q, kt, v = diff_pre(p, B, L, tables, True)
    qc, ktc, vc = diff_pre(pc, B, C, tables, False)
    y = diff_attend(q, jnp.concatenate([kt, ktc], axis=2), jnp.concatenate([v, vc], axis=1),
                    lam, lam_init, subln_g)
    yc = diff_attend(qc, ktc, vc, lam, lam_init, subln_g) if need_ctx else None
    return y, yc


def _ln_rows(x):
    mu = jnp.mean(x, -1, keepdims=True)
    xc = x - mu
    return xc * lax.rsqrt(jnp.mean(xc * xc, -1, keepdims=True) + LN_EPS)


def _ffn_kernel(tm, tiles_per_seq, n_chunks,
                h_ref, hprev_ref, hnext_ref, sh_ref, sc_ref, gate_ref,
                wa_ref, wb_ref, cwa_ref, cwb_ref, cba_ref, cbb_ref, wd_ref, lng_ref, lnb_ref,
                o_ref, u_ref, acc_ref):
    i = pl.program_id(0)
    j = pl.program_id(1)
    n = tm + 2 * HALO

    @pl.when(j == 0)
    def _():
        scale = 1.0 + sc_ref[0]
        shift = sh_ref[0]
        first = (i % tiles_per_seq) == 0
        last = (i % tiles_per_seq) == tiles_per_seq - 1
        u_ref[0:tm, :] = (_ln_rows(h_ref[...]) * scale + shift).astype(BF16)
        un = _ln_rows(hnext_ref[...]) * scale + shift
        up = _ln_rows(hprev_ref[...]) * scale + shift
        u_ref[tm:tm + HALO, :] = jnp.where(last, 0.0, un).astype(BF16)
        u_ref[tm + HALO:n, :] = jnp.where(first, 0.0, up).astype(BF16)
        acc_ref[...] = jnp.zeros_like(acc_ref)

    u = u_ref[...]

    def conv(w_ref, cw_ref, cb_ref):
        x = jnp.dot(u, w_ref[...], preferred_element_type=F32)
        cw = cw_ref[...]
        y = (pltpu.roll(x, 1, 0)[0:tm] * cw[0:1] + x[0:tm] * cw[1:2]
             + pltpu.roll(x, n - 1, 0)[0:tm] * cw[2:3] + cb_ref[...])
        return y

    a = conv(wa_ref, cwa_ref, cba_ref)
    b = conv(wb_ref, cwb_ref, cbb_ref)
    g = ((0.5 * a) * (1.0 + jnp.tanh(0.5 * a)) * b).astype(BF16)
    acc_ref[...] += jnp.dot(g, wd_ref[...], preferred_element_type=F32)

    @pl.when(j == n_chunks - 1)
    def _():
        y = DN_ALPHA * h_ref[...] + gate_ref[0] * acc_ref[...]
        o_ref[...] = _ln_rows(y) * lng_ref[...] + lnb_ref[...]


def ffn_block(h, shift, scale, gate, w_up, conv_w, conv_b, w_down, ln_g, ln_b):
    B, L, D = h.shape
    tm = min(FFN_TM, L)
    tiles_per_seq = L // tm
    n_tok = B * L
    n_chunks = D_FF // FFN_CHUNK
    hb = tm // HALO

    def bcast(t):
        return jnp.broadcast_to(t.astype(F32), (B, 1, D))

    wu = w_up.astype(BF16)
    wd = w_down.astype(BF16)
    cb = conv_b.reshape(1, 2 * D_FF)
    h2 = h.reshape(n_tok, D)
    mod_spec = pl.BlockSpec((1, 1, D), lambda i, j: (i // tiles_per_seq, 0, 0))
    vec_spec = pl.BlockSpec((1, D), lambda i, j: (0, 0))
    out = pl.pallas_call(
        functools.partial(_ffn_kernel, tm, tiles_per_seq, n_chunks),
        out_shape=jax.ShapeDtypeStruct((n_tok, D), F32),
        grid=(n_tok // tm, n_chunks),
        in_specs=[pl.BlockSpec((tm, D), lambda i, j: (i, 0)),
                  pl.BlockSpec((HALO, D), lambda i, j: (jnp.maximum(i * hb - 1, 0), 0)),
                  pl.BlockSpec((HALO, D), lambda i, j: (jnp.minimum((i + 1) * hb, n_tok // HALO - 1), 0)),
                  mod_spec, mod_spec, mod_spec,
                  pl.BlockSpec((D, FFN_CHUNK), lambda i, j: (0, j)),
                  pl.BlockSpec((D, FFN_CHUNK), lambda i, j: (0, n_chunks + j)),
                  pl.BlockSpec((3, FFN_CHUNK), lambda i, j: (0, j)),
                  pl.BlockSpec((3, FFN_CHUNK), lambda i, j: (0, n_chunks + j)),
                  pl.BlockSpec((1, FFN_CHUNK), lambda i, j: (0, j)),
                  pl.BlockSpec((1, FFN_CHUNK), lambda i, j: (0, n_chunks + j)),
                  pl.BlockSpec((FFN_CHUNK, D), lambda i, j: (j, 0)),
                  vec_spec, vec_spec],
        out_specs=pl.BlockSpec((tm, D), lambda i, j: (i, 0)),
        scratch_shapes=[pltpu.VMEM((tm + 2 * HALO, D), BF16), pltpu.VMEM((tm, D), F32)],
        compiler_params=pltpu.CompilerParams(
            dimension_semantics=("parallel", "arbitrary"), vmem_limit_bytes=VMEM_LIMIT),
        name="conv_ffn",
    )(h2, h2, h2, bcast(shift), bcast(scale), bcast(gate), wu, wu, conv_w, conv_w, cb, cb, wd,
      ln_g.reshape(1, D), ln_b.reshape(1, D))
    return out.reshape(B, L, D)


P_GATE, P_HY, P_RW, P_SW, P_DF = 0, 4096, 5120, 6400, 6912
P_COLS = 7680
IN_CHUNK = 768


def _in_proj_kernel(x_ref, sh_ref, sc_ref, w_ref, o_ref, u_ref):
    @pl.when(pl.program_id(1) == 0)
    def _():
        u_ref[...] = (_ln_rows(x_ref[...]) * (1.0 + sc_ref[0]) + sh_ref[0]).astype(BF16)

    o_ref[...] = jnp.dot(u_ref[...], w_ref[...], preferred_element_type=F32).astype(BF16)


def in_proj(h, shift, scale, w_in_p):
    B, L, D = h.shape
    tm = min(IN_TM, L)
    tiles_per_seq = L // tm
    n_tok = B * L

    def bcast(t):
        return jnp.broadcast_to(t.astype(F32), (B, 1, D))

    mod_spec = pl.BlockSpec((1, 1, D), lambda i, j: (i // tiles_per_seq, 0, 0))
    return pl.pallas_call(
        _in_proj_kernel,
        out_shape=jax.ShapeDtypeStruct((n_tok, P_COLS), BF16),
        grid=(n_tok // tm, P_COLS // IN_CHUNK),
        in_specs=[pl.BlockSpec((tm, D), lambda i, j: (i, 0)), mod_spec, mod_spec,
                  pl.BlockSpec((D, IN_CHUNK), lambda i, j: (0, j))],
        out_specs=pl.BlockSpec((tm, IN_CHUNK), lambda i, j: (i, j)),
        scratch_shapes=[pltpu.VMEM((tm, D), BF16)],
        compiler_params=pltpu.CompilerParams(
            dimension_semantics=("parallel", "arbitrary"), vmem_limit_bytes=VMEM_LIMIT),
        name="in_proj",
    )(h.reshape(n_tok, D), bcast(shift), bcast(scale), w_in_p)


def pack_w_in(w):
    offs = np.cumsum((0,) + IN_SIZES)
    hy, sw, rw, df, gt = [w[:, offs[j]:offs[j + 1]] for j in range(len(IN_SIZES))]

    def padto(t, n):
        return jnp.pad(t, ((0, 0), (0, n - t.shape[1])))

    return jnp.concatenate([gt, padto(hy, P_RW - P_HY), padto(rw, P_SW - P_RW), sw, df],
                           axis=1).astype(BF16)


def _merge_kernel(yh_ref, ys_ref, yr_ref, yd_ref, g0_ref, g1_ref, g2_ref, g3_ref, h_ref, gate_ref,
                  wbr_ref, wo_ref, lng_ref, lnb_ref, o_ref):
    acc = None
    for j, (y_ref, g_ref) in enumerate(((yh_ref, g0_ref), (ys_ref, g1_ref), (yr_ref, g2_ref),
                                        (yd_ref, g3_ref))):
        term = jax.nn.sigmoid(g_ref[...].astype(F32)) * jnp.dot(y_ref[...].astype(BF16), wbr_ref[j],
                                                    preferred_element_type=F32)
        acc = term if acc is None else acc + term
    mix = jnp.dot(acc.astype(BF16), wo_ref[...], preferred_element_type=F32)
    y = DN_ALPHA * h_ref[...] + gate_ref[0] * mix
    o_ref[...] = _ln_rows(y) * lng_ref[...] + lnb_ref[...]


def merge_block(ys, p, h, gate, w_br, w_o, ln_g, ln_b):
    B, L, D = h.shape
    n_tok = B * L
    tm = min(MERGE_TM, L)
    tiles_per_seq = L // tm
    y_spec = pl.BlockSpec((tm, BR_W), lambda i: (i, 0))
    vec_spec = pl.BlockSpec((1, D), lambda i: (0, 0))
    g_specs = [pl.BlockSpec((tm, D), functools.partial(lambda j, i: (i, j), j)) for j in range(N_BRANCH)]
    out = pl.pallas_call(
        _merge_kernel,
        out_shape=jax.ShapeDtypeStruct((n_tok, D), F32),
        grid=(n_tok // tm,),
        in_specs=[y_spec] * N_BRANCH + g_specs + [
            pl.BlockSpec((tm, D), lambda i: (i, 0)),
            pl.BlockSpec((1, 1, D), lambda i: (i // tiles_per_seq, 0, 0)),
            pl.BlockSpec((N_BRANCH, BR_W, D), lambda i: (0, 0, 0)),
            pl.BlockSpec((D, D), lambda i: (0, 0)), vec_spec, vec_spec],
        out_specs=pl.BlockSpec((tm, D), lambda i: (i, 0)),
        compiler_params=pltpu.CompilerParams(
            dimension_semantics=("parallel",), vmem_limit_bytes=VMEM_LIMIT),
        name="merge",
    )(*[y.reshape(n_tok, BR_W) for y in ys], p, p, p, p, h.reshape(n_tok, D),
      jnp.broadcast_to(gate.astype(F32), (B, 1, D)), w_br.astype(BF16), w_o.astype(BF16),
      ln_g.reshape(1, D), ln_b.reshape(1, D))
    return out.reshape(B, L, D)


def kernel(x, c, ctx, c_ctx, ada_w, ada_b, w_in, hy_conv_w, hy_conv_b, hy_f_w1, hy_f_b1,
           hy_f_w2, hy_f_b2, hy_f_w3, hy_f_freq, hy_bias, swa_sink, rwkv_mu, rwkv_w0, rwkv_w2,
           rwkv_a0, rwkv_a2, rwkv_g2, rwkv_kk, rwkv_ka, rwkv_rk, rwkv_lnx_g, rwkv_lnx_b,
           diff_lq1, diff_lk1, diff_lq2, diff_lk2, diff_subln_g, w_branch, w_out, ln1_g, ln1_b,
           ffn_w_up, ffn_conv_w, ffn_conv_b, ffn_w_down, ln2_g, ln2_b):
    h, hc = x, ctx
    mats, mats_c = dft_mats(x.shape[1]), dft_mats(ctx.shape[1])
    swa_tables = rope_tables(x.shape[1], SWA_HD, SWA_HEADS * SWA_HD)
    df_tables = rope_tables(x.shape[1], DF_HD, DF_W)
    s_lat = jax.nn.silu(c)
    s_ctx = jax.nn.silu(c_ctx)
    for i in range(DEPTH):
        need_ctx = i < DEPTH - 1
        mod = (s_lat @ ada_w[i] + ada_b[i])[:, None, :]
        mod_c = s_ctx @ ada_w[i] + ada_b[i]
        sh1, sc1, g1, sh2, sc2, g2 = jnp.split(mod, 6, -1)
        csh1, csc1, cg1, csh2, csc2, cg2 = jnp.split(mod_c, 6, -1)

        B, L, D = h.shape
        C = hc.shape[1]
        w_in_p = pack_w_in(w_in[i])
        p = in_proj(h, sh1, sc1, w_in_p)
        pc = in_proj(hc, csh1.reshape(1, 1, D), csc1.reshape(1, 1, D), w_in_p)

        y_hy, yc_hy = hyena_branch(p, pc, B, L, C, need_ctx, mats, mats_c, hy_conv_w[i], hy_conv_b[i],
                                   hy_f_w1[i], hy_f_b1[i], hy_f_w2[i], hy_f_b2[i], hy_f_w3[i],
                                   hy_f_freq[i], hy_bias[i])
        y_sw, yc_sw = swa_branch(p, pc, B, L, C, need_ctx, swa_sink[i], swa_tables)
        y_rw, yc_rw = rwkv_branch(p, pc, B, L, C, need_ctx, rwkv_mu[i], rwkv_w0[i], rwkv_w2[i],
                                  rwkv_a0[i], rwkv_a2[i], rwkv_g2[i], rwkv_kk[i], rwkv_ka[i],
                                  rwkv_rk[i], rwkv_lnx_g[i], rwkv_lnx_b[i])
        lam_init = 0.8 - 0.6 * math.exp(-0.3 * i)
        y_df, yc_df = diff_branch(p, pc, B, L, C, need_ctx, df_tables, diff_lq1[i], diff_lk1[i],
                                  diff_lq2[i], diff_lk2[i], diff_subln_g[i], lam_init)

        h = merge_block((y_hy, y_sw, y_rw, y_df), p, h, g1, w_branch[i], w_out[i], ln1_g[i], ln1_b[i])
        h = ffn_block(h, sh2, sc2, g2, ffn_w_up[i], ffn_conv_w[i], ffn_conv_b[i], ffn_w_down[i],
                      ln2_g[i], ln2_b[i])

        if need_ctx:
            hc = merge_block((yc_hy, yc_sw, yc_rw, yc_df), pc, hc, cg1.reshape(1, 1, D), w_branch[i],
                             w_out[i], ln1_g[i], ln1_b[i])
            hc = ffn_block(hc, csh2.reshape(1, 1, -1), csc2.reshape(1, 1, -1), cg2.reshape(1, 1, -1),
                           ffn_w_up[i], ffn_conv_w[i], ffn_conv_b[i], ffn_w_down[i], ln2_g[i], ln2_b[i])
    return h
```

```python
import functools
import math

import jax
import jax.numpy as jnp
import numpy as np
from jax import lax
from jax.experimental import pallas as pl
from jax.experimental.pallas import tpu as pltpu

D_MODEL = 1024
DEPTH = 2
GRID_W = 64
ROPE_BASE = 10000.0
F32 = jnp.float32
BF16 = jnp.bfloat16
NEG_INF = -1e30
LN_EPS = 1e-6

HY_W = 256
HY_ORDER = 2
HY_DIRS = 2
HY_EMB = 33
HY_MIN_DECAY = math.log(1e-2) / 1.5
HY_MAX_DECAY = math.log(1e-2) / 0.3

SWA_HEADS = 4
SWA_KV = 2
SWA_HD = 64
SWA_WIN = 128
SWA_BLOCK = 128

RW_HEADS = 4
RW_HD = 64
RW_W = RW_HEADS * RW_HD
RW_DECAY_R = 64
RW_AAA_R = 64
RW_GATE_R = 128
RW_GN_EPS = 64e-5

DF_HEADS = 4
DF_HD = 32
DF_VD = 2 * DF_HD
DF_W = DF_HEADS * DF_VD

N_BRANCH = 4
BR_W = 256
D_FF = 2816
DN_ALPHA = (2 * DEPTH) ** 0.25

SWA_SIZES = (SWA_HEADS * SWA_HD, SWA_KV * SWA_HD, SWA_KV * SWA_HD)
RW_SIZES = (RW_W, RW_W, RW_W, RW_DECAY_R, RW_DECAY_R, RW_AAA_R, RW_GATE_R, RW_GATE_R)
DF_SIZES = (DF_W, DF_W, DF_W)
IN_SIZES = (3 * HY_W, sum(SWA_SIZES), sum(RW_SIZES), sum(DF_SIZES), N_BRANCH * D_MODEL)

V7X_VMEM_BYTES = 64 * 1024 * 1024
LANES = 128
MM_TM = 512
MM_TN = 512
RW_TBLK = 64
DF_TQ = 256
SWA_QROWS = 256
HALO = 8
P_HALO = 16
IN_TM = 2048
FFN_TM = 1024
FFN_CHUNK = 256
MERGE_TM = 512
HY_TK = 512
RW_PREP_TM = 512
RW_POST_TM = 256
VMEM_LIMIT = V7X_VMEM_BYTES * 3 // 4


def _mm_kernel(x_ref, w_ref, o_ref):
    o_ref[...] = jnp.dot(x_ref[...].astype(BF16), w_ref[...], preferred_element_type=F32)


def _mm(x, w):
    m, k = x.shape
    n = w.shape[1]
    n_pad = -(-n // LANES) * LANES
    wb = w.astype(BF16)
    if n_pad != n:
        wb = jnp.pad(wb, ((0, 0), (0, n_pad - n)))
    tn = MM_TN if n_pad % MM_TN == 0 else (256 if n_pad % 256 == 0 else LANES)
    tm = MM_TM if m % MM_TM == 0 else m
    out = pl.pallas_call(
        _mm_kernel,
        out_shape=jax.ShapeDtypeStruct((m, n_pad), F32),
        grid=(m // tm, n_pad // tn),
        in_specs=[pl.BlockSpec((tm, k), lambda i, j: (i, 0)),
                  pl.BlockSpec((k, tn), lambda i, j: (0, j))],
        out_specs=pl.BlockSpec((tm, tn), lambda i, j: (i, j)),
        compiler_params=pltpu.CompilerParams(
            dimension_semantics=("parallel", "parallel"), vmem_limit_bytes=VMEM_LIMIT),
        name="matmul",
    )(x, wb)
    return out[:, :n] if n_pad != n else out


def dft_mats(L):
    tb = 64
    k = jnp.arange(L, dtype=jnp.int32)[:, None]

    def table(t):
        ang = (((2 * k + 1) * t[None, :]) % (4 * L)).astype(F32) * (math.pi / (2 * L))
        return jnp.cos(ang), jnp.sin(ang)

    ca, sa = table(tb * jnp.arange(L // tb, dtype=jnp.int32))
    cb, sb = table(jnp.arange(tb, dtype=jnp.int32))
    c = (ca[:, :, None] * cb[:, None, :] - sa[:, :, None] * sb[:, None, :]).reshape(L, L)
    s = (sa[:, :, None] * cb[:, None, :] + ca[:, :, None] * sb[:, None, :]).reshape(L, L)
    ct = (ca.T[:, None, :] * cb.T[None, :, :] - sa.T[:, None, :] * sb.T[None, :, :]).reshape(L, L)
    st = (sa.T[:, None, :] * cb.T[None, :, :] + ca.T[:, None, :] * sb.T[None, :, :]).reshape(L, L)
    return c.astype(BF16), s.astype(BF16), ct.astype(BF16), st.astype(BF16)


def hyena_kspec(L, mats, w1, b1, w2, b2, w3, freq):
    t = jnp.linspace(0.0, 1.0, L, dtype=F32)[:, None]
    bands = (HY_EMB - 1) // 2
    w = 2.0 * math.pi * jnp.arange(L, dtype=F32) / L
    fb = jnp.linspace(1e-4, bands - 1, bands, dtype=F32)
    ang = w[:, None] * fb[None, :]
    z = jnp.concatenate([t, jnp.cos(ang), -jnp.sin(ang)], -1)
    fr = freq.astype(F32)
    hdn = jnp.sin(fr * (z @ w1.astype(F32) + b1.astype(F32)))
    hdn = jnp.sin(fr * (hdn @ w2.astype(F32) + b2.astype(F32)))
    filt = (hdn @ w3.astype(F32)).reshape(L, HY_ORDER, HY_DIRS, HY_W)
    deltas = jnp.abs(jnp.linspace(HY_MIN_DECAY, HY_MAX_DECAY, HY_W, dtype=F32))
    filt = filt * jnp.exp(-t * deltas[None, :])[:, None, None, :]
    filt = filt / jnp.sum(jnp.abs(filt), axis=(0, 2), keepdims=True)
    fwd, bwd = filt[:, :, 0], filt[:, :, 1]
    nw = HY_ORDER * HY_W
    bwd = bwd.at[0].set(0.0)
    kr = _mm(mats[0], (fwd + bwd).reshape(L, nw))
    ki = -_mm(mats[1], (fwd - bwd).reshape(L, nw))

    def per_order(a):
        return jnp.transpose(a.reshape(L, HY_ORDER, HY_W), (1, 0, 2))

    return per_order(kr), per_order(ki)


def _hy_fwd_kernel(ac_ref, as_ref, z_ref, kr_ref, ki_ref, yr_ref, yi_ref):
    z = z_ref[0].astype(BF16)
    zr = jnp.dot(ac_ref[...], z, preferred_element_type=F32)
    zi = -jnp.dot(as_ref[...], z, preferred_element_type=F32)
    kr, ki = kr_ref[...], ki_ref[...]
    yr_ref[0] = (zr * kr - zi * ki).astype(BF16)
    yi_ref[0] = (zr * ki + zi * kr).astype(BF16)


def _hy_inv_kernel(inv_l, act_ref, ast_ref, yr_ref, yi_ref, z_ref, x_ref, bias_ref, o_ref, *o16_ref):
    y = (jnp.dot(act_ref[...], yr_ref[0], preferred_element_type=F32)
         - jnp.dot(ast_ref[...], yi_ref[0], preferred_element_type=F32))
    o = x_ref[0] * (y * inv_l + bias_ref[...] * z_ref[0])
    o_ref[0] = o
    for ref in o16_ref:
        ref[0] = o.astype(BF16)


def hyena_longconv(z, z16, xmul, mats, kr, ki, bias, want16):
    B, L, W = z.shape
    ac, as_, act, ast = mats
    tk = min(HY_TK, L)
    a_spec = pl.BlockSpec((tk, L), lambda i, b: (i, 0))
    full_spec = pl.BlockSpec((1, L, W), lambda i, b: (b, 0, 0))
    tile_spec = pl.BlockSpec((1, tk, W), lambda i, b: (b, i, 0))
    k_spec = pl.BlockSpec((tk, W), lambda i, b: (i, 0))
    params = pltpu.CompilerParams(dimension_semantics=("parallel", "arbitrary"),
                                  vmem_limit_bytes=VMEM_LIMIT)
    yr, yi = pl.pallas_call(
        _hy_fwd_kernel,
        out_shape=[jax.ShapeDtypeStruct((B, L, W), BF16)] * 2,
        grid=(L // tk, B),
        in_specs=[a_spec, a_spec, full_spec, k_spec, k_spec],
        out_specs=[tile_spec, tile_spec],
        compiler_params=params,
        name="hyena_fwd",
    )(ac, as_, z16, kr, ki)
    n_out = 2 if want16 else 1
    return pl.pallas_call(
        functools.partial(_hy_inv_kernel, 1.0 / L),
        out_shape=[jax.ShapeDtypeStruct((B, L, W), F32), jax.ShapeDtypeStruct((B, L, W), BF16)][:n_out],
        grid=(L // tk, B),
        in_specs=[a_spec, a_spec, full_spec, full_spec, tile_spec, tile_spec,
                  pl.BlockSpec((1, W), lambda i, b: (0, 0))],
        out_specs=[tile_spec] * n_out,
        compiler_params=params,
        name="hyena_inv",
    )(act, ast, yr, yi, z, xmul, bias.reshape(1, W))


def _hy_pre_kernel(tm, tiles_per_seq, x_ref, xp_ref, xn_ref, cw_ref, cb_ref, v_ref, v16_ref, x1_ref,
                   x2_ref):
    i = pl.program_id(0)
    first = (i % tiles_per_seq) == 0
    last = (i % tiles_per_seq) == tiles_per_seq - 1
    x = x_ref[...].astype(F32)
    row = lax.broadcasted_iota(jnp.int32, x.shape, 0)
    prev_row = jnp.where(first, 0.0, xp_ref[...].astype(F32)[P_HALO - 1:P_HALO, :])
    next_row = jnp.where(last, 0.0, xn_ref[...].astype(F32)[0:1, :])
    xm1 = jnp.where(row == 0, prev_row, pltpu.roll(x, 1, 0))
    xp1 = jnp.where(row == tm - 1, next_row, pltpu.roll(x, tm - 1, 0))
    cw = cw_ref[...]
    y = xm1 * cw[0:1] + x * cw[1:2] + xp1 * cw[2:3] + cb_ref[...]
    v_ref[...] = y[:, 0:HY_W]
    v16_ref[...] = y[:, 0:HY_W].astype(BF16)
    x1_ref[...] = y[:, HY_W:2 * HY_W]
    x2_ref[...] = y[:, 2 * HY_W:3 * HY_W]


def hyena_pre(p, B, L, conv_w, conv_b):
    tm = min(RW_PREP_TM, L)
    tiles_per_seq = L // tm
    n_tok = B * L
    hb = tm // P_HALO
    width = P_RW - P_HY
    col = P_HY // width
    cw = jnp.pad(conv_w, ((0, 0), (0, width - conv_w.shape[1])))
    cb = jnp.pad(conv_b, (0, width - conv_b.shape[0])).reshape(1, width)
    out_spec = pl.BlockSpec((tm, HY_W), lambda i: (i, 0))
    outs = pl.pallas_call(
        functools.partial(_hy_pre_kernel, tm, tiles_per_seq),
        out_shape=[jax.ShapeDtypeStruct((n_tok, HY_W), dt) for dt in (F32, BF16, F32, F32)],
        grid=(n_tok // tm,),
        in_specs=[pl.BlockSpec((tm, width), lambda i: (i, col)),
                  pl.BlockSpec((P_HALO, width), lambda i: (jnp.maximum(i * hb - 1, 0), col)),
                  pl.BlockSpec((P_HALO, width), lambda i: (jnp.minimum((i + 1) * hb, n_tok // P_HALO - 1), col)),
                  pl.BlockSpec((3, width), lambda i: (0, 0)),
                  pl.BlockSpec((1, width), lambda i: (0, 0))],
        out_specs=[out_spec] * 4,
        compiler_params=pltpu.CompilerParams(
            dimension_semantics=("parallel",), vmem_limit_bytes=VMEM_LIMIT),
        name="hyena_pre",
    )(p, p, p, cw, cb)
    return [o.reshape(B, L, HY_W) for o in outs]


def hyena_mix(p, B, L, conv_w, conv_b, mats, kspec, bias):
    v, v16, x1, x2 = hyena_pre(p, B, L, conv_w, conv_b)
    kr, ki = kspec
    zz, zz16 = hyena_longconv(v, v16, x1, mats, kr[0], ki[0], bias[0], True)
    return hyena_longconv(zz, zz16, x2, mats, kr[1], ki[1], bias[1], False)[0]


def hyena_branch(p, pc, B, L, C, need_ctx, mats, mats_c, conv_w, conv_b, fw1, fb1, fw2, fb2, fw3, ffreq,
                 bias):
    kspec = hyena_kspec(L, mats, fw1, fb1, fw2, fb2, fw3, ffreq)
    y = hyena_mix(p, B, L, conv_w, conv_b, mats, kspec, bias)
    yc = None
    if need_ctx:
        kspec_c = hyena_kspec(C, mats_c, fw1, fb1, fw2, fb2, fw3, ffreq)
        yc = hyena_mix(pc, B, C, conv_w, conv_b, mats_c, kspec_c, bias)
    return y, yc


def rope_tables(L, d, width):
    nf = d // 4
    pos = jnp.arange(L)
    inv = ROPE_BASE ** (-jnp.arange(nf, dtype=F32) / nf)
    cs, sn = [], []
    for p in (pos // GRID_W, pos % GRID_W):
        ang = p.astype(F32)[:, None] * inv[None, :]
        cs += [jnp.cos(ang), jnp.cos(ang)]
        sn += [-jnp.sin(ang), jnp.sin(ang)]
    reps = width // d
    return jnp.tile(jnp.concatenate(cs, -1), (1, reps)), jnp.tile(jnp.concatenate(sn, -1), (1, reps))


def _rope(x, cos, sin, d):
    width = x.shape[-1]
    q = d // 4
    lane = lax.broadcasted_iota(jnp.int32, x.shape, 1)
    swapped = jnp.where(lane % (2 * q) < q, pltpu.roll(x, width - q, 1), pltpu.roll(x, q, 1))
    return x * cos + swapped * sin


def _swa_kernel(local, seq_len, qrows, q_ref, *refs):
    nk = qrows // SWA_BLOCK + 2
    k_refs, v_refs = refs[0:nk], refs[nk:2 * nk]
    ck_ref, cv_ref, cos_ref, sin_ref, sink_ref, o_ref = refs[2 * nk:]
    n = pl.program_id(1)
    blk = SWA_BLOCK
    qscale = SWA_HD ** -0.5 * math.log2(math.e)
    q = q_ref[...].astype(F32)
    if local:
        q0 = pl.multiple_of(n * qrows, qrows)
        q = _rope(q, cos_ref[pl.ds(q0, qrows), :], sin_ref[pl.ds(q0, qrows), :], SWA_HD)
        first = n * (qrows // blk) - 1
        ks, vs = [], []
        for j, (k_ref, v_ref) in enumerate(zip(k_refs, v_refs)):
            k0 = pl.multiple_of(jnp.clip(first + j, 0, seq_len // blk - 1) * blk, blk)
            ks.append(_rope(k_ref[...].astype(F32), cos_ref[pl.ds(k0, blk), 0:SWA_KV * SWA_HD],
                            sin_ref[pl.ds(k0, blk), 0:SWA_KV * SWA_HD], SWA_HD).astype(BF16))
            vs.append(v_ref[...].astype(BF16))
        k_loc = jnp.concatenate(ks, axis=0)
        v_loc = jnp.concatenate(vs, axis=0)
        qpos = n * qrows + lax.broadcasted_iota(jnp.int32, (qrows, nk * blk), 0)
        kpos = first * blk + lax.broadcasted_iota(jnp.int32, (qrows, nk * blk), 1)
        valid = (jnp.abs(kpos - qpos) <= SWA_WIN) & (kpos >= 0) & (kpos < seq_len)
    qb = (q * qscale).astype(BF16)
    ck = ck_ref[...].astype(BF16)
    cv = cv_ref[...].astype(BF16)
    nt = (((1,), (1,)), ((), ()))
    outs = []
    for h in range(SWA_HEADS):
        g = h // (SWA_HEADS // SWA_KV)
        qh = qb[:, h * SWA_HD:(h + 1) * SWA_HD]
        gs = slice(g * SWA_HD, (g + 1) * SWA_HD)
        sink = sink_ref[h:h + 1, 0:1]
        s_ctx = lax.dot_general(qh, ck[:, gs], nt, preferred_element_type=F32)
        m = jnp.maximum(jnp.max(s_ctx, -1, keepdims=True), sink)
        if local:
            s_loc = lax.dot_general(qh, k_loc[:, gs], nt, preferred_element_type=F32)
            s_loc = jnp.where(valid, s_loc, NEG_INF)
            m = jnp.maximum(m, jnp.max(s_loc, -1, keepdims=True))
        p_ctx = jnp.exp2(s_ctx - m)
        den = jnp.sum(p_ctx, -1, keepdims=True) + jnp.exp2(sink - m)
        acc = jnp.dot(p_ctx.astype(BF16), cv[:, gs], preferred_element_type=F32)
        if local:
            p_loc = jnp.exp2(s_loc - m)
            den = den + jnp.sum(p_loc, -1, keepdims=True)
            acc = acc + jnp.dot(p_loc.astype(BF16), v_loc[:, gs], preferred_element_type=F32)
        outs.append(acc / den)
    o_ref[...] = jnp.concatenate(outs, axis=-1)


def swa_attend(p, pc, B, L, C, sink, tables, local):
    blk = SWA_BLOCK
    nb = L // blk
    qrows = min(SWA_QROWS, L)
    nq = L // qrows
    qpb = qrows // blk
    qcol = P_SW // (SWA_HEADS * SWA_HD)
    kcol = (P_SW + SWA_HEADS * SWA_HD) // (SWA_KV * SWA_HD)
    kvw = SWA_KV * SWA_HD

    def nbr(j, col):
        return lambda b, n: (b * nb + jnp.clip(n * qpb + j, 0, nb - 1), col)

    nbrs = range(-1, qpb + 1)

    cos, sin = tables
    sink_rows = jnp.broadcast_to((sink.astype(F32) * math.log2(math.e))[:, None], (SWA_HEADS, LANES))
    tab_spec = pl.BlockSpec(cos.shape, lambda b, n: (0, 0))
    out = pl.pallas_call(
        functools.partial(_swa_kernel, local, L, qrows),
        out_shape=jax.ShapeDtypeStruct((B * L, SWA_HEADS * SWA_HD), F32),
        grid=(B, nq),
        in_specs=[pl.BlockSpec((qrows, SWA_HEADS * SWA_HD), lambda b, n: (b * nq + n, qcol))]
        + [pl.BlockSpec((blk, kvw), nbr(j, kcol)) for j in nbrs]
        + [pl.BlockSpec((blk, kvw), nbr(j, kcol + 1)) for j in nbrs]
        + [pl.BlockSpec((C, kvw), lambda b, n: (b, kcol)),
           pl.BlockSpec((C, kvw), lambda b, n: (b, kcol + 1)),
           tab_spec, tab_spec,
           pl.BlockSpec((SWA_HEADS, LANES), lambda b, n: (0, 0))],
        out_specs=pl.BlockSpec((qrows, SWA_HEADS * SWA_HD), lambda b, n: (b * nq + n, 0)),
        compiler_params=pltpu.CompilerParams(
            dimension_semantics=("parallel", "parallel"), vmem_limit_bytes=VMEM_LIMIT),
        name="swa",
    )(p, *([p] * (2 * len(nbrs))), pc, pc, cos, sin, sink_rows)
    return out.reshape(B, L, SWA_HEADS * SWA_HD)


def swa_branch(p, pc, B, L, C, need_ctx, sink, tables):
    y = swa_attend(p, pc, B, L, C, sink, tables, True)
    yc = swa_attend(pc, pc, B, C, C, sink, tables, False) if need_ctx else None
    return y, yc


def _seg_blocks(value, width, seg):
    row = lax.broadcasted_iota(jnp.int32, (width, width), 0)
    col = lax.broadcasted_iota(jnp.int32, (width, width), 1)
    return jnp.where(row // seg == col // seg, value, 0.0).astype(BF16)


def _head_blocks(value):
    return _seg_blocks(value, RW_W, RW_HD)


def _head_sum(x, blocks):
    hi = x.astype(BF16)
    lo = (x - hi.astype(F32)).astype(BF16)
    return (jnp.dot(hi, blocks, preferred_element_type=F32)
            + jnp.dot(lo, blocks, preferred_element_type=F32))


RW_OFF = tuple(int(o) for o in np.cumsum((0,) + RW_SIZES))
RW_PAD = 1280
RW_WIN_WD = RW_OFF[3]
RW_WIN_LO = RW_OFF[5]
RW_WIN_HI = RW_WIN_LO + LANES


def _rwkv_prep_kernel(tm, tiles_per_seq, x_ref, xp_ref, xn_ref, mu_ref, w0_ref, w2f_ref, w2b_ref,
                      a0_ref, a2_ref, g2f_ref, g2b_ref, kk_ref, ka_ref,
                      r_o, k_o, v_o, kkn_o, b_o, df_o, db_o, gf_o, gb_o):
    i = pl.program_id(0)
    first = (i % tiles_per_seq) == 0
    last = (i % tiles_per_seq) == tiles_per_seq - 1
    x = x_ref[...].astype(F32)
    row = lax.broadcasted_iota(jnp.int32, x.shape, 0)
    prev_row = jnp.where(first, 0.0, xp_ref[...].astype(F32)[P_HALO - 1:P_HALO, :])
    next_row = jnp.where(last, 0.0, xn_ref[...].astype(F32)[0:1, :])
    xm1 = jnp.where(row == 0, prev_row, pltpu.roll(x, 1, 0))
    xp1 = jnp.where(row == tm - 1, next_row, pltpu.roll(x, tm - 1, 0))
    x = x + (0.5 * (xm1 + xp1) - x) * mu_ref[...]

    r = x[:, RW_OFF[0]:RW_OFF[1]]
    k = x[:, RW_OFF[1]:RW_OFF[2]]
    v = x[:, RW_OFF[2]:RW_OFF[3]]
    wd = jnp.tanh(x[:, RW_WIN_WD:RW_WIN_WD + LANES]).astype(BF16)
    lo = x[:, RW_WIN_LO:RW_WIN_LO + RW_W]
    hi = x[:, RW_WIN_HI:RW_WIN_HI + RW_W]

    def decay(w2_ref, w0):
        wlog = -jax.nn.softplus(-(w0 + jnp.dot(wd, w2_ref[...], preferred_element_type=F32))) - 0.5
        return jnp.exp(-jnp.exp(wlog))

    a = jax.nn.sigmoid(a0_ref[...] + jnp.dot(lo[:, 0:LANES].astype(BF16), a2_ref[...],
                                             preferred_element_type=F32))
    kk = k * kk_ref[...]
    norm = jnp.sqrt(_head_sum(kk * kk, _head_blocks(1.0)))
    kk = kk / jnp.maximum(norm, 1e-12)
    r_o[...] = r
    k_o[...] = k * (1.0 + (a - 1.0) * ka_ref[...])
    v_o[...] = v
    kkn_o[...] = kk
    b_o[...] = kk * a
    df_o[...] = decay(w2f_ref, w0_ref[0:1, :])
    db_o[...] = decay(w2b_ref, w0_ref[1:2, :])
    gf_o[...] = jnp.dot(jax.nn.sigmoid(lo).astype(BF16), g2f_ref[...], preferred_element_type=F32)
    gb_o[...] = jnp.dot(jax.nn.sigmoid(hi).astype(BF16), g2b_ref[...], preferred_element_type=F32)


def rwkv_prep(p, B, L, mu, w0, w2, a0, a2, g2, k_k, k_a):
    tm = min(RW_PREP_TM, L)
    tiles_per_seq = L // tm
    n_tok = B * L
    hb = tm // P_HALO
    col = P_RW // RW_PAD

    def rows(n, r0, src):
        return jnp.zeros((n, RW_W), F32).at[r0:r0 + src.shape[0]].set(src).astype(BF16)

    w2f = rows(LANES, RW_OFF[3] - RW_WIN_WD, w2[0])
    w2b = rows(LANES, RW_OFF[4] - RW_WIN_WD, w2[1])
    a2p = rows(LANES, RW_OFF[5] - RW_WIN_LO, a2)
    g2f = rows(RW_W, RW_OFF[6] - RW_WIN_LO, g2[0])
    g2b = rows(RW_W, RW_OFF[7] - RW_WIN_HI, g2[1])
    mu_p = jnp.pad(mu, (0, RW_PAD - mu.shape[0])).reshape(1, RW_PAD)

    def const(shape):
        return pl.BlockSpec(shape, lambda i: (0,) * len(shape))

    vec = const((1, RW_W))
    out_spec = pl.BlockSpec((tm, RW_W), lambda i: (i, 0))
    return pl.pallas_call(
        functools.partial(_rwkv_prep_kernel, tm, tiles_per_seq),
        out_shape=[jax.ShapeDtypeStruct((n_tok, RW_W), F32)] * 9,
        grid=(n_tok // tm,),
        in_specs=[pl.BlockSpec((tm, RW_PAD), lambda i: (i, col)),
                  pl.BlockSpec((P_HALO, RW_PAD), lambda i: (jnp.maximum(i * hb - 1, 0), col)),
                  pl.BlockSpec((P_HALO, RW_PAD), lambda i: (jnp.minimum((i + 1) * hb, n_tok // P_HALO - 1), col)),
                  const((1, RW_PAD)), const((2, RW_W)), const((LANES, RW_W)), const((LANES, RW_W)),
                  vec, const((LANES, RW_W)), const((RW_W, RW_W)), const((RW_W, RW_W)), vec, vec],
        out_specs=[out_spec] * 9,
        compiler_params=pltpu.CompilerParams(
            dimension_semantics=("parallel",), vmem_limit_bytes=VMEM_LIMIT),
        name="rwkv_prep",
    )(p, p, p, mu_p, w0, w2f, w2b, a0.reshape(1, RW_W), a2p, g2f, g2b,
      k_k.reshape(1, RW_W), k_a.reshape(1, RW_W))


def _rwkv_scan_kernel(tblk, nb,
                      wf_ref, kkf_ref, bf_ref, kf_ref, vf_ref, rf_ref,
                      wb_ref, kkb_ref, bb_ref, kb_ref, vb_ref, rb_ref,
                      of_ref, ob_ref, sf_ref, sb_ref, sf16_ref, sb16_ref):
    c = pl.program_id(0)

    @pl.when(c == 0)
    def _():
        for ref in (sf_ref, sb_ref, sf16_ref, sb16_ref):
            ref[...] = jnp.zeros_like(ref)

    row = lax.broadcasted_iota(jnp.int32, (RW_W, RW_W), 0)
    col = lax.broadcasted_iota(jnp.int32, (RW_W, RW_W), 1)
    ones_blk = jnp.where(row // RW_HD == col // RW_HD, 1.0, 0.0).astype(BF16)
    r64 = lax.broadcasted_iota(jnp.int32, (RW_HD, RW_W), 0)
    c64 = lax.broadcasted_iota(jnp.int32, (RW_HD, RW_W), 1)
    eye_t = jnp.where(r64 == c64 % RW_HD, 1.0, 0.0).astype(BF16)
    r8 = lax.broadcasted_iota(jnp.int32, (8, RW_W), 0)
    c8 = lax.broadcasted_iota(jnp.int32, (8, RW_W), 1)
    head_sel = jnp.where(r8 == c8 // RW_HD, 1.0, 0.0).astype(BF16)

    fwd = (wf_ref, kkf_ref, bf_ref, kf_ref, vf_ref, rf_ref, sf_ref, sf16_ref)
    bwd = (wb_ref, kkb_ref, bb_ref, kb_ref, vb_ref, rb_ref, sb_ref, sb16_ref)
    nrow = nb * RW_HD

    def one_dir(idx, refs):
        w_ref, kk_ref, b_ref, k_ref, v_ref, r_ref, s_ref, s16_ref = refs
        zs = []
        for b in range(nb):
            kk16 = kk_ref[idx, b:b + 1, :].astype(BF16)
            v16 = v_ref[idx, b:b + 1, :].astype(BF16)
            lhs = jnp.concatenate([s16_ref[b] * kk16, eye_t * v16], axis=0)
            red = jnp.dot(lhs, ones_blk, preferred_element_type=F32)
            s_new = (s_ref[b] * w_ref[idx, b:b + 1, :] - red[0:RW_HD] * b_ref[idx, b:b + 1, :]
                     + red[RW_HD:2 * RW_HD] * k_ref[idx, b:b + 1, :])
            s_ref[b] = s_new
            s16 = s_new.astype(BF16)
            s16_ref[b] = s16
            zs.append(s16 * r_ref[idx, b:b + 1, :].astype(BF16))
        return jnp.concatenate(zs, axis=0)

    def emit(idx, z, o_ref):
        o = lax.dot_general(head_sel, z, (((1,), (1,)), ((), ())), preferred_element_type=F32)
        o_ref[idx] = o[0:RW_HEADS]

    def step(i, carry):
        z_f, z_b = carry
        ib = tblk - 1 - i
        emit(jnp.maximum(i - 1, 0), z_f, of_ref)
        emit(jnp.minimum(ib + 1, tblk - 1), z_b, ob_ref)
        return one_dir(i, fwd), one_dir(ib, bwd)

    z0 = jnp.zeros((nrow, RW_W), BF16)
    z_f, z_b = lax.fori_loop(0, tblk, step, (z0, z0), unroll=8)
    emit(tblk - 1, z_f, of_ref)
    emit(0, z_b, ob_ref)


def rwkv_scan_pallas(w_f, w_b, kk, bvec, k, v, r, n_ctx):
    n, nb, _ = kk.shape
    tblk = RW_TBLK
    nblk = n // tblk
    nblk_ctx = n_ctx // tblk

    def fmap(c):
        return (c, 0, 0)

    def bmap(c):
        return (jnp.where(c < nblk_ctx, nblk_ctx - 1 - c, nblk + nblk_ctx - 1 - c), 0, 0)

    blk = (tblk, nb, RW_W)
    oblk = (tblk, RW_HEADS, nb * RW_HD)
    return pl.pallas_call(
        functools.partial(_rwkv_scan_kernel, tblk, nb),
        out_shape=[jax.ShapeDtypeStruct((n, RW_HEADS, nb * RW_HD), F32)] * 2,
        grid=(nblk,),
        in_specs=[pl.BlockSpec(blk, fmap)] * 6 + [pl.BlockSpec(blk, bmap)] * 6,
        out_specs=[pl.BlockSpec(oblk, fmap), pl.BlockSpec(oblk, bmap)],
        scratch_shapes=[pltpu.VMEM((nb, RW_HD, RW_W), F32), pltpu.VMEM((nb, RW_HD, RW_W), F32),
                        pltpu.VMEM((nb, RW_HD, RW_W), BF16), pltpu.VMEM((nb, RW_HD, RW_W), BF16)],
        compiler_params=pltpu.CompilerParams(dimension_semantics=("arbitrary",),
                                             vmem_limit_bytes=VMEM_LIMIT),
        name="rwkv_scan",
    )(w_f, kk, bvec, k, v, r, w_b, kk, bvec, k, v, r)


def _rwkv_post_kernel(nb, of_ref, ob_ref, r_ref, k_ref, v_ref, gf_ref, gb_ref, rk_ref, lng_ref, lnb_ref,
                      y_ref):
    mean_blk = _head_blocks(1.0 / RW_HD)
    sum_blk = _head_blocks(1.0)

    def gn(o_ref, b):
        o = jnp.concatenate([o_ref[:, h, b * RW_HD:(b + 1) * RW_HD] for h in range(RW_HEADS)], axis=-1)
        oc = o - _head_sum(o, mean_blk)
        var = _head_sum(oc * oc, mean_blk)
        return oc * lax.rsqrt(var + RW_GN_EPS) * lng_ref[...] + lnb_ref[...]

    for b in range(nb):
        bonus = _head_sum(r_ref[b] * k_ref[b] * rk_ref[...], sum_blk) * v_ref[b]
        y_ref[b] = (gn(of_ref, b) + bonus) * gf_ref[b] + (gn(ob_ref, b) + bonus) * gb_ref[b]


def rwkv_post(o_f, o_b, r, k, v, g_f, g_b, r_k, lnx_g, lnx_b, B, L, start):
    tm = min(RW_POST_TM, L)
    assert start % tm == 0 and L % tm == 0
    tok = pl.BlockSpec((B, tm, RW_W), lambda i: (0, i, 0))
    o_spec = pl.BlockSpec((tm, RW_HEADS, B * RW_HD), lambda i: (start // tm + i, 0, 0))
    vec = pl.BlockSpec((1, RW_W), lambda i: (0, 0))
    streams = [t.reshape(B, L, RW_W) for t in (r, k, v, g_f, g_b)]
    return pl.pallas_call(
        functools.partial(_rwkv_post_kernel, B),
        out_shape=jax.ShapeDtypeStruct((B, L, RW_W), F32),
        grid=(L // tm,),
        in_specs=[o_spec] * 2 + [tok] * 5 + [vec] * 3,
        out_specs=tok,
        compiler_params=pltpu.CompilerParams(
            dimension_semantics=("parallel",), vmem_limit_bytes=VMEM_LIMIT),
        name="rwkv_post",
    )(o_f, o_b, *streams, r_k.reshape(1, RW_W), lnx_g.reshape(1, RW_W), lnx_b.reshape(1, RW_W))


def rwkv_branch(p, pc, B, L, C, need_ctx, mu, w0, w2, a0, a2, g2, k_k, k_a, r_k, lnx_g, lnx_b):
    r, k, v, kk, bv, d_f, d_b, g_f, g_b = rwkv_prep(p, B, L, mu, w0, w2, a0, a2, g2, k_k, k_a)
    rc, kc, vc, kkc, bc, dc_f, dc_b, gc_f, gc_b = rwkv_prep(pc, B, C, mu, w0, w2, a0, a2, g2, k_k, k_a)

    def tmajor(xc, xl):
        t = jnp.concatenate([xc.reshape(B, C, RW_W), xl.reshape(B, L, RW_W)], axis=1)
        return jnp.moveaxis(t, 1, 0)

    pf, pb = rwkv_scan_pallas(tmajor(dc_f, d_f), tmajor(dc_b, d_b), tmajor(kkc, kk), tmajor(bc, bv),
                              tmajor(kc, k), tmajor(vc, v), tmajor(rc, r), C)

    y = rwkv_post(pf, pb, r, k, v, g_f, g_b, r_k, lnx_g, lnx_b, B, L, C)
    yc = rwkv_post(pf, pb, rc, kc, vc, gc_f, gc_b, r_k, lnx_g, lnx_b, B, C, 0) if need_ctx else None
    return y, yc


def _diff_attn_kernel(q_ref, kt_ref, v_ref, lam_ref, gain_ref, o_ref):
    q = q_ref[0]
    v = v_ref[0]
    lam = lam_ref[...]
    tq = q.shape[0]
    head_of_lane = lax.broadcasted_iota(jnp.int32, (1, DF_W), 1) // DF_VD
    acc = jnp.zeros((tq, DF_W), F32)
    for h in range(DF_HEADS):
        parts = []
        for c in range(2):
            j = 2 * h + c
            s = jnp.dot(q[:, DF_HD * j:DF_HD * (j + 1)], kt_ref[0, j],
                        preferred_element_type=F32)
            p = jnp.exp2(s - jnp.max(s, -1, keepdims=True))
            l = jnp.sum(p, -1, keepdims=True)
            pv = jnp.dot(p.astype(BF16), v, preferred_element_type=F32)
            parts.append(pv / l)
        acc = jnp.where(head_of_lane == h, parts[0] - lam * parts[1], acc)
    ms = _head_sum(acc * acc, _seg_blocks(1.0 / DF_VD, DF_W, DF_VD))
    o_ref[0] = acc * lax.rsqrt(ms + 1e-5) * gain_ref[...]


def _diff_pre_kernel(rope, x_ref, cos_ref, sin_ref, q_ref, kt_ref, v_ref):
    x = x_ref[...].astype(F32)
    q, k, v = x[:, 0:DF_W], x[:, DF_W:2 * DF_W], x[:, 2 * DF_W:3 * DF_W]
    if rope:
        cos, sin = cos_ref[...], sin_ref[...]
        q = _rope(q, cos, sin, DF_HD)
        k = _rope(k, cos, sin, DF_HD)
    q_ref[...] = (q * (DF_HD ** -0.5 * math.log2(math.e))).astype(BF16)
    kt_ref[0] = k.T.astype(BF16)
    v_ref[...] = v.astype(BF16)


def diff_pre(p, B, L, tables, rope):
    tm = min(RW_PREP_TM, L)
    tiles_per_seq = L // tm
    n_tok = B * L
    width = 3 * DF_W
    col = P_DF // width
    cos, sin = tables
    tok = pl.BlockSpec((tm, DF_W), lambda i: (i, 0))
    tab = pl.BlockSpec((tm, DF_W), lambda i: (i % tiles_per_seq, 0))
    q, kt, v = pl.pallas_call(
        functools.partial(_diff_pre_kernel, rope),
        out_shape=[jax.ShapeDtypeStruct((n_tok, DF_W), BF16),
                   jax.ShapeDtypeStruct((B, DF_W, L), BF16),
                   jax.ShapeDtypeStruct((n_tok, DF_W), BF16)],
        grid=(n_tok // tm,),
        in_specs=[pl.BlockSpec((tm, width), lambda i: (i, col)), tab, tab],
        out_specs=[tok, pl.BlockSpec((1, DF_W, tm), lambda i: (i // tiles_per_seq, 0, i % tiles_per_seq)),
                   tok],
        compiler_params=pltpu.CompilerParams(
            dimension_semantics=("parallel",), vmem_limit_bytes=VMEM_LIMIT),
        name="diff_pre",
    )(p, cos, sin)
    return q.reshape(B, L, DF_W), kt, v.reshape(B, L, DF_W)


def diff_attend(qs, kt, vb, lam, lam_init, subln_g):
    B, Lq = qs.shape[:2]
    S = vb.shape[1]
    tq = min(DF_TQ, Lq)
    kt = kt.reshape(B, 2 * DF_HEADS, DF_HD, S)
    gain = jnp.tile(subln_g.astype(F32) * (1.0 - lam_init), DF_HEADS).reshape(1, DF_W)
    return pl.pallas_call(
        _diff_attn_kernel,
        out_shape=jax.ShapeDtypeStruct((B, Lq, DF_W), F32),
        grid=(B, Lq // tq),
        in_specs=[pl.BlockSpec((1, tq, DF_W), lambda b, i: (b, i, 0)),
                  pl.BlockSpec((1, 2 * DF_HEADS, DF_HD, S), lambda b, i: (b, 0, 0, 0)),
                  pl.BlockSpec((1, S, DF_W), lambda b, i: (b, 0, 0)),
                  pl.BlockSpec((1, 1), lambda b, i: (0, 0)),
                  pl.BlockSpec((1, DF_W), lambda b, i: (0, 0))],
        out_specs=pl.BlockSpec((1, tq, DF_W), lambda b, i: (b, i, 0)),
        compiler_params=pltpu.CompilerParams(
            dimension_semantics=("parallel", "parallel"), vmem_limit_bytes=VMEM_LIMIT),
        name="diff_attn",
    )(qs, kt, vb, lam.reshape(1, 1).astype(F32), gain)


def diff_branch(p, pc, B, L, C, need_ctx, tables, lq1, lk1, lq2, lk2, subln_g, lam_init):
    lam = (jnp.exp(jnp.sum(lq1.astype(F32) * lk1.astype(F32)))
           - jnp.exp(jnp.sum(lq2.astype(F32) * lk2.astype(F32))) + lam_init)
    q, kt, v = diff_pre(p, B, L, tables, True)
    qc, ktc, vc = diff_pre(pc, B, C, tables, False)
    y = diff_attend(q, jnp.concatenate([kt, ktc], axis=2), jnp.concatenate([v, vc], axis=1),
                    lam, lam_init, subln_g)
    yc = diff_attend(qc, ktc, vc, lam, lam_init, subln_g) if need_ctx else None
    return y, yc


def _ln_rows(x):
    mu = jnp.mean(x, -1, keepdims=True)
    xc = x - mu
    return xc * lax.rsqrt(jnp.mean(xc * xc, -1, keepdims=True) + LN_EPS)


def _ffn_kernel(tm, tiles_per_seq, n_chunks,
                h_ref, hprev_ref, hnext_ref, sh_ref, sc_ref, gate_ref,
                wa_ref, wb_ref, cwa_ref, cwb_ref, cba_ref, cbb_ref, wd_ref, lng_ref, lnb_ref,
                o_ref, u_ref, acc_ref):
    i = pl.program_id(0)
    j = pl.program_id(1)
    n = tm + 2 * HALO

    @pl.when(j == 0)
    def _():
        scale = 1.0 + sc_ref[0]
        shift = sh_ref[0]
        first = (i % tiles_per_seq) == 0
        last = (i % tiles_per_seq) == tiles_per_seq - 1
        u_ref[0:tm, :] = (_ln_rows(h_ref[...]) * scale + shift).astype(BF16)
        un = _ln_rows(hnext_ref[...]) * scale + shift
        up = _ln_rows(hprev_ref[...]) * scale + shift
        u_ref[tm:tm + HALO, :] = jnp.where(last, 0.0, un).astype(BF16)
        u_ref[tm + HALO:n, :] = jnp.where(first, 0.0, up).astype(BF16)
        acc_ref[...] = jnp.zeros_like(acc_ref)

    u = u_ref[...]

    def conv(w_ref, cw_ref, cb_ref):
        x = jnp.dot(u, w_ref[...], preferred_element_type=F32)
        cw = cw_ref[...]
        y = (pltpu.roll(x, 1, 0)[0:tm] * cw[0:1] + x[0:tm] * cw[1:2]
             + pltpu.roll(x, n - 1, 0)[0:tm] * cw[2:3] + cb_ref[...])
        return y

    a = conv(wa_ref, cwa_ref, cba_ref)
    b = conv(wb_ref, cwb_ref, cbb_ref)
    g = ((0.5 * a) * (1.0 + jnp.tanh(0.5 * a)) * b).astype(BF16)
    acc_ref[...] += jnp.dot(g, wd_ref[...], preferred_element_type=F32)

    @pl.when(j == n_chunks - 1)
    def _():
        y = DN_ALPHA * h_ref[...] + gate_ref[0] * acc_ref[...]
        o_ref[...] = _ln_rows(y) * lng_ref[...] + lnb_ref[...]


def ffn_block(h, shift, scale, gate, w_up, conv_w, conv_b, w_down, ln_g, ln_b):
    B, L, D = h.shape
    tm = min(FFN_TM, L)
    tiles_per_seq = L // tm
    n_tok = B * L
    n_chunks = D_FF // FFN_CHUNK
    hb = tm // HALO

    def bcast(t):
        return jnp.broadcast_to(t.astype(F32), (B, 1, D))

    wu = w_up.astype(BF16)
    wd = w_down.astype(BF16)
    cb = conv_b.reshape(1, 2 * D_FF)
    h2 = h.reshape(n_tok, D)
    mod_spec = pl.BlockSpec((1, 1, D), lambda i, j: (i // tiles_per_seq, 0, 0))
    vec_spec = pl.BlockSpec((1, D), lambda i, j: (0, 0))
    out = pl.pallas_call(
        functools.partial(_ffn_kernel, tm, tiles_per_seq, n_chunks),
        out_shape=jax.ShapeDtypeStruct((n_tok, D), F32),
        grid=(n_tok // tm, n_chunks),
        in_specs=[pl.BlockSpec((tm, D), lambda i, j: (i, 0)),
                  pl.BlockSpec((HALO, D), lambda i, j: (jnp.maximum(i * hb - 1, 0), 0)),
                  pl.BlockSpec((HALO, D), lambda i, j: (jnp.minimum((i + 1) * hb, n_tok // HALO - 1), 0)),
                  mod_spec, mod_spec, mod_spec,
                  pl.BlockSpec((D, FFN_CHUNK), lambda i, j: (0, j)),
                  pl.BlockSpec((D, FFN_CHUNK), lambda i, j: (0, n_chunks + j)),
                  pl.BlockSpec((3, FFN_CHUNK), lambda i, j: (0, j)),
                  pl.BlockSpec((3, FFN_CHUNK), lambda i, j: (0, n_chunks + j)),
                  pl.BlockSpec((1, FFN_CHUNK), lambda i, j: (0, j)),
                  pl.BlockSpec((1, FFN_CHUNK), lambda i, j: (0, n_chunks + j)),
                  pl.BlockSpec((FFN_CHUNK, D), lambda i, j: (j, 0)),
                  vec_spec, vec_spec],
        out_specs=pl.BlockSpec((tm, D), lambda i, j: (i, 0)),
        scratch_shapes=[pltpu.VMEM((tm + 2 * HALO, D), BF16), pltpu.VMEM((tm, D), F32)],
        compiler_params=pltpu.CompilerParams(
            dimension_semantics=("parallel", "arbitrary"), vmem_limit_bytes=VMEM_LIMIT),
        name="conv_ffn",
    )(h2, h2, h2, bcast(shift), bcast(scale), bcast(gate), wu, wu, conv_w, conv_w, cb, cb, wd,
      ln_g.reshape(1, D), ln_b.reshape(1, D))
    return out.reshape(B, L, D)


P_GATE, P_HY, P_RW, P_SW, P_DF = 0, 4096, 5120, 6400, 6912
P_COLS = 7680
IN_CHUNK = 768


def _in_proj_kernel(x_ref, sh_ref, sc_ref, w_ref, o_ref, u_ref):
    @pl.when(pl.program_id(1) == 0)
    def _():
        u_ref[...] = (_ln_rows(x_ref[...]) * (1.0 + sc_ref[0]) + sh_ref[0]).astype(BF16)

    o_ref[...] = jnp.dot(u_ref[...], w_ref[...], preferred_element_type=F32).astype(BF16)


def in_proj(h, shift, scale, w_in_p):
    B, L, D = h.shape
    tm = min(IN_TM, L)
    tiles_per_seq = L // tm
    n_tok = B * L

    def bcast(t):
        return jnp.broadcast_to(t.astype(F32), (B, 1, D))

    mod_spec = pl.BlockSpec((1, 1, D), lambda i, j: (i // tiles_per_seq, 0, 0))
    return pl.pallas_call(
        _in_proj_kernel,
        out_shape=jax.ShapeDtypeStruct((n_tok, P_COLS), BF16),
        grid=(n_tok // tm, P_COLS // IN_CHUNK),
        in_specs=[pl.BlockSpec((tm, D), lambda i, j: (i, 0)), mod_spec, mod_spec,
                  pl.BlockSpec((D, IN_CHUNK), lambda i, j: (0, j))],
        out_specs=pl.BlockSpec((tm, IN_CHUNK), lambda i, j: (i, j)),
        scratch_shapes=[pltpu.VMEM((tm, D), BF16)],
        compiler_params=pltpu.CompilerParams(
            dimension_semantics=("parallel", "arbitrary"), vmem_limit_bytes=VMEM_LIMIT),
        name="in_proj",
    )(h.reshape(n_tok, D), bcast(shift), bcast(scale), w_in_p)


def pack_w_in(w):
    offs = np.cumsum((0,) + IN_SIZES)
    hy, sw, rw, df, gt = [w[:, offs[j]:offs[j + 1]] for j in range(len(IN_SIZES))]

    def padto(t, n):
        return jnp.pad(t, ((0, 0), (0, n - t.shape[1])))

    return jnp.concatenate([gt, padto(hy, P_RW - P_HY), padto(rw, P_SW - P_RW), sw, df],
                           axis=1).astype(BF16)


def _merge_kernel(yh_ref, ys_ref, yr_ref, yd_ref, g0_ref, g1_ref, g2_ref, g3_ref, h_ref, gate_ref,
                  wbr_ref, wo_ref, lng_ref, lnb_ref, o_ref):
    acc = None
    for j, (y_ref, g_ref) in enumerate(((yh_ref, g0_ref), (ys_ref, g1_ref), (yr_ref, g2_ref),
                                        (yd_ref, g3_ref))):
        term = jax.nn.sigmoid(g_ref[...].astype(F32)) * jnp.dot(y_ref[0].astype(BF16), wbr_ref[j],
                                                    preferred_element_type=F32)
        acc = term if acc is None else acc + term
    mix = jnp.dot(acc.astype(BF16), wo_ref[...], preferred_element_type=F32)
    y = DN_ALPHA * h_ref[...] + gate_ref[0] * mix
    o_ref[...] = _ln_rows(y) * lng_ref[...] + lnb_ref[...]


def merge_block(ys, p, h, gate, w_br, w_o, ln_g, ln_b):
    B, L, D = h.shape
    n_tok = B * L
    tm = min(MERGE_TM, L)
    tiles_per_seq = L // tm
    y_spec = pl.BlockSpec((1, tm, BR_W), lambda i: (i // tiles_per_seq, i % tiles_per_seq, 0))
    vec_spec = pl.BlockSpec((1, D), lambda i: (0, 0))
    g_specs = [pl.BlockSpec((tm, D), functools.partial(lambda j, i: (i, j), j)) for j in range(N_BRANCH)]
    out = pl.pallas_call(
        _merge_kernel,
        out_shape=jax.ShapeDtypeStruct((n_tok, D), F32),
        grid=(n_tok // tm,),
        in_specs=[y_spec] * N_BRANCH + g_specs + [
            pl.BlockSpec((tm, D), lambda i: (i, 0)),
            pl.BlockSpec((1, 1, D), lambda i: (i // tiles_per_seq, 0, 0)),
            pl.BlockSpec((N_BRANCH, BR_W, D), lambda i: (0, 0, 0)),
            pl.BlockSpec((D, D), lambda i: (0, 0)), vec_spec, vec_spec],
        out_specs=pl.BlockSpec((tm, D), lambda i: (i, 0)),
        compiler_params=pltpu.CompilerParams(
            dimension_semantics=("parallel",), vmem_limit_bytes=VMEM_LIMIT),
        name="merge",
    )(*ys, p, p, p, p, h.reshape(n_tok, D),
      jnp.broadcast_to(gate.astype(F32), (B, 1, D)), w_br.astype(BF16), w_o.astype(BF16),
      ln_g.reshape(1, D), ln_b.reshape(1, D))
    return out.reshape(B, L, D)


def kernel(x, c, ctx, c_ctx, ada_w, ada_b, w_in, hy_conv_w, hy_conv_b, hy_f_w1, hy_f_b1,
           hy_f_w2, hy_f_b2, hy_f_w3, hy_f_freq, hy_bias, swa_sink, rwkv_mu, rwkv_w0, rwkv_w2,
           rwkv_a0, rwkv_a2, rwkv_g2, rwkv_kk, rwkv_ka, rwkv_rk, rwkv_lnx_g, rwkv_lnx_b,
           diff_lq1, diff_lk1, diff_lq2, diff_lk2, diff_subln_g, w_branch, w_out, ln1_g, ln1_b,
           ffn_w_up, ffn_conv_w, ffn_conv_b, ffn_w_down, ln2_g, ln2_b):
    h, hc = x, ctx
    mats, mats_c = dft_mats(x.shape[1]), dft_mats(ctx.shape[1])
    swa_tables = rope_tables(x.shape[1], SWA_HD, SWA_HEADS * SWA_HD)
    df_tables = rope_tables(x.shape[1], DF_HD, DF_W)
    s_lat = jax.nn.silu(c)
    s_ctx = jax.nn.silu(c_ctx)
    for i in range(DEPTH):
        need_ctx = i < DEPTH - 1
        mod = (s_lat @ ada_w[i] + ada_b[i])[:, None, :]
        mod_c = s_ctx @ ada_w[i] + ada_b[i]
        sh1, sc1, g1, sh2, sc2, g2 = jnp.split(mod, 6, -1)
        csh1, csc1, cg1, csh2, csc2, cg2 = jnp.split(mod_c, 6, -1)

        B, L, D = h.shape
        C = hc.shape[1]
        w_in_p = pack_w_in(w_in[i])
        p = in_proj(h, sh1, sc1, w_in_p)
        pc = in_proj(hc, csh1.reshape(1, 1, D), csc1.reshape(1, 1, D), w_in_p)

        y_hy, yc_hy = hyena_branch(p, pc, B, L, C, need_ctx, mats, mats_c, hy_conv_w[i], hy_conv_b[i],
                                   hy_f_w1[i], hy_f_b1[i], hy_f_w2[i], hy_f_b2[i], hy_f_w3[i],
                                   hy_f_freq[i], hy_bias[i])
        y_sw, yc_sw = swa_branch(p, pc, B, L, C, need_ctx, swa_sink[i], swa_tables)
        y_rw, yc_rw = rwkv_branch(p, pc, B, L, C, need_ctx, rwkv_mu[i], rwkv_w0[i], rwkv_w2[i],
                                  rwkv_a0[i], rwkv_a2[i], rwkv_g2[i], rwkv_kk[i], rwkv_ka[i],
                                  rwkv_rk[i], rwkv_lnx_g[i], rwkv_lnx_b[i])
        lam_init = 0.8 - 0.6 * math.exp(-0.3 * i)
        y_df, yc_df = diff_branch(p, pc, B, L, C, need_ctx, df_tables, diff_lq1[i], diff_lk1[i],
                                  diff_lq2[i], diff_lk2[i], diff_subln_g[i], lam_init)

        h = merge_block((y_hy, y_sw, y_rw, y_df), p, h, g1, w_branch[i], w_out[i], ln1_g[i], ln1_b[i])
        h = ffn_block(h, sh2, sc2, g2, ffn_w_up[i], ffn_conv_w[i], ffn_conv_b[i], ffn_w_down[i],
                      ln2_g[i], ln2_b[i])

        if need_ctx:
            hc = merge_block((yc_hy, yc_sw, yc_rw, yc_df), pc, hc, cg1.reshape(1, 1, D), w_branch[i],
                             w_out[i], ln1_g[i], ln1_b[i])
            hc = ffn_block(hc, csh2.reshape(1, 1, -1), csc2.reshape(1, 1, -1), cg2.reshape(1, 1, -1),
                           ffn_w_up[i], ffn_conv_w[i], ffn_conv_b[i], ffn_w_down[i], ln2_g[i], ln2_b[i])
    return h
```

```python
import functools
import math

import jax
import jax.numpy as jnp
import numpy as np
from jax import lax
from jax.experimental import pallas as pl
from jax.experimental.pallas import tpu as pltpu

D_MODEL = 1024
DEPTH = 2
GRID_W = 64
ROPE_BASE = 10000.0
F32 = jnp.float32
BF16 = jnp.bfloat16
NEG_INF = -1e30
LN_EPS = 1e-6

HY_W = 256
HY_ORDER = 2
HY_DIRS = 2
HY_EMB = 33
HY_MIN_DECAY = math.log(1e-2) / 1.5
HY_MAX_DECAY = math.log(1e-2) / 0.3

SWA_HEADS = 4
SWA_KV = 2
SWA_HD = 64
SWA_WIN = 128
SWA_BLOCK = 128

RW_HEADS = 4
RW_HD = 64
RW_W = RW_HEADS * RW_HD
RW_DECAY_R = 64
RW_AAA_R = 64
RW_GATE_R = 128
RW_GN_EPS = 64e-5

DF_HEADS = 4
DF_HD = 32
DF_VD = 2 * DF_HD
DF_W = DF_HEADS * DF_VD

N_BRANCH = 4
BR_W = 256
D_FF = 2816
DN_ALPHA = (2 * DEPTH) ** 0.25

SWA_SIZES = (SWA_HEADS * SWA_HD, SWA_KV * SWA_HD, SWA_KV * SWA_HD)
RW_SIZES = (RW_W, RW_W, RW_W, RW_DECAY_R, RW_DECAY_R, RW_AAA_R, RW_GATE_R, RW_GATE_R)
DF_SIZES = (DF_W, DF_W, DF_W)
IN_SIZES = (3 * HY_W, sum(SWA_SIZES), sum(RW_SIZES), sum(DF_SIZES), N_BRANCH * D_MODEL)

V7X_VMEM_BYTES = 64 * 1024 * 1024
LANES = 128
MM_TM = 512
MM_TN = 512
RW_TBLK = 64
DF_TQ = 256
SWA_QROWS = 256
HALO = 8
P_HALO = 16
IN_TM = 2048
FFN_TM = 1024
FFN_CHUNK = 256
MERGE_TM = 512
HY_TK = 512
RW_PREP_TM = 512
RW_POST_TM = 256
VMEM_LIMIT = V7X_VMEM_BYTES * 3 // 4


def _mm_kernel(x_ref, w_ref, o_ref):
    o_ref[...] = jnp.dot(x_ref[...].astype(BF16), w_ref[...], preferred_element_type=F32)


def _mm(x, w):
    m, k = x.shape
    n = w.shape[1]
    n_pad = -(-n // LANES) * LANES
    wb = w.astype(BF16)
    if n_pad != n:
        wb = jnp.pad(wb, ((0, 0), (0, n_pad - n)))
    tn = MM_TN if n_pad % MM_TN == 0 else (256 if n_pad % 256 == 0 else LANES)
    tm = MM_TM if m % MM_TM == 0 else m
    out = pl.pallas_call(
        _mm_kernel,
        out_shape=jax.ShapeDtypeStruct((m, n_pad), F32),
        grid=(m // tm, n_pad // tn),
        in_specs=[pl.BlockSpec((tm, k), lambda i, j: (i, 0)),
                  pl.BlockSpec((k, tn), lambda i, j: (0, j))],
        out_specs=pl.BlockSpec((tm, tn), lambda i, j: (i, j)),
        compiler_params=pltpu.CompilerParams(
            dimension_semantics=("parallel", "parallel"), vmem_limit_bytes=VMEM_LIMIT),
        name="matmul",
    )(x, wb)
    return out[:, :n] if n_pad != n else out


def dft_mats(L):
    tb = 64
    k = jnp.arange(L, dtype=jnp.int32)[:, None]

    def table(t):
        ang = (((2 * k + 1) * t[None, :]) % (4 * L)).astype(F32) * (math.pi / (2 * L))
        return jnp.cos(ang), jnp.sin(ang)

    ca, sa = table(tb * jnp.arange(L // tb, dtype=jnp.int32))
    cb, sb = table(jnp.arange(tb, dtype=jnp.int32))
    c = (ca[:, :, None] * cb[:, None, :] - sa[:, :, None] * sb[:, None, :]).reshape(L, L)
    s = (sa[:, :, None] * cb[:, None, :] + ca[:, :, None] * sb[:, None, :]).reshape(L, L)
    ct = (ca.T[:, None, :] * cb.T[None, :, :] - sa.T[:, None, :] * sb.T[None, :, :]).reshape(L, L)
    st = (sa.T[:, None, :] * cb.T[None, :, :] + ca.T[:, None, :] * sb.T[None, :, :]).reshape(L, L)
    return c.astype(BF16), s.astype(BF16), ct.astype(BF16), st.astype(BF16)


def hyena_kspec(L, mats, w1, b1, w2, b2, w3, freq):
    t = jnp.linspace(0.0, 1.0, L, dtype=F32)[:, None]
    bands = (HY_EMB - 1) // 2
    w = 2.0 * math.pi * jnp.arange(L, dtype=F32) / L
    fb = jnp.linspace(1e-4, bands - 1, bands, dtype=F32)
    ang = w[:, None] * fb[None, :]
    z = jnp.concatenate([t, jnp.cos(ang), -jnp.sin(ang)], -1)
    fr = freq.astype(F32)
    hdn = jnp.sin(fr * (z @ w1.astype(F32) + b1.astype(F32)))
    hdn = jnp.sin(fr * (hdn @ w2.astype(F32) + b2.astype(F32)))
    filt = (hdn @ w3.astype(F32)).reshape(L, HY_ORDER, HY_DIRS, HY_W)
    deltas = jnp.abs(jnp.linspace(HY_MIN_DECAY, HY_MAX_DECAY, HY_W, dtype=F32))
    filt = filt * jnp.exp(-t * deltas[None, :])[:, None, None, :]
    filt = filt / jnp.sum(jnp.abs(filt), axis=(0, 2), keepdims=True)
    fwd, bwd = filt[:, :, 0], filt[:, :, 1]
    nw = HY_ORDER * HY_W
    bwd = bwd.at[0].set(0.0)
    kr = _mm(mats[0], (fwd + bwd).reshape(L, nw))
    ki = -_mm(mats[1], (fwd - bwd).reshape(L, nw))

    def per_order(a):
        return jnp.transpose(a.reshape(L, HY_ORDER, HY_W), (1, 0, 2))

    return per_order(kr), per_order(ki)


def _hy_fwd_kernel(ac_ref, as_ref, z_ref, kr_ref, ki_ref, yr_ref, yi_ref):
    z = z_ref[0].astype(BF16)
    zr = jnp.dot(ac_ref[...], z, preferred_element_type=F32)
    zi = -jnp.dot(as_ref[...], z, preferred_element_type=F32)
    kr, ki = kr_ref[...], ki_ref[...]
    yr_ref[0] = (zr * kr - zi * ki).astype(BF16)
    yi_ref[0] = (zr * ki + zi * kr).astype(BF16)


def _hy_inv_kernel(inv_l, act_ref, ast_ref, yr_ref, yi_ref, z_ref, x_ref, bias_ref, o_ref, *o16_ref):
    y = (jnp.dot(act_ref[...], yr_ref[0], preferred_element_type=F32)
         - jnp.dot(ast_ref[...], yi_ref[0], preferred_element_type=F32))
    o = x_ref[0] * (y * inv_l + bias_ref[...] * z_ref[0])
    o_ref[0] = o
    for ref in o16_ref:
        ref[0] = o.astype(BF16)


def hyena_longconv(z, z16, xmul, mats, kr, ki, bias, want16):
    B, L, W = z.shape
    ac, as_, act, ast = mats
    tk = min(HY_TK, L)
    a_spec = pl.BlockSpec((tk, L), lambda i, b: (i, 0))
    full_spec = pl.BlockSpec((1, L, W), lambda i, b: (b, 0, 0))
    tile_spec = pl.BlockSpec((1, tk, W), lambda i, b: (b, i, 0))
    k_spec = pl.BlockSpec((tk, W), lambda i, b: (i, 0))
    params = pltpu.CompilerParams(dimension_semantics=("parallel", "arbitrary"),
                                  vmem_limit_bytes=VMEM_LIMIT)
    yr, yi = pl.pallas_call(
        _hy_fwd_kernel,
        out_shape=[jax.ShapeDtypeStruct((B, L, W), BF16)] * 2,
        grid=(L // tk, B),
        in_specs=[a_spec, a_spec, full_spec, k_spec, k_spec],
        out_specs=[tile_spec, tile_spec],
        compiler_params=params,
        name="hyena_fwd",
    )(ac, as_, z16, kr, ki)
    n_out = 2 if want16 else 1
    return pl.pallas_call(
        functools.partial(_hy_inv_kernel, 1.0 / L),
        out_shape=[jax.ShapeDtypeStruct((B, L, W), F32), jax.ShapeDtypeStruct((B, L, W), BF16)][:n_out],
        grid=(L // tk, B),
        in_specs=[a_spec, a_spec, full_spec, full_spec, tile_spec, tile_spec,
                  pl.BlockSpec((1, W), lambda i, b: (0, 0))],
        out_specs=[tile_spec] * n_out,
        compiler_params=params,
        name="hyena_inv",
    )(act, ast, yr, yi, z, xmul, bias.reshape(1, W))


def _hy_pre_kernel(tm, tiles_per_seq, x_ref, xp_ref, xn_ref, cw_ref, cb_ref, v_ref, v16_ref, x1_ref,
                   x2_ref):
    i = pl.program_id(0)
    first = (i % tiles_per_seq) == 0
    last = (i % tiles_per_seq) == tiles_per_seq - 1
    x = x_ref[...].astype(F32)
    row = lax.broadcasted_iota(jnp.int32, x.shape, 0)
    prev_row = jnp.where(first, 0.0, xp_ref[...].astype(F32)[P_HALO - 1:P_HALO, :])
    next_row = jnp.where(last, 0.0, xn_ref[...].astype(F32)[0:1, :])
    xm1 = jnp.where(row == 0, prev_row, pltpu.roll(x, 1, 0))
    xp1 = jnp.where(row == tm - 1, next_row, pltpu.roll(x, tm - 1, 0))
    cw = cw_ref[...]
    y = xm1 * cw[0:1] + x * cw[1:2] + xp1 * cw[2:3] + cb_ref[...]
    v_ref[0] = y[:, 0:HY_W]
    v16_ref[0] = y[:, 0:HY_W].astype(BF16)
    x1_ref[0] = y[:, HY_W:2 * HY_W]
    x2_ref[0] = y[:, 2 * HY_W:3 * HY_W]


def hyena_pre(p, B, L, conv_w, conv_b):
    tm = min(RW_PREP_TM, L)
    tiles_per_seq = L // tm
    n_tok = B * L
    hb = tm // P_HALO
    width = P_RW - P_HY
    col = P_HY // width
    cw = jnp.pad(conv_w, ((0, 0), (0, width - conv_w.shape[1])))
    cb = jnp.pad(conv_b, (0, width - conv_b.shape[0])).reshape(1, width)
    out_spec = pl.BlockSpec((1, tm, HY_W), lambda i: (i // tiles_per_seq, i % tiles_per_seq, 0))
    return pl.pallas_call(
        functools.partial(_hy_pre_kernel, tm, tiles_per_seq),
        out_shape=[jax.ShapeDtypeStruct((B, L, HY_W), dt) for dt in (F32, BF16, F32, F32)],
        grid=(n_tok // tm,),
        in_specs=[pl.BlockSpec((tm, width), lambda i: (i, col)),
                  pl.BlockSpec((P_HALO, width), lambda i: (jnp.maximum(i * hb - 1, 0), col)),
                  pl.BlockSpec((P_HALO, width), lambda i: (jnp.minimum((i + 1) * hb, n_tok // P_HALO - 1), col)),
                  pl.BlockSpec((3, width), lambda i: (0, 0)),
                  pl.BlockSpec((1, width), lambda i: (0, 0))],
        out_specs=[out_spec] * 4,
        compiler_params=pltpu.CompilerParams(
            dimension_semantics=("parallel",), vmem_limit_bytes=VMEM_LIMIT),
        name="hyena_pre",
    )(p, p, p, cw, cb)


def hyena_mix(p, B, L, conv_w, conv_b, mats, kspec, bias):
    v, v16, x1, x2 = hyena_pre(p, B, L, conv_w, conv_b)
    kr, ki = kspec
    zz, zz16 = hyena_longconv(v, v16, x1, mats, kr[0], ki[0], bias[0], True)
    return hyena_longconv(zz, zz16, x2, mats, kr[1], ki[1], bias[1], False)[0]


def hyena_branch(p, pc, B, L, C, need_ctx, mats, mats_c, conv_w, conv_b, fw1, fb1, fw2, fb2, fw3, ffreq,
                 bias):
    kspec = hyena_kspec(L, mats, fw1, fb1, fw2, fb2, fw3, ffreq)
    y = hyena_mix(p, B, L, conv_w, conv_b, mats, kspec, bias)
    yc = None
    if need_ctx:
        kspec_c = hyena_kspec(C, mats_c, fw1, fb1, fw2, fb2, fw3, ffreq)
        yc = hyena_mix(pc, B, C, conv_w, conv_b, mats_c, kspec_c, bias)
    return y, yc


def rope_tables(L, d, width):
    nf = d // 4
    pos = jnp.arange(L)
    inv = ROPE_BASE ** (-jnp.arange(nf, dtype=F32) / nf)
    cs, sn = [], []
    for p in (pos // GRID_W, pos % GRID_W):
        ang = p.astype(F32)[:, None] * inv[None, :]
        cs += [jnp.cos(ang), jnp.cos(ang)]
        sn += [-jnp.sin(ang), jnp.sin(ang)]
    reps = width // d
    return jnp.tile(jnp.concatenate(cs, -1), (1, reps)), jnp.tile(jnp.concatenate(sn, -1), (1, reps))


def _rope(x, cos, sin, d):
    width = x.shape[-1]
    q = d // 4
    lane = lax.broadcasted_iota(jnp.int32, x.shape, 1)
    swapped = jnp.where(lane % (2 * q) < q, pltpu.roll(x, width - q, 1), pltpu.roll(x, q, 1))
    return x * cos + swapped * sin


def _swa_kernel(local, seq_len, qrows, q_ref, *refs):
    nk = qrows // SWA_BLOCK + 2
    k_refs, v_refs = refs[0:nk], refs[nk:2 * nk]
    ck_ref, cv_ref, cos_ref, sin_ref, sink_ref, o_ref = refs[2 * nk:]
    n = pl.program_id(1)
    blk = SWA_BLOCK
    qscale = SWA_HD ** -0.5 * math.log2(math.e)
    q = q_ref[...].astype(F32)
    if local:
        q0 = pl.multiple_of(n * qrows, qrows)
        q = _rope(q, cos_ref[pl.ds(q0, qrows), :], sin_ref[pl.ds(q0, qrows), :], SWA_HD)
        first = n * (qrows // blk) - 1
        ks, vs = [], []
        for j, (k_ref, v_ref) in enumerate(zip(k_refs, v_refs)):
            k0 = pl.multiple_of(jnp.clip(first + j, 0, seq_len // blk - 1) * blk, blk)
            ks.append(_rope(k_ref[...].astype(F32), cos_ref[pl.ds(k0, blk), 0:SWA_KV * SWA_HD],
                            sin_ref[pl.ds(k0, blk), 0:SWA_KV * SWA_HD], SWA_HD).astype(BF16))
            vs.append(v_ref[...].astype(BF16))
        k_loc = jnp.concatenate(ks, axis=0)
        v_loc = jnp.concatenate(vs, axis=0)
        qpos = n * qrows + lax.broadcasted_iota(jnp.int32, (qrows, nk * blk), 0)
        kpos = first * blk + lax.broadcasted_iota(jnp.int32, (qrows, nk * blk), 1)
        valid = (jnp.abs(kpos - qpos) <= SWA_WIN) & (kpos >= 0) & (kpos < seq_len)
    qb = (q * qscale).astype(BF16)
    ck = ck_ref[...].astype(BF16)
    cv = cv_ref[...].astype(BF16)
    nt = (((1,), (1,)), ((), ()))
    outs = []
    for h in range(SWA_HEADS):
        g = h // (SWA_HEADS // SWA_KV)
        qh = qb[:, h * SWA_HD:(h + 1) * SWA_HD]
        gs = slice(g * SWA_HD, (g + 1) * SWA_HD)
        sink = sink_ref[h:h + 1, 0:1]
        s_ctx = lax.dot_general(qh, ck[:, gs], nt, preferred_element_type=F32)
        m = jnp.maximum(jnp.max(s_ctx, -1, keepdims=True), sink)
        if local:
            s_loc = lax.dot_general(qh, k_loc[:, gs], nt, preferred_element_type=F32)
            s_loc = jnp.where(valid, s_loc, NEG_INF)
            m = jnp.maximum(m, jnp.max(s_loc, -1, keepdims=True))
        p_ctx = jnp.exp2(s_ctx - m)
        den = jnp.sum(p_ctx, -1, keepdims=True) + jnp.exp2(sink - m)
        acc = jnp.dot(p_ctx.astype(BF16), cv[:, gs], preferred_element_type=F32)
        if local:
            p_loc = jnp.exp2(s_loc - m)
            den = den + jnp.sum(p_loc, -1, keepdims=True)
            acc = acc + jnp.dot(p_loc.astype(BF16), v_loc[:, gs], preferred_element_type=F32)
        outs.append(acc / den)
    o_ref[...] = jnp.concatenate(outs, axis=-1)


def swa_attend(p, pc, B, L, C, sink, tables, local):
    blk = SWA_BLOCK
    nb = L // blk
    qrows = min(SWA_QROWS, L)
    nq = L // qrows
    qpb = qrows // blk
    qcol = P_SW // (SWA_HEADS * SWA_HD)
    kcol = (P_SW + SWA_HEADS * SWA_HD) // (SWA_KV * SWA_HD)
    kvw = SWA_KV * SWA_HD

    def nbr(j, col):
        return lambda b, n: (b * nb + jnp.clip(n * qpb + j, 0, nb - 1), col)

    nbrs = range(-1, qpb + 1)

    cos, sin = tables
    sink_rows = jnp.broadcast_to((sink.astype(F32) * math.log2(math.e))[:, None], (SWA_HEADS, LANES))
    tab_spec = pl.BlockSpec(cos.shape, lambda b, n: (0, 0))
    out = pl.pallas_call(
        functools.partial(_swa_kernel, local, L, qrows),
        out_shape=jax.ShapeDtypeStruct((B * L, SWA_HEADS * SWA_HD), F32),
        grid=(B, nq),
        in_specs=[pl.BlockSpec((qrows, SWA_HEADS * SWA_HD), lambda b, n: (b * nq + n, qcol))]
        + [pl.BlockSpec((blk, kvw), nbr(j, kcol)) for j in nbrs]
        + [pl.BlockSpec((blk, kvw), nbr(j, kcol + 1)) for j in nbrs]
        + [pl.BlockSpec((C, kvw), lambda b, n: (b, kcol)),
           pl.BlockSpec((C, kvw), lambda b, n: (b, kcol + 1)),
           tab_spec, tab_spec,
           pl.BlockSpec((SWA_HEADS, LANES), lambda b, n: (0, 0))],
        out_specs=pl.BlockSpec((qrows, SWA_HEADS * SWA_HD), lambda b, n: (b * nq + n, 0)),
        compiler_params=pltpu.CompilerParams(
            dimension_semantics=("parallel", "parallel"), vmem_limit_bytes=VMEM_LIMIT),
        name="swa",
    )(p, *([p] * (2 * len(nbrs))), pc, pc, cos, sin, sink_rows)
    return out.reshape(B, L, SWA_HEADS * SWA_HD)


def swa_branch(p, pc, B, L, C, need_ctx, sink, tables):
    y = swa_attend(p, pc, B, L, C, sink, tables, True)
    yc = swa_attend(pc, pc, B, C, C, sink, tables, False) if need_ctx else None
    return y, yc


def _seg_blocks(value, width, seg):
    row = lax.broadcasted_iota(jnp.int32, (width, width), 0)
    col = lax.broadcasted_iota(jnp.int32, (width, width), 1)
    return jnp.where(row // seg == col // seg, value, 0.0).astype(BF16)


def _head_blocks(value):
    return _seg_blocks(value, RW_W, RW_HD)


def _head_sum(x, blocks):
    hi = x.astype(BF16)
    lo = (x - hi.astype(F32)).astype(BF16)
    return (jnp.dot(hi, blocks, preferred_element_type=F32)
            + jnp.dot(lo, blocks, preferred_element_type=F32))


RW_OFF = tuple(int(o) for o in np.cumsum((0,) + RW_SIZES))
RW_PAD = 1280
RW_WIN_WD = RW_OFF[3]
RW_WIN_LO = RW_OFF[5]
RW_WIN_HI = RW_WIN_LO + LANES


def _rwkv_prep_kernel(tm, tiles_per_seq, x_ref, xp_ref, xn_ref, mu_ref, w0_ref, w2f_ref, w2b_ref,
                      a0_ref, a2_ref, g2f_ref, g2b_ref, kk_ref, ka_ref,
                      r_o, k_o, v_o, kkn_o, b_o, df_o, db_o, gf_o, gb_o):
    i = pl.program_id(0)
    first = (i % tiles_per_seq) == 0
    last = (i % tiles_per_seq) == tiles_per_seq - 1
    x = x_ref[...].astype(F32)
    row = lax.broadcasted_iota(jnp.int32, x.shape, 0)
    prev_row = jnp.where(first, 0.0, xp_ref[...].astype(F32)[P_HALO - 1:P_HALO, :])
    next_row = jnp.where(last, 0.0, xn_ref[...].astype(F32)[0:1, :])
    xm1 = jnp.where(row == 0, prev_row, pltpu.roll(x, 1, 0))
    xp1 = jnp.where(row == tm - 1, next_row, pltpu.roll(x, tm - 1, 0))
    x = x + (0.5 * (xm1 + xp1) - x) * mu_ref[...]

    r = x[:, RW_OFF[0]:RW_OFF[1]]
    k = x[:, RW_OFF[1]:RW_OFF[2]]
    v = x[:, RW_OFF[2]:RW_OFF[3]]
    wd = jnp.tanh(x[:, RW_WIN_WD:RW_WIN_WD + LANES]).astype(BF16)
    lo = x[:, RW_WIN_LO:RW_WIN_LO + RW_W]
    hi = x[:, RW_WIN_HI:RW_WIN_HI + RW_W]

    def decay(w2_ref, w0):
        wlog = -jax.nn.softplus(-(w0 + jnp.dot(wd, w2_ref[...], preferred_element_type=F32))) - 0.5
        return jnp.exp(-jnp.exp(wlog))

    a = jax.nn.sigmoid(a0_ref[...] + jnp.dot(lo[:, 0:LANES].astype(BF16), a2_ref[...],
                                             preferred_element_type=F32))
    kk = k * kk_ref[...]
    norm = jnp.sqrt(_head_sum(kk * kk, _head_blocks(1.0)))
    kk = kk / jnp.maximum(norm, 1e-12)
    r_o[...] = r
    k_o[...] = k * (1.0 + (a - 1.0) * ka_ref[...])
    v_o[...] = v
    kkn_o[...] = kk
    b_o[...] = kk * a
    df_o[...] = decay(w2f_ref, w0_ref[0:1, :])
    db_o[...] = decay(w2b_ref, w0_ref[1:2, :])
    gf_o[...] = jnp.dot(jax.nn.sigmoid(lo).astype(BF16), g2f_ref[...], preferred_element_type=F32)
    gb_o[...] = jnp.dot(jax.nn.sigmoid(hi).astype(BF16), g2b_ref[...], preferred_element_type=F32)


def rwkv_prep(p, B, L, mu, w0, w2, a0, a2, g2, k_k, k_a):
    tm = min(RW_PREP_TM, L)
    tiles_per_seq = L // tm
    n_tok = B * L
    hb = tm // P_HALO
    col = P_RW // RW_PAD

    def rows(n, r0, src):
        return jnp.zeros((n, RW_W), F32).at[r0:r0 + src.shape[0]].set(src).astype(BF16)

    w2f = rows(LANES, RW_OFF[3] - RW_WIN_WD, w2[0])
    w2b = rows(LANES, RW_OFF[4] - RW_WIN_WD, w2[1])
    a2p = rows(LANES, RW_OFF[5] - RW_WIN_LO, a2)
    g2f = rows(RW_W, RW_OFF[6] - RW_WIN_LO, g2[0])
    g2b = rows(RW_W, RW_OFF[7] - RW_WIN_HI, g2[1])
    mu_p = jnp.pad(mu, (0, RW_PAD - mu.shape[0])).reshape(1, RW_PAD)

    def const(shape):
        return pl.BlockSpec(shape, lambda i: (0,) * len(shape))

    vec = const((1, RW_W))
    out_spec = pl.BlockSpec((tm, RW_W), lambda i: (i, 0))
    return pl.pallas_call(
        functools.partial(_rwkv_prep_kernel, tm, tiles_per_seq),
        out_shape=[jax.ShapeDtypeStruct((n_tok, RW_W), F32)] * 9,
        grid=(n_tok // tm,),
        in_specs=[pl.BlockSpec((tm, RW_PAD), lambda i: (i, col)),
                  pl.BlockSpec((P_HALO, RW_PAD), lambda i: (jnp.maximum(i * hb - 1, 0), col)),
                  pl.BlockSpec((P_HALO, RW_PAD), lambda i: (jnp.minimum((i + 1) * hb, n_tok // P_HALO - 1), col)),
                  const((1, RW_PAD)), const((2, RW_W)), const((LANES, RW_W)), const((LANES, RW_W)),
                  vec, const((LANES, RW_W)), const((RW_W, RW_W)), const((RW_W, RW_W)), vec, vec],
        out_specs=[out_spec] * 9,
        compiler_params=pltpu.CompilerParams(
            dimension_semantics=("parallel",), vmem_limit_bytes=VMEM_LIMIT),
        name="rwkv_prep",
    )(p, p, p, mu_p, w0, w2f, w2b, a0.reshape(1, RW_W), a2p, g2f, g2b,
      k_k.reshape(1, RW_W), k_a.reshape(1, RW_W))


def _rwkv_scan_kernel(tblk, nb,
                      wf_ref, kkf_ref, bf_ref, kf_ref, vf_ref, rf_ref,
                      wb_ref, kkb_ref, bb_ref, kb_ref, vb_ref, rb_ref,
                      of_ref, ob_ref, sf_ref, sb_ref, sf16_ref, sb16_ref):
    c = pl.program_id(0)

    @pl.when(c == 0)
    def _():
        for ref in (sf_ref, sb_ref, sf16_ref, sb16_ref):
            ref[...] = jnp.zeros_like(ref)

    row = lax.broadcasted_iota(jnp.int32, (RW_W, RW_W), 0)
    col = lax.broadcasted_iota(jnp.int32, (RW_W, RW_W), 1)
    ones_blk = jnp.where(row // RW_HD == col // RW_HD, 1.0, 0.0).astype(BF16)
    r64 = lax.broadcasted_iota(jnp.int32, (RW_HD, RW_W), 0)
    c64 = lax.broadcasted_iota(jnp.int32, (RW_HD, RW_W), 1)
    eye_t = jnp.where(r64 == c64 % RW_HD, 1.0, 0.0).astype(BF16)
    r8 = lax.broadcasted_iota(jnp.int32, (8, RW_W), 0)
    c8 = lax.broadcasted_iota(jnp.int32, (8, RW_W), 1)
    head_sel = jnp.where(r8 == c8 // RW_HD, 1.0, 0.0).astype(BF16)

    fwd = (wf_ref, kkf_ref, bf_ref, kf_ref, vf_ref, rf_ref, sf_ref, sf16_ref)
    bwd = (wb_ref, kkb_ref, bb_ref, kb_ref, vb_ref, rb_ref, sb_ref, sb16_ref)
    nrow = nb * RW_HD

    def one_dir(idx, refs):
        w_ref, kk_ref, b_ref, k_ref, v_ref, r_ref, s_ref, s16_ref = refs
        zs = []
        for b in range(nb):
            kk16 = kk_ref[idx, b:b + 1, :].astype(BF16)
            v16 = v_ref[idx, b:b + 1, :].astype(BF16)
            lhs = jnp.concatenate([s16_ref[b] * kk16, eye_t * v16], axis=0)
            red = jnp.dot(lhs, ones_blk, preferred_element_type=F32)
            s_new = (s_ref[b] * w_ref[idx, b:b + 1, :] - red[0:RW_HD] * b_ref[idx, b:b + 1, :]
                     + red[RW_HD:2 * RW_HD] * k_ref[idx, b:b + 1, :])
            s_ref[b] = s_new
            s16 = s_new.astype(BF16)
            s16_ref[b] = s16
            zs.append(s16 * r_ref[idx, b:b + 1, :].astype(BF16))
        return jnp.concatenate(zs, axis=0)

    def emit(idx, z, o_ref):
        o = lax.dot_general(head_sel, z, (((1,), (1,)), ((), ())), preferred_element_type=F32)
        o_ref[idx] = o[0:RW_HEADS]

    def step(i, carry):
        z_f, z_b = carry
        ib = tblk - 1 - i
        emit(jnp.maximum(i - 1, 0), z_f, of_ref)
        emit(jnp.minimum(ib + 1, tblk - 1), z_b, ob_ref)
        return one_dir(i, fwd), one_dir(ib, bwd)

    z0 = jnp.zeros((nrow, RW_W), BF16)
    z_f, z_b = lax.fori_loop(0, tblk, step, (z0, z0), unroll=8)
    emit(tblk - 1, z_f, of_ref)
    emit(0, z_b, ob_ref)


def rwkv_scan_pallas(w_f, w_b, kk, bvec, k, v, r, n_ctx):
    n, nb, _ = kk.shape
    tblk = RW_TBLK
    nblk = n // tblk
    nblk_ctx = n_ctx // tblk

    def fmap(c):
        return (c, 0, 0)

    def bmap(c):
        return (jnp.where(c < nblk_ctx, nblk_ctx - 1 - c, nblk + nblk_ctx - 1 - c), 0, 0)

    blk = (tblk, nb, RW_W)
    oblk = (tblk, RW_HEADS, nb * RW_HD)
    return pl.pallas_call(
        functools.partial(_rwkv_scan_kernel, tblk, nb),
        out_shape=[jax.ShapeDtypeStruct((n, RW_HEADS, nb * RW_HD), F32)] * 2,
        grid=(nblk,),
        in_specs=[pl.BlockSpec(blk, fmap)] * 6 + [pl.BlockSpec(blk, bmap)] * 6,
        out_specs=[pl.BlockSpec(oblk, fmap), pl.BlockSpec(oblk, bmap)],
        scratch_shapes=[pltpu.VMEM((nb, RW_HD, RW_W), F32), pltpu.VMEM((nb, RW_HD, RW_W), F32),
                        pltpu.VMEM((nb, RW_HD, RW_W), BF16), pltpu.VMEM((nb, RW_HD, RW_W), BF16)],
        compiler_params=pltpu.CompilerParams(dimension_semantics=("arbitrary",),
                                             vmem_limit_bytes=VMEM_LIMIT),
        name="rwkv_scan",
    )(w_f, kk, bvec, k, v, r, w_b, kk, bvec, k, v, r)


def _rwkv_post_kernel(nb, of_ref, ob_ref, r_ref, k_ref, v_ref, gf_ref, gb_ref, rk_ref, lng_ref, lnb_ref,
                      y_ref):
    mean_blk = _head_blocks(1.0 / RW_HD)
    sum_blk = _head_blocks(1.0)

    def gn(o_ref, b):
        o = jnp.concatenate([o_ref[:, h, b * RW_HD:(b + 1) * RW_HD] for h in range(RW_HEADS)], axis=-1)
        oc = o - _head_sum(o, mean_blk)
        var = _head_sum(oc * oc, mean_blk)
        return oc * lax.rsqrt(var + RW_GN_EPS) * lng_ref[...] + lnb_ref[...]

    for b in range(nb):
        bonus = _head_sum(r_ref[b] * k_ref[b] * rk_ref[...], sum_blk) * v_ref[b]
        y_ref[b] = (gn(of_ref, b) + bonus) * gf_ref[b] + (gn(ob_ref, b) + bonus) * gb_ref[b]


def rwkv_post(o_f, o_b, r, k, v, g_f, g_b, r_k, lnx_g, lnx_b, B, L, start):
    tm = min(RW_POST_TM, L)
    assert start % tm == 0 and L % tm == 0
    tok = pl.BlockSpec((B, tm, RW_W), lambda i: (0, i, 0))
    o_spec = pl.BlockSpec((tm, RW_HEADS, B * RW_HD), lambda i: (start // tm + i, 0, 0))
    vec = pl.BlockSpec((1, RW_W), lambda i: (0, 0))
    streams = [t.reshape(B, L, RW_W) for t in (r, k, v, g_f, g_b)]
    return pl.pallas_call(
        functools.partial(_rwkv_post_kernel, B),
        out_shape=jax.ShapeDtypeStruct((B, L, RW_W), F32),
        grid=(L // tm,),
        in_specs=[o_spec] * 2 + [tok] * 5 + [vec] * 3,
        out_specs=tok,
        compiler_params=pltpu.CompilerParams(
            dimension_semantics=("parallel",), vmem_limit_bytes=VMEM_LIMIT),
        name="rwkv_post",
    )(o_f, o_b, *streams, r_k.reshape(1, RW_W), lnx_g.reshape(1, RW_W), lnx_b.reshape(1, RW_W))


def rwkv_branch(p, pc, B, L, C, need_ctx, mu, w0, w2, a0, a2, g2, k_k, k_a, r_k, lnx_g, lnx_b):
    r, k, v, kk, bv, d_f, d_b, g_f, g_b = rwkv_prep(p, B, L, mu, w0, w2, a0, a2, g2, k_k, k_a)
    rc, kc, vc, kkc, bc, dc_f, dc_b, gc_f, gc_b = rwkv_prep(pc, B, C, mu, w0, w2, a0, a2, g2, k_k, k_a)

    def tmajor(xc, xl):
        t = jnp.concatenate([xc.reshape(B, C, RW_W), xl.reshape(B, L, RW_W)], axis=1)
        return jnp.moveaxis(t, 1, 0)

    pf, pb = rwkv_scan_pallas(tmajor(dc_f, d_f), tmajor(dc_b, d_b), tmajor(kkc, kk), tmajor(bc, bv),
                              tmajor(kc, k), tmajor(vc, v), tmajor(rc, r), C)

    y = rwkv_post(pf, pb, r, k, v, g_f, g_b, r_k, lnx_g, lnx_b, B, L, C)
    yc = rwkv_post(pf, pb, rc, kc, vc, gc_f, gc_b, r_k, lnx_g, lnx_b, B, C, 0) if need_ctx else None
    return y, yc


def _diff_attn_kernel(q_ref, kt_ref, v_ref, lam_ref, gain_ref, o_ref):
    q = q_ref[0]
    v = v_ref[0]
    lam = lam_ref[...]
    tq = q.shape[0]
    head_of_lane = lax.broadcasted_iota(jnp.int32, (1, DF_W), 1) // DF_VD
    acc = jnp.zeros((tq, DF_W), F32)
    for h in range(DF_HEADS):
        parts = []
        for c in range(2):
            j = 2 * h + c
            s = jnp.dot(q[:, DF_HD * j:DF_HD * (j + 1)], kt_ref[0, j],
                        preferred_element_type=F32)
            p = jnp.exp2(s - jnp.max(s, -1, keepdims=True))
            l = jnp.sum(p, -1, keepdims=True)
            pv = jnp.dot(p.astype(BF16), v, preferred_element_type=F32)
            parts.append(pv / l)
        acc = jnp.where(head_of_lane == h, parts[0] - lam * parts[1], acc)
    ms = _head_sum(acc * acc, _seg_blocks(1.0 / DF_VD, DF_W, DF_VD))
    o_ref[0] = acc * lax.rsqrt(ms + 1e-5) * gain_ref[...]


def _diff_pre_kernel(rope, x_ref, cos_ref, sin_ref, q_ref, kt_ref, v_ref):
    x = x_ref[...].astype(F32)
    q, k, v = x[:, 0:DF_W], x[:, DF_W:2 * DF_W], x[:, 2 * DF_W:3 * DF_W]
    if rope:
        cos, sin = cos_ref[...], sin_ref[...]
        q = _rope(q, cos, sin, DF_HD)
        k = _rope(k, cos, sin, DF_HD)
    q_ref[...] = (q * (DF_HD ** -0.5 * math.log2(math.e))).astype(BF16)
    kt_ref[0] = k.T.astype(BF16)
    v_ref[...] = v.astype(BF16)


def diff_pre(p, B, L, tables, rope):
    tm = min(RW_PREP_TM, L)
    tiles_per_seq = L // tm
    n_tok = B * L
    width = 3 * DF_W
    col = P_DF // width
    cos, sin = tables
    tok = pl.BlockSpec((tm, DF_W), lambda i: (i, 0))
    tab = pl.BlockSpec((tm, DF_W), lambda i: (i % tiles_per_seq, 0))
    q, kt, v = pl.pallas_call(
        functools.partial(_diff_pre_kernel, rope),
        out_shape=[jax.ShapeDtypeStruct((n_tok, DF_W), BF16),
                   jax.ShapeDtypeStruct((B, DF_W, L), BF16),
                   jax.ShapeDtypeStruct((n_tok, DF_W), BF16)],
        grid=(n_tok // tm,),
        in_specs=[pl.BlockSpec((tm, width), lambda i: (i, col)), tab, tab],
        out_specs=[tok, pl.BlockSpec((1, DF_W, tm), lambda i: (i // tiles_per_seq, 0, i % tiles_per_seq)),
                   tok],
        compiler_params=pltpu.CompilerParams(
            dimension_semantics=("parallel",), vmem_limit_bytes=VMEM_LIMIT),
        name="diff_pre",
    )(p, cos, sin)
    return q.reshape(B, L, DF_W), kt, v.reshape(B, L, DF_W)


def diff_attend(qs, kt, vb, lam, lam_init, subln_g):
    B, Lq = qs.shape[:2]
    S = vb.shape[1]
    tq = min(DF_TQ, Lq)
    kt = kt.reshape(B, 2 * DF_HEADS, DF_HD, S)
    gain = jnp.tile(subln_g.astype(F32) * (1.0 - lam_init), DF_HEADS).reshape(1, DF_W)
    return pl.pallas_call(
        _diff_attn_kernel,
        out_shape=jax.ShapeDtypeStruct((B, Lq, DF_W), F32),
        grid=(B, Lq // tq),
        in_specs=[pl.BlockSpec((1, tq, DF_W), lambda b, i: (b, i, 0)),
                  pl.BlockSpec((1, 2 * DF_HEADS, DF_HD, S), lambda b, i: (b, 0, 0, 0)),
                  pl.BlockSpec((1, S, DF_W), lambda b, i: (b, 0, 0)),
                  pl.BlockSpec((1, 1), lambda b, i: (0, 0)),
                  pl.BlockSpec((1, DF_W), lambda b, i: (0, 0))],
        out_specs=pl.BlockSpec((1, tq, DF_W), lambda b, i: (b, i, 0)),
        compiler_params=pltpu.CompilerParams(
            dimension_semantics=("parallel", "parallel"), vmem_limit_bytes=VMEM_LIMIT),
        name="diff_attn",
    )(qs, kt, vb, lam.reshape(1, 1).astype(F32), gain)


def diff_branch(p, pc, B, L, C, need_ctx, tables, lq1, lk1, lq2, lk2, subln_g, lam_init):
    lam = (jnp.exp(jnp.sum(lq1.astype(F32) * lk1.astype(F32)))
           - jnp.exp(jnp.sum(lq2.astype(F32) * lk2.astype(F32))) + lam_init)
    q, kt, v = diff_pre(p, B, L, tables, True)
    qc, ktc, vc = diff_pre(pc, B, C, tables, False)
    y = diff_attend(q, jnp.concatenate([kt, ktc], axis=2), jnp.concatenate([v, vc], axis=1),
                    lam, lam_init, subln_g)
    yc = diff_attend(qc, ktc, vc, lam, lam_init, subln_g) if need_ctx else None
    return y, yc


def _ln_rows(x):
    mu = jnp.mean(x, -1, keepdims=True)
    xc = x - mu
    return xc * lax.rsqrt(jnp.mean(xc * xc, -1, keepdims=True) + LN_EPS)


def _ffn_kernel(tm, tiles_per_seq, n_chunks,
                h_ref, hprev_ref, hnext_ref, sh_ref, sc_ref, gate_ref,
                wa_ref, wb_ref, cwa_ref, cwb_ref, cba_ref, cbb_ref, wd_ref, lng_ref, lnb_ref,
                o_ref, u_ref, acc_ref):
    i = pl.program_id(0)
    j = pl.program_id(1)
    n = tm + 2 * HALO

    @pl.when(j == 0)
    def _():
        scale = 1.0 + sc_ref[0]
        shift = sh_ref[0]
        first = (i % tiles_per_seq) == 0
        last = (i % tiles_per_seq) == tiles_per_seq - 1
        u_ref[0:tm, :] = (_ln_rows(h_ref[...]) * scale + shift).astype(BF16)
        un = _ln_rows(hnext_ref[...]) * scale + shift
        up = _ln_rows(hprev_ref[...]) * scale + shift
        u_ref[tm:tm + HALO, :] = jnp.where(last, 0.0, un).astype(BF16)
        u_ref[tm + HALO:n, :] = jnp.where(first, 0.0, up).astype(BF16)
        acc_ref[...] = jnp.zeros_like(acc_ref)

    u = u_ref[...]

    def conv(w_ref, cw_ref, cb_ref):
        x = jnp.dot(u, w_ref[...], preferred_element_type=F32)
        cw = cw_ref[...]
        y = (pltpu.roll(x, 1, 0)[0:tm] * cw[0:1] + x[0:tm] * cw[1:2]
             + pltpu.roll(x, n - 1, 0)[0:tm] * cw[2:3] + cb_ref[...])
        return y

    a = conv(wa_ref, cwa_ref, cba_ref)
    b = conv(wb_ref, cwb_ref, cbb_ref)
    g = ((0.5 * a) * (1.0 + jnp.tanh(0.5 * a)) * b).astype(BF16)
    acc_ref[...] += jnp.dot(g, wd_ref[...], preferred_element_type=F32)

    @pl.when(j == n_chunks - 1)
    def _():
        y = DN_ALPHA * h_ref[...] + gate_ref[0] * acc_ref[...]
        o_ref[...] = _ln_rows(y) * lng_ref[...] + lnb_ref[...]


def ffn_block(h, shift, scale, gate, w_up, conv_w, conv_b, w_down, ln_g, ln_b):
    B, L, D = h.shape
    tm = min(FFN_TM, L)
    tiles_per_seq = L // tm
    n_tok = B * L
    n_chunks = D_FF // FFN_CHUNK
    hb = tm // HALO

    def bcast(t):
        return jnp.broadcast_to(t.astype(F32), (B, 1, D))

    wu = w_up.astype(BF16)
    wd = w_down.astype(BF16)
    cb = conv_b.reshape(1, 2 * D_FF)
    h2 = h.reshape(n_tok, D)
    mod_spec = pl.BlockSpec((1, 1, D), lambda i, j: (i // tiles_per_seq, 0, 0))
    vec_spec = pl.BlockSpec((1, D), lambda i, j: (0, 0))
    out = pl.pallas_call(
        functools.partial(_ffn_kernel, tm, tiles_per_seq, n_chunks),
        out_shape=jax.ShapeDtypeStruct((n_tok, D), F32),
        grid=(n_tok // tm, n_chunks),
        in_specs=[pl.BlockSpec((tm, D), lambda i, j: (i, 0)),
                  pl.BlockSpec((HALO, D), lambda i, j: (jnp.maximum(i * hb - 1, 0), 0)),
                  pl.BlockSpec((HALO, D), lambda i, j: (jnp.minimum((i + 1) * hb, n_tok // HALO - 1), 0)),
                  mod_spec, mod_spec, mod_spec,
                  pl.BlockSpec((D, FFN_CHUNK), lambda i, j: (0, j)),
                  pl.BlockSpec((D, FFN_CHUNK), lambda i, j: (0, n_chunks + j)),
                  pl.BlockSpec((3, FFN_CHUNK), lambda i, j: (0, j)),
                  pl.BlockSpec((3, FFN_CHUNK), lambda i, j: (0, n_chunks + j)),
                  pl.BlockSpec((1, FFN_CHUNK), lambda i, j: (0, j)),
                  pl.BlockSpec((1, FFN_CHUNK), lambda i, j: (0, n_chunks + j)),
                  pl.BlockSpec((FFN_CHUNK, D), lambda i, j: (j, 0)),
                  vec_spec, vec_spec],
        out_specs=pl.BlockSpec((tm, D), lambda i, j: (i, 0)),
        scratch_shapes=[pltpu.VMEM((tm + 2 * HALO, D), BF16), pltpu.VMEM((tm, D), F32)],
        compiler_params=pltpu.CompilerParams(
            dimension_semantics=("parallel", "arbitrary"), vmem_limit_bytes=VMEM_LIMIT),
        name="conv_ffn",
    )(h2, h2, h2, bcast(shift), bcast(scale), bcast(gate), wu, wu, conv_w, conv_w, cb, cb, wd,
      ln_g.reshape(1, D), ln_b.reshape(1, D))
    return out.reshape(B, L, D)


P_GATE, P_HY, P_RW, P_SW, P_DF = 0, 4096, 5120, 6400, 6912
P_COLS = 7680
IN_CHUNK = 768


def _in_proj_kernel(x_ref, sh_ref, sc_ref, w_ref, o_ref, u_ref):
    @pl.when(pl.program_id(1) == 0)
    def _():
        u_ref[...] = (_ln_rows(x_ref[...]) * (1.0 + sc_ref[0]) + sh_ref[0]).astype(BF16)

    o_ref[...] = jnp.dot(u_ref[...], w_ref[...], preferred_element_type=F32).astype(BF16)


def in_proj(h, shift, scale, w_in_p):
    B, L, D = h.shape
    tm = min(IN_TM, L)
    tiles_per_seq = L // tm
    n_tok = B * L

    def bcast(t):
        return jnp.broadcast_to(t.astype(F32), (B, 1, D))

    mod_spec = pl.BlockSpec((1, 1, D), lambda i, j: (i // tiles_per_seq, 0, 0))
    return pl.pallas_call(
        _in_proj_kernel,
        out_shape=jax.ShapeDtypeStruct((n_tok, P_COLS), BF16),
        grid=(n_tok // tm, P_COLS // IN_CHUNK),
        in_specs=[pl.BlockSpec((tm, D), lambda i, j: (i, 0)), mod_spec, mod_spec,
                  pl.BlockSpec((D, IN_CHUNK), lambda i, j: (0, j))],
        out_specs=pl.BlockSpec((tm, IN_CHUNK), lambda i, j: (i, j)),
        scratch_shapes=[pltpu.VMEM((tm, D), BF16)],
        compiler_params=pltpu.CompilerParams(
            dimension_semantics=("parallel", "arbitrary"), vmem_limit_bytes=VMEM_LIMIT),
        name="in_proj",
    )(h.reshape(n_tok, D), bcast(shift), bcast(scale), w_in_p)


def pack_w_in(w):
    offs = np.cumsum((0,) + IN_SIZES)
    hy, sw, rw, df, gt = [w[:, offs[j]:offs[j + 1]] for j in range(len(IN_SIZES))]

    def padto(t, n):
        return jnp.pad(t, ((0, 0), (0, n - t.shape[1])))

    return jnp.concatenate([gt, padto(hy, P_RW - P_HY), padto(rw, P_SW - P_RW), sw, df],
                           axis=1).astype(BF16)


def _merge_kernel(yh_ref, ys_ref, yr_ref, yd_ref, g0_ref, g1_ref, g2_ref, g3_ref, h_ref, gate_ref,
                  wbr_ref, wo_ref, lng_ref, lnb_ref, o_ref):
    acc = None
    for j, (y_ref, g_ref) in enumerate(((yh_ref, g0_ref), (ys_ref, g1_ref), (yr_ref, g2_ref),
                                        (yd_ref, g3_ref))):
        term = jax.nn.sigmoid(g_ref[...].astype(F32)) * jnp.dot(y_ref[...].astype(BF16), wbr_ref[j],
                                                    preferred_element_type=F32)
        acc = term if acc is None else acc + term
    mix = jnp.dot(acc.astype(BF16), wo_ref[...], preferred_element_type=F32)
    y = DN_ALPHA * h_ref[...] + gate_ref[0] * mix
    o_ref[...] = _ln_rows(y) * lng_ref[...] + lnb_ref[...]


def merge_block(ys, p, h, gate, w_br, w_o, ln_g, ln_b):
    B, L, D = h.shape
    n_tok = B * L
    tm = min(MERGE_TM, L)
    tiles_per_seq = L // tm
    y_spec = pl.BlockSpec((tm, BR_W), lambda i: (i, 0))
    vec_spec = pl.BlockSpec((1, D), lambda i: (0, 0))
    g_specs = [pl.BlockSpec((tm, D), functools.partial(lambda j, i: (i, j), j)) for j in range(N_BRANCH)]
    out = pl.pallas_call(
        _merge_kernel,
        out_shape=jax.ShapeDtypeStruct((n_tok, D), F32),
        grid=(n_tok // tm,),
        in_specs=[y_spec] * N_BRANCH + g_specs + [
            pl.BlockSpec((tm, D), lambda i: (i, 0)),
            pl.BlockSpec((1, 1, D), lambda i: (i // tiles_per_seq, 0, 0)),
            pl.BlockSpec((N_BRANCH, BR_W, D), lambda i: (0, 0, 0)),
            pl.BlockSpec((D, D), lambda i: (0, 0)), vec_spec, vec_spec],
        out_specs=pl.BlockSpec((tm, D), lambda i: (i, 0)),
        compiler_params=pltpu.CompilerParams(
            dimension_semantics=("parallel",), vmem_limit_bytes=VMEM_LIMIT),
        name="merge",
    )(*[y.reshape(n_tok, BR_W) for y in ys], p, p, p, p, h.reshape(n_tok, D),
      jnp.broadcast_to(gate.astype(F32), (B, 1, D)), w_br.astype(BF16), w_o.astype(BF16),
      ln_g.reshape(1, D), ln_b.reshape(1, D))
    return out.reshape(B, L, D)


def kernel(x, c, ctx, c_ctx, ada_w, ada_b, w_in, hy_conv_w, hy_conv_b, hy_f_w1, hy_f_b1,
           hy_f_w2, hy_f_b2, hy_f_w3, hy_f_freq, hy_bias, swa_sink, rwkv_mu, rwkv_w0, rwkv_w2,
           rwkv_a0, rwkv_a2, rwkv_g2, rwkv_kk, rwkv_ka, rwkv_rk, rwkv_lnx_g, rwkv_lnx_b,
           diff_lq1, diff_lk1, diff_lq2, diff_lk2, diff_subln_g, w_branch, w_out, ln1_g, ln1_b,
           ffn_w_up, ffn_conv_w, ffn_conv_b, ffn_w_down, ln2_g, ln2_b):
    h, hc = x, ctx
    mats, mats_c = dft_mats(x.shape[1]), dft_mats(ctx.shape[1])
    swa_tables = rope_tables(x.shape[1], SWA_HD, SWA_HEADS * SWA_HD)
    df_tables = rope_tables(x.shape[1], DF_HD, DF_W)
    s_lat = jax.nn.silu(c)
    s_ctx = jax.nn.silu(c_ctx)
    for i in range(DEPTH):
        need_ctx = i < DEPTH - 1
        mod = (s_lat @ ada_w[i] + ada_b[i])[:, None, :]
        mod_c = s_ctx @ ada_w[i] + ada_b[i]
        sh1, sc1, g1, sh2, sc2, g2 = jnp.split(mod, 6, -1)
        csh1, csc1, cg1, csh2, csc2, cg2 = jnp.split(mod_c, 6, -1)

        B, L, D = h.shape
        C = hc.shape[1]
        w_in_p = pack_w_in(w_in[i])
        p = in_proj(h, sh1, sc1, w_in_p)
        pc = in_proj(hc, csh1.reshape(1, 1, D), csc1.reshape(1, 1, D), w_in_p)

        y_hy, yc_hy = hyena_branch(p, pc, B, L, C, need_ctx, mats, mats_c, hy_conv_w[i], hy_conv_b[i],
                                   hy_f_w1[i], hy_f_b1[i], hy_f_w2[i], hy_f_b2[i], hy_f_w3[i],
                                   hy_f_freq[i], hy_bias[i])
        y_sw, yc_sw = swa_branch(p, pc, B, L, C, need_ctx, swa_sink[i], swa_tables)
        y_rw, yc_rw = rwkv_branch(p, pc, B, L, C, need_ctx, rwkv_mu[i], rwkv_w0[i], rwkv_w2[i],
                                  rwkv_a0[i], rwkv_a2[i], rwkv_g2[i], rwkv_kk[i], rwkv_ka[i],
                                  rwkv_rk[i], rwkv_lnx_g[i], rwkv_lnx_b[i])
        lam_init = 0.8 - 0.6 * math.exp(-0.3 * i)
        y_df, yc_df = diff_branch(p, pc, B, L, C, need_ctx, df_tables, diff_lq1[i], diff_lk1[i],
                                  diff_lq2[i], diff_lk2[i], diff_subln_g[i], lam_init)

        h = merge_block((y_hy, y_sw, y_rw, y_df), p, h, g1, w_branch[i], w_out[i], ln1_g[i], ln1_b[i])
        h = ffn_block(h, sh2, sc2, g2, ffn_w_up[i], ffn_conv_w[i], ffn_conv_b[i], ffn_w_down[i],
                      ln2_g[i], ln2_b[i])

        if need_ctx:
            hc = merge_block((yc_hy, yc_sw, yc_rw, yc_df), pc, hc, cg1.reshape(1, 1, D), w_branch[i],
                             w_out[i], ln1_g[i], ln1_b[i])
            hc = ffn_block(hc, csh2.reshape(1, 1, -1), csc2.reshape(1, 1, -1), cg2.reshape(1, 1, -1),
                           ffn_w_up[i], ffn_conv_w[i], ffn_conv_b[i], ffn_w_down[i], ln2_g[i], ln2_b[i])
    return h
```
